```python
import jax
import jax.numpy as jnp
from jax import lax
import numpy as np

D_MODEL = 1024
BATCH = 8
SEQ = 4096
DEPTH = 2
DEC_BATCH = 32
DEC_SEQ = 16
PAST_LEN = 2048

CHUNK = 64
N_HEADS = 8
N_KV_HEADS = 2
HEAD_DIM = 64
GQA_GROUP = N_HEADS // N_KV_HEADS
ATT_WIDTH = N_HEADS * HEAD_DIM
KV_WIDTH = N_KV_HEADS * HEAD_DIM
IDX_HEADS = 8
IDX_DIM = 32
TOPK_MAX = 256
Q_BLOCK = 128
SGU_GROUPS = 4
SGU_CHUNK = 128
SGU_WIDTH = 512
SGU_GDIM = SGU_WIDTH // SGU_GROUPS
N_GROUPS = 4
EXP_PER_GROUP = 8
EXP_TOPK = 2
EXP_HIDDEN = 128
ROPE_THETA = 10000.0
EPS = 1e-6
IN_SPLITS = (ATT_WIDTH, KV_WIDTH, KV_WIDTH, IDX_HEADS * IDX_DIM, IDX_DIM, IDX_HEADS, SGU_WIDTH, SGU_WIDTH, D_MODEL, D_MODEL)
IN_OFFSETS = tuple(int(o) for o in np.cumsum(IN_SPLITS)[:-1])
IN_COLS = int(sum(IN_SPLITS))

kernel_name = 'hybrid_dsa_sgu_hmoe_stream_step'


def rms_norm(x, g):
    xf = x.astype(jnp.float32)
    y = xf * lax.rsqrt(jnp.mean(xf * xf, axis=-1, keepdims=True) + EPS)
    return (y * g.astype(jnp.float32)).astype(x.dtype)


def layer_norm(x, g, b):
    xf = x.astype(jnp.float32)
    mu = jnp.mean(xf, axis=-1, keepdims=True)
    var = jnp.mean(jnp.square(xf - mu), axis=-1, keepdims=True)
    y = (xf - mu) * lax.rsqrt(var + EPS)
    return (y * g.astype(jnp.float32) + b.astype(jnp.float32)).astype(x.dtype)


def rope(x, pos):
    d = x.shape[-1]
    inv = ROPE_THETA ** (-jnp.arange(0, d, 2, dtype=jnp.float32) / d)
    ang = pos[:, None] * inv[None, :]
    cos = jnp.cos(ang)[:, None, :]
    sin = jnp.sin(ang)[:, None, :]
    xf = x.astype(jnp.float32)
    x1, x2 = xf[..., : d // 2], xf[..., d // 2:]
    return jnp.concatenate([x1 * cos - x2 * sin, x2 * cos + x1 * sin], axis=-1).astype(x.dtype)


def attn_inputs(q, k, v, qi, ki, wi, pos):
    b, t = q.shape[:2]
    q = rope(q.reshape(b, t, N_HEADS, HEAD_DIM), pos).reshape(b, t, N_KV_HEADS, GQA_GROUP, HEAD_DIM)
    k = rope(k.reshape(b, t, N_KV_HEADS, HEAD_DIM), pos)
    v = v.reshape(b, t, N_KV_HEADS, HEAD_DIM)
    qi = rope(qi.reshape(b, t, IDX_HEADS, IDX_DIM), pos)
    ki = rope(ki[:, :, None, :], pos)[:, :, 0, :]
    wi = wi * (IDX_HEADS ** -0.5)
    return q, k, v, qi, ki, wi


def dsa_attend(q, qi, wi, limit, keys, vals, kidx, topk):
    b, tq = q.shape[:2]
    n_keys = keys.shape[1]
    rel = jax.nn.relu(jnp.einsum('bqhd,bsd->bqhs', qi, kidx).astype(jnp.float32))
    score = jnp.einsum('bqh,bqhs->bqs', wi.astype(jnp.float32), rel)
    admissible = jnp.arange(n_keys, dtype=jnp.int32)[None, :] < limit[:, None]
    score = jnp.where(admissible[None], score, -jnp.inf)
    _, sel = lax.top_k(score, topk)
    valid = sel < limit[None, :, None]
    gather = jax.vmap(lambda a, i: a[i])
    k_sel = gather(keys, sel)
    v_sel = gather(vals, sel)
    logits = jnp.einsum('bqhgd,bqkhd->bqhgk', q, k_sel).astype(jnp.float32) * (HEAD_DIM ** -0.5)
    logits = jnp.where(valid[:, :, None, None, :], logits, -jnp.inf)
    p = jax.nn.softmax(logits, axis=-1).astype(v_sel.dtype)
    o = jnp.einsum('bqhgk,bqkhd->bqhgd', p, v_sel)
    return o.reshape(b, tq, ATT_WIDTH)


def dsa_prompt(q, k, v, qi, ki, wi):
    b, t = q.shape[:2]
    n_blk = t // Q_BLOCK
    topk = min(TOPK_MAX, t // 4)
    limit = (jnp.arange(t, dtype=jnp.int32) // CHUNK + 1) * CHUNK

    def to_blocks(a):
        return jnp.moveaxis(a.reshape((b, n_blk, Q_BLOCK) + a.shape[2:]), 1, 0)

    xs = (to_blocks(q), to_blocks(qi), to_blocks(wi), limit.reshape(n_blk, Q_BLOCK))
    out = lax.map(lambda blk: dsa_attend(blk[0], blk[1], blk[2], blk[3], k, v, ki, topk), xs)
    return jnp.moveaxis(out, 0, 1).reshape(b, t, ATT_WIDTH)


def dsa_sample(q, k, v, qi, ki, wi, k_past, v_past, ki_past):
    t = q.shape[1]
    keys = jnp.concatenate([k_past, k], axis=1)
    vals = jnp.concatenate([v_past, v], axis=1)
    kidx = jnp.concatenate([ki_past, ki], axis=1)
    n_keys = keys.shape[1]
    limit = jnp.full((t,), n_keys, dtype=jnp.int32)
    return dsa_attend(q, qi, wi, limit, keys, vals, kidx, min(TOPK_MAX, n_keys // 4))


def sgu(u, v, ln_g, ln_b, w_s, b_s):
    b, t, _ = u.shape
    u = jax.nn.gelu(u)
    v = layer_norm(jax.nn.gelu(v), ln_g, ln_b)
    pad = (-t) % SGU_CHUNK
    n_chunk = (t + pad) // SGU_CHUNK
    vc = jnp.pad(v, ((0, 0), (0, pad), (0, 0))).reshape(b, n_chunk, SGU_CHUNK, SGU_GROUPS, SGU_GDIM)
    w = w_s * jnp.tril(jnp.ones((SGU_CHUNK, SGU_CHUNK), w_s.dtype))[None]
    s = jnp.einsum('gij,bnjgc->bnigc', w, vc) + b_s.T[None, None, :, :, None]
    s = s.reshape(b, t + pad, SGU_WIDTH)[:, :t]
    return u * s, v


def hier_moe(x, w_rg, b_rg, w_re, b_re, w_eg, w_eu, w_ed):
    b, t, d = x.shape
    xf = x.reshape(b * t, d)
    pg = jax.nn.softmax((xf @ w_rg + b_rg).astype(jnp.float32), axis=-1)
    g_w, g_idx = lax.top_k(pg, 1)
    g_hot = jax.nn.one_hot(g_idx[:, 0], N_GROUPS, dtype=jnp.float32)
    le = (xf @ w_re + b_re).astype(jnp.float32).reshape(-1, N_GROUPS, EXP_PER_GROUP)
    pe = jax.nn.softmax(jnp.einsum('ng,nge->ne', g_hot, le), axis=-1)
    e_w, e_idx = lax.top_k(pe, EXP_TOPK)
    e_w = e_w / jnp.sum(e_w, axis=-1, keepdims=True)
    in_group = jnp.einsum('nk,nke->ne', e_w, jax.nn.one_hot(e_idx, EXP_PER_GROUP, dtype=jnp.float32))
    gate = ((g_hot * g_w)[:, :, None] * in_group[:, None, :]).astype(x.dtype)
    out = jnp.zeros_like(xf)
    for g in range(N_GROUPS):
        h = jax.nn.silu(jnp.einsum('nd,edf->nef', xf, w_eg[g])) * jnp.einsum('nd,edf->nef', xf, w_eu[g])
        out = out + jnp.einsum('nef,efd->nd', h * gate[:, g, :, None], w_ed[g])
    return out.reshape(b, t, d)


def trunk_layer(x, pos, past, norm1, w_in, sgu_ln_g, sgu_ln_b, sgu_w, sgu_b, w_pa, w_pb, w_out,
                norm2, w_rg, b_rg, w_re, b_re, w_eg, w_eu, w_ed):
    xn = rms_norm(x, norm1)
    q, k, v, qi, ki, wi, u_b, v_b, g_a, g_b = jnp.split(jnp.einsum('btd,dc->btc', xn, w_in), IN_OFFSETS, axis=-1)
    q, k, v, qi, ki, wi = attn_inputs(q, k, v, qi, ki, wi, pos)
    if past is None:
        att = dsa_prompt(q, k, v, qi, ki, wi)
    else:
        att = dsa_sample(q, k, v, qi, ki, wi, past[0], past[1], past[2])
    sg, v_rows = sgu(u_b, v_b, sgu_ln_g, sgu_ln_b, sgu_w, sgu_b)
    merged = jax.nn.sigmoid(g_a) * (att @ w_pa) + jax.nn.sigmoid(g_b) * (sg @ w_pb)
    h = x + merged @ w_out
    h = h + hier_moe(rms_norm(h, norm2), w_rg, b_rg, w_re, b_re, w_eg, w_eu, w_ed)
    return h, k, v, ki, v_rows


def setup_inputs(seed: int = 0) -> dict:
    key = jax.random.key(seed)
    ks = jax.random.split(key, 23)

    def nrm(k, shape, scale):
        return scale * jax.random.normal(k, shape, jnp.float32)

    L, D = DEPTH, D_MODEL
    return {
        'x_prompt': nrm(ks[0], (BATCH, SEQ, D), 1.0),
        'x_sample': nrm(ks[1], (DEC_BATCH, DEC_SEQ, D), 1.0),
        'cache_k': nrm(ks[2], (L, DEC_BATCH, PAST_LEN, N_KV_HEADS, HEAD_DIM), 1.0),
        'cache_v': nrm(ks[3], (L, DEC_BATCH, PAST_LEN, N_KV_HEADS, HEAD_DIM), 1.0),
        'cache_kidx': nrm(ks[4], (L, DEC_BATCH, PAST_LEN, IDX_DIM), 1.0),
        'norm1': 1.0 + nrm(ks[5], (L, D), 0.02),
        'w_in': nrm(ks[6], (L, D, IN_COLS), D ** -0.5),
        'sgu_ln_g': 1.0 + nrm(ks[7], (L, SGU_WIDTH), 0.02),
        'sgu_ln_b': nrm(ks[8], (L, SGU_WIDTH), 0.02),
        'sgu_w': nrm(ks[9], (L, SGU_GROUPS, SGU_CHUNK, SGU_CHUNK), SGU_CHUNK ** -0.5),
        'sgu_b': 1.0 + nrm(ks[10], (L, SGU_GROUPS, SGU_CHUNK), 0.1),
        'w_pa': nrm(ks[11], (L, ATT_WIDTH, D), ATT_WIDTH ** -0.5),
        'w_pb': nrm(ks[12], (L, SGU_WIDTH, D), SGU_WIDTH ** -0.5),
        'w_out': nrm(ks[13], (L, D, D), D ** -0.5),
        'norm2': 1.0 + nrm(ks[14], (L, D), 0.02),
        'w_rg': nrm(ks[15], (L, D, N_GROUPS), D ** -0.5),
        'b_rg': nrm(ks[16], (L, N_GROUPS), 0.01),
        'w_re': nrm(ks[17], (L, D, N_GROUPS * EXP_PER_GROUP), D ** -0.5),
        'b_re': nrm(ks[18], (L, N_GROUPS * EXP_PER_GROUP), 0.01),
        'w_eg': nrm(ks[19], (L, N_GROUPS, EXP_PER_GROUP, D, EXP_HIDDEN), D ** -0.5),
        'w_eu': nrm(ks[20], (L, N_GROUPS, EXP_PER_GROUP, D, EXP_HIDDEN), D ** -0.5),
        'w_ed': nrm(ks[21], (L, N_GROUPS, EXP_PER_GROUP, EXP_HIDDEN, D), EXP_HIDDEN ** -0.5),
        'final_norm': 1.0 + nrm(ks[22], (D,), 0.02),
    }


def reference(x_prompt, x_sample, cache_k, cache_v, cache_kidx, norm1, w_in, sgu_ln_g, sgu_ln_b, sgu_w, sgu_b,
              w_pa, w_pb, w_out, norm2, w_rg, b_rg, w_re, b_re, w_eg, w_eu, w_ed, final_norm):
    pos_p = jnp.arange(x_prompt.shape[1], dtype=jnp.float32)
    pos_s = PAST_LEN + jnp.arange(x_sample.shape[1], dtype=jnp.float32)
    hp, hs = x_prompt, x_sample
    kp, vp, kip, ksm, vsm, kism, svsm = [], [], [], [], [], [], []
    for l in range(DEPTH):
        w = (norm1[l], w_in[l], sgu_ln_g[l], sgu_ln_b[l], sgu_w[l], sgu_b[l], w_pa[l], w_pb[l], w_out[l],
             norm2[l], w_rg[l], b_rg[l], w_re[l], b_re[l], w_eg[l], w_eu[l], w_ed[l])
        hp, k1, v1, ki1, _ = trunk_layer(hp, pos_p, None, *w)
        hs, k2, v2, ki2, sv2 = trunk_layer(hs, pos_s, (cache_k[l], cache_v[l], cache_kidx[l]), *w)
        kp.append(k1)
        vp.append(v1)
        kip.append(ki1)
        ksm.append(k2)
        vsm.append(v2)
        kism.append(ki2)
        svsm.append(sv2)
    y_prompt = rms_norm(hp, final_norm)
    y_sample = rms_norm(hs, final_norm)
    return (y_prompt, y_sample, jnp.stack(kp), jnp.stack(vp), jnp.stack(kip),
            jnp.stack(ksm), jnp.stack(vsm), jnp.stack(kism), jnp.stack(svsm))
```

```python
import functools

import jax
import jax.numpy as jnp
import numpy as np
from jax import lax
from jax.experimental import pallas as pl
from jax.experimental.pallas import tpu as pltpu

F32 = jnp.float32
BF16 = jnp.bfloat16

CHUNK = 64
N_HEADS = 8
N_KV_HEADS = 2
HEAD_DIM = 64
GQA_GROUP = N_HEADS // N_KV_HEADS
ATT_WIDTH = N_HEADS * HEAD_DIM
KV_WIDTH = N_KV_HEADS * HEAD_DIM
IDX_HEADS = 8
IDX_DIM = 32
IDX_WIDTH = IDX_HEADS * IDX_DIM
TOPK_MAX = 256
Q_BLOCK = 128
SGU_GROUPS = 4
SGU_CHUNK = 128
SGU_WIDTH = 512
SGU_GDIM = SGU_WIDTH // SGU_GROUPS
N_GROUPS = 4
EXP_PER_GROUP = 8
EXP_HIDDEN = 128
GROUP_HIDDEN = EXP_PER_GROUP * EXP_HIDDEN
ROPE_THETA = 10000.0
EPS = 1e-6

LANES = 128
ROUTER_LANES = LANES
ROUTER_E0 = N_GROUPS
VMEM_LIMIT = 56 * 1024 * 1024
KEY_TILE = 512
NEG_INF = float("-inf")


def _dot(a, b):
    return jnp.dot(a, b, preferred_element_type=F32)


def _dot_t(a, b):
    return lax.dot_general(a, b, (((1,), (1,)), ((), ())), preferred_element_type=F32)


def _rms(x, g):
    return x * lax.rsqrt(jnp.mean(x * x, axis=-1, keepdims=True) + EPS) * g


def _swap_halves(x, d):
    lane = lax.broadcasted_iota(jnp.int32, x.shape, 1)
    first = (lane % d) < (d // 2)
    return jnp.where(first, pltpu.roll(x, LANES - d // 2, 1), pltpu.roll(x, d // 2, 1))


def _rope(x, cos, sin_signed, d):
    return x * cos + _swap_halves(x, d) * sin_signed


def _inproj_kernel(x_ref, g1_ref, wa_ref, wb_ref, wc_ref, cq_ref, sq_ref, ci_ref, si_ref,
                   sguw_ref, sgub_ref, lng_ref, lnb_ref, wpb_ref,
                   q_ref, kf_ref, kb_ref, vf_ref, vb_ref, qi_ref, kif_ref, kib_ref, wi_ref,
                   mb_ref, sa_ref, *maybe_vln_ref, sgu_chunk):
    tm = x_ref.shape[0]
    xn = _rms(x_ref[...], g1_ref[...]).astype(BF16)
    cq, sq, ci, si = cq_ref[...], sq_ref[...], ci_ref[...], si_ref[...]

    a = _dot(xn, wa_ref[...])
    for c in range(ATT_WIDTH // LANES):
        r = _rope(a[:, c * LANES:(c + 1) * LANES], cq, sq, HEAD_DIM)
        q_ref[:, c * LANES:(c + 1) * LANES] = (r * (HEAD_DIM ** -0.5)).astype(BF16)
    k = _rope(a[:, ATT_WIDTH:ATT_WIDTH + KV_WIDTH], cq, sq, HEAD_DIM)
    kf_ref[...] = k
    kb_ref[...] = k.astype(BF16)
    v = a[:, ATT_WIDTH + KV_WIDTH:ATT_WIDTH + 2 * KV_WIDTH]
    vf_ref[...] = v
    vb_ref[...] = v.astype(BF16)
    qi0 = ATT_WIDTH + 2 * KV_WIDTH
    for c in range(IDX_WIDTH // LANES):
        r = _rope(a[:, qi0 + c * LANES:qi0 + (c + 1) * LANES], ci, si, IDX_DIM)
        qi_ref[:, c * LANES:(c + 1) * LANES] = r.astype(BF16)

    b = _dot(xn, wb_ref[...])
    ki = _rope(b, ci, si, IDX_DIM)[:, :IDX_DIM]
    kif_ref[...] = ki
    kib_ref[...] = ki.astype(BF16)
    wi_ref[...] = b[:, IDX_DIM:IDX_DIM + IDX_HEADS] * (IDX_HEADS ** -0.5)

    u = jax.nn.gelu(_dot(xn, wc_ref[:, 0:SGU_WIDTH]))
    vs = jax.nn.gelu(_dot(xn, wc_ref[:, SGU_WIDTH:2 * SGU_WIDTH]))
    mu = jnp.mean(vs, axis=-1, keepdims=True)
    var = jnp.mean(jnp.square(vs - mu), axis=-1, keepdims=True)
    vln = (vs - mu) * lax.rsqrt(var + EPS) * lng_ref[...] + lnb_ref[...]
    if maybe_vln_ref:
        maybe_vln_ref[0][...] = vln
    ri = lax.broadcasted_iota(jnp.int32, (SGU_CHUNK, SGU_CHUNK), 0)
    cj = lax.broadcasted_iota(jnp.int32, (SGU_CHUNK, SGU_CHUNK), 1)
    mix_mask = (cj <= ri) & ((ri // sgu_chunk) == (cj // sgu_chunk))
    vln_b = vln.astype(BF16)
    w_eff = [jnp.where(mix_mask, sguw_ref[g], 0.0).astype(BF16) for g in range(SGU_GROUPS)]
    sg_rows = []
    for r in range(tm // SGU_CHUNK):
        rows = slice(r * SGU_CHUNK, (r + 1) * SGU_CHUNK)
        mixed = [_dot(w_eff[g], vln_b[rows, g * SGU_GDIM:(g + 1) * SGU_GDIM]) for g in range(SGU_GROUPS)]
        sg_rows.append(u[rows, :] * (jnp.concatenate(mixed, axis=1) + sgub_ref[...]))
    sg = jnp.concatenate(sg_rows, axis=0).astype(BF16)
    d = x_ref.shape[1]
    sa_ref[...] = jax.nn.sigmoid(_dot(xn, wc_ref[:, 2 * SGU_WIDTH:2 * SGU_WIDTH + d])).astype(BF16)
    gb = jax.nn.sigmoid(_dot(xn, wc_ref[:, 2 * SGU_WIDTH + d:]))
    mb_ref[...] = (gb * _dot(sg, wpb_ref[...])).astype(BF16)


def _inproj(x, w, tabs, *, tm, sgu_chunk, want_vln):
    n, d = x.shape
    n_tab_tiles = tabs[0].shape[0] // tm
    row = lambda t: (t, 0)
    const = lambda t: (0, 0)
    tab = lambda t: (t % n_tab_tiles, 0)
    out_shapes = [
        jax.ShapeDtypeStruct((n, ATT_WIDTH), BF16),
        jax.ShapeDtypeStruct((n, KV_WIDTH), F32),
        jax.ShapeDtypeStruct((n, KV_WIDTH), BF16),
        jax.ShapeDtypeStruct((n, KV_WIDTH), F32),
        jax.ShapeDtypeStruct((n, KV_WIDTH), BF16),
        jax.ShapeDtypeStruct((n, IDX_WIDTH), BF16),
        jax.ShapeDtypeStruct((n, IDX_DIM), F32),
        jax.ShapeDtypeStruct((n, IDX_DIM), BF16),
        jax.ShapeDtypeStruct((n, IDX_HEADS), F32),
        jax.ShapeDtypeStruct((n, d), BF16),
        jax.ShapeDtypeStruct((n, d), BF16),
    ]
    if want_vln:
        out_shapes.append(jax.ShapeDtypeStruct((n, SGU_WIDTH), F32))
    out_specs = [pl.BlockSpec((tm, s.shape[1]), row) for s in out_shapes]
    in_specs = [
        pl.BlockSpec((tm, d), row),
        pl.BlockSpec((1, d), const),
        pl.BlockSpec(w["wa"].shape, const),
        pl.BlockSpec(w["wb"].shape, const),
        pl.BlockSpec(w["wc"].shape, const),
        pl.BlockSpec((tm, LANES), tab), pl.BlockSpec((tm, LANES), tab),
        pl.BlockSpec((tm, LANES), tab), pl.BlockSpec((tm, LANES), tab),
        pl.BlockSpec(w["sguw"].shape, lambda t: (0, 0, 0)),
        pl.BlockSpec(w["sgub"].shape, const),
        pl.BlockSpec((1, SGU_WIDTH), const),
        pl.BlockSpec((1, SGU_WIDTH), const),
        pl.BlockSpec(w["wpb"].shape, const),
    ]
    return pl.pallas_call(
        functools.partial(_inproj_kernel, sgu_chunk=sgu_chunk),
        grid=(n // tm,),
        in_specs=in_specs,
        out_specs=out_specs,
        out_shape=out_shapes,
        compiler_params=pltpu.CompilerParams(dimension_semantics=("arbitrary",), vmem_limit_bytes=VMEM_LIMIT),
        name="inproj",
    )(x, w["g1"], w["wa"], w["wb"], w["wc"], *tabs, w["sguw"], w["sgub"], w["lng"], w["lnb"], w["wpb"])


def _float_key(x):
    b = lax.bitcast_convert_type(x, jnp.int32)
    return b ^ ((b >> 31) & jnp.int32(0x7FFFFFFF))


def _key_float(k):
    return lax.bitcast_convert_type(k ^ ((k >> 31) & jnp.int32(0x7FFFFFFF)), F32)


def _lane_fold(x):
    acc = x[:, 0:LANES]
    for c in range(1, x.shape[1] // LANES):
        acc = acc + x[:, c * LANES:(c + 1) * LANES]
    return acc


def _dsa_core(q_of, qi, wi, k_of, v_of, ki_of, s_scr, lg_scr, n_tiles, tk, lim, topk, write_out):
    r = qi.shape[0]
    kk = jnp.minimum(lim, topk).astype(F32)
    wi_cols = [wi[:, h:h + 1] for h in range(IDX_HEADS)]
    qi_heads = [qi[:, h * IDX_DIM:(h + 1) * IDX_DIM] for h in range(IDX_HEADS)]

    def score_tile(j, carry):
        smax, smin = carry
        off = pl.multiple_of(j * tk, tk)
        kt = ki_of(off)
        acc = jnp.zeros((r, tk), F32)
        for h in range(IDX_HEADS):
            acc = acc + wi_cols[h] * jnp.maximum(_dot_t(qi_heads[h], kt), 0.0)
        col = off + lax.broadcasted_iota(jnp.int32, (r, tk), 1)
        ok = col < lim
        s_scr[j] = jnp.where(ok, acc, NEG_INF)
        smax = jnp.maximum(smax, jnp.max(jnp.where(ok, acc, NEG_INF), axis=1, keepdims=True))
        smin = jnp.minimum(smin, jnp.min(jnp.where(ok, acc, -NEG_INF), axis=1, keepdims=True))
        return smax, smin

    smax, smin = lax.fori_loop(0, n_tiles, score_tile,
                               (jnp.full((r, 1), NEG_INF, F32), jnp.full((r, 1), -NEG_INF, F32)))

    def count_ge(t):
        def body(j, acc):
            return acc + _lane_fold(jnp.where(s_scr[j] >= t, 1.0, 0.0))
        acc = lax.fori_loop(0, n_tiles, body, jnp.zeros((r, LANES), F32))
        return jnp.sum(acc, axis=1, keepdims=True)

    lo0 = _float_key(smin + 0.0)
    hi0 = _float_key(smax + 0.0)
    c0 = lim.astype(F32)
    hi0 = jnp.where(c0 == kk, lo0, hi0)

    def bis_cond(st):
        it, active, _, _, _ = st
        return jnp.logical_and(it < 34, active > 0.5)

    def bis_body(st):
        it, _, lo, hi, c_lo = st
        mid = (lo >> 1) + (hi >> 1) + ((lo | hi) & 1)
        c = count_ge(_key_float(mid))
        ok = c >= kk
        eq = c == kk
        lo_n = jnp.where(ok, mid, lo)
        hi_n = jnp.where(eq, mid, jnp.where(ok, hi, mid - 1))
        c_lo = jnp.where(ok, c, c_lo)
        active = jnp.max(jnp.where(lo_n < hi_n, 1.0, 0.0))
        return it + 1, active, lo_n, hi_n, c_lo

    active0 = jnp.max(jnp.where(lo0 < hi0, 1.0, 0.0))
    _, _, lo, _, c_lo = lax.while_loop(bis_cond, bis_body, (jnp.int32(0), active0, lo0, hi0, c0))
    thr = _key_float(lo)

    @pl.when(jnp.max(c_lo - kk) > 0.5)
    def _():
        def gt_body(j, acc):
            return acc + _lane_fold(jnp.where(s_scr[j] > thr, 1.0, 0.0))
        n_gt = jnp.sum(lax.fori_loop(0, n_tiles, gt_body, jnp.zeros((r, LANES), F32)), axis=1, keepdims=True)
        need = kk - n_gt
        ui = lax.broadcasted_iota(jnp.int32, (LANES, LANES), 0)
        uj = lax.broadcasted_iota(jnp.int32, (LANES, LANES), 1)
        upper = jnp.where(ui <= uj, 1.0, 0.0).astype(BF16)

        def tie_body(j, run):
            blk = s_scr[j]
            outs = []
            for c in range(tk // LANES):
                sub = blk[:, c * LANES:(c + 1) * LANES]
                tied = sub == thr
                tied_f = jnp.where(tied, 1.0, 0.0)
                prefix = _dot(tied_f.astype(BF16), upper) + run
                outs.append(jnp.where(jnp.logical_and(tied, prefix > need), NEG_INF, sub))
                run = run + jnp.sum(tied_f, axis=1, keepdims=True)
            s_scr[j] = jnp.concatenate(outs, axis=1) if len(outs) > 1 else outs[0]
            return run

        lax.fori_loop(0, n_tiles, tie_body, jnp.zeros((r, 1), F32))

    def bias_body(j, carry):
        s_scr[j] = jnp.where(s_scr[j] >= thr, 0.0, NEG_INF)
        return carry

    lax.fori_loop(0, n_tiles, bias_body, 0)

    for h in range(N_HEADS):
        g = h // GQA_GROUP
        qh = q_of(h)

        def logit_body(j, m, qh=qh, g=g):
            off = pl.multiple_of(j * tk, tk)
            lg = _dot_t(qh, k_of(off)[:, g * HEAD_DIM:(g + 1) * HEAD_DIM]) + s_scr[j]
            lg_scr[j] = lg
            return jnp.maximum(m, jnp.max(lg, axis=1, keepdims=True))

        m = lax.fori_loop(0, n_tiles, logit_body, jnp.full((r, 1), NEG_INF, F32))

        def pv_body(j, carry, m=m):
            l, acc = carry
            off = pl.multiple_of(j * tk, tk)
            p = jnp.exp(lg_scr[j] - m)
            l = l + jnp.sum(p, axis=1, keepdims=True)
            acc = acc + _dot(p.astype(BF16), v_of(off))
            return l, acc

        l, acc = lax.fori_loop(0, n_tiles, pv_body,
                               (jnp.zeros((r, 1), F32), jnp.zeros((r, KV_WIDTH), F32)))
        write_out(h, (acc[:, g * HEAD_DIM:(g + 1) * HEAD_DIM] / l).astype(BF16))


def _dsa_prompt_kernel(q_ref, qi_ref, wi_ref, k_ref, v_ref, ki_ref, o_ref, s_scr, lg_scr, *, topk, tk):
    i = pl.program_id(1)
    row = lax.broadcasted_iota(jnp.int32, (Q_BLOCK, 1), 0)
    lim = ((i * Q_BLOCK + row) // CHUNK + 1) * CHUNK
    n_keys = (i + 1) * Q_BLOCK
    n_tiles = (n_keys + tk - 1) // tk

    def write_out(h, val):
        o_ref[0, :, h * HEAD_DIM:(h + 1) * HEAD_DIM] = val

    _dsa_core(lambda h: q_ref[0, :, h * HEAD_DIM:(h + 1) * HEAD_DIM], qi_ref[0], wi_ref[0],
              lambda off: k_ref[0, pl.ds(off, tk), :], lambda off: v_ref[0, pl.ds(off, tk), :],
              lambda off: ki_ref[0, pl.ds(off, tk), :], s_scr, lg_scr, n_tiles, tk, lim, topk, write_out)


def _dsa_prompt(q, qi, wi, kb, vb, kib):
    b, t, _ = q.shape
    tk = min(KEY_TILE, t)
    topk = min(TOPK_MAX, t // 4)
    blk = lambda w: pl.BlockSpec((1, Q_BLOCK, w), lambda bi, i: (bi, i, 0))
    whole = lambda w: pl.BlockSpec((1, t, w), lambda bi, i: (bi, 0, 0))
    return pl.pallas_call(
        functools.partial(_dsa_prompt_kernel, topk=topk, tk=tk),
        grid=(b, t // Q_BLOCK),
        in_specs=[blk(ATT_WIDTH), blk(IDX_WIDTH), blk(IDX_HEADS), whole(KV_WIDTH), whole(KV_WIDTH), whole(IDX_DIM)],
        out_specs=blk(ATT_WIDTH),
        out_shape=jax.ShapeDtypeStruct((b, t, ATT_WIDTH), BF16),
        scratch_shapes=[pltpu.VMEM((t // tk, Q_BLOCK, tk), F32), pltpu.VMEM((t // tk, Q_BLOCK, tk), F32)],
        compiler_params=pltpu.CompilerParams(dimension_semantics=("arbitrary", "arbitrary"),
                                             vmem_limit_bytes=VMEM_LIMIT),
        name="dsa_prompt",
    )(q, qi, wi, kb, vb, kib)


def _dsa_sample_kernel(q_ref, qi_ref, wi_ref, kn_ref, vn_ref, kin_ref, ck_ref, cv_ref, cki_ref, o_ref,
                       kcat, vcat, kicat, s_scr, lg_scr, *, topk, tk):
    tq = q_ref.shape[1]
    past = ck_ref.shape[0]
    total = kcat.shape[0]
    kcat[0:past, :] = ck_ref[...].astype(BF16)
    vcat[0:past, :] = cv_ref[...].astype(BF16)
    kicat[0:past, :] = cki_ref[...].astype(BF16)
    kcat[past:past + tq, :] = kn_ref[0]
    vcat[past:past + tq, :] = vn_ref[0]
    kicat[past:past + tq, :] = kin_ref[0]
    pad = total - past - tq
    kcat[past + tq:total, :] = jnp.zeros((pad, KV_WIDTH), BF16)
    vcat[past + tq:total, :] = jnp.zeros((pad, KV_WIDTH), BF16)
    kicat[past + tq:total, :] = jnp.zeros((pad, IDX_DIM), BF16)
    lim = jnp.full((tq, 1), past + tq, jnp.int32)

    def write_out(h, val):
        o_ref[0, :, h * HEAD_DIM:(h + 1) * HEAD_DIM] = val

    _dsa_core(lambda h: q_ref[0, :, h * HEAD_DIM:(h + 1) * HEAD_DIM], qi_ref[0], wi_ref[0],
              lambda off: kcat[pl.ds(off, tk), :], lambda off: vcat[pl.ds(off, tk), :],
              lambda off: kicat[pl.ds(off, tk), :], s_scr, lg_scr, total // tk, tk, lim, topk, write_out)


def _dsa_sample(q, qi, wi, kb, vb, kib, cache_k, cache_v, cache_kidx, layer):
    b, tq, _ = q.shape
    past = cache_k.shape[2]
    tk = LANES
    total = -(-(past + tq) // tk) * tk
    topk = min(TOPK_MAX, (past + tq) // 4)
    blk = lambda w: pl.BlockSpec((1, tq, w), lambda bi: (bi, 0, 0))
    cache = lambda w: pl.BlockSpec((None, None, past, w), lambda bi: (layer, bi, 0, 0))
    return pl.pallas_call(
        functools.partial(_dsa_sample_kernel, topk=topk, tk=tk),
        grid=(b,),
        in_specs=[blk(ATT_WIDTH), blk(IDX_WIDTH), blk(IDX_HEADS), blk(KV_WIDTH), blk(KV_WIDTH), blk(IDX_DIM),
                  cache(KV_WIDTH), cache(KV_WIDTH), cache(IDX_DIM)],
        out_specs=blk(ATT_WIDTH),
        out_shape=jax.ShapeDtypeStruct((b, tq, ATT_WIDTH), BF16),
        scratch_shapes=[pltpu.VMEM((total, KV_WIDTH), BF16), pltpu.VMEM((total, KV_WIDTH), BF16),
                        pltpu.VMEM((total, IDX_DIM), BF16),
                        pltpu.VMEM((total // tk, tq, tk), F32), pltpu.VMEM((total // tk, tq, tk), F32)],
        compiler_params=pltpu.CompilerParams(dimension_semantics=("arbitrary",), vmem_limit_bytes=VMEM_LIMIT),
        name="dsa_sample",
    )(q, qi, wi, kb, vb, kib, cache_k, cache_v, cache_kidx)


def _first_lane_where(mask, lane):
    return jnp.min(jnp.where(mask, lane, LANES), axis=1, keepdims=True)


def _post_kernel(att_ref, sa_ref, mb_ref, x_ref, wpa_ref, wout_ref, g2_ref, wr_ref, br_ref,
                 h_ref, xn2_ref, gate_ref):
    pa = _dot(att_ref[...], wpa_ref[...])
    merged = sa_ref[...].astype(F32) * pa + mb_ref[...].astype(F32)
    h = x_ref[...] + _dot(merged.astype(BF16), wout_ref[...])
    h_ref[...] = h
    xn2 = _rms(h, g2_ref[...])
    xh = xn2.astype(BF16)
    xn2_ref[...] = xh

    xl = (xn2 - xh.astype(F32)).astype(BF16)
    wr = wr_ref[...]
    wh = wr.astype(BF16)
    wl = (wr - wh.astype(F32)).astype(BF16)
    lg = _dot(xh, wh) + (_dot(xl, wh) + _dot(xh, wl)) + br_ref[...]

    lane = lax.broadcasted_iota(jnp.int32, lg.shape, 1)
    is_g = lane < N_GROUPS
    mg = jnp.max(jnp.where(is_g, lg, NEG_INF), axis=1, keepdims=True)
    eg = jnp.where(is_g, jnp.exp(lg - mg), 0.0)
    pg = eg / jnp.sum(eg, axis=1, keepdims=True)
    g_w = jnp.max(pg, axis=1, keepdims=True)
    g_idx = _first_lane_where(jnp.logical_and(is_g, pg == g_w), lane)

    e_lo = ROUTER_E0 + EXP_PER_GROUP * g_idx
    is_e = jnp.logical_and(lane >= e_lo, lane < e_lo + EXP_PER_GROUP)
    me = jnp.max(jnp.where(is_e, lg, NEG_INF), axis=1, keepdims=True)
    ee = jnp.where(is_e, jnp.exp(lg - me), 0.0)
    pe = ee / jnp.sum(ee, axis=1, keepdims=True)
    e1 = jnp.max(pe, axis=1, keepdims=True)
    i1 = _first_lane_where(jnp.logical_and(is_e, pe == e1), lane)
    rest = jnp.logical_and(is_e, lane != i1)
    e2 = jnp.max(jnp.where(rest, pe, -1.0), axis=1, keepdims=True)
    i2 = _first_lane_where(jnp.logical_and(rest, pe == e2), lane)
    den = e1 + e2
    gate = jnp.where(lane == i1, e1 / den, jnp.where(lane == i2, e2 / den, 0.0)) * g_w
    for g in range(N_GROUPS):
        shifted = pltpu.roll(gate, LANES - (ROUTER_E0 + EXP_PER_GROUP * g), 1)
        gate_ref[:, g * LANES:(g + 1) * LANES] = jnp.where(lane < EXP_PER_GROUP, shifted, 0.0)


def _post(att, sa, mb, x, w, *, tm):
    n, d = x.shape
    row = lambda t: (t, 0)
    const = lambda t: (0, 0)
    return pl.pallas_call(
        _post_kernel,
        grid=(n // tm,),
        in_specs=[pl.BlockSpec((tm, ATT_WIDTH), row), pl.BlockSpec((tm, d), row), pl.BlockSpec((tm, d), row),
                  pl.BlockSpec((tm, d), row), pl.BlockSpec(w["wpa"].shape, const), pl.BlockSpec(w["wout"].shape, const),
                  pl.BlockSpec((1, d), const), pl.BlockSpec(w["wr"].shape, const), pl.BlockSpec((1, ROUTER_LANES), const)],
        out_specs=[pl.BlockSpec((tm, d), row), pl.BlockSpec((tm, d), row), pl.BlockSpec((tm, N_GROUPS * LANES), row)],
        out_shape=[jax.ShapeDtypeStruct((n, d), F32), jax.ShapeDtypeStruct((n, d), BF16),
                   jax.ShapeDtypeStruct((n, N_GROUPS * LANES), F32)],
        compiler_params=pltpu.CompilerParams(dimension_semantics=("arbitrary",), vmem_limit_bytes=VMEM_LIMIT),
        name="post_attn",
    )(att, sa, mb, x, w["wpa"], w["wout"], w["g2"], w["wr"], w["br"])


def _moe_kernel(xn2_ref, gate_ref, h_ref, wg_ref, wu_ref, wd_ref, fn_ref, o_ref, acc_ref, *, final):
    g = pl.program_id(1)

    @pl.when(g == 0)
    def _():
        acc_ref[...] = jnp.zeros_like(acc_ref)

    x = xn2_ref[...]
    gate = gate_ref[...]
    act = jax.nn.silu(_dot(x, wg_ref[0])) * _dot(x, wu_ref[0])
    parts = []
    for e in range(EXP_PER_GROUP):
        parts.append((act[:, e * EXP_HIDDEN:(e + 1) * EXP_HIDDEN] * gate[:, e:e + 1]).astype(BF16))
    acc_ref[...] += _dot(jnp.concatenate(parts, axis=1), wd_ref[0])

    @pl.when(g == N_GROUPS - 1)
    def _():
        y = h_ref[...] + acc_ref[...]
        o_ref[...] = _rms(y, fn_ref[...]) if final else y


def _moe(xn2, gate, h, w, fn, *, tm, final):
    n, d = h.shape
    row = lambda t, g: (t, 0)
    wspec = lambda shp: pl.BlockSpec((1,) + shp[1:], lambda t, g: (g, 0, 0))
    return pl.pallas_call(
        functools.partial(_moe_kernel, final=final),
        grid=(n // tm, N_GROUPS),
        in_specs=[pl.BlockSpec((tm, d), row), pl.BlockSpec((tm, LANES), lambda t, g: (t, g)), pl.BlockSpec((tm, d), row),
                  wspec(w["weg"].shape), wspec(w["weu"].shape), wspec(w["wed"].shape),
                  pl.BlockSpec((1, d), lambda t, g: (0, 0))],
        out_specs=pl.BlockSpec((tm, d), row),
        out_shape=jax.ShapeDtypeStruct((n, d), F32),
        scratch_shapes=[pltpu.VMEM((tm, d), F32)],
        compiler_params=pltpu.CompilerParams(dimension_semantics=("arbitrary", "arbitrary"),
                                             vmem_limit_bytes=VMEM_LIMIT),
        name="moe",
    )(xn2, gate, h, w["weg"], w["weu"], w["wed"], fn)


def _rope_tables(pos, d):
    inv = ROPE_THETA ** (-jnp.arange(0, d, 2, dtype=F32) / d)
    ang = pos[:, None] * inv[None, :]
    cos, sin = jnp.cos(ang), jnp.sin(ang)
    reps = LANES // d
    return (jnp.tile(jnp.concatenate([cos, cos], axis=-1), (1, reps)),
            jnp.tile(jnp.concatenate([-sin, sin], axis=-1), (1, reps)))


def _layer_weights(l, norm1, w_in, sgu_ln_g, sgu_ln_b, sgu_w, sgu_b, w_pa, w_pb, w_out, norm2,
                   w_rg, b_rg, w_re, b_re, w_eg, w_eu, w_ed, sgu_chunk):
    d = w_in.shape[1]
    a_end = ATT_WIDTH + 2 * KV_WIDTH + IDX_WIDTH
    b_end = a_end + IDX_DIM + IDX_HEADS
    reps = SGU_CHUNK // sgu_chunk
    n_router = N_GROUPS + N_GROUPS * EXP_PER_GROUP
    return dict(
        g1=norm1[l][None, :],
        wa=w_in[l][:, :a_end].astype(BF16),
        wb=jnp.pad(w_in[l][:, a_end:b_end], ((0, 0), (0, LANES - (b_end - a_end)))).astype(BF16),
        wc=w_in[l][:, b_end:].astype(BF16),
        sguw=jnp.tile(sgu_w[l][:, :sgu_chunk, :sgu_chunk], (1, reps, reps)),
        sgub=jnp.repeat(jnp.tile(sgu_b[l][:, :sgu_chunk], (1, reps)).T, SGU_GDIM, axis=1),
        lng=sgu_ln_g[l][None, :], lnb=sgu_ln_b[l][None, :],
        wpb=w_pb[l].astype(BF16), wpa=w_pa[l].astype(BF16), wout=w_out[l].astype(BF16),
        g2=norm2[l][None, :],
        wr=jnp.pad(jnp.concatenate([w_rg[l], w_re[l]], axis=1), ((0, 0), (0, ROUTER_LANES - n_router))),
        br=jnp.pad(jnp.concatenate([b_rg[l], b_re[l]]), (0, ROUTER_LANES - n_router))[None, :],
        weg=jnp.transpose(w_eg[l], (0, 2, 1, 3)).reshape(N_GROUPS, d, GROUP_HIDDEN).astype(BF16),
        weu=jnp.transpose(w_eu[l], (0, 2, 1, 3)).reshape(N_GROUPS, d, GROUP_HIDDEN).astype(BF16),
        wed=w_ed[l].reshape(N_GROUPS, GROUP_HIDDEN, d).astype(BF16),
    )


def _token_tile(n):
    return 512 if n % 512 == 0 else n


def kernel(x_prompt, x_sample, cache_k, cache_v, cache_kidx, norm1, w_in, sgu_ln_g, sgu_ln_b, sgu_w, sgu_b,
           w_pa, w_pb, w_out, norm2, w_rg, b_rg, w_re, b_re, w_eg, w_eu, w_ed, final_norm):
    bp, tp, d = x_prompt.shape
    bs, ts, _ = x_sample.shape
    depth, _, past = cache_k.shape[:3]
    np_, ns = bp * tp, bs * ts
    tm_p, tm_s = _token_tile(tp), _token_tile(ns)
    assert tp % tm_p == 0 and tp % Q_BLOCK == 0 and SGU_CHUNK % ts == 0 and tm_s % SGU_CHUNK == 0

    pos_p = jnp.arange(tp, dtype=F32)
    pos_s = jnp.tile(past + jnp.arange(ts, dtype=F32), tm_s // ts)
    tabs_p = _rope_tables(pos_p, HEAD_DIM) + _rope_tables(pos_p, IDX_DIM)
    tabs_s = _rope_tables(pos_s, HEAD_DIM) + _rope_tables(pos_s, IDX_DIM)
    ck = cache_k.reshape(depth, bs, past, KV_WIDTH)
    cv = cache_v.reshape(depth, bs, past, KV_WIDTH)
    fn = final_norm[None, :]

    hp = x_prompt.reshape(np_, d)
    hs = x_sample.reshape(ns, d)
    outs = [[] for _ in range(7)]
    for l in range(depth):
        wargs = (norm1, w_in, sgu_ln_g, sgu_ln_b, sgu_w, sgu_b, w_pa, w_pb, w_out, norm2,
                 w_rg, b_rg, w_re, b_re, w_eg, w_eu, w_ed)
        wp = _layer_weights(l, *wargs, sgu_chunk=SGU_CHUNK)
        ws = dict(wp)
        ws.update({k: v for k, v in _layer_weights(l, *wargs, sgu_chunk=ts).items() if k in ("sguw", "sgub")})
        final = l == depth - 1

        q, kf, kb, vf, vb, qi, kif, kib, wi, mb, sa = _inproj(hp, wp, tabs_p, tm=tm_p, sgu_chunk=SGU_CHUNK,
                                                              want_vln=False)
        r3 = lambda a: a.reshape(bp, tp, a.shape[-1])
        att = _dsa_prompt(r3(q), r3(qi), r3(wi), r3(kb), r3(vb), r3(kib)).reshape(np_, ATT_WIDTH)
        h, xn2, gate = _post(att, sa, mb, hp, wp, tm=tm_p)
        hp = _moe(xn2, gate, h, wp, fn, tm=tm_p, final=final)
        outs[0].append(kf.reshape(bp, tp, N_KV_HEADS, HEAD_DIM))
        outs[1].append(vf.reshape(bp, tp, N_KV_HEADS, HEAD_DIM))
        outs[2].append(kif.reshape(bp, tp, IDX_DIM))

        q, kf, kb, vf, vb, qi, kif, kib, wi, mb, sa, vln = _inproj(hs, ws, tabs_s, tm=tm_s, sgu_chunk=ts,
                                                                   want_vln=True)
        r3 = lambda a: a.reshape(bs, ts, a.shape[-1])
        att = _dsa_sample(r3(q), r3(qi), r3(wi), r3(kb), r3(vb), r3(kib), ck, cv, cache_kidx, l)
        h, xn2, gate = _post(att.reshape(ns, ATT_WIDTH), sa, mb, hs, ws, tm=tm_s)
        hs = _moe(xn2, gate, h, ws, fn, tm=tm_s, final=final)
        outs[3].append(kf.reshape(bs, ts, N_KV_HEADS, HEAD_DIM))
        outs[4].append(vf.reshape(bs, ts, N_KV_HEADS, HEAD_DIM))
        outs[5].append(kif.reshape(bs, ts, IDX_DIM))
        outs[6].append(vln.reshape(bs, ts, SGU_WIDTH))

    return (hp.reshape(bp, tp, d), hs.reshape(bs, ts, d)) + tuple(jnp.stack(o) for o in outs)
```

```python
import functools

import jax
import jax.numpy as jnp
from jax import lax
from jax.experimental import pallas as pl
from jax.experimental.pallas import tpu as pltpu

F32 = jnp.float32
BF16 = jnp.bfloat16

CHUNK = 64
N_HEADS = 8
N_KV_HEADS = 2
HEAD_DIM = 64
GQA_GROUP = N_HEADS // N_KV_HEADS
ATT_WIDTH = N_HEADS * HEAD_DIM
KV_WIDTH = N_KV_HEADS * HEAD_DIM
IDX_HEADS = 8
IDX_DIM = 32
IDX_WIDTH = IDX_HEADS * IDX_DIM
TOPK_MAX = 256
Q_BLOCK = 128
SGU_GROUPS = 4
SGU_CHUNK = 128
SGU_WIDTH = 512
SGU_GDIM = SGU_WIDTH // SGU_GROUPS
N_GROUPS = 4
EXP_PER_GROUP = 8
EXP_HIDDEN = 128
GROUP_HIDDEN = EXP_PER_GROUP * EXP_HIDDEN
ROPE_THETA = 10000.0
EPS = 1e-6
Q_SCALE = HEAD_DIM ** -0.5 * 1.4426950408889634

LANES = 128
SUBLANES = 8
MXU_DIM = 256
ROUTER_LANES = LANES
ROUTER_E0 = N_GROUPS
VMEM_LIMIT = 56 * 1024 * 1024
KEY_TILE = 2 * MXU_DIM
KEY_SUB = MXU_DIM
FOLD_WAYS = 4
SEARCH_FIRST_STEPS = 8
SEARCH_ROUND_STEPS = 3
SEARCH_MAX_ROUNDS = 34
NEG_INF = float("-inf")


def _dot(a, b):
    return jnp.dot(a, b, preferred_element_type=F32)


def _dot_t(a, b):
    return lax.dot_general(a, b, (((1,), (1,)), ((), ())), preferred_element_type=F32)


def _rms(x, g):
    return x * lax.rsqrt(jnp.mean(x * x, axis=-1, keepdims=True) + EPS) * g


def _swap_halves(x, d):
    lane = lax.broadcasted_iota(jnp.int32, x.shape, 1)
    first = (lane % d) < (d // 2)
    return jnp.where(first, pltpu.roll(x, LANES - d // 2, 1), pltpu.roll(x, d // 2, 1))


def _rope(x, cos, sin_signed, d):
    return x * cos + _swap_halves(x, d) * sin_signed


def _inproj_kernel(x_ref, g1_ref, wa_ref, wb_ref, wc_ref, cq_ref, sq_ref, ci_ref, si_ref,
                   sguw_ref, sgub_ref, lng_ref, lnb_ref, wpb_ref,
                   q_ref, kf_ref, kb_ref, vf_ref, vb_ref, vt_ref, qi_ref, kif_ref, kib_ref, wi_ref,
                   mb_ref, sa_ref, *maybe_vln_ref, sgu_chunk):
    tm = x_ref.shape[0]
    xn = _rms(x_ref[...], g1_ref[...]).astype(BF16)
    cq, sq, ci, si = cq_ref[...], sq_ref[...], ci_ref[...], si_ref[...]

    a = _dot(xn, wa_ref[...])
    for c in range(ATT_WIDTH // LANES):
        r = _rope(a[:, c * LANES:(c + 1) * LANES], cq, sq, HEAD_DIM)
        q_ref[:, c * LANES:(c + 1) * LANES] = (r * Q_SCALE).astype(BF16)
    k = _rope(a[:, ATT_WIDTH:ATT_WIDTH + KV_WIDTH], cq, sq, HEAD_DIM)
    kf_ref[...] = k
    kb_ref[...] = k.astype(BF16)
    v = a[:, ATT_WIDTH + KV_WIDTH:ATT_WIDTH + 2 * KV_WIDTH]
    vf_ref[...] = v
    vb_ref[...] = v.astype(BF16)
    for c in range(tm // KEY_SUB):
        vt_ref[c] = v[c * KEY_SUB:(c + 1) * KEY_SUB, :].T.astype(BF16)
    qi0 = ATT_WIDTH + 2 * KV_WIDTH
    for c in range(IDX_WIDTH // LANES):
        r = _rope(a[:, qi0 + c * LANES:qi0 + (c + 1) * LANES], ci, si, IDX_DIM)
        qi_ref[:, c * LANES:(c + 1) * LANES] = r.astype(BF16)

    b = _dot(xn, wb_ref[...])
    ki = _rope(b, ci, si, IDX_DIM)[:, :IDX_DIM]
    kif_ref[...] = ki
    kib_ref[...] = ki.astype(BF16)
    wi_ref[...] = pltpu.roll(b, LANES - IDX_DIM, 1) * (IDX_HEADS ** -0.5)

    u = jax.nn.gelu(_dot(xn, wc_ref[:, 0:SGU_WIDTH]))
    vs = jax.nn.gelu(_dot(xn, wc_ref[:, SGU_WIDTH:2 * SGU_WIDTH]))
    mu = jnp.mean(vs, axis=-1, keepdims=True)
    var = jnp.mean(jnp.square(vs - mu), axis=-1, keepdims=True)
    vln = (vs - mu) * lax.rsqrt(var + EPS) * lng_ref[...] + lnb_ref[...]
    if maybe_vln_ref:
        maybe_vln_ref[0][...] = vln
    ri = lax.broadcasted_iota(jnp.int32, (SGU_CHUNK, SGU_CHUNK), 0)
    cj = lax.broadcasted_iota(jnp.int32, (SGU_CHUNK, SGU_CHUNK), 1)
    mix_mask = (cj <= ri) & ((ri // sgu_chunk) == (cj // sgu_chunk))
    vln_b = vln.astype(BF16)
    w_eff = [jnp.where(mix_mask, sguw_ref[g], 0.0).astype(BF16) for g in range(SGU_GROUPS)]
    sg_rows = []
    for r in range(tm // SGU_CHUNK):
        rows = slice(r * SGU_CHUNK, (r + 1) * SGU_CHUNK)
        mixed = [_dot(w_eff[g], vln_b[rows, g * SGU_GDIM:(g + 1) * SGU_GDIM]) for g in range(SGU_GROUPS)]
        sg_rows.append(u[rows, :] * (jnp.concatenate(mixed, axis=1) + sgub_ref[...]))
    sg = jnp.concatenate(sg_rows, axis=0).astype(BF16)
    d = x_ref.shape[1]
    sa_ref[...] = jax.nn.sigmoid(_dot(xn, wc_ref[:, 2 * SGU_WIDTH:2 * SGU_WIDTH + d])).astype(BF16)
    gb = jax.nn.sigmoid(_dot(xn, wc_ref[:, 2 * SGU_WIDTH + d:]))
    mb_ref[...] = (gb * _dot(sg, wpb_ref[...])).astype(BF16)


def _inproj(x, w, tabs, *, tm, sgu_chunk, want_vln):
    n, d = x.shape
    n_tab_tiles = tabs[0].shape[0] // tm
    row = lambda t: (t, 0)
    const = lambda t: (0, 0)
    tab = lambda t: (t % n_tab_tiles, 0)
    out_shapes = [
        jax.ShapeDtypeStruct((n, ATT_WIDTH), BF16),
        jax.ShapeDtypeStruct((n, KV_WIDTH), F32),
        jax.ShapeDtypeStruct((n, KV_WIDTH), BF16),
        jax.ShapeDtypeStruct((n, KV_WIDTH), F32),
        jax.ShapeDtypeStruct((n, KV_WIDTH), BF16),
        jax.ShapeDtypeStruct((n // KEY_SUB, KV_WIDTH, KEY_SUB), BF16),
        jax.ShapeDtypeStruct((n, IDX_WIDTH), BF16),
        jax.ShapeDtypeStruct((n, IDX_DIM), F32),
        jax.ShapeDtypeStruct((n, IDX_DIM), BF16),
        jax.ShapeDtypeStruct((n, LANES), F32),
        jax.ShapeDtypeStruct((n, d), BF16),
        jax.ShapeDtypeStruct((n, d), BF16),
    ]
    if want_vln:
        out_shapes.append(jax.ShapeDtypeStruct((n, SGU_WIDTH), F32))
    out_specs = [pl.BlockSpec((tm // KEY_SUB, KV_WIDTH, KEY_SUB), lambda t: (t, 0, 0)) if len(s.shape) == 3
                 else pl.BlockSpec((tm, s.shape[1]), row) for s in out_shapes]
    in_specs = [
        pl.BlockSpec((tm, d), row),
        pl.BlockSpec((1, d), const),
        pl.BlockSpec(w["wa"].shape, const),
        pl.BlockSpec(w["wb"].shape, const),
        pl.BlockSpec(w["wc"].shape, const),
        pl.BlockSpec((tm, LANES), tab), pl.BlockSpec((tm, LANES), tab),
        pl.BlockSpec((tm, LANES), tab), pl.BlockSpec((tm, LANES), tab),
        pl.BlockSpec(w["sguw"].shape, lambda t: (0, 0, 0)),
        pl.BlockSpec(w["sgub"].shape, const),
        pl.BlockSpec((1, SGU_WIDTH), const),
        pl.BlockSpec((1, SGU_WIDTH), const),
        pl.BlockSpec(w["wpb"].shape, const),
    ]
    return pl.pallas_call(
        functools.partial(_inproj_kernel, sgu_chunk=sgu_chunk),
        grid=(n // tm,),
        in_specs=in_specs,
        out_specs=out_specs,
        out_shape=out_shapes,
        compiler_params=pltpu.CompilerParams(dimension_semantics=("arbitrary",), vmem_limit_bytes=VMEM_LIMIT),
        name="inproj",
    )(x, w["g1"], w["wa"], w["wb"], w["wc"], *tabs, w["sguw"], w["sgub"], w["lng"], w["lnb"], w["wpb"])


def _float_key(x):
    b = lax.bitcast_convert_type(x, jnp.int32)
    return b ^ ((b >> 31) & jnp.int32(0x7FFFFFFF))


def _key_float(k):
    return lax.bitcast_convert_type(k ^ ((k >> 31) & jnp.int32(0x7FFFFFFF)), F32)


def _fold_rows(x):
    n = x.shape[0] // SUBLANES
    ways = FOLD_WAYS if n % FOLD_WAYS == 0 else 1
    part = jnp.sum(x.reshape(n // ways, ways, SUBLANES, x.shape[1]), axis=0)
    return jnp.sum(part, axis=0)


def _topk_threshold(count_ge, snap, smin, smax, n_adm, kk):
    lo0 = _float_key(smin + 0.0)
    hx0 = _float_key(smax + 0.0) + 1

    def is_active(st):
        lo, hx, c_lo, _ = st
        return jnp.logical_and(c_lo != kk, lo + 1 < hx)

    def step(st, by_key):
        lo, hx, c_lo, c_hx = st
        active = is_active(st)
        if by_key:
            mid = (lo >> 1) + (hx >> 1) + (lo & hx & 1)
        else:
            mid = _float_key(0.5 * _key_float(lo) + 0.5 * _key_float(hx))
        mid = jnp.where(active, jnp.minimum(jnp.maximum(mid, lo + 1), hx - 1), lo)
        c = count_ge(_key_float(mid))
        up = jnp.logical_and(active, c >= kk)
        down = jnp.logical_and(active, c < kk)
        return (jnp.where(up, mid, lo), jnp.where(down, mid, hx),
                jnp.where(up, c, c_lo), jnp.where(down, c, c_hx))

    def snap_step(st):
        lo, hx, c_lo, c_hx = st
        active = is_active(st)
        first_in, last_in = snap(_key_float(lo), _key_float(hx))
        return (jnp.where(active, _float_key(first_in + 0.0), lo),
                jnp.where(active, _float_key(last_in + 0.0) + 1, hx), c_lo, c_hx)

    def n_active(st):
        return jnp.max(jnp.where(is_active(st), 1.0, 0.0))

    st = (lo0, hx0, n_adm, jnp.zeros_like(n_adm))
    for _ in range(SEARCH_FIRST_STEPS):
        st = step(st, by_key=False)

    def cond(carry):
        it, act, _ = carry
        return jnp.logical_and(it < SEARCH_MAX_ROUNDS, act > 0.5)

    def body(carry):
        it, _, st = carry
        st = step(snap_step(st), by_key=True)
        for _ in range(SEARCH_ROUND_STEPS):
            st = step(st, by_key=False)
        return it + 1, n_active(st), st

    _, _, (lo, _, c_lo, c_hx) = lax.while_loop(cond, body, (jnp.int32(0), n_active(st), st))
    return _key_float(lo), c_lo, c_hx


def _demote_excess_ties(s_ref, n_chunks, thr, need):
    ri = lax.broadcasted_iota(jnp.int32, (LANES, LANES), 0)
    cj = lax.broadcasted_iota(jnp.int32, (LANES, LANES), 1)
    lower = jnp.where(cj <= ri, 1.0, 0.0).astype(BF16)

    def body(j, run):
        off = pl.multiple_of(j * LANES, LANES)
        blk = s_ref[pl.ds(off, LANES), :]
        tied = blk == thr
        tied_f = jnp.where(tied, 1.0, 0.0)
        prefix = _dot(lower, tied_f.astype(BF16)) + run
        s_ref[pl.ds(off, LANES), :] = jnp.where(jnp.logical_and(tied, prefix > need), NEG_INF, blk)
        return run + jnp.sum(tied_f, axis=0, keepdims=True)

    lax.fori_loop(0, n_chunks, body, jnp.zeros((1, LANES), F32))


def _select_mask(s_ref, n_tiles, tk, smin, smax, n_adm, topk):
    kk = jnp.minimum(n_adm, float(topk))

    def count_ge(t):
        def body(j, acc):
            off = pl.multiple_of(j * tk, tk)
            return acc + _fold_rows(jnp.where(s_ref[pl.ds(off, tk), :] >= t, 1.0, 0.0))
        acc = lax.fori_loop(0, n_tiles, body, jnp.zeros((SUBLANES, LANES), F32))
        return jnp.sum(acc, axis=0, keepdims=True)

    def snap(lo_f, hx_f):
        def body(j, carry):
            first_in, last_in = carry
            off = pl.multiple_of(j * tk, tk)
            blk = s_ref[pl.ds(off, tk), :]
            first_in = jnp.minimum(first_in, -_fold_max(jnp.where(blk >= lo_f, -blk, NEG_INF)))
            last_in = jnp.maximum(last_in, _fold_max(jnp.where(blk < hx_f, blk, NEG_INF)))
            return first_in, last_in
        first_in, last_in = lax.fori_loop(0, n_tiles, body, (jnp.full((SUBLANES, LANES), -NEG_INF, F32),
                                                             jnp.full((SUBLANES, LANES), NEG_INF, F32)))
        return jnp.min(first_in, axis=0, keepdims=True), jnp.max(last_in, axis=0, keepdims=True)

    thr, c_ge, c_gt = _topk_threshold(count_ge, snap, smin, smax, n_adm, kk)

    @pl.when(jnp.max(c_ge - kk) > 0.5)
    def _():
        _demote_excess_ties(s_ref, n_tiles * (tk // LANES), thr, kk - c_gt)

    def mask_body(j, carry):
        off = pl.multiple_of(j * tk, tk)
        s_ref[pl.ds(off, tk), :] = jnp.where(s_ref[pl.ds(off, tk), :] >= thr, 0.0, NEG_INF)
        return carry

    lax.fori_loop(0, n_tiles, mask_body, 0)


def _dsa_prompt_kernel(q_ref, qi_ref, wi_ref, k_ref, vt_ref, ki_ref, o_ref, s_scr, acc_scr, *, topk):
    i = pl.program_id(1)
    tk, ts = KEY_TILE, KEY_SUB
    n_keys = (i + 1) * Q_BLOCK
    n_tiles = (n_keys + tk - 1) // tk
    qcol = lax.broadcasted_iota(jnp.int32, (1, Q_BLOCK), 1)
    lim = ((i * Q_BLOCK + qcol) // CHUNK + 1) * CHUNK

    qt = q_ref[0].astype(F32).T.astype(BF16)
    qit = qi_ref[0].astype(F32).T.astype(BF16)
    wit = wi_ref[0].T
    qi_rhs = jnp.concatenate([qit[h * IDX_DIM:(h + 1) * IDX_DIM, :] for h in range(IDX_HEADS)], axis=1)
    zero = jnp.zeros((HEAD_DIM, Q_BLOCK), BF16)
    q_cols = []
    for h in range(N_HEADS):
        qh = qt[h * HEAD_DIM:(h + 1) * HEAD_DIM, :]
        q_cols.append(jnp.concatenate([qh, zero] if h < GQA_GROUP else [zero, qh], axis=0))
    q_rhs = jnp.concatenate(q_cols, axis=1)

    def score_tile(j, carry):
        smax, smin = carry
        for c in range(tk // ts):
            off = pl.multiple_of(j * tk + c * ts, ts)
            r = _dot(ki_ref[0, pl.ds(off, ts), :], qi_rhs)
            acc = wit[0:1, :] * jnp.maximum(r[:, 0:LANES], 0.0)
            for h in range(1, IDX_HEADS):
                acc = acc + wit[h:h + 1, :] * jnp.maximum(r[:, h * LANES:(h + 1) * LANES], 0.0)
            key = off + lax.broadcasted_iota(jnp.int32, (ts, Q_BLOCK), 0)
            ok = key < lim
            s_scr[pl.ds(off, ts), :] = jnp.where(ok, acc, NEG_INF)
            smax = jnp.maximum(smax, _fold_max(jnp.where(ok, acc, NEG_INF)))
            smin = jnp.minimum(smin, -_fold_max(jnp.where(ok, -acc, NEG_INF)))
        return smax, smin

    smax, smin = lax.fori_loop(0, n_tiles, score_tile,
                               (jnp.full((SUBLANES, Q_BLOCK), NEG_INF, F32), jnp.full((SUBLANES, Q_BLOCK), -NEG_INF, F32)))
    smax = jnp.max(smax, axis=0, keepdims=True)
    smin = jnp.min(smin, axis=0, keepdims=True)
    _select_mask(s_scr, n_tiles, tk, smin, smax, lim.astype(F32), topk)

    acc_scr[...] = jnp.zeros_like(acc_scr)

    def attn_tile(j, carry):
        m, l = carry
        for c in range(tk // ts):
            off = pl.multiple_of(j * tk + c * ts, ts)
            lg = _dot(k_ref[0, pl.ds(off, ts), :], q_rhs)
            bias = s_scr[pl.ds(off, ts), :]
            vt = vt_ref[j * (tk // ts) + c]
            m_rows, l_rows, alphas, ps = [], [], [], []
            for h in range(N_HEADS):
                x = lg[:, h * LANES:(h + 1) * LANES] + bias
                m_new = jnp.maximum(m[h:h + 1, :], jnp.max(_fold_max(x), axis=0, keepdims=True))
                m_use = jnp.where(m_new > NEG_INF, m_new, 0.0)
                alpha = jnp.exp2(m[h:h + 1, :] - m_use)
                p = jnp.exp2(x - m_use)
                l_rows.append(alpha * l[h:h + 1, :] + jnp.sum(_fold_rows(p), axis=0, keepdims=True))
                m_rows.append(m_new)
                alphas.append(alpha)
                ps.append(p.astype(BF16))
            for pair in range(N_HEADS // 2):
                g = (2 * pair) // GQA_GROUP
                o = _dot(vt[g * HEAD_DIM:(g + 1) * HEAD_DIM, :], jnp.concatenate(ps[2 * pair:2 * pair + 2], axis=1))
                for u in range(2):
                    h = 2 * pair + u
                    rows = slice(h * HEAD_DIM, (h + 1) * HEAD_DIM)
                    acc_scr[rows, :] = alphas[h] * acc_scr[rows, :] + o[:, u * LANES:(u + 1) * LANES]
            m = jnp.concatenate(m_rows, axis=0)
            l = jnp.concatenate(l_rows, axis=0)
        return m, l

    _, l = lax.fori_loop(0, n_tiles, attn_tile,
                         (jnp.full((N_HEADS, Q_BLOCK), NEG_INF, F32), jnp.zeros((N_HEADS, Q_BLOCK), F32)))
    out_t = jnp.concatenate([acc_scr[h * HEAD_DIM:(h + 1) * HEAD_DIM, :] / l[h:h + 1, :] for h in range(N_HEADS)],
                            axis=0)
    o_ref[0] = out_t.T.astype(BF16)


def _fold_max(x):
    n = x.shape[0] // SUBLANES
    ways = FOLD_WAYS if n % FOLD_WAYS == 0 else 1
    part = jnp.max(x.reshape(n // ways, ways, SUBLANES, x.shape[1]), axis=0)
    return jnp.max(part, axis=0)


def _dsa_prompt(q, qi, wi, kb, vt, kib):
    b, t, _ = q.shape
    assert t % KEY_TILE == 0
    topk = min(TOPK_MAX, t // 4)
    blk = lambda w: pl.BlockSpec((1, Q_BLOCK, w), lambda bi, i: (bi, i, 0))
    whole = lambda w: pl.BlockSpec((1, t, w), lambda bi, i: (bi, 0, 0))
    return pl.pallas_call(
        functools.partial(_dsa_prompt_kernel, topk=topk),
        grid=(b, t // Q_BLOCK),
        in_specs=[blk(ATT_WIDTH), blk(IDX_WIDTH), blk(LANES), whole(KV_WIDTH),
                  pl.BlockSpec((t // KEY_SUB, KV_WIDTH, KEY_SUB), lambda bi, i: (bi, 0, 0)), whole(IDX_DIM)],
        out_specs=blk(ATT_WIDTH),
        out_shape=jax.ShapeDtypeStruct((b, t, ATT_WIDTH), BF16),
        scratch_shapes=[pltpu.VMEM((t, Q_BLOCK), F32), pltpu.VMEM((ATT_WIDTH, Q_BLOCK), F32)],
        compiler_params=pltpu.CompilerParams(dimension_semantics=("arbitrary", "arbitrary"),
                                             vmem_limit_bytes=VMEM_LIMIT),
        name="dsa_prompt",
    )(q, qi, wi, kb, vt, kib)


def _dsa_sample_kernel(q_ref, qi_ref, wi_ref, kn_ref, vn_ref, kin_ref, ck_ref, cv_ref, cki_ref, o_ref,
                       kcat, vcat_t, kicat, s_scr, *, topk):
    tq = q_ref.shape[1]
    past = ck_ref.shape[0]
    total = kcat.shape[0]
    pad = total - past - tq
    kcat[0:past, :] = ck_ref[...].astype(BF16)
    kcat[past:past + tq, :] = kn_ref[0]
    kcat[past + tq:total, :] = jnp.zeros((pad, KV_WIDTH), BF16)
    kicat[0:past, :] = cki_ref[...].astype(BF16)
    kicat[past:past + tq, :] = kin_ref[0]
    kicat[past + tq:total, :] = jnp.zeros((pad, IDX_DIM), BF16)
    vcat_t[:, 0:past] = cv_ref[...].T.astype(BF16)
    v_new = jnp.concatenate([vn_ref[0].astype(F32), jnp.zeros((pad, KV_WIDTH), F32)], axis=0)
    vcat_t[:, past:total] = v_new.T.astype(BF16)

    lane = lax.broadcasted_iota(jnp.int32, (1, LANES), 1)
    qi = qi_ref[0]
    qi_stack = jnp.concatenate([qi[:, h * IDX_DIM:(h + 1) * IDX_DIM] for h in range(IDX_HEADS)], axis=0)
    wi_rows = jnp.concatenate([wi_ref[0], jnp.zeros((LANES - tq, LANES), F32)], axis=0).T
    w_flat = jnp.zeros((1, LANES), F32)
    for h in range(IDX_HEADS):
        row = wi_rows[h:h + 1, :]
        shifted = row if h == 0 else pltpu.roll(row, h * tq, 1)
        w_flat = jnp.where(lane // tq == h, shifted, w_flat)

    sc = w_flat * jnp.maximum(_dot_t(kicat[...], qi_stack), 0.0)
    for shift in (LANES // 2, LANES // 4, LANES // 8):
        sc = sc + pltpu.roll(sc, shift, 1)
    sc = jnp.where(lane < tq, sc, 0.0)
    for shift in (LANES // 8, LANES // 4, LANES // 2):
        sc = sc + pltpu.roll(sc, shift, 1)
    key = lax.broadcasted_iota(jnp.int32, (total, LANES), 0)
    ok = key < past + tq
    s_scr[...] = jnp.where(ok, sc, NEG_INF)
    smax = jnp.max(jnp.where(ok, sc, NEG_INF), axis=0, keepdims=True)
    smin = jnp.min(jnp.where(ok, sc, -NEG_INF), axis=0, keepdims=True)
    _select_mask(s_scr, 1, total, smin, smax, jnp.full((1, LANES), float(past + tq), F32), topk)

    q = q_ref[0]
    zero = jnp.zeros((tq, HEAD_DIM), BF16)
    q_stack = jnp.concatenate(
        [jnp.concatenate([q[:, h * HEAD_DIM:(h + 1) * HEAD_DIM], zero] if h < GQA_GROUP
                         else [zero, q[:, h * HEAD_DIM:(h + 1) * HEAD_DIM]], axis=1) for h in range(N_HEADS)], axis=0)
    lg = _dot_t(kcat[...], q_stack) + s_scr[...]
    m = jnp.max(lg, axis=0, keepdims=True)
    p = jnp.exp2(lg - m)
    l = jnp.sum(p, axis=0, keepdims=True)
    o = _dot(vcat_t[...], p.astype(BF16))
    o = jnp.where(lane < GQA_GROUP * tq, o[0:HEAD_DIM, :], o[HEAD_DIM:2 * HEAD_DIM, :]) / l
    o_t = jnp.concatenate([o, jnp.zeros((LANES - HEAD_DIM, LANES), F32)], axis=0).T
    o_ref[0] = jnp.concatenate([o_t[h * tq:(h + 1) * tq, 0:HEAD_DIM] for h in range(N_HEADS)], axis=1).astype(BF16)


def _dsa_sample(q, qi, wi, kb, vb, kib, cache_k, cache_v, cache_kidx, layer):
    b, tq, _ = q.shape
    past = cache_k.shape[2]
    assert tq * N_HEADS == LANES and tq * IDX_HEADS == LANES and past % LANES == 0
    total = past + LANES
    topk = min(TOPK_MAX, (past + tq) // 4)
    blk = lambda w: pl.BlockSpec((1, tq, w), lambda bi: (bi, 0, 0))
    cache = lambda w: pl.BlockSpec((None, None, past, w), lambda bi: (layer, bi, 0, 0))
    return pl.pallas_call(
        functools.partial(_dsa_sample_kernel, topk=topk),
        grid=(b,),
        in_specs=[blk(ATT_WIDTH), blk(IDX_WIDTH), blk(LANES), blk(KV_WIDTH), blk(KV_WIDTH), blk(IDX_DIM),
                  cache(KV_WIDTH), cache(KV_WIDTH), cache(IDX_DIM)],
        out_specs=blk(ATT_WIDTH),
        out_shape=jax.ShapeDtypeStruct((b, tq, ATT_WIDTH), BF16),
        scratch_shapes=[pltpu.VMEM((total, KV_WIDTH), BF16), pltpu.VMEM((KV_WIDTH, total), BF16),
                        pltpu.VMEM((total, IDX_DIM), BF16), pltpu.VMEM((total, LANES), F32)],
        compiler_params=pltpu.CompilerParams(dimension_semantics=("arbitrary",), vmem_limit_bytes=VMEM_LIMIT),
        name="dsa_sample",
    )(q, qi, wi, kb, vb, kib, cache_k, cache_v, cache_kidx)


def _first_lane_where(mask, lane):
    return jnp.min(jnp.where(mask, lane, LANES), axis=1, keepdims=True)


def _post_kernel(att_ref, sa_ref, mb_ref, x_ref, wpa_ref, wout_ref, g2_ref, wr_ref, br_ref,
                 h_ref, xn2_ref, gate_ref):
    pa = _dot(att_ref[...], wpa_ref[...])
    merged = sa_ref[...].astype(F32) * pa + mb_ref[...].astype(F32)
    h = x_ref[...] + _dot(merged.astype(BF16), wout_ref[...])
    h_ref[...] = h
    xn2 = _rms(h, g2_ref[...])
    xh = xn2.astype(BF16)
    xn2_ref[...] = xh

    xl = (xn2 - xh.astype(F32)).astype(BF16)
    wr = wr_ref[...]
    wh = wr.astype(BF16)
    wl = (wr - wh.astype(F32)).astype(BF16)
    lg = _dot(xh, wh) + (_dot(xl, wh) + _dot(xh, wl)) + br_ref[...]

    lane = lax.broadcasted_iota(jnp.int32, lg.shape, 1)
    is_g = lane < N_GROUPS
    mg = jnp.max(jnp.where(is_g, lg, NEG_INF), axis=1, keepdims=True)
    eg = jnp.where(is_g, jnp.exp(lg - mg), 0.0)
    pg = eg / jnp.sum(eg, axis=1, keepdims=True)
    g_w = jnp.max(pg, axis=1, keepdims=True)
    g_idx = _first_lane_where(jnp.logical_and(is_g, pg == g_w), lane)

    e_lo = ROUTER_E0 + EXP_PER_GROUP * g_idx
    is_e = jnp.logical_and(lane >= e_lo, lane < e_lo + EXP_PER_GROUP)
    me = jnp.max(jnp.where(is_e, lg, NEG_INF), axis=1, keepdims=True)
    ee = jnp.where(is_e, jnp.exp(lg - me), 0.0)
    pe = ee / jnp.sum(ee, axis=1, keepdims=True)
    e1 = jnp.max(pe, axis=1, keepdims=True)
    i1 = _first_lane_where(jnp.logical_and(is_e, pe == e1), lane)
    rest = jnp.logical_and(is_e, lane != i1)
    e2 = jnp.max(jnp.where(rest, pe, -1.0), axis=1, keepdims=True)
    i2 = _first_lane_where(jnp.logical_and(rest, pe == e2), lane)
    den = e1 + e2
    gate = jnp.where(lane == i1, e1 / den, jnp.where(lane == i2, e2 / den, 0.0)) * g_w
    for g in range(N_GROUPS):
        shifted = pltpu.roll(gate, LANES - (ROUTER_E0 + EXP_PER_GROUP * g), 1)
        gate_ref[:, g * LANES:(g + 1) * LANES] = jnp.where(lane < EXP_PER_GROUP, shifted, 0.0)


def _post(att, sa, mb, x, w, *, tm):
    n, d = x.shape
    row = lambda t: (t, 0)
    const = lambda t: (0, 0)
    return pl.pallas_call(
        _post_kernel,
        grid=(n // tm,),
        in_specs=[pl.BlockSpec((tm, ATT_WIDTH), row), pl.BlockSpec((tm, d), row), pl.BlockSpec((tm, d), row),
                  pl.BlockSpec((tm, d), row), pl.BlockSpec(w["wpa"].shape, const), pl.BlockSpec(w["wout"].shape, const),
                  pl.BlockSpec((1, d), const), pl.BlockSpec(w["wr"].shape, const), pl.BlockSpec((1, ROUTER_LANES), const)],
        out_specs=[pl.BlockSpec((tm, d), row), pl.BlockSpec((tm, d), row), pl.BlockSpec((tm, N_GROUPS * LANES), row)],
        out_shape=[jax.ShapeDtypeStruct((n, d), F32), jax.ShapeDtypeStruct((n, d), BF16),
                   jax.ShapeDtypeStruct((n, N_GROUPS * LANES), F32)],
        compiler_params=pltpu.CompilerParams(dimension_semantics=("arbitrary",), vmem_limit_bytes=VMEM_LIMIT),
        name="post_attn",
    )(att, sa, mb, x, w["wpa"], w["wout"], w["g2"], w["wr"], w["br"])


def _moe_kernel(xn2_ref, gate_ref, h_ref, wg_ref, wu_ref, wd_ref, fn_ref, o_ref, acc_ref, *, final):
    g = pl.program_id(1)

    @pl.when(g == 0)
    def _():
        acc_ref[...] = jnp.zeros_like(acc_ref)

    x = xn2_ref[...]
    gate = gate_ref[...]
    act = jax.nn.silu(_dot(x, wg_ref[0])) * _dot(x, wu_ref[0])
    parts = []
    for e in range(EXP_PER_GROUP):
        parts.append((act[:, e * EXP_HIDDEN:(e + 1) * EXP_HIDDEN] * gate[:, e:e + 1]).astype(BF16))
    acc_ref[...] += _dot(jnp.concatenate(parts, axis=1), wd_ref[0])

    @pl.when(g == N_GROUPS - 1)
    def _():
        y = h_ref[...] + acc_ref[...]
        o_ref[...] = _rms(y, fn_ref[...]) if final else y


def _moe(xn2, gate, h, w, fn, *, tm, final):
    n, d = h.shape
    row = lambda t, g: (t, 0)
    wspec = lambda shp: pl.BlockSpec((1,) + shp[1:], lambda t, g: (g, 0, 0))
    return pl.pallas_call(
        functools.partial(_moe_kernel, final=final),
        grid=(n // tm, N_GROUPS),
        in_specs=[pl.BlockSpec((tm, d), row), pl.BlockSpec((tm, LANES), lambda t, g: (t, g)), pl.BlockSpec((tm, d), row),
                  wspec(w["weg"].shape), wspec(w["weu"].shape), wspec(w["wed"].shape),
                  pl.BlockSpec((1, d), lambda t, g: (0, 0))],
        out_specs=pl.BlockSpec((tm, d), row),
        out_shape=jax.ShapeDtypeStruct((n, d), F32),
        scratch_shapes=[pltpu.VMEM((tm, d), F32)],
        compiler_params=pltpu.CompilerParams(dimension_semantics=("arbitrary", "arbitrary"),
                                             vmem_limit_bytes=VMEM_LIMIT),
        name="moe",
    )(xn2, gate, h, w["weg"], w["weu"], w["wed"], fn)


def _rope_tables(pos, d):
    inv = ROPE_THETA ** (-jnp.arange(0, d, 2, dtype=F32) / d)
    ang = pos[:, None] * inv[None, :]
    cos, sin = jnp.cos(ang), jnp.sin(ang)
    reps = LANES // d
    return (jnp.tile(jnp.concatenate([cos, cos], axis=-1), (1, reps)),
            jnp.tile(jnp.concatenate([-sin, sin], axis=-1), (1, reps)))


def _layer_weights(l, norm1, w_in, sgu_ln_g, sgu_ln_b, sgu_w, sgu_b, w_pa, w_pb, w_out, norm2,
                   w_rg, b_rg, w_re, b_re, w_eg, w_eu, w_ed, sgu_chunk):
    d = w_in.shape[1]
    a_end = ATT_WIDTH + 2 * KV_WIDTH + IDX_WIDTH
    b_end = a_end + IDX_DIM + IDX_HEADS
    reps = SGU_CHUNK // sgu_chunk
    n_router = N_GROUPS + N_GROUPS * EXP_PER_GROUP
    return dict(
        g1=norm1[l][None, :],
        wa=w_in[l][:, :a_end].astype(BF16),
        wb=jnp.pad(w_in[l][:, a_end:b_end], ((0, 0), (0, LANES - (b_end - a_end)))).astype(BF16),
        wc=w_in[l][:, b_end:].astype(BF16),
        sguw=jnp.tile(sgu_w[l][:, :sgu_chunk, :sgu_chunk], (1, reps, reps)),
        sgub=jnp.repeat(jnp.tile(sgu_b[l][:, :sgu_chunk], (1, reps)).T, SGU_GDIM, axis=1),
        lng=sgu_ln_g[l][None, :], lnb=sgu_ln_b[l][None, :],
        wpb=w_pb[l].astype(BF16), wpa=w_pa[l].astype(BF16), wout=w_out[l].astype(BF16),
        g2=norm2[l][None, :],
        wr=jnp.pad(jnp.concatenate([w_rg[l], w_re[l]], axis=1), ((0, 0), (0, ROUTER_LANES - n_router))),
        br=jnp.pad(jnp.concatenate([b_rg[l], b_re[l]]), (0, ROUTER_LANES - n_router))[None, :],
        weg=jnp.transpose(w_eg[l], (0, 2, 1, 3)).reshape(N_GROUPS, d, GROUP_HIDDEN).astype(BF16),
        weu=jnp.transpose(w_eu[l], (0, 2, 1, 3)).reshape(N_GROUPS, d, GROUP_HIDDEN).astype(BF16),
        wed=w_ed[l].reshape(N_GROUPS, GROUP_HIDDEN, d).astype(BF16),
    )


def _token_tile(n):
    return 512 if n % 512 == 0 else n


def kernel(x_prompt, x_sample, cache_k, cache_v, cache_kidx, norm1, w_in, sgu_ln_g, sgu_ln_b, sgu_w, sgu_b,
           w_pa, w_pb, w_out, norm2, w_rg, b_rg, w_re, b_re, w_eg, w_eu, w_ed, final_norm):
    bp, tp, d = x_prompt.shape
    bs, ts, _ = x_sample.shape
    depth, _, past = cache_k.shape[:3]
    np_, ns = bp * tp, bs * ts
    tm_p, tm_s = _token_tile(tp), _token_tile(ns)
    assert tp % tm_p == 0 and tp % Q_BLOCK == 0 and SGU_CHUNK % ts == 0 and tm_s % SGU_CHUNK == 0
    assert tm_p % KEY_SUB == 0 and tm_s % KEY_SUB == 0

    pos_p = jnp.arange(tp, dtype=F32)
    pos_s = jnp.tile(past + jnp.arange(ts, dtype=F32), tm_s // ts)
    tabs_p = _rope_tables(pos_p, HEAD_DIM) + _rope_tables(pos_p, IDX_DIM)
    tabs_s = _rope_tables(pos_s, HEAD_DIM) + _rope_tables(pos_s, IDX_DIM)
    ck = cache_k.reshape(depth, bs, past, KV_WIDTH)
    cv = cache_v.reshape(depth, bs, past, KV_WIDTH)
    fn = final_norm[None, :]

    hp = x_prompt.reshape(np_, d)
    hs = x_sample.reshape(ns, d)
    outs = [[] for _ in range(7)]
    for l in range(depth):
        wargs = (norm1, w_in, sgu_ln_g, sgu_ln_b, sgu_w, sgu_b, w_pa, w_pb, w_out, norm2,
                 w_rg, b_rg, w_re, b_re, w_eg, w_eu, w_ed)
        wp = _layer_weights(l, *wargs, sgu_chunk=SGU_CHUNK)
        ws = dict(wp)
        ws.update({k: v for k, v in _layer_weights(l, *wargs, sgu_chunk=ts).items() if k in ("sguw", "sgub")})
        final = l == depth - 1

        q, kf, kb, vf, _, vt, qi, kif, kib, wi, mb, sa = _inproj(hp, wp, tabs_p, tm=tm_p, sgu_chunk=SGU_CHUNK,
                                                                 want_vln=False)
        r3 = lambda a: a.reshape(bp, tp, a.shape[-1])
        att = _dsa_prompt(r3(q), r3(qi), r3(wi), r3(kb), vt, r3(kib)).reshape(np_, ATT_WIDTH)
        h, xn2, gate = _post(att, sa, mb, hp, wp, tm=tm_p)
        hp = _moe(xn2, gate, h, wp, fn, tm=tm_p, final=final)
        outs[0].append(kf.reshape(bp, tp, N_KV_HEADS, HEAD_DIM))
        outs[1].append(vf.reshape(bp, tp, N_KV_HEADS, HEAD_DIM))
        outs[2].append(kif.reshape(bp, tp, IDX_DIM))

        q, kf, kb, vf, vb, _, qi, kif, kib, wi, mb, sa, vln = _inproj(hs, ws, tabs_s, tm=tm_s, sgu_chunk=ts,
                                                                      want_vln=True)
        r3 = lambda a: a.reshape(bs, ts, a.shape[-1])
        att = _dsa_sample(r3(q), r3(qi), r3(wi), r3(kb), r3(vb), r3(kib), ck, cv, cache_kidx, l)
        h, xn2, gate = _post(att.reshape(ns, ATT_WIDTH), sa, mb, hs, ws, tm=tm_s)
        hs = _moe(xn2, gate, h, ws, fn, tm=tm_s, final=final)
        outs[3].append(kf.reshape(bs, ts, N_KV_HEADS, HEAD_DIM))
        outs[4].append(vf.reshape(bs, ts, N_KV_HEADS, HEAD_DIM))
        outs[5].append(kif.reshape(bs, ts, IDX_DIM))
        outs[6].append(vln.reshape(bs, ts, SGU_WIDTH))

    return (hp.reshape(bp, tp, d), hs.reshape(bs, ts, d)) + tuple(jnp.stack(o) for o in outs)
```

```python
import functools

import jax
import jax.numpy as jnp
from jax import lax
from jax.experimental import pallas as pl
from jax.experimental.pallas import tpu as pltpu

F32 = jnp.float32
BF16 = jnp.bfloat16

CHUNK = 64
N_HEADS = 8
N_KV_HEADS = 2
HEAD_DIM = 64
GQA_GROUP = N_HEADS // N_KV_HEADS
ATT_WIDTH = N_HEADS * HEAD_DIM
KV_WIDTH = N_KV_HEADS * HEAD_DIM
IDX_HEADS = 8
IDX_DIM = 32
IDX_WIDTH = IDX_HEADS * IDX_DIM
TOPK_MAX = 256
Q_BLOCK = 128
SGU_GROUPS = 4
SGU_CHUNK = 128
SGU_WIDTH = 512
SGU_GDIM = SGU_WIDTH // SGU_GROUPS
N_GROUPS = 4
EXP_PER_GROUP = 8
EXP_HIDDEN = 128
GROUP_HIDDEN = EXP_PER_GROUP * EXP_HIDDEN
ROPE_THETA = 10000.0
EPS = 1e-6
Q_SCALE = HEAD_DIM ** -0.5 * 1.4426950408889634

LANES = 128
SUBLANES = 8
MXU_DIM = 256
ROUTER_LANES = LANES
ROUTER_E0 = N_GROUPS
VMEM_LIMIT = 56 * 1024 * 1024
KEY_TILE = 2 * MXU_DIM
KEY_SUB = MXU_DIM
FOLD_WAYS = 4
SEARCH_FIRST_STEPS = 8
SEARCH_PLAIN_ROUNDS = 2
SEARCH_ROUND_STEPS = 3
SEARCH_MAX_ROUNDS = 34
KNORM_SLACK = 1.0 + 2.0 ** -6
SOFTMAX_SUM_FLOOR = 2.0 ** -80
NEG_INF = float("-inf")


def _dot(a, b):
    return jnp.dot(a, b, preferred_element_type=F32)


def _dot_t(a, b):
    return lax.dot_general(a, b, (((1,), (1,)), ((), ())), preferred_element_type=F32)


def _rms(x, g):
    return x * lax.rsqrt(jnp.mean(x * x, axis=-1, keepdims=True) + EPS) * g


def _swap_halves(x, d):
    lane = lax.broadcasted_iota(jnp.int32, x.shape, 1)
    first = (lane % d) < (d // 2)
    return jnp.where(first, pltpu.roll(x, LANES - d // 2, 1), pltpu.roll(x, d // 2, 1))


def _rope(x, cos, sin_signed, d):
    return x * cos + _swap_halves(x, d) * sin_signed


def _inproj_kernel(x_ref, g1_ref, wa_ref, wb_ref, wc_ref, cq_ref, sq_ref, ci_ref, si_ref,
                   sguw_ref, sgub_ref, lng_ref, lnb_ref, wpb_ref,
                   q_ref, kf_ref, kb_ref, vf_ref, vb_ref, vt_ref, qi_ref, kif_ref, kib_ref, wi_ref,
                   mb_ref, sa_ref, *maybe_vln_ref, sgu_chunk):
    tm = x_ref.shape[0]
    xn = _rms(x_ref[...], g1_ref[...]).astype(BF16)
    cq, sq, ci, si = cq_ref[...], sq_ref[...], ci_ref[...], si_ref[...]

    a = _dot(xn, wa_ref[...])
    for c in range(ATT_WIDTH // LANES):
        r = _rope(a[:, c * LANES:(c + 1) * LANES], cq, sq, HEAD_DIM)
        q_ref[:, c * LANES:(c + 1) * LANES] = (r * Q_SCALE).astype(BF16)
    k = _rope(a[:, ATT_WIDTH:ATT_WIDTH + KV_WIDTH], cq, sq, HEAD_DIM)
    kf_ref[...] = k
    kb_ref[...] = k.astype(BF16)
    v = a[:, ATT_WIDTH + KV_WIDTH:ATT_WIDTH + 2 * KV_WIDTH]
    vf_ref[...] = v
    vb_ref[...] = v.astype(BF16)
    for c in range(tm // KEY_SUB):
        vt_ref[c] = v[c * KEY_SUB:(c + 1) * KEY_SUB, :].T.astype(BF16)
    qi0 = ATT_WIDTH + 2 * KV_WIDTH
    for c in range(IDX_WIDTH // LANES):
        r = _rope(a[:, qi0 + c * LANES:qi0 + (c + 1) * LANES], ci, si, IDX_DIM)
        qi_ref[:, c * LANES:(c + 1) * LANES] = r.astype(BF16)

    b = _dot(xn, wb_ref[...])
    ki = _rope(b, ci, si, IDX_DIM)[:, :IDX_DIM]
    kif_ref[...] = ki
    kib_ref[...] = ki.astype(BF16)
    wi_ref[...] = pltpu.roll(b, LANES - IDX_DIM, 1) * (IDX_HEADS ** -0.5)

    u = jax.nn.gelu(_dot(xn, wc_ref[:, 0:SGU_WIDTH]))
    vs = jax.nn.gelu(_dot(xn, wc_ref[:, SGU_WIDTH:2 * SGU_WIDTH]))
    mu = jnp.mean(vs, axis=-1, keepdims=True)
    var = jnp.mean(jnp.square(vs - mu), axis=-1, keepdims=True)
    vln = (vs - mu) * lax.rsqrt(var + EPS) * lng_ref[...] + lnb_ref[...]
    if maybe_vln_ref:
        maybe_vln_ref[0][...] = vln
    ri = lax.broadcasted_iota(jnp.int32, (SGU_CHUNK, SGU_CHUNK), 0)
    cj = lax.broadcasted_iota(jnp.int32, (SGU_CHUNK, SGU_CHUNK), 1)
    mix_mask = (cj <= ri) & ((ri // sgu_chunk) == (cj // sgu_chunk))
    vln_b = vln.astype(BF16)
    w_eff = [jnp.where(mix_mask, sguw_ref[g], 0.0).astype(BF16) for g in range(SGU_GROUPS)]
    sg_rows = []
    for r in range(tm // SGU_CHUNK):
        rows = slice(r * SGU_CHUNK, (r + 1) * SGU_CHUNK)
        mixed = [_dot(w_eff[g], vln_b[rows, g * SGU_GDIM:(g + 1) * SGU_GDIM]) for g in range(SGU_GROUPS)]
        sg_rows.append(u[rows, :] * (jnp.concatenate(mixed, axis=1) + sgub_ref[...]))
    sg = jnp.concatenate(sg_rows, axis=0).astype(BF16)
    d = x_ref.shape[1]
    sa_ref[...] = jax.nn.sigmoid(_dot(xn, wc_ref[:, 2 * SGU_WIDTH:2 * SGU_WIDTH + d])).astype(BF16)
    gb = jax.nn.sigmoid(_dot(xn, wc_ref[:, 2 * SGU_WIDTH + d:]))
    mb_ref[...] = (gb * _dot(sg, wpb_ref[...])).astype(BF16)


def _inproj(x, w, tabs, *, tm, sgu_chunk, want_vln):
    n, d = x.shape
    n_tab_tiles = tabs[0].shape[0] // tm
    row = lambda t: (t, 0)
    const = lambda t: (0, 0)
    tab = lambda t: (t % n_tab_tiles, 0)
    out_shapes = [
        jax.ShapeDtypeStruct((n, ATT_WIDTH), BF16),
        jax.ShapeDtypeStruct((n, KV_WIDTH), F32),
        jax.ShapeDtypeStruct((n, KV_WIDTH), BF16),
        jax.ShapeDtypeStruct((n, KV_WIDTH), F32),
        jax.ShapeDtypeStruct((n, KV_WIDTH), BF16),
        jax.ShapeDtypeStruct((n // KEY_SUB, KV_WIDTH, KEY_SUB), BF16),
        jax.ShapeDtypeStruct((n, IDX_WIDTH), BF16),
        jax.ShapeDtypeStruct((n, IDX_DIM), F32),
        jax.ShapeDtypeStruct((n, IDX_DIM), BF16),
        jax.ShapeDtypeStruct((n, LANES), F32),
        jax.ShapeDtypeStruct((n, d), BF16),
        jax.ShapeDtypeStruct((n, d), BF16),
    ]
    if want_vln:
        out_shapes.append(jax.ShapeDtypeStruct((n, SGU_WIDTH), F32))
    out_specs = [pl.BlockSpec((tm // KEY_SUB, KV_WIDTH, KEY_SUB), lambda t: (t, 0, 0)) if len(s.shape) == 3
                 else pl.BlockSpec((tm, s.shape[1]), row) for s in out_shapes]
    in_specs = [
        pl.BlockSpec((tm, d), row),
        pl.BlockSpec((1, d), const),
        pl.BlockSpec(w["wa"].shape, const),
        pl.BlockSpec(w["wb"].shape, const),
        pl.BlockSpec(w["wc"].shape, const),
        pl.BlockSpec((tm, LANES), tab), pl.BlockSpec((tm, LANES), tab),
        pl.BlockSpec((tm, LANES), tab), pl.BlockSpec((tm, LANES), tab),
        pl.BlockSpec(w["sguw"].shape, lambda t: (0, 0, 0)),
        pl.BlockSpec(w["sgub"].shape, const),
        pl.BlockSpec((1, SGU_WIDTH), const),
        pl.BlockSpec((1, SGU_WIDTH), const),
        pl.BlockSpec(w["wpb"].shape, const),
    ]
    return pl.pallas_call(
        functools.partial(_inproj_kernel, sgu_chunk=sgu_chunk),
        grid=(n // tm,),
        in_specs=in_specs,
        out_specs=out_specs,
        out_shape=out_shapes,
        compiler_params=pltpu.CompilerParams(dimension_semantics=("arbitrary",), vmem_limit_bytes=VMEM_LIMIT),
        name="inproj",
    )(x, w["g1"], w["wa"], w["wb"], w["wc"], *tabs, w["sguw"], w["sgub"], w["lng"], w["lnb"], w["wpb"])


def _float_key(x):
    b = lax.bitcast_convert_type(x, jnp.int32)
    return b ^ ((b >> 31) & jnp.int32(0x7FFFFFFF))


def _key_float(k):
    return lax.bitcast_convert_type(k ^ ((k >> 31) & jnp.int32(0x7FFFFFFF)), F32)


def _fold_rows(x):
    n = x.shape[0] // SUBLANES
    ways = FOLD_WAYS if n % FOLD_WAYS == 0 else 1
    part = jnp.sum(x.reshape(n // ways, ways, SUBLANES, x.shape[1]), axis=0)
    return jnp.sum(part, axis=0)


def _topk_threshold(count_ge, snap, smin, smax, n_adm, kk):
    lo0 = _float_key(smin + 0.0)
    hx0 = _float_key(smax + 0.0) + 1

    def is_active(st):
        lo, hx, c_lo, _ = st
        return jnp.logical_and(c_lo != kk, lo + 1 < hx)

    def step(st, by_key):
        lo, hx, c_lo, c_hx = st
        active = is_active(st)
        if by_key:
            mid = (lo >> 1) + (hx >> 1) + (lo & hx & 1)
        else:
            mid = _float_key(0.5 * _key_float(lo) + 0.5 * _key_float(hx))
        mid = jnp.where(active, jnp.minimum(jnp.maximum(mid, lo + 1), hx - 1), lo)
        c = count_ge(_key_float(mid))
        up = jnp.logical_and(active, c >= kk)
        down = jnp.logical_and(active, c < kk)
        return (jnp.where(up, mid, lo), jnp.where(down, mid, hx),
                jnp.where(up, c, c_lo), jnp.where(down, c, c_hx))

    def snap_step(st):
        lo, hx, c_lo, c_hx = st
        active = is_active(st)
        first_in, last_in = snap(_key_float(lo), _key_float(hx))
        return (jnp.where(active, _float_key(first_in + 0.0), lo),
                jnp.where(active, _float_key(last_in + 0.0) + 1, hx), c_lo, c_hx)

    def n_active(st):
        return jnp.max(jnp.where(is_active(st), 1.0, 0.0))

    st = (lo0, hx0, n_adm, jnp.zeros_like(n_adm))
    for _ in range(SEARCH_FIRST_STEPS):
        st = step(st, by_key=False)

    def plain_cond(carry):
        it, act, _ = carry
        return jnp.logical_and(it < SEARCH_PLAIN_ROUNDS, act > 0.5)

    def plain_body(carry):
        it, _, st = carry
        for _ in range(SEARCH_ROUND_STEPS + 1):
            st = step(st, by_key=False)
        return it + 1, n_active(st), st

    _, act, st = lax.while_loop(plain_cond, plain_body, (jnp.int32(0), n_active(st), st))

    def cond(carry):
        it, act, _ = carry
        return jnp.logical_and(it < SEARCH_MAX_ROUNDS, act > 0.5)

    def body(carry):
        it, _, st = carry
        st = step(snap_step(st), by_key=True)
        for _ in range(SEARCH_ROUND_STEPS):
            st = step(st, by_key=False)
        return it + 1, n_active(st), st

    _, _, (lo, _, c_lo, c_hx) = lax.while_loop(cond, body, (jnp.int32(0), act, st))
    return _key_float(lo), c_lo, c_hx


def _demote_excess_ties(s_ref, n_chunks, thr, need):
    ri = lax.broadcasted_iota(jnp.int32, (LANES, LANES), 0)
    cj = lax.broadcasted_iota(jnp.int32, (LANES, LANES), 1)
    lower = jnp.where(cj <= ri, 1.0, 0.0).astype(BF16)

    def body(j, run):
        off = pl.multiple_of(j * LANES, LANES)
        blk = s_ref[pl.ds(off, LANES), :]
        tied = blk == thr
        tied_f = jnp.where(tied, 1.0, 0.0)
        prefix = _dot(lower, tied_f.astype(BF16)) + run
        s_ref[pl.ds(off, LANES), :] = jnp.where(jnp.logical_and(tied, prefix > need), NEG_INF, blk)
        return run + jnp.sum(tied_f, axis=0, keepdims=True)

    lax.fori_loop(0, n_chunks, body, jnp.zeros((1, LANES), F32))


def _select_mask(s_ref, n_tiles, tk, smin, smax, n_adm, topk):
    kk = jnp.minimum(n_adm, float(topk))

    def count_ge(t):
        def body(j, acc):
            off = pl.multiple_of(j * tk, tk)
            return acc + _fold_rows(jnp.where(s_ref[pl.ds(off, tk), :] >= t, 1.0, 0.0))
        acc = lax.fori_loop(0, n_tiles, body, jnp.zeros((SUBLANES, LANES), F32))
        return jnp.sum(acc, axis=0, keepdims=True)

    def snap(lo_f, hx_f):
        def body(j, carry):
            first_in, last_in = carry
            off = pl.multiple_of(j * tk, tk)
            blk = s_ref[pl.ds(off, tk), :]
            first_in = jnp.minimum(first_in, -_fold_max(jnp.where(blk >= lo_f, -blk, NEG_INF)))
            last_in = jnp.maximum(last_in, _fold_max(jnp.where(blk < hx_f, blk, NEG_INF)))
            return first_in, last_in
        first_in, last_in = lax.fori_loop(0, n_tiles, body, (jnp.full((SUBLANES, LANES), -NEG_INF, F32),
                                                             jnp.full((SUBLANES, LANES), NEG_INF, F32)))
        return jnp.min(first_in, axis=0, keepdims=True), jnp.max(last_in, axis=0, keepdims=True)

    thr, c_ge, c_gt = _topk_threshold(count_ge, snap, smin, smax, n_adm, kk)

    @pl.when(jnp.max(c_ge - kk) > 0.5)
    def _():
        _demote_excess_ties(s_ref, n_tiles * (tk // LANES), thr, kk - c_gt)

    def mask_body(j, carry):
        off = pl.multiple_of(j * tk, tk)
        s_ref[pl.ds(off, tk), :] = jnp.where(s_ref[pl.ds(off, tk), :] >= thr, 0.0, NEG_INF)
        return carry

    lax.fori_loop(0, n_tiles, mask_body, 0)


def _dsa_prompt_kernel(q_ref, qi_ref, wi_ref, k_ref, vt_ref, ki_ref, o_ref, s_scr, acc_scr, l_scr, kn_scr, lg_scr,
                       *, topk):
    i = pl.program_id(1)
    tk, ts = KEY_TILE, KEY_SUB
    n_keys = (i + 1) * Q_BLOCK
    n_tiles = (n_keys + tk - 1) // tk
    qcol = lax.broadcasted_iota(jnp.int32, (1, Q_BLOCK), 1)
    lim = ((i * Q_BLOCK + qcol) // CHUNK + 1) * CHUNK

    @pl.when(i == 0)
    def _():
        ri = lax.broadcasted_iota(jnp.int32, (KV_WIDTH, LANES), 0)
        cj = lax.broadcasted_iota(jnp.int32, (KV_WIDTH, LANES), 1)
        head_sum = jnp.where(ri // HEAD_DIM == cj, 1.0, 0.0).astype(BF16)

        def body(j, acc):
            kt = k_ref[0, pl.ds(pl.multiple_of(j * tk, tk), tk), :].astype(F32)
            return jnp.maximum(acc, _fold_max(_dot((kt * kt).astype(BF16), head_sum)))

        acc = lax.fori_loop(0, k_ref.shape[1] // tk, body, jnp.zeros((SUBLANES, LANES), F32))
        kn_scr[...] = jnp.max(acc, axis=0, keepdims=True) * KNORM_SLACK

    qt = q_ref[0].astype(F32).T.astype(BF16)
    qit = qi_ref[0].astype(F32).T.astype(BF16)
    wit = wi_ref[0].T
    qi_rhs = jnp.concatenate([qit[h * IDX_DIM:(h + 1) * IDX_DIM, :] for h in range(IDX_HEADS)], axis=1)
    zero = jnp.zeros((HEAD_DIM, Q_BLOCK), BF16)
    q_cols = []
    for h in range(N_HEADS):
        qh = qt[h * HEAD_DIM:(h + 1) * HEAD_DIM, :]
        q_cols.append(jnp.concatenate([qh, zero] if h < GQA_GROUP else [zero, qh], axis=0))
    q_rhs = jnp.concatenate(q_cols, axis=1)

    def score_tile(j, carry):
        smax, smin = carry
        for c in range(tk // ts):
            off = pl.multiple_of(j * tk + c * ts, ts)
            r = _dot(ki_ref[0, pl.ds(off, ts), :], qi_rhs)
            acc = wit[0:1, :] * jnp.maximum(r[:, 0:LANES], 0.0)
            for h in range(1, IDX_HEADS):
                acc = acc + wit[h:h + 1, :] * jnp.maximum(r[:, h * LANES:(h + 1) * LANES], 0.0)
            key = off + lax.broadcasted_iota(jnp.int32, (ts, Q_BLOCK), 0)
            ok = key < lim
            s_scr[pl.ds(off, ts), :] = jnp.where(ok, acc, NEG_INF)
            smax = jnp.maximum(smax, _fold_max(jnp.where(ok, acc, NEG_INF)))
            smin = jnp.minimum(smin, -_fold_max(jnp.where(ok, -acc, NEG_INF)))
        return smax, smin

    smax, smin = lax.fori_loop(0, n_tiles, score_tile,
                               (jnp.full((SUBLANES, Q_BLOCK), NEG_INF, F32), jnp.full((SUBLANES, Q_BLOCK), -NEG_INF, F32)))
    smax = jnp.max(smax, axis=0, keepdims=True)
    smin = jnp.min(smin, axis=0, keepdims=True)
    _select_mask(s_scr, n_tiles, tk, smin, smax, lim.astype(F32), topk)

    qsq = jnp.square(qt.astype(F32))
    kn = kn_scr[...]
    lane1 = lax.broadcasted_iota(jnp.int32, (1, LANES), 1)
    shift_rows = []
    for h in range(N_HEADS):
        qn2 = jnp.sum(_fold_rows(qsq[h * HEAD_DIM:(h + 1) * HEAD_DIM, :]), axis=0, keepdims=True)
        kmax2 = jnp.max(jnp.where(lane1 == h // GQA_GROUP, kn, 0.0))
        shift_rows.append(jnp.sqrt(qn2 * kmax2))
    shift = jnp.concatenate(shift_rows, axis=0)

    def pv_accumulate(vt, ps, alphas):
        for pair in range(N_HEADS // 2):
            g = (2 * pair) // GQA_GROUP
            o = _dot(vt[g * HEAD_DIM:(g + 1) * HEAD_DIM, :], jnp.concatenate(ps[2 * pair:2 * pair + 2], axis=1))
            for u in range(2):
                h = 2 * pair + u
                rows = slice(h * HEAD_DIM, (h + 1) * HEAD_DIM)
                prev = acc_scr[rows, :] if alphas is None else alphas[h] * acc_scr[rows, :]
                acc_scr[rows, :] = prev + o[:, u * LANES:(u + 1) * LANES]

    def logits_into(buf, sub):
        off = pl.multiple_of(jnp.minimum(sub * ts, k_ref.shape[1] - ts), ts)
        lg_scr[buf] = _dot(k_ref[0, pl.ds(off, ts), :], q_rhs)

    def softmax_pv(buf, sub, l):
        bias = s_scr[pl.ds(pl.multiple_of(sub * ts, ts), ts), :]
        l_rows, ps = [], []
        for h in range(N_HEADS):
            p = jnp.exp2(lg_scr[buf, :, h * LANES:(h + 1) * LANES] + (bias - shift[h:h + 1, :]))
            l_rows.append(l[h * SUBLANES:(h + 1) * SUBLANES, :] + _fold_rows(p))
            ps.append(p.astype(BF16))
        pv_accumulate(vt_ref[sub], ps, None)
        return jnp.concatenate(l_rows, axis=0)

    def attn_bounded(j, l):
        logits_into(1, 2 * j + 1)
        l = softmax_pv(0, 2 * j, l)
        logits_into(0, 2 * j + 2)
        return softmax_pv(1, 2 * j + 1, l)

    def attn_online(j, carry):
        m, l = carry
        off = pl.multiple_of(j * ts, ts)
        lg = _dot(k_ref[0, pl.ds(off, ts), :], q_rhs)
        bias = s_scr[pl.ds(off, ts), :]
        m_rows, l_rows, alphas, ps = [], [], [], []
        for h in range(N_HEADS):
            x = lg[:, h * LANES:(h + 1) * LANES] + bias
            m_new = jnp.maximum(m[h:h + 1, :], jnp.max(_fold_max(x), axis=0, keepdims=True))
            m_use = jnp.where(m_new > NEG_INF, m_new, 0.0)
            alpha = jnp.exp2(m[h:h + 1, :] - m_use)
            p = jnp.exp2(x - m_use)
            l_rows.append(alpha * l[h:h + 1, :] + jnp.sum(_fold_rows(p), axis=0, keepdims=True))
            m_rows.append(m_new)
            alphas.append(alpha)
            ps.append(p.astype(BF16))
        pv_accumulate(vt_ref[j], ps, alphas)
        return jnp.concatenate(m_rows, axis=0), jnp.concatenate(l_rows, axis=0)

    n_sub = (n_keys + ts - 1) // ts
    acc_scr[...] = jnp.zeros_like(acc_scr)
    logits_into(0, 0)
    l_part = lax.fori_loop(0, n_tiles, attn_bounded, jnp.zeros((N_HEADS * SUBLANES, Q_BLOCK), F32))
    l_fast = jnp.concatenate([jnp.sum(l_part[h * SUBLANES:(h + 1) * SUBLANES, :], axis=0, keepdims=True)
                              for h in range(N_HEADS)], axis=0)
    l_scr[...] = l_fast

    @pl.when(jnp.min(l_fast) < SOFTMAX_SUM_FLOOR)
    def _():
        acc_scr[...] = jnp.zeros_like(acc_scr)
        _, l_exact = lax.fori_loop(0, n_sub, attn_online,
                                   (jnp.full((N_HEADS, Q_BLOCK), NEG_INF, F32), jnp.zeros((N_HEADS, Q_BLOCK), F32)))
        l_scr[...] = l_exact

    l = l_scr[...]
    out_t = jnp.concatenate([acc_scr[h * HEAD_DIM:(h + 1) * HEAD_DIM, :] / l[h:h + 1, :] for h in range(N_HEADS)],
                            axis=0)
    o_ref[0] = out_t.T.astype(BF16)


def _fold_max(x):
    n = x.shape[0] // SUBLANES
    ways = FOLD_WAYS if n % FOLD_WAYS == 0 else 1
    part = jnp.max(x.reshape(n // ways, ways, SUBLANES, x.shape[1]), axis=0)
    return jnp.max(part, axis=0)


def _dsa_prompt(q, qi, wi, kb, vt, kib):
    b, t, _ = q.shape
    assert t % KEY_TILE == 0
    topk = min(TOPK_MAX, t // 4)
    blk = lambda w: pl.BlockSpec((1, Q_BLOCK, w), lambda bi, i: (bi, i, 0))
    whole = lambda w: pl.BlockSpec((1, t, w), lambda bi, i: (bi, 0, 0))
    return pl.pallas_call(
        functools.partial(_dsa_prompt_kernel, topk=topk),
        grid=(b, t // Q_BLOCK),
        in_specs=[blk(ATT_WIDTH), blk(IDX_WIDTH), blk(LANES), whole(KV_WIDTH),
                  pl.BlockSpec((t // KEY_SUB, KV_WIDTH, KEY_SUB), lambda bi, i: (bi, 0, 0)), whole(IDX_DIM)],
        out_specs=blk(ATT_WIDTH),
        out_shape=jax.ShapeDtypeStruct((b, t, ATT_WIDTH), BF16),
        scratch_shapes=[pltpu.VMEM((t, Q_BLOCK), F32), pltpu.VMEM((ATT_WIDTH, Q_BLOCK), F32),
                        pltpu.VMEM((N_HEADS, Q_BLOCK), F32), pltpu.VMEM((1, LANES), F32),
                        pltpu.VMEM((2, KEY_SUB, N_HEADS * Q_BLOCK), F32)],
        compiler_params=pltpu.CompilerParams(dimension_semantics=("arbitrary", "arbitrary"),
                                             vmem_limit_bytes=VMEM_LIMIT),
        name="dsa_prompt",
    )(q, qi, wi, kb, vt, kib)


def _dsa_sample_kernel(q_ref, qi_ref, wi_ref, kn_ref, vn_ref, kin_ref, ck_ref, cv_ref, cki_ref, o_ref,
                       kcat, vcat_t, kicat, s_scr, *, topk):
    tq = q_ref.shape[1]
    past = ck_ref.shape[0]
    total = kcat.shape[0]
    pad = total - past - tq
    kcat[0:past, :] = ck_ref[...].astype(BF16)
    kcat[past:past + tq, :] = kn_ref[0]
    kcat[past + tq:total, :] = jnp.zeros((pad, KV_WIDTH), BF16)
    kicat[0:past, :] = cki_ref[...].astype(BF16)
    kicat[past:past + tq, :] = kin_ref[0]
    kicat[past + tq:total, :] = jnp.zeros((pad, IDX_DIM), BF16)
    vcat_t[:, 0:past] = cv_ref[...].T.astype(BF16)
    v_new = jnp.concatenate([vn_ref[0].astype(F32), jnp.zeros((pad, KV_WIDTH), F32)], axis=0)
    vcat_t[:, past:total] = v_new.T.astype(BF16)

    lane = lax.broadcasted_iota(jnp.int32, (1, LANES), 1)
    qi = qi_ref[0]
    qi_stack = jnp.concatenate([qi[:, h * IDX_DIM:(h + 1) * IDX_DIM] for h in range(IDX_HEADS)], axis=0)
    wi_rows = jnp.concatenate([wi_ref[0], jnp.zeros((LANES - tq, LANES), F32)], axis=0).T
    w_flat = jnp.zeros((1, LANES), F32)
    for h in range(IDX_HEADS):
        row = wi_rows[h:h + 1, :]
        shifted = row if h == 0 else pltpu.roll(row, h * tq, 1)
        w_flat = jnp.where(lane // tq == h, shifted, w_flat)

    sc = w_flat * jnp.maximum(_dot_t(kicat[...], qi_stack), 0.0)
    for shift in (LANES // 2, LANES // 4, LANES // 8):
        sc = sc + pltpu.roll(sc, shift, 1)
    sc = jnp.where(lane < tq, sc, 0.0)
    for shift in (LANES // 8, LANES // 4, LANES // 2):
        sc = sc + pltpu.roll(sc, shift, 1)
    key = lax.broadcasted_iota(jnp.int32, (total, LANES), 0)
    ok = key < past + tq
    s_scr[...] = jnp.where(ok, sc, NEG_INF)
    smax = jnp.max(jnp.where(ok, sc, NEG_INF), axis=0, keepdims=True)
    smin = jnp.min(jnp.where(ok, sc, -NEG_INF), axis=0, keepdims=True)
    _select_mask(s_scr, 1, total, smin, smax, jnp.full((1, LANES), float(past + tq), F32), topk)

    q = q_ref[0]
    zero = jnp.zeros((tq, HEAD_DIM), BF16)
    q_stack = jnp.concatenate(
        [jnp.concatenate([q[:, h * HEAD_DIM:(h + 1) * HEAD_DIM], zero] if h < GQA_GROUP
                         else [zero, q[:, h * HEAD_DIM:(h + 1) * HEAD_DIM]], axis=1) for h in range(N_HEADS)], axis=0)
    lg = _dot_t(kcat[...], q_stack) + s_scr[...]
    m = jnp.max(lg, axis=0, keepdims=True)
    p = jnp.exp2(lg - m)
    l = jnp.sum(p, axis=0, keepdims=True)
    o = _dot(vcat_t[...], p.astype(BF16))
    o = jnp.where(lane < GQA_GROUP * tq, o[0:HEAD_DIM, :], o[HEAD_DIM:2 * HEAD_DIM, :]) / l
    o_t = jnp.concatenate([o, jnp.zeros((LANES - HEAD_DIM, LANES), F32)], axis=0).T
    o_ref[0] = jnp.concatenate([o_t[h * tq:(h + 1) * tq, 0:HEAD_DIM] for h in range(N_HEADS)], axis=1).astype(BF16)


def _dsa_sample(q, qi, wi, kb, vb, kib, cache_k, cache_v, cache_kidx, layer):
    b, tq, _ = q.shape
    past = cache_k.shape[2]
    assert tq * N_HEADS == LANES and tq * IDX_HEADS == LANES and past % LANES == 0
    total = past + LANES
    topk = min(TOPK_MAX, (past + tq) // 4)
    blk = lambda w: pl.BlockSpec((1, tq, w), lambda bi: (bi, 0, 0))
    cache = lambda w: pl.BlockSpec((None, None, past, w), lambda bi: (layer, bi, 0, 0))
    return pl.pallas_call(
        functools.partial(_dsa_sample_kernel, topk=topk),
        grid=(b,),
        in_specs=[blk(ATT_WIDTH), blk(IDX_WIDTH), blk(LANES), blk(KV_WIDTH), blk(KV_WIDTH), blk(IDX_DIM),
                  cache(KV_WIDTH), cache(KV_WIDTH), cache(IDX_DIM)],
        out_specs=blk(ATT_WIDTH),
        out_shape=jax.ShapeDtypeStruct((b, tq, ATT_WIDTH), BF16),
        scratch_shapes=[pltpu.VMEM((total, KV_WIDTH), BF16), pltpu.VMEM((KV_WIDTH, total), BF16),
                        pltpu.VMEM((total, IDX_DIM), BF16), pltpu.VMEM((total, LANES), F32)],
        compiler_params=pltpu.CompilerParams(dimension_semantics=("arbitrary",), vmem_limit_bytes=VMEM_LIMIT),
        name="dsa_sample",
    )(q, qi, wi, kb, vb, kib, cache_k, cache_v, cache_kidx)


def _first_lane_where(mask, lane):
    return jnp.min(jnp.where(mask, lane, LANES), axis=1, keepdims=True)


def _post_kernel(att_ref, sa_ref, mb_ref, x_ref, wpa_ref, wout_ref, g2_ref, wr_ref, br_ref,
                 h_ref, xn2_ref, gate_ref):
    pa = _dot(att_ref[...], wpa_ref[...])
    merged = sa_ref[...].astype(F32) * pa + mb_ref[...].astype(F32)
    h = x_ref[...] + _dot(merged.astype(BF16), wout_ref[...])
    h_ref[...] = h
    xn2 = _rms(h, g2_ref[...])
    xh = xn2.astype(BF16)
    xn2_ref[...] = xh

    xl = (xn2 - xh.astype(F32)).astype(BF16)
    wr = wr_ref[...]
    wh = wr.astype(BF16)
    wl = (wr - wh.astype(F32)).astype(BF16)
    lg = _dot(xh, wh) + (_dot(xl, wh) + _dot(xh, wl)) + br_ref[...]

    lane = lax.broadcasted_iota(jnp.int32, lg.shape, 1)
    is_g = lane < N_GROUPS
    mg = jnp.max(jnp.where(is_g, lg, NEG_INF), axis=1, keepdims=True)
    eg = jnp.where(is_g, jnp.exp(lg - mg), 0.0)
    pg = eg / jnp.sum(eg, axis=1, keepdims=True)
    g_w = jnp.max(pg, axis=1, keepdims=True)
    g_idx = _first_lane_where(jnp.logical_and(is_g, pg == g_w), lane)

    e_lo = ROUTER_E0 + EXP_PER_GROUP * g_idx
    is_e = jnp.logical_and(lane >= e_lo, lane < e_lo + EXP_PER_GROUP)
    me = jnp.max(jnp.where(is_e, lg, NEG_INF), axis=1, keepdims=True)
    ee = jnp.where(is_e, jnp.exp(lg - me), 0.0)
    pe = ee / jnp.sum(ee, axis=1, keepdims=True)
    e1 = jnp.max(pe, axis=1, keepdims=True)
    i1 = _first_lane_where(jnp.logical_and(is_e, pe == e1), lane)
    rest = jnp.logical_and(is_e, lane != i1)
    e2 = jnp.max(jnp.where(rest, pe, -1.0), axis=1, keepdims=True)
    i2 = _first_lane_where(jnp.logical_and(rest, pe == e2), lane)
    den = e1 + e2
    gate = jnp.where(lane == i1, e1 / den, jnp.where(lane == i2, e2 / den, 0.0)) * g_w
    for g in range(N_GROUPS):
        shifted = pltpu.roll(gate, LANES - (ROUTER_E0 + EXP_PER_GROUP * g), 1)
        gate_ref[:, g * LANES:(g + 1) * LANES] = jnp.where(lane < EXP_PER_GROUP, shifted, 0.0)


def _post(att, sa, mb, x, w, *, tm):
    n, d = x.shape
    row = lambda t: (t, 0)
    const = lambda t: (0, 0)
    return pl.pallas_call(
        _post_kernel,
        grid=(n // tm,),
        in_specs=[pl.BlockSpec((tm, ATT_WIDTH), row), pl.BlockSpec((tm, d), row), pl.BlockSpec((tm, d), row),
                  pl.BlockSpec((tm, d), row), pl.BlockSpec(w["wpa"].shape, const), pl.BlockSpec(w["wout"].shape, const),
                  pl.BlockSpec((1, d), const), pl.BlockSpec(w["wr"].shape, const), pl.BlockSpec((1, ROUTER_LANES), const)],
        out_specs=[pl.BlockSpec((tm, d), row), pl.BlockSpec((tm, d), row), pl.BlockSpec((tm, N_GROUPS * LANES), row)],
        out_shape=[jax.ShapeDtypeStruct((n, d), F32), jax.ShapeDtypeStruct((n, d), BF16),
                   jax.ShapeDtypeStruct((n, N_GROUPS * LANES), F32)],
        compiler_params=pltpu.CompilerParams(dimension_semantics=("arbitrary",), vmem_limit_bytes=VMEM_LIMIT),
        name="post_attn",
    )(att, sa, mb, x, w["wpa"], w["wout"], w["g2"], w["wr"], w["br"])


def _moe_kernel(xn2_ref, gate_ref, h_ref, wg_ref, wu_ref, wd_ref, fn_ref, o_ref, acc_ref, *, final):
    g = pl.program_id(1)

    @pl.when(g == 0)
    def _():
        acc_ref[...] = jnp.zeros_like(acc_ref)

    x = xn2_ref[...]
    gate = gate_ref[...]
    act = jax.nn.silu(_dot(x, wg_ref[0])) * _dot(x, wu_ref[0])
    parts = []
    for e in range(EXP_PER_GROUP):
        parts.append((act[:, e * EXP_HIDDEN:(e + 1) * EXP_HIDDEN] * gate[:, e:e + 1]).astype(BF16))
    acc_ref[...] += _dot(jnp.concatenate(parts, axis=1), wd_ref[0])

    @pl.when(g == N_GROUPS - 1)
    def _():
        y = h_ref[...] + acc_ref[...]
        o_ref[...] = _rms(y, fn_ref[...]) if final else y


def _moe(xn2, gate, h, w, fn, *, tm, final):
    n, d = h.shape
    row = lambda t, g: (t, 0)
    wspec = lambda shp: pl.BlockSpec((1,) + shp[1:], lambda t, g: (g, 0, 0))
    return pl.pallas_call(
        functools.partial(_moe_kernel, final=final),
        grid=(n // tm, N_GROUPS),
        in_specs=[pl.BlockSpec((tm, d), row), pl.BlockSpec((tm, LANES), lambda t, g: (t, g)), pl.BlockSpec((tm, d), row),
                  wspec(w["weg"].shape), wspec(w["weu"].shape), wspec(w["wed"].shape),
                  pl.BlockSpec((1, d), lambda t, g: (0, 0))],
        out_specs=pl.BlockSpec((tm, d), row),
        out_shape=jax.ShapeDtypeStruct((n, d), F32),
        scratch_shapes=[pltpu.VMEM((tm, d), F32)],
        compiler_params=pltpu.CompilerParams(dimension_semantics=("arbitrary", "arbitrary"),
                                             vmem_limit_bytes=VMEM_LIMIT),
        name="moe",
    )(xn2, gate, h, w["weg"], w["weu"], w["wed"], fn)


def _rope_tables(pos, d):
    inv = ROPE_THETA ** (-jnp.arange(0, d, 2, dtype=F32) / d)
    ang = pos[:, None] * inv[None, :]
    cos, sin = jnp.cos(ang), jnp.sin(ang)
    reps = LANES // d
    return (jnp.tile(jnp.concatenate([cos, cos], axis=-1), (1, reps)),
            jnp.tile(jnp.concatenate([-sin, sin], axis=-1), (1, reps)))


def _layer_weights(l, norm1, w_in, sgu_ln_g, sgu_ln_b, sgu_w, sgu_b, w_pa, w_pb, w_out, norm2,
                   w_rg, b_rg, w_re, b_re, w_eg, w_eu, w_ed, sgu_chunk):
    d = w_in.shape[1]
    a_end = ATT_WIDTH + 2 * KV_WIDTH + IDX_WIDTH
    b_end = a_end + IDX_DIM + IDX_HEADS
    reps = SGU_CHUNK // sgu_chunk
    n_router = N_GROUPS + N_GROUPS * EXP_PER_GROUP
    return dict(
        g1=norm1[l][None, :],
        wa=w_in[l][:, :a_end].astype(BF16),
        wb=jnp.pad(w_in[l][:, a_end:b_end], ((0, 0), (0, LANES - (b_end - a_end)))).astype(BF16),
        wc=w_in[l][:, b_end:].astype(BF16),
        sguw=jnp.tile(sgu_w[l][:, :sgu_chunk, :sgu_chunk], (1, reps, reps)),
        sgub=jnp.repeat(jnp.tile(sgu_b[l][:, :sgu_chunk], (1, reps)).T, SGU_GDIM, axis=1),
        lng=sgu_ln_g[l][None, :], lnb=sgu_ln_b[l][None, :],
        wpb=w_pb[l].astype(BF16), wpa=w_pa[l].astype(BF16), wout=w_out[l].astype(BF16),
        g2=norm2[l][None, :],
        wr=jnp.pad(jnp.concatenate([w_rg[l], w_re[l]], axis=1), ((0, 0), (0, ROUTER_LANES - n_router))),
        br=jnp.pad(jnp.concatenate([b_rg[l], b_re[l]]), (0, ROUTER_LANES - n_router))[None, :],
        weg=jnp.transpose(w_eg[l], (0, 2, 1, 3)).reshape(N_GROUPS, d, GROUP_HIDDEN).astype(BF16),
        weu=jnp.transpose(w_eu[l], (0, 2, 1, 3)).reshape(N_GROUPS, d, GROUP_HIDDEN).astype(BF16),
        wed=w_ed[l].reshape(N_GROUPS, GROUP_HIDDEN, d).astype(BF16),
    )


def _token_tile(n):
    return 512 if n % 512 == 0 else n


def kernel(x_prompt, x_sample, cache_k, cache_v, cache_kidx, norm1, w_in, sgu_ln_g, sgu_ln_b, sgu_w, sgu_b,
           w_pa, w_pb, w_out, norm2, w_rg, b_rg, w_re, b_re, w_eg, w_eu, w_ed, final_norm):
    bp, tp, d = x_prompt.shape
    bs, ts, _ = x_sample.shape
    depth, _, past = cache_k.shape[:3]
    np_, ns = bp * tp, bs * ts
    tm_p, tm_s = _token_tile(tp), _token_tile(ns)
    assert tp % tm_p == 0 and tp % Q_BLOCK == 0 and SGU_CHUNK % ts == 0 and tm_s % SGU_CHUNK == 0
    assert tm_p % KEY_SUB == 0 and tm_s % KEY_SUB == 0

    pos_p = jnp.arange(tp, dtype=F32)
    pos_s = jnp.tile(past + jnp.arange(ts, dtype=F32), tm_s // ts)
    tabs_p = _rope_tables(pos_p, HEAD_DIM) + _rope_tables(pos_p, IDX_DIM)
    tabs_s = _rope_tables(pos_s, HEAD_DIM) + _rope_tables(pos_s, IDX_DIM)
    ck = cache_k.reshape(depth, bs, past, KV_WIDTH)
    cv = cache_v.reshape(depth, bs, past, KV_WIDTH)
    fn = final_norm[None, :]

    hp = x_prompt.reshape(np_, d)
    hs = x_sample.reshape(ns, d)
    outs = [[] for _ in range(7)]
    for l in range(depth):
        wargs = (norm1, w_in, sgu_ln_g, sgu_ln_b, sgu_w, sgu_b, w_pa, w_pb, w_out, norm2,
                 w_rg, b_rg, w_re, b_re, w_eg, w_eu, w_ed)
        wp = _layer_weights(l, *wargs, sgu_chunk=SGU_CHUNK)
        ws = dict(wp)
        ws.update({k: v for k, v in _layer_weights(l, *wargs, sgu_chunk=ts).items() if k in ("sguw", "sgub")})
        final = l == depth - 1

        q, kf, kb, vf, _, vt, qi, kif, kib, wi, mb, sa = _inproj(hp, wp, tabs_p, tm=tm_p, sgu_chunk=SGU_CHUNK,
                                                                 want_vln=False)
        r3 = lambda a: a.reshape(bp, tp, a.shape[-1])
        att = _dsa_prompt(r3(q), r3(qi), r3(wi), r3(kb), vt, r3(kib)).reshape(np_, ATT_WIDTH)
        h, xn2, gate = _post(att, sa, mb, hp, wp, tm=tm_p)
        hp = _moe(xn2, gate, h, wp, fn, tm=tm_p, final=final)
        outs[0].append(kf.reshape(bp, tp, N_KV_HEADS, HEAD_DIM))
        outs[1].append(vf.reshape(bp, tp, N_KV_HEADS, HEAD_DIM))
        outs[2].append(kif.reshape(bp, tp, IDX_DIM))

        q, kf, kb, vf, vb, _, qi, kif, kib, wi, mb, sa, vln = _inproj(hs, ws, tabs_s, tm=tm_s, sgu_chunk=ts,
                                                                      want_vln=True)
        r3 = lambda a: a.reshape(bs, ts, a.shape[-1])
        att = _dsa_sample(r3(q), r3(qi), r3(wi), r3(kb), r3(vb), r3(kib), ck, cv, cache_kidx, l)
        h, xn2, gate = _post(att.reshape(ns, ATT_WIDTH), sa, mb, hs, ws, tm=tm_s)
        hs = _moe(xn2, gate, h, ws, fn, tm=tm_s, final=final)
        outs[3].append(kf.reshape(bs, ts, N_KV_HEADS, HEAD_DIM))
        outs[4].append(vf.reshape(bs, ts, N_KV_HEADS, HEAD_DIM))
        outs[5].append(kif.reshape(bs, ts, IDX_DIM))
        outs[6].append(vln.reshape(bs, ts, SGU_WIDTH))

    return (hp.reshape(bp, tp, d), hs.reshape(bs, ts, d)) + tuple(jnp.stack(o) for o in outs)
```

```python
import functools

import jax
import jax.numpy as jnp
from jax import lax
from jax.experimental import pallas as pl
from jax.experimental.pallas import tpu as pltpu

F32 = jnp.float32
BF16 = jnp.bfloat16

CHUNK = 64
N_HEADS = 8
N_KV_HEADS = 2
HEAD_DIM = 64
GQA_GROUP = N_HEADS // N_KV_HEADS
ATT_WIDTH = N_HEADS * HEAD_DIM
KV_WIDTH = N_KV_HEADS * HEAD_DIM
IDX_HEADS = 8
IDX_DIM = 32
IDX_WIDTH = IDX_HEADS * IDX_DIM
TOPK_MAX = 256
Q_BLOCK = 128
SGU_GROUPS = 4
SGU_CHUNK = 128
SGU_WIDTH = 512
SGU_GDIM = SGU_WIDTH // SGU_GROUPS
N_GROUPS = 4
EXP_PER_GROUP = 8
EXP_HIDDEN = 128
GROUP_HIDDEN = EXP_PER_GROUP * EXP_HIDDEN
ROPE_THETA = 10000.0
EPS = 1e-6
Q_SCALE = HEAD_DIM ** -0.5 * 1.4426950408889634

LANES = 128
SUBLANES = 8
MXU_DIM = 256
ROUTER_LANES = LANES
ROUTER_E0 = N_GROUPS
VMEM_LIMIT = 56 * 1024 * 1024
KEY_TILE = 2 * MXU_DIM
KEY_SUB = MXU_DIM
FOLD_WAYS = 4
SEARCH_FIRST_STEPS = 8
SEARCH_PLAIN_ROUNDS = 2
SEARCH_ROUND_STEPS = 3
SEARCH_MAX_ROUNDS = 34
SORT_PAD = 16
MOE_CHUNK = 144
KNORM_SLACK = 1.0 + 2.0 ** -6
SOFTMAX_SUM_FLOOR = 2.0 ** -80
NEG_INF = float("-inf")


def _dot(a, b):
    return jnp.dot(a, b, preferred_element_type=F32)


def _dot_t(a, b):
    return lax.dot_general(a, b, (((1,), (1,)), ((), ())), preferred_element_type=F32)


def _rms(x, g):
    return x * lax.rsqrt(jnp.mean(x * x, axis=-1, keepdims=True) + EPS) * g


def _swap_halves(x, d):
    lane = lax.broadcasted_iota(jnp.int32, x.shape, 1)
    first = (lane % d) < (d // 2)
    return jnp.where(first, pltpu.roll(x, LANES - d // 2, 1), pltpu.roll(x, d // 2, 1))


def _rope(x, cos, sin_signed, d):
    return x * cos + _swap_halves(x, d) * sin_signed


def _inproj_kernel(x_ref, g1_ref, wa_ref, wb_ref, wc_ref, cq_ref, sq_ref, ci_ref, si_ref,
                   sguw_ref, sgub_ref, lng_ref, lnb_ref, wpb_ref,
                   q_ref, kf_ref, kb_ref, vf_ref, vb_ref, vt_ref, qi_ref, kif_ref, kib_ref, wi_ref,
                   mb_ref, sa_ref, *maybe_vln_ref, sgu_chunk):
    tm = x_ref.shape[0]
    xn = _rms(x_ref[...], g1_ref[...]).astype(BF16)
    cq, sq, ci, si = cq_ref[...], sq_ref[...], ci_ref[...], si_ref[...]

    a = _dot(xn, wa_ref[...])
    for c in range(ATT_WIDTH // LANES):
        r = _rope(a[:, c * LANES:(c + 1) * LANES], cq, sq, HEAD_DIM)
        q_ref[:, c * LANES:(c + 1) * LANES] = (r * Q_SCALE).astype(BF16)
    k = _rope(a[:, ATT_WIDTH:ATT_WIDTH + KV_WIDTH], cq, sq, HEAD_DIM)
    kf_ref[...] = k
    kb_ref[...] = k.astype(BF16)
    v = a[:, ATT_WIDTH + KV_WIDTH:ATT_WIDTH + 2 * KV_WIDTH]
    vf_ref[...] = v
    vb_ref[...] = v.astype(BF16)
    for c in range(tm // KEY_SUB):
        vt_ref[c] = v[c * KEY_SUB:(c + 1) * KEY_SUB, :].T.astype(BF16)
    qi0 = ATT_WIDTH + 2 * KV_WIDTH
    for c in range(IDX_WIDTH // LANES):
        r = _rope(a[:, qi0 + c * LANES:qi0 + (c + 1) * LANES], ci, si, IDX_DIM)
        qi_ref[:, c * LANES:(c + 1) * LANES] = r.astype(BF16)

    b = _dot(xn, wb_ref[...])
    ki = _rope(b, ci, si, IDX_DIM)[:, :IDX_DIM]
    kif_ref[...] = ki
    kib_ref[...] = ki.astype(BF16)
    wi_ref[...] = pltpu.roll(b, LANES - IDX_DIM, 1) * (IDX_HEADS ** -0.5)

    u = jax.nn.gelu(_dot(xn, wc_ref[:, 0:SGU_WIDTH]))
    vs = jax.nn.gelu(_dot(xn, wc_ref[:, SGU_WIDTH:2 * SGU_WIDTH]))
    mu = jnp.mean(vs, axis=-1, keepdims=True)
    var = jnp.mean(jnp.square(vs - mu), axis=-1, keepdims=True)
    vln = (vs - mu) * lax.rsqrt(var + EPS) * lng_ref[...] + lnb_ref[...]
    if maybe_vln_ref:
        maybe_vln_ref[0][...] = vln
    ri = lax.broadcasted_iota(jnp.int32, (SGU_CHUNK, SGU_CHUNK), 0)
    cj = lax.broadcasted_iota(jnp.int32, (SGU_CHUNK, SGU_CHUNK), 1)
    mix_mask = (cj <= ri) & ((ri // sgu_chunk) == (cj // sgu_chunk))
    vln_b = vln.astype(BF16)
    w_eff = [jnp.where(mix_mask, sguw_ref[g], 0.0).astype(BF16) for g in range(SGU_GROUPS)]
    sg_rows = []
    for r in range(tm // SGU_CHUNK):
        rows = slice(r * SGU_CHUNK, (r + 1) * SGU_CHUNK)
        mixed = [_dot(w_eff[g], vln_b[rows, g * SGU_GDIM:(g + 1) * SGU_GDIM]) for g in range(SGU_GROUPS)]
        sg_rows.append(u[rows, :] * (jnp.concatenate(mixed, axis=1) + sgub_ref[...]))
    sg = jnp.concatenate(sg_rows, axis=0).astype(BF16)
    d = x_ref.shape[1]
    sa_ref[...] = jax.nn.sigmoid(_dot(xn, wc_ref[:, 2 * SGU_WIDTH:2 * SGU_WIDTH + d])).astype(BF16)
    gb = jax.nn.sigmoid(_dot(xn, wc_ref[:, 2 * SGU_WIDTH + d:]))
    mb_ref[...] = (gb * _dot(sg, wpb_ref[...])).astype(BF16)


def _inproj(x, w, tabs, *, tm, sgu_chunk, want_vln):
    n, d = x.shape
    n_tab_tiles = tabs[0].shape[0] // tm
    row = lambda t: (t, 0)
    const = lambda t: (0, 0)
    tab = lambda t: (t % n_tab_tiles, 0)
    out_shapes = [
        jax.ShapeDtypeStruct((n, ATT_WIDTH), BF16),
        jax.ShapeDtypeStruct((n, KV_WIDTH), F32),
        jax.ShapeDtypeStruct((n, KV_WIDTH), BF16),
        jax.ShapeDtypeStruct((n, KV_WIDTH), F32),
        jax.ShapeDtypeStruct((n, KV_WIDTH), BF16),
        jax.ShapeDtypeStruct((n // KEY_SUB, KV_WIDTH, KEY_SUB), BF16),
        jax.ShapeDtypeStruct((n, IDX_WIDTH), BF16),
        jax.ShapeDtypeStruct((n, IDX_DIM), F32),
        jax.ShapeDtypeStruct((n, IDX_DIM), BF16),
        jax.ShapeDtypeStruct((n, LANES), F32),
        jax.ShapeDtypeStruct((n, d), BF16),
        jax.ShapeDtypeStruct((n, d), BF16),
    ]
    if want_vln:
        out_shapes.append(jax.ShapeDtypeStruct((n, SGU_WIDTH), F32))
    out_specs = [pl.BlockSpec((tm // KEY_SUB, KV_WIDTH, KEY_SUB), lambda t: (t, 0, 0)) if len(s.shape) == 3
                 else pl.BlockSpec((tm, s.shape[1]), row) for s in out_shapes]
    in_specs = [
        pl.BlockSpec((tm, d), row),
        pl.BlockSpec((1, d), const),
        pl.BlockSpec(w["wa"].shape, const),
        pl.BlockSpec(w["wb"].shape, const),
        pl.BlockSpec(w["wc"].shape, const),
        pl.BlockSpec((tm, LANES), tab), pl.BlockSpec((tm, LANES), tab),
        pl.BlockSpec((tm, LANES), tab), pl.BlockSpec((tm, LANES), tab),
        pl.BlockSpec(w["sguw"].shape, lambda t: (0, 0, 0)),
        pl.BlockSpec(w["sgub"].shape, const),
        pl.BlockSpec((1, SGU_WIDTH), const),
        pl.BlockSpec((1, SGU_WIDTH), const),
        pl.BlockSpec(w["wpb"].shape, const),
    ]
    return pl.pallas_call(
        functools.partial(_inproj_kernel, sgu_chunk=sgu_chunk),
        grid=(n // tm,),
        in_specs=in_specs,
        out_specs=out_specs,
        out_shape=out_shapes,
        compiler_params=pltpu.CompilerParams(dimension_semantics=("arbitrary",), vmem_limit_bytes=VMEM_LIMIT),
        name="inproj",
    )(x, w["g1"], w["wa"], w["wb"], w["wc"], *tabs, w["sguw"], w["sgub"], w["lng"], w["lnb"], w["wpb"])


def _float_key(x):
    b = lax.bitcast_convert_type(x, jnp.int32)
    return b ^ ((b >> 31) & jnp.int32(0x7FFFFFFF))


def _key_float(k):
    return lax.bitcast_convert_type(k ^ ((k >> 31) & jnp.int32(0x7FFFFFFF)), F32)


def _fold_rows(x):
    n = x.shape[0] // SUBLANES
    ways = FOLD_WAYS if n % FOLD_WAYS == 0 else 1
    part = jnp.sum(x.reshape(n // ways, ways, SUBLANES, x.shape[1]), axis=0)
    return jnp.sum(part, axis=0)


def _topk_threshold(count_ge, snap, smin, smax, n_adm, kk):
    lo0 = _float_key(smin + 0.0)
    hx0 = _float_key(smax + 0.0) + 1

    def is_active(st):
        lo, hx, c_lo, _ = st
        return jnp.logical_and(c_lo != kk, lo + 1 < hx)

    def step(st, by_key):
        lo, hx, c_lo, c_hx = st
        active = is_active(st)
        if by_key:
            mid = (lo >> 1) + (hx >> 1) + (lo & hx & 1)
        else:
            mid = _float_key(0.5 * _key_float(lo) + 0.5 * _key_float(hx))
        mid = jnp.where(active, jnp.minimum(jnp.maximum(mid, lo + 1), hx - 1), lo)
        c = count_ge(_key_float(mid))
        up = jnp.logical_and(active, c >= kk)
        down = jnp.logical_and(active, c < kk)
        return (jnp.where(up, mid, lo), jnp.where(down, mid, hx),
                jnp.where(up, c, c_lo), jnp.where(down, c, c_hx))

    def snap_step(st):
        lo, hx, c_lo, c_hx = st
        active = is_active(st)
        first_in, last_in = snap(_key_float(lo), _key_float(hx))
        return (jnp.where(active, _float_key(first_in + 0.0), lo),
                jnp.where(active, _float_key(last_in + 0.0) + 1, hx), c_lo, c_hx)

    def n_active(st):
        return jnp.max(jnp.where(is_active(st), 1.0, 0.0))

    st = (lo0, hx0, n_adm, jnp.zeros_like(n_adm))
    for _ in range(SEARCH_FIRST_STEPS):
        st = step(st, by_key=False)

    def plain_cond(carry):
        it, act, _ = carry
        return jnp.logical_and(it < SEARCH_PLAIN_ROUNDS, act > 0.5)

    def plain_body(carry):
        it, _, st = carry
        for _ in range(SEARCH_ROUND_STEPS + 1):
            st = step(st, by_key=False)
        return it + 1, n_active(st), st

    _, act, st = lax.while_loop(plain_cond, plain_body, (jnp.int32(0), n_active(st), st))

    def cond(carry):
        it, act, _ = carry
        return jnp.logical_and(it < SEARCH_MAX_ROUNDS, act > 0.5)

    def body(carry):
        it, _, st = carry
        st = step(snap_step(st), by_key=True)
        for _ in range(SEARCH_ROUND_STEPS):
            st = step(st, by_key=False)
        return it + 1, n_active(st), st

    _, _, (lo, _, c_lo, c_hx) = lax.while_loop(cond, body, (jnp.int32(0), act, st))
    return _key_float(lo), c_lo, c_hx


def _demote_excess_ties(s_ref, n_chunks, thr, need):
    ri = lax.broadcasted_iota(jnp.int32, (LANES, LANES), 0)
    cj = lax.broadcasted_iota(jnp.int32, (LANES, LANES), 1)
    lower = jnp.where(cj <= ri, 1.0, 0.0).astype(BF16)

    def body(j, run):
        off = pl.multiple_of(j * LANES, LANES)
        blk = s_ref[pl.ds(off, LANES), :]
        tied = blk == thr
        tied_f = jnp.where(tied, 1.0, 0.0)
        prefix = _dot(lower, tied_f.astype(BF16)) + run
        s_ref[pl.ds(off, LANES), :] = jnp.where(jnp.logical_and(tied, prefix > need), NEG_INF, blk)
        return run + jnp.sum(tied_f, axis=0, keepdims=True)

    lax.fori_loop(0, n_chunks, body, jnp.zeros((1, LANES), F32))


def _select_mask(s_ref, n_tiles, tk, smin, smax, n_adm, topk):
    kk = jnp.minimum(n_adm, float(topk))

    def count_ge(t):
        def body(j, acc):
            off = pl.multiple_of(j * tk, tk)
            return acc + _fold_rows(jnp.where(s_ref[pl.ds(off, tk), :] >= t, 1.0, 0.0))
        acc = lax.fori_loop(0, n_tiles, body, jnp.zeros((SUBLANES, LANES), F32))
        return jnp.sum(acc, axis=0, keepdims=True)

    def snap(lo_f, hx_f):
        def body(j, carry):
            first_in, last_in = carry
            off = pl.multiple_of(j * tk, tk)
            blk = s_ref[pl.ds(off, tk), :]
            first_in = jnp.minimum(first_in, -_fold_max(jnp.where(blk >= lo_f, -blk, NEG_INF)))
            last_in = jnp.maximum(last_in, _fold_max(jnp.where(blk < hx_f, blk, NEG_INF)))
            return first_in, last_in
        first_in, last_in = lax.fori_loop(0, n_tiles, body, (jnp.full((SUBLANES, LANES), -NEG_INF, F32),
                                                             jnp.full((SUBLANES, LANES), NEG_INF, F32)))
        return jnp.min(first_in, axis=0, keepdims=True), jnp.max(last_in, axis=0, keepdims=True)

    thr, c_ge, c_gt = _topk_threshold(count_ge, snap, smin, smax, n_adm, kk)

    @pl.when(jnp.max(c_ge - kk) > 0.5)
    def _():
        _demote_excess_ties(s_ref, n_tiles * (tk // LANES), thr, kk - c_gt)

    def mask_body(j, carry):
        off = pl.multiple_of(j * tk, tk)
        s_ref[pl.ds(off, tk), :] = jnp.where(s_ref[pl.ds(off, tk), :] >= thr, 0.0, NEG_INF)
        return carry

    lax.fori_loop(0, n_tiles, mask_body, 0)


def _dsa_prompt_kernel(q_ref, qi_ref, wi_ref, k_ref, vt_ref, ki_ref, o_ref, s_scr, acc_scr, l_scr, kn_scr, lg_scr,
                       *, topk):
    i = pl.program_id(1)
    tk, ts = KEY_TILE, KEY_SUB
    n_keys = (i + 1) * Q_BLOCK
    n_tiles = (n_keys + tk - 1) // tk
    qcol = lax.broadcasted_iota(jnp.int32, (1, Q_BLOCK), 1)
    lim = ((i * Q_BLOCK + qcol) // CHUNK + 1) * CHUNK

    @pl.when(i == 0)
    def _():
        ri = lax.broadcasted_iota(jnp.int32, (KV_WIDTH, LANES), 0)
        cj = lax.broadcasted_iota(jnp.int32, (KV_WIDTH, LANES), 1)
        head_sum = jnp.where(ri // HEAD_DIM == cj, 1.0, 0.0).astype(BF16)

        def body(j, acc):
            kt = k_ref[0, pl.ds(pl.multiple_of(j * tk, tk), tk), :].astype(F32)
            return jnp.maximum(acc, _fold_max(_dot((kt * kt).astype(BF16), head_sum)))

        acc = lax.fori_loop(0, k_ref.shape[1] // tk, body, jnp.zeros((SUBLANES, LANES), F32))
        kn_scr[...] = jnp.max(acc, axis=0, keepdims=True) * KNORM_SLACK

    qt = q_ref[0].astype(F32).T.astype(BF16)
    qit = qi_ref[0].astype(F32).T.astype(BF16)
    wit = wi_ref[0].T
    qi_rhs = jnp.concatenate([qit[h * IDX_DIM:(h + 1) * IDX_DIM, :] for h in range(IDX_HEADS)], axis=1)
    zero = jnp.zeros((HEAD_DIM, Q_BLOCK), BF16)
    q_cols = []
    for h in range(N_HEADS):
        qh = qt[h * HEAD_DIM:(h + 1) * HEAD_DIM, :]
        q_cols.append(jnp.concatenate([qh, zero] if h < GQA_GROUP else [zero, qh], axis=0))
    q_rhs = jnp.concatenate(q_cols, axis=1)

    def score_tile(j, carry):
        smax, smin = carry
        for c in range(tk // ts):
            off = pl.multiple_of(j * tk + c * ts, ts)
            r = _dot(ki_ref[0, pl.ds(off, ts), :], qi_rhs)
            acc = wit[0:1, :] * jnp.maximum(r[:, 0:LANES], 0.0)
            for h in range(1, IDX_HEADS):
                acc = acc + wit[h:h + 1, :] * jnp.maximum(r[:, h * LANES:(h + 1) * LANES], 0.0)
            key = off + lax.broadcasted_iota(jnp.int32, (ts, Q_BLOCK), 0)
            ok = key < lim
            s_scr[pl.ds(off, ts), :] = jnp.where(ok, acc, NEG_INF)
            smax = jnp.maximum(smax, _fold_max(jnp.where(ok, acc, NEG_INF)))
            smin = jnp.minimum(smin, -_fold_max(jnp.where(ok, -acc, NEG_INF)))
        return smax, smin

    smax, smin = lax.fori_loop(0, n_tiles, score_tile,
                               (jnp.full((SUBLANES, Q_BLOCK), NEG_INF, F32), jnp.full((SUBLANES, Q_BLOCK), -NEG_INF, F32)))
    smax = jnp.max(smax, axis=0, keepdims=True)
    smin = jnp.min(smin, axis=0, keepdims=True)
    _select_mask(s_scr, n_tiles, tk, smin, smax, lim.astype(F32), topk)

    qsq = jnp.square(qt.astype(F32))
    kn = kn_scr[...]
    lane1 = lax.broadcasted_iota(jnp.int32, (1, LANES), 1)
    shift_rows = []
    for h in range(N_HEADS):
        qn2 = jnp.sum(_fold_rows(qsq[h * HEAD_DIM:(h + 1) * HEAD_DIM, :]), axis=0, keepdims=True)
        kmax2 = jnp.max(jnp.where(lane1 == h // GQA_GROUP, kn, 0.0))
        shift_rows.append(jnp.sqrt(qn2 * kmax2))
    shift = jnp.concatenate(shift_rows, axis=0)

    def pv_accumulate(vt, ps, alphas):
        for pair in range(N_HEADS // 2):
            g = (2 * pair) // GQA_GROUP
            o = _dot(vt[g * HEAD_DIM:(g + 1) * HEAD_DIM, :], jnp.concatenate(ps[2 * pair:2 * pair + 2], axis=1))
            for u in range(2):
                h = 2 * pair + u
                rows = slice(h * HEAD_DIM, (h + 1) * HEAD_DIM)
                prev = acc_scr[rows, :] if alphas is None else alphas[h] * acc_scr[rows, :]
                acc_scr[rows, :] = prev + o[:, u * LANES:(u + 1) * LANES]

    def logits_into(buf, sub):
        off = pl.multiple_of(jnp.minimum(sub * ts, k_ref.shape[1] - ts), ts)
        lg_scr[buf] = _dot(k_ref[0, pl.ds(off, ts), :], q_rhs)

    def softmax_pv(buf, sub, l):
        bias = s_scr[pl.ds(pl.multiple_of(sub * ts, ts), ts), :]
        l_rows, ps = [], []
        for h in range(N_HEADS):
            p = jnp.exp2(lg_scr[buf, :, h * LANES:(h + 1) * LANES] + (bias - shift[h:h + 1, :]))
            l_rows.append(l[h * SUBLANES:(h + 1) * SUBLANES, :] + _fold_rows(p))
            ps.append(p.astype(BF16))
        pv_accumulate(vt_ref[sub], ps, None)
        return jnp.concatenate(l_rows, axis=0)

    def attn_bounded(j, l):
        logits_into(1, 2 * j + 1)
        l = softmax_pv(0, 2 * j, l)
        logits_into(0, 2 * j + 2)
        return softmax_pv(1, 2 * j + 1, l)

    def attn_online(j, carry):
        m, l = carry
        off = pl.multiple_of(j * ts, ts)
        lg = _dot(k_ref[0, pl.ds(off, ts), :], q_rhs)
        bias = s_scr[pl.ds(off, ts), :]
        m_rows, l_rows, alphas, ps = [], [], [], []
        for h in range(N_HEADS):
            x = lg[:, h * LANES:(h + 1) * LANES] + bias
            m_new = jnp.maximum(m[h:h + 1, :], jnp.max(_fold_max(x), axis=0, keepdims=True))
            m_use = jnp.where(m_new > NEG_INF, m_new, 0.0)
            alpha = jnp.exp2(m[h:h + 1, :] - m_use)
            p = jnp.exp2(x - m_use)
            l_rows.append(alpha * l[h:h + 1, :] + jnp.sum(_fold_rows(p), axis=0, keepdims=True))
            m_rows.append(m_new)
            alphas.append(alpha)
            ps.append(p.astype(BF16))
        pv_accumulate(vt_ref[j], ps, alphas)
        return jnp.concatenate(m_rows, axis=0), jnp.concatenate(l_rows, axis=0)

    n_sub = (n_keys + ts - 1) // ts
    acc_scr[...] = jnp.zeros_like(acc_scr)
    logits_into(0, 0)
    l_part = lax.fori_loop(0, n_tiles, attn_bounded, jnp.zeros((N_HEADS * SUBLANES, Q_BLOCK), F32))
    l_fast = jnp.concatenate([jnp.sum(l_part[h * SUBLANES:(h + 1) * SUBLANES, :], axis=0, keepdims=True)
                              for h in range(N_HEADS)], axis=0)
    l_scr[...] = l_fast

    @pl.when(jnp.min(l_fast) < SOFTMAX_SUM_FLOOR)
    def _():
        acc_scr[...] = jnp.zeros_like(acc_scr)
        _, l_exact = lax.fori_loop(0, n_sub, attn_online,
                                   (jnp.full((N_HEADS, Q_BLOCK), NEG_INF, F32), jnp.zeros((N_HEADS, Q_BLOCK), F32)))
        l_scr[...] = l_exact

    l = l_scr[...]
    out_t = jnp.concatenate([acc_scr[h * HEAD_DIM:(h + 1) * HEAD_DIM, :] / l[h:h + 1, :] for h in range(N_HEADS)],
                            axis=0)
    o_ref[0] = out_t.T.astype(BF16)


def _fold_max(x):
    n = x.shape[0] // SUBLANES
    ways = FOLD_WAYS if n % FOLD_WAYS == 0 else 1
    part = jnp.max(x.reshape(n // ways, ways, SUBLANES, x.shape[1]), axis=0)
    return jnp.max(part, axis=0)


def _dsa_prompt(q, qi, wi, kb, vt, kib):
    b, t, _ = q.shape
    assert t % KEY_TILE == 0
    topk = min(TOPK_MAX, t // 4)
    blk = lambda w: pl.BlockSpec((1, Q_BLOCK, w), lambda bi, i: (bi, i, 0))
    whole = lambda w: pl.BlockSpec((1, t, w), lambda bi, i: (bi, 0, 0))
    return pl.pallas_call(
        functools.partial(_dsa_prompt_kernel, topk=topk),
        grid=(b, t // Q_BLOCK),
        in_specs=[blk(ATT_WIDTH), blk(IDX_WIDTH), blk(LANES), whole(KV_WIDTH),
                  pl.BlockSpec((t // KEY_SUB, KV_WIDTH, KEY_SUB), lambda bi, i: (bi, 0, 0)), whole(IDX_DIM)],
        out_specs=blk(ATT_WIDTH),
        out_shape=jax.ShapeDtypeStruct((b, t, ATT_WIDTH), BF16),
        scratch_shapes=[pltpu.VMEM((t, Q_BLOCK), F32), pltpu.VMEM((ATT_WIDTH, Q_BLOCK), F32),
                        pltpu.VMEM((N_HEADS, Q_BLOCK), F32), pltpu.VMEM((1, LANES), F32),
                        pltpu.VMEM((2, KEY_SUB, N_HEADS * Q_BLOCK), F32)],
        compiler_params=pltpu.CompilerParams(dimension_semantics=("arbitrary", "arbitrary"),
                                             vmem_limit_bytes=VMEM_LIMIT),
        name="dsa_prompt",
    )(q, qi, wi, kb, vt, kib)


def _dsa_sample_kernel(q_ref, qi_ref, wi_ref, kn_ref, vn_ref, kin_ref, ck_ref, cv_ref, cki_ref, o_ref,
                       kcat, vcat_t, kicat, s_scr, *, topk):
    tq = q_ref.shape[1]
    past = ck_ref.shape[0]
    total = kcat.shape[0]
    pad = total - past - tq
    kcat[0:past, :] = ck_ref[...].astype(BF16)
    kcat[past:past + tq, :] = kn_ref[0]
    kcat[past + tq:total, :] = jnp.zeros((pad, KV_WIDTH), BF16)
    kicat[0:past, :] = cki_ref[...].astype(BF16)
    kicat[past:past + tq, :] = kin_ref[0]
    kicat[past + tq:total, :] = jnp.zeros((pad, IDX_DIM), BF16)
    vcat_t[:, 0:past] = cv_ref[...].T.astype(BF16)
    v_new = jnp.concatenate([vn_ref[0].astype(F32), jnp.zeros((pad, KV_WIDTH), F32)], axis=0)
    vcat_t[:, past:total] = v_new.T.astype(BF16)

    lane = lax.broadcasted_iota(jnp.int32, (1, LANES), 1)
    qi = qi_ref[0]
    qi_stack = jnp.concatenate([qi[:, h * IDX_DIM:(h + 1) * IDX_DIM] for h in range(IDX_HEADS)], axis=0)
    wi_rows = jnp.concatenate([wi_ref[0], jnp.zeros((LANES - tq, LANES), F32)], axis=0).T
    w_flat = jnp.zeros((1, LANES), F32)
    for h in range(IDX_HEADS):
        row = wi_rows[h:h + 1, :]
        shifted = row if h == 0 else pltpu.roll(row, h * tq, 1)
        w_flat = jnp.where(lane // tq == h, shifted, w_flat)

    sc = w_flat * jnp.maximum(_dot_t(kicat[...], qi_stack), 0.0)
    for shift in (LANES // 2, LANES // 4, LANES // 8):
        sc = sc + pltpu.roll(sc, shift, 1)
    sc = jnp.where(lane < tq, sc, 0.0)
    for shift in (LANES // 8, LANES // 4, LANES // 2):
        sc = sc + pltpu.roll(sc, shift, 1)
    key = lax.broadcasted_iota(jnp.int32, (total, LANES), 0)
    ok = key < past + tq
    s_scr[...] = jnp.where(ok, sc, NEG_INF)
    smax = jnp.max(jnp.where(ok, sc, NEG_INF), axis=0, keepdims=True)
    smin = jnp.min(jnp.where(ok, sc, -NEG_INF), axis=0, keepdims=True)
    _select_mask(s_scr, 1, total, smin, smax, jnp.full((1, LANES), float(past + tq), F32), topk)

    q = q_ref[0]
    zero = jnp.zeros((tq, HEAD_DIM), BF16)
    q_stack = jnp.concatenate(
        [jnp.concatenate([q[:, h * HEAD_DIM:(h + 1) * HEAD_DIM], zero] if h < GQA_GROUP
                         else [zero, q[:, h * HEAD_DIM:(h + 1) * HEAD_DIM]], axis=1) for h in range(N_HEADS)], axis=0)
    lg = _dot_t(kcat[...], q_stack) + s_scr[...]
    m = jnp.max(lg, axis=0, keepdims=True)
    p = jnp.exp2(lg - m)
    l = jnp.sum(p, axis=0, keepdims=True)
    o = _dot(vcat_t[...], p.astype(BF16))
    o = jnp.where(lane < GQA_GROUP * tq, o[0:HEAD_DIM, :], o[HEAD_DIM:2 * HEAD_DIM, :]) / l
    o_t = jnp.concatenate([o, jnp.zeros((LANES - HEAD_DIM, LANES), F32)], axis=0).T
    o_ref[0] = jnp.concatenate([o_t[h * tq:(h + 1) * tq, 0:HEAD_DIM] for h in range(N_HEADS)], axis=1).astype(BF16)


def _dsa_sample(q, qi, wi, kb, vb, kib, cache_k, cache_v, cache_kidx, layer):
    b, tq, _ = q.shape
    past = cache_k.shape[2]
    assert tq * N_HEADS == LANES and tq * IDX_HEADS == LANES and past % LANES == 0
    total = past + LANES
    topk = min(TOPK_MAX, (past + tq) // 4)
    blk = lambda w: pl.BlockSpec((1, tq, w), lambda bi: (bi, 0, 0))
    cache = lambda w: pl.BlockSpec((None, None, past, w), lambda bi: (layer, bi, 0, 0))
    return pl.pallas_call(
        functools.partial(_dsa_sample_kernel, topk=topk),
        grid=(b,),
        in_specs=[blk(ATT_WIDTH), blk(IDX_WIDTH), blk(LANES), blk(KV_WIDTH), blk(KV_WIDTH), blk(IDX_DIM),
                  cache(KV_WIDTH), cache(KV_WIDTH), cache(IDX_DIM)],
        out_specs=blk(ATT_WIDTH),
        out_shape=jax.ShapeDtypeStruct((b, tq, ATT_WIDTH), BF16),
        scratch_shapes=[pltpu.VMEM((total, KV_WIDTH), BF16), pltpu.VMEM((KV_WIDTH, total), BF16),
                        pltpu.VMEM((total, IDX_DIM), BF16), pltpu.VMEM((total, LANES), F32)],
        compiler_params=pltpu.CompilerParams(dimension_semantics=("arbitrary",), vmem_limit_bytes=VMEM_LIMIT),
        name="dsa_sample",
    )(q, qi, wi, kb, vb, kib, cache_k, cache_v, cache_kidx)


def _first_lane_where(mask, lane):
    return jnp.min(jnp.where(mask, lane, LANES), axis=1, keepdims=True)


def _post_kernel(att_ref, sa_ref, mb_ref, x_ref, wpa_ref, wout_ref, g2_ref, wr_ref, br_ref,
                 h_ref, xn2_ref, gate_ref, *sort_refs):
    pa = _dot(att_ref[...], wpa_ref[...])
    merged = sa_ref[...].astype(F32) * pa + mb_ref[...].astype(F32)
    h = x_ref[...] + _dot(merged.astype(BF16), wout_ref[...])
    h_ref[...] = h
    xn2 = _rms(h, g2_ref[...])
    xh = xn2.astype(BF16)
    xn2_ref[...] = xh

    xl = (xn2 - xh.astype(F32)).astype(BF16)
    wr = wr_ref[...]
    wh = wr.astype(BF16)
    wl = (wr - wh.astype(F32)).astype(BF16)
    lg = _dot(xh, wh) + (_dot(xl, wh) + _dot(xh, wl)) + br_ref[...]

    lane = lax.broadcasted_iota(jnp.int32, lg.shape, 1)
    is_g = lane < N_GROUPS
    mg = jnp.max(jnp.where(is_g, lg, NEG_INF), axis=1, keepdims=True)
    eg = jnp.where(is_g, jnp.exp(lg - mg), 0.0)
    pg = eg / jnp.sum(eg, axis=1, keepdims=True)
    g_w = jnp.max(pg, axis=1, keepdims=True)
    g_idx = _first_lane_where(jnp.logical_and(is_g, pg == g_w), lane)

    e_lo = ROUTER_E0 + EXP_PER_GROUP * g_idx
    is_e = jnp.logical_and(lane >= e_lo, lane < e_lo + EXP_PER_GROUP)
    me = jnp.max(jnp.where(is_e, lg, NEG_INF), axis=1, keepdims=True)
    ee = jnp.where(is_e, jnp.exp(lg - me), 0.0)
    pe = ee / jnp.sum(ee, axis=1, keepdims=True)
    e1 = jnp.max(pe, axis=1, keepdims=True)
    i1 = _first_lane_where(jnp.logical_and(is_e, pe == e1), lane)
    rest = jnp.logical_and(is_e, lane != i1)
    e2 = jnp.max(jnp.where(rest, pe, -1.0), axis=1, keepdims=True)
    i2 = _first_lane_where(jnp.logical_and(rest, pe == e2), lane)
    den = e1 + e2
    gate = jnp.where(lane == i1, e1 / den, jnp.where(lane == i2, e2 / den, 0.0)) * g_w
    group_gates = [jnp.where(lane < EXP_PER_GROUP, pltpu.roll(gate, LANES - (ROUTER_E0 + EXP_PER_GROUP * g), 1), 0.0)
                   for g in range(N_GROUPS)]
    if not sort_refs:
        for g in range(N_GROUPS):
            gate_ref[:, g * LANES:(g + 1) * LANES] = group_gates[g]
        return

    pt_ref, meta_ref = sort_refs
    gate_ref[...] = (group_gates[0] + group_gates[1]) + (group_gates[2] + group_gates[3])
    tm = lg.shape[0]
    onehot = jnp.where(lane == g_idx, 1.0, 0.0)
    ri = lax.broadcasted_iota(jnp.int32, (tm, tm), 0)
    cj = lax.broadcasted_iota(jnp.int32, (tm, tm), 1)
    earlier = jnp.where(cj < ri, 1.0, 0.0).astype(BF16)
    rank = jnp.sum(_dot(earlier, onehot.astype(BF16)) * onehot, axis=1, keepdims=True)
    cnt = jnp.sum(onehot, axis=0, keepdims=True)
    cnt_pad = jnp.floor((cnt + (SORT_PAD - 1.0)) * (1.0 / SORT_PAD)) * SORT_PAD
    seg_off = pltpu.roll(cnt_pad, 1, 1) + pltpu.roll(cnt_pad, 2, 1) + pltpu.roll(cnt_pad, 3, 1)
    dest = (rank + jnp.sum(onehot * seg_off, axis=1, keepdims=True)).astype(jnp.int32)
    col = lax.broadcasted_iota(jnp.int32, pt_ref.shape, 1)
    pt_ref[...] = jnp.where(col == dest, 1.0, 0.0).astype(BF16)
    meta_ref[0] = jnp.concatenate([cnt_pad, seg_off, jnp.zeros((SUBLANES - 2, LANES), F32)], axis=0)


def _sort_rows(tm):
    return -(-(tm + N_GROUPS * SORT_PAD) // LANES) * LANES


def _post(att, sa, mb, x, w, *, tm, sort):
    n, d = x.shape
    row = lambda t: (t, 0)
    const = lambda t: (0, 0)
    gate_w = LANES if sort else N_GROUPS * LANES
    out_specs = [pl.BlockSpec((tm, d), row), pl.BlockSpec((tm, d), row), pl.BlockSpec((tm, gate_w), row)]
    out_shape = [jax.ShapeDtypeStruct((n, d), F32), jax.ShapeDtypeStruct((n, d), BF16),
                 jax.ShapeDtypeStruct((n, gate_w), F32)]
    if sort:
        out_specs += [pl.BlockSpec((tm, _sort_rows(tm)), row), pl.BlockSpec((1, SUBLANES, LANES), lambda t: (t, 0, 0))]
        out_shape += [jax.ShapeDtypeStruct((n, _sort_rows(tm)), BF16),
                      jax.ShapeDtypeStruct((n // tm, SUBLANES, LANES), F32)]
    return pl.pallas_call(
        _post_kernel,
        grid=(n // tm,),
        in_specs=[pl.BlockSpec((tm, ATT_WIDTH), row), pl.BlockSpec((tm, d), row), pl.BlockSpec((tm, d), row),
                  pl.BlockSpec((tm, d), row), pl.BlockSpec(w["wpa"].shape, const), pl.BlockSpec(w["wout"].shape, const),
                  pl.BlockSpec((1, d), const), pl.BlockSpec(w["wr"].shape, const), pl.BlockSpec((1, ROUTER_LANES), const)],
        out_specs=out_specs,
        out_shape=out_shape,
        compiler_params=pltpu.CompilerParams(dimension_semantics=("arbitrary",), vmem_limit_bytes=VMEM_LIMIT),
        name="post_attn",
    )(att, sa, mb, x, w["wpa"], w["wout"], w["g2"], w["wr"], w["br"])


def _moe_kernel(xn2_ref, gate_ref, h_ref, wg_ref, wu_ref, wd_ref, fn_ref, o_ref, acc_ref, *, final):
    g = pl.program_id(1)

    @pl.when(g == 0)
    def _():
        acc_ref[...] = jnp.zeros_like(acc_ref)

    x = xn2_ref[...]
    gate = gate_ref[...]
    act = jax.nn.silu(_dot(x, wg_ref[0])) * _dot(x, wu_ref[0])
    parts = []
    for e in range(EXP_PER_GROUP):
        parts.append((act[:, e * EXP_HIDDEN:(e + 1) * EXP_HIDDEN] * gate[:, e:e + 1]).astype(BF16))
    acc_ref[...] += _dot(jnp.concatenate(parts, axis=1), wd_ref[0])

    @pl.when(g == N_GROUPS - 1)
    def _():
        y = h_ref[...] + acc_ref[...]
        o_ref[...] = _rms(y, fn_ref[...]) if final else y


def _moe(xn2, gate, h, w, fn, *, tm, final):
    n, d = h.shape
    row = lambda t, g: (t, 0)
    wspec = lambda shp: pl.BlockSpec((1,) + shp[1:], lambda t, g: (g, 0, 0))
    return pl.pallas_call(
        functools.partial(_moe_kernel, final=final),
        grid=(n // tm, N_GROUPS),
        in_specs=[pl.BlockSpec((tm, d), row), pl.BlockSpec((tm, LANES), lambda t, g: (t, g)), pl.BlockSpec((tm, d), row),
                  wspec(w["weg"].shape), wspec(w["weu"].shape), wspec(w["wed"].shape),
                  pl.BlockSpec((1, d), lambda t, g: (0, 0))],
        out_specs=pl.BlockSpec((tm, d), row),
        out_shape=jax.ShapeDtypeStruct((n, d), F32),
        scratch_shapes=[pltpu.VMEM((tm, d), F32)],
        compiler_params=pltpu.CompilerParams(dimension_semantics=("arbitrary", "arbitrary"),
                                             vmem_limit_bytes=VMEM_LIMIT),
        name="moe",
    )(xn2, gate, h, w["weg"], w["weu"], w["wed"], fn)


def _moe_sorted_kernel(meta_ref, xn2_ref, gate_ref, h_ref, pt_ref, wg_ref, wu_ref, wd_ref, fn_ref, o_ref,
                       xs_scr, gs_scr, ys_scr, *, final):
    t = pl.program_id(0)
    rows = pt_ref.shape[1]
    pt = pt_ref[...]
    perm = pt.astype(F32).T.astype(BF16)
    xs_scr[0:rows, :] = _dot(perm, xn2_ref[...]).astype(BF16)
    xs_scr[rows:rows + MOE_CHUNK, :] = jnp.zeros((MOE_CHUNK, xs_scr.shape[1]), BF16)
    gate = gate_ref[...]
    g_hi = gate.astype(BF16)
    g_rest = gate - g_hi.astype(F32)
    g_mid = g_rest.astype(BF16)
    g_lo = (g_rest - g_mid.astype(F32)).astype(BF16)
    gs_scr[0:rows, :] = _dot(perm, g_hi) + (_dot(perm, g_mid) + _dot(perm, g_lo))
    gs_scr[rows:rows + MOE_CHUNK, :] = jnp.zeros((MOE_CHUNK, LANES), F32)
    ys_scr[...] = jnp.zeros_like(ys_scr)

    for g in range(N_GROUPS):
        cnt = meta_ref[t * 2 * N_GROUPS + g]
        seg = meta_ref[t * 2 * N_GROUPS + N_GROUPS + g]

        def chunk(c, carry, g=g, cnt=cnt, seg=seg):
            r0 = pl.multiple_of(seg + c * MOE_CHUNK, SORT_PAD)
            x = xs_scr[pl.ds(r0, MOE_CHUNK), :]
            gt = gs_scr[pl.ds(r0, MOE_CHUNK), :]
            act = jax.nn.silu(_dot(x, wg_ref[g])) * _dot(x, wu_ref[g])
            parts = [(act[:, e * EXP_HIDDEN:(e + 1) * EXP_HIDDEN] * gt[:, e:e + 1]).astype(BF16)
                     for e in range(EXP_PER_GROUP)]
            y = _dot(jnp.concatenate(parts, axis=1), wd_ref[g])
            in_seg = lax.broadcasted_iota(jnp.int32, (MOE_CHUNK, 1), 0) + c * MOE_CHUNK < cnt
            ys_scr[pl.ds(r0, MOE_CHUNK), :] = jnp.where(in_seg, y, ys_scr[pl.ds(r0, MOE_CHUNK), :])
            return carry

        lax.fori_loop(0, (cnt + MOE_CHUNK - 1) // MOE_CHUNK, chunk, 0)

    y = h_ref[...] + _dot(pt, ys_scr[0:rows, :].astype(BF16))
    o_ref[...] = _rms(y, fn_ref[...]) if final else y


def _moe_sorted(xn2, gate, h, pt, meta, w, fn, *, tm, final):
    n, d = h.shape
    rows = pt.shape[1]
    row = lambda t, m: (t, 0)
    resident = lambda shp: pl.BlockSpec(shp, lambda t, m: (0, 0, 0), pipeline_mode=pl.Buffered(1))
    grid_spec = pltpu.PrefetchScalarGridSpec(
        num_scalar_prefetch=1,
        grid=(n // tm,),
        in_specs=[pl.BlockSpec((tm, d), row), pl.BlockSpec((tm, LANES), row), pl.BlockSpec((tm, d), row),
                  pl.BlockSpec((tm, rows), row),
                  resident(w["weg"].shape), resident(w["weu"].shape), resident(w["wed"].shape),
                  pl.BlockSpec((1, d), lambda t, m: (0, 0))],
        out_specs=pl.BlockSpec((tm, d), row),
        scratch_shapes=[pltpu.VMEM((rows + MOE_CHUNK, d), BF16), pltpu.VMEM((rows + MOE_CHUNK, LANES), F32),
                        pltpu.VMEM((rows + MOE_CHUNK, d), F32)],
    )
    return pl.pallas_call(
        functools.partial(_moe_sorted_kernel, final=final),
        grid_spec=grid_spec,
        out_shape=jax.ShapeDtypeStruct((n, d), F32),
        compiler_params=pltpu.CompilerParams(dimension_semantics=("arbitrary",), vmem_limit_bytes=VMEM_LIMIT),
        name="moe_sorted",
    )(meta, xn2, gate, h, pt, w["weg"], w["weu"], w["wed"], fn)


def _rope_tables(pos, d):
    inv = ROPE_THETA ** (-jnp.arange(0, d, 2, dtype=F32) / d)
    ang = pos[:, None] * inv[None, :]
    cos, sin = jnp.cos(ang), jnp.sin(ang)
    reps = LANES // d
    return (jnp.tile(jnp.concatenate([cos, cos], axis=-1), (1, reps)),
            jnp.tile(jnp.concatenate([-sin, sin], axis=-1), (1, reps)))


def _layer_weights(l, norm1, w_in, sgu_ln_g, sgu_ln_b, sgu_w, sgu_b, w_pa, w_pb, w_out, norm2,
                   w_rg, b_rg, w_re, b_re, w_eg, w_eu, w_ed, sgu_chunk):
    d = w_in.shape[1]
    a_end = ATT_WIDTH + 2 * KV_WIDTH + IDX_WIDTH
    b_end = a_end + IDX_DIM + IDX_HEADS
    reps = SGU_CHUNK // sgu_chunk
    n_router = N_GROUPS + N_GROUPS * EXP_PER_GROUP
    return dict(
        g1=norm1[l][None, :],
        wa=w_in[l][:, :a_end].astype(BF16),
        wb=jnp.pad(w_in[l][:, a_end:b_end], ((0, 0), (0, LANES - (b_end - a_end)))).astype(BF16),
        wc=w_in[l][:, b_end:].astype(BF16),
        sguw=jnp.tile(sgu_w[l][:, :sgu_chunk, :sgu_chunk], (1, reps, reps)),
        sgub=jnp.repeat(jnp.tile(sgu_b[l][:, :sgu_chunk], (1, reps)).T, SGU_GDIM, axis=1),
        lng=sgu_ln_g[l][None, :], lnb=sgu_ln_b[l][None, :],
        wpb=w_pb[l].astype(BF16), wpa=w_pa[l].astype(BF16), wout=w_out[l].astype(BF16),
        g2=norm2[l][None, :],
        wr=jnp.pad(jnp.concatenate([w_rg[l], w_re[l]], axis=1), ((0, 0), (0, ROUTER_LANES - n_router))),
        br=jnp.pad(jnp.concatenate([b_rg[l], b_re[l]]), (0, ROUTER_LANES - n_router))[None, :],
        weg=jnp.transpose(w_eg[l], (0, 2, 1, 3)).reshape(N_GROUPS, d, GROUP_HIDDEN).astype(BF16),
        weu=jnp.transpose(w_eu[l], (0, 2, 1, 3)).reshape(N_GROUPS, d, GROUP_HIDDEN).astype(BF16),
        wed=w_ed[l].reshape(N_GROUPS, GROUP_HIDDEN, d).astype(BF16),
    )


def _token_tile(n):
    return 512 if n % 512 == 0 else n


def kernel(x_prompt, x_sample, cache_k, cache_v, cache_kidx, norm1, w_in, sgu_ln_g, sgu_ln_b, sgu_w, sgu_b,
           w_pa, w_pb, w_out, norm2, w_rg, b_rg, w_re, b_re, w_eg, w_eu, w_ed, final_norm):
    bp, tp, d = x_prompt.shape
    bs, ts, _ = x_sample.shape
    depth, _, past = cache_k.shape[:3]
    np_, ns = bp * tp, bs * ts
    tm_p, tm_s = _token_tile(tp), _token_tile(ns)
    assert tp % tm_p == 0 and tp % Q_BLOCK == 0 and SGU_CHUNK % ts == 0 and tm_s % SGU_CHUNK == 0
    assert tm_p % KEY_SUB == 0 and tm_s % KEY_SUB == 0

    pos_p = jnp.arange(tp, dtype=F32)
    pos_s = jnp.tile(past + jnp.arange(ts, dtype=F32), tm_s // ts)
    tabs_p = _rope_tables(pos_p, HEAD_DIM) + _rope_tables(pos_p, IDX_DIM)
    tabs_s = _rope_tables(pos_s, HEAD_DIM) + _rope_tables(pos_s, IDX_DIM)
    ck = cache_k.reshape(depth, bs, past, KV_WIDTH)
    cv = cache_v.reshape(depth, bs, past, KV_WIDTH)
    fn = final_norm[None, :]

    hp = x_prompt.reshape(np_, d)
    hs = x_sample.reshape(ns, d)
    outs = [[] for _ in range(7)]
    for l in range(depth):
        wargs = (norm1, w_in, sgu_ln_g, sgu_ln_b, sgu_w, sgu_b, w_pa, w_pb, w_out, norm2,
                 w_rg, b_rg, w_re, b_re, w_eg, w_eu, w_ed)
        wp = _layer_weights(l, *wargs, sgu_chunk=SGU_CHUNK)
        ws = dict(wp)
        ws.update({k: v for k, v in _layer_weights(l, *wargs, sgu_chunk=ts).items() if k in ("sguw", "sgub")})
        final = l == depth - 1

        q, kf, kb, vf, _, vt, qi, kif, kib, wi, mb, sa = _inproj(hp, wp, tabs_p, tm=tm_p, sgu_chunk=SGU_CHUNK,
                                                                 want_vln=False)
        r3 = lambda a: a.reshape(bp, tp, a.shape[-1])
        att = _dsa_prompt(r3(q), r3(qi), r3(wi), r3(kb), vt, r3(kib)).reshape(np_, ATT_WIDTH)
        h, xn2, gate, pt, meta = _post(att, sa, mb, hp, wp, tm=tm_p, sort=True)
        meta = meta[:, 0:2, 0:N_GROUPS].astype(jnp.int32).reshape(-1)
        hp = _moe_sorted(xn2, gate, h, pt, meta, wp, fn, tm=tm_p, final=final)
        outs[0].append(kf.reshape(bp, tp, N_KV_HEADS, HEAD_DIM))
        outs[1].append(vf.reshape(bp, tp, N_KV_HEADS, HEAD_DIM))
        outs[2].append(kif.reshape(bp, tp, IDX_DIM))

        q, kf, kb, vf, vb, _, qi, kif, kib, wi, mb, sa, vln = _inproj(hs, ws, tabs_s, tm=tm_s, sgu_chunk=ts,
                                                                      want_vln=True)
        r3 = lambda a: a.reshape(bs, ts, a.shape[-1])
        att = _dsa_sample(r3(q), r3(qi), r3(wi), r3(kb), r3(vb), r3(kib), ck, cv, cache_kidx, l)
        h, xn2, gate = _post(att.reshape(ns, ATT_WIDTH), sa, mb, hs, ws, tm=tm_s, sort=False)
        hs = _moe(xn2, gate, h, ws, fn, tm=tm_s, final=final)
        outs[3].append(kf.reshape(bs, ts, N_KV_HEADS, HEAD_DIM))
        outs[4].append(vf.reshape(bs, ts, N_KV_HEADS, HEAD_DIM))
        outs[5].append(kif.reshape(bs, ts, IDX_DIM))
        outs[6].append(vln.reshape(bs, ts, SGU_WIDTH))

    return (hp.reshape(bp, tp, d), hs.reshape(bs, ts, d)) + tuple(jnp.stack(o) for o in outs)
```

```python
import functools

import jax
import jax.numpy as jnp
from jax import lax
from jax.experimental import pallas as pl
from jax.experimental.pallas import tpu as pltpu

F32 = jnp.float32
BF16 = jnp.bfloat16

CHUNK = 64
N_HEADS = 8
N_KV_HEADS = 2
HEAD_DIM = 64
GQA_GROUP = N_HEADS // N_KV_HEADS
ATT_WIDTH = N_HEADS * HEAD_DIM
KV_WIDTH = N_KV_HEADS * HEAD_DIM
IDX_HEADS = 8
IDX_DIM = 32
IDX_WIDTH = IDX_HEADS * IDX_DIM
TOPK_MAX = 256
Q_BLOCK = 128
SGU_GROUPS = 4
SGU_CHUNK = 128
SGU_WIDTH = 512
SGU_GDIM = SGU_WIDTH // SGU_GROUPS
N_GROUPS = 4
EXP_PER_GROUP = 8
EXP_HIDDEN = 128
GROUP_HIDDEN = EXP_PER_GROUP * EXP_HIDDEN
ROPE_THETA = 10000.0
EPS = 1e-6
Q_SCALE = HEAD_DIM ** -0.5 * 1.4426950408889634

LANES = 128
SUBLANES = 8
MXU_DIM = 256
ROUTER_LANES = LANES
ROUTER_E0 = N_GROUPS
VMEM_LIMIT = 56 * 1024 * 1024
KEY_TILE = 2 * MXU_DIM
KEY_SUB = MXU_DIM
FOLD_WAYS = 4
SEARCH_FIRST_STEPS = 11
SEARCH_SECOND_STEPS = 3
SEARCH_ROUND_STEPS = 2
SEARCH_MAX_ROUNDS = 34
SORT_PAD = 16
MOE_CHUNK = 144
KNORM_SLACK = 1.0 + 2.0 ** -6
SOFTMAX_SUM_FLOOR = 2.0 ** -80
NEG_INF = float("-inf")


def _dot(a, b):
    return jnp.dot(a, b, preferred_element_type=F32)


def _dot_t(a, b):
    return lax.dot_general(a, b, (((1,), (1,)), ((), ())), preferred_element_type=F32)


def _rms(x, g):
    return x * lax.rsqrt(jnp.mean(x * x, axis=-1, keepdims=True) + EPS) * g


def _swap_halves(x, d):
    lane = lax.broadcasted_iota(jnp.int32, x.shape, 1)
    first = (lane % d) < (d // 2)
    return jnp.where(first, pltpu.roll(x, LANES - d // 2, 1), pltpu.roll(x, d // 2, 1))


def _rope(x, cos, sin_signed, d):
    return x * cos + _swap_halves(x, d) * sin_signed


def _inproj_kernel(*refs, sgu_chunk, kv_transposed, n_alias, want_vln):
    (x_ref, g1_ref, wa_ref, wb_ref, wc_ref, cq_ref, sq_ref, ci_ref, si_ref,
     sguw_ref, sgub_ref, lng_ref, lnb_ref, wpb_ref) = refs[:14]
    (q_ref, kf_ref, kb_ref, vf_ref, vb_ref, vt_ref, qi_ref, kif_ref, kib_ref, wi_ref,
     mb_ref, sa_ref) = refs[14 + n_alias:26 + n_alias]
    tm = x_ref.shape[0]
    xn = _rms(x_ref[...], g1_ref[...]).astype(BF16)
    cq, sq, ci, si = cq_ref[...], sq_ref[...], ci_ref[...], si_ref[...]

    a = _dot(xn, wa_ref[...])
    for c in range(ATT_WIDTH // LANES):
        r = _rope(a[:, c * LANES:(c + 1) * LANES], cq, sq, HEAD_DIM)
        q_ref[:, c * LANES:(c + 1) * LANES] = (r * Q_SCALE).astype(BF16)
    k = _rope(a[:, ATT_WIDTH:ATT_WIDTH + KV_WIDTH], cq, sq, HEAD_DIM)
    kb_ref[...] = k.astype(BF16)
    v = a[:, ATT_WIDTH + KV_WIDTH:ATT_WIDTH + 2 * KV_WIDTH]
    vb_ref[...] = v.astype(BF16)
    if kv_transposed:
        kf_ref[...] = k.T
        vf_ref[...] = v.T
    else:
        kf_ref[...] = k
        vf_ref[...] = v
    for c in range(tm // KEY_SUB):
        vt_ref[c] = v[c * KEY_SUB:(c + 1) * KEY_SUB, :].T.astype(BF16)
    qi0 = ATT_WIDTH + 2 * KV_WIDTH
    for c in range(IDX_WIDTH // LANES):
        r = _rope(a[:, qi0 + c * LANES:qi0 + (c + 1) * LANES], ci, si, IDX_DIM)
        qi_ref[:, c * LANES:(c + 1) * LANES] = r.astype(BF16)

    b = _dot(xn, wb_ref[...])
    ki_wide = _rope(b, ci, si, IDX_DIM)
    ki = ki_wide[:, :IDX_DIM]
    kif_ref[...] = ki_wide.T[0:IDX_DIM, :] if kv_transposed else ki
    kib_ref[...] = ki.astype(BF16)
    wi_ref[...] = pltpu.roll(b, LANES - IDX_DIM, 1) * (IDX_HEADS ** -0.5)

    u = jax.nn.gelu(_dot(xn, wc_ref[:, 0:SGU_WIDTH]))
    vs = jax.nn.gelu(_dot(xn, wc_ref[:, SGU_WIDTH:2 * SGU_WIDTH]))
    mu = jnp.mean(vs, axis=-1, keepdims=True)
    var = jnp.mean(jnp.square(vs - mu), axis=-1, keepdims=True)
    vln = (vs - mu) * lax.rsqrt(var + EPS) * lng_ref[...] + lnb_ref[...]
    if want_vln:
        refs[26 + n_alias][...] = vln
    ri = lax.broadcasted_iota(jnp.int32, (SGU_CHUNK, SGU_CHUNK), 0)
    cj = lax.broadcasted_iota(jnp.int32, (SGU_CHUNK, SGU_CHUNK), 1)
    mix_mask = (cj <= ri) & ((ri // sgu_chunk) == (cj // sgu_chunk))
    vln_b = vln.astype(BF16)
    w_eff = [jnp.where(mix_mask, sguw_ref[g], 0.0).astype(BF16) for g in range(SGU_GROUPS)]
    sg_rows = []
    for r in range(tm // SGU_CHUNK):
        rows = slice(r * SGU_CHUNK, (r + 1) * SGU_CHUNK)
        mixed = [_dot(w_eff[g], vln_b[rows, g * SGU_GDIM:(g + 1) * SGU_GDIM]) for g in range(SGU_GROUPS)]
        sg_rows.append(u[rows, :] * (jnp.concatenate(mixed, axis=1) + sgub_ref[...]))
    sg = jnp.concatenate(sg_rows, axis=0).astype(BF16)
    d = x_ref.shape[1]
    sa_ref[...] = jax.nn.sigmoid(_dot(xn, wc_ref[:, 2 * SGU_WIDTH:2 * SGU_WIDTH + d])).astype(BF16)
    gb = jax.nn.sigmoid(_dot(xn, wc_ref[:, 2 * SGU_WIDTH + d:]))
    mb_ref[...] = (gb * _dot(sg, wpb_ref[...])).astype(BF16)


def _inproj(x, w, tabs, *, tm, sgu_chunk, want_vln, kv_layer=None, kv_bufs=None, kv_shape=None):
    n, d = x.shape
    n_tab_tiles = tabs[0].shape[0] // tm
    row = lambda t: (t, 0)
    const = lambda t: (0, 0)
    tab = lambda t: (t % n_tab_tiles, 0)
    kv_transposed = kv_layer is not None
    if kv_transposed:
        depth, streams, steps = kv_shape
        per_stream = steps // tm
        kv_sds = lambda wd: jax.ShapeDtypeStruct((depth, streams, wd, steps), F32)
        kv_spec = lambda wd: pl.BlockSpec((None, None, wd, tm),
                                          lambda t: (kv_layer, t // per_stream, 0, t % per_stream))
    else:
        kv_sds = lambda wd: jax.ShapeDtypeStruct((n, wd), F32)
        kv_spec = lambda wd: pl.BlockSpec((tm, wd), row)
    tiled = lambda wd, dt: (jax.ShapeDtypeStruct((n, wd), dt), pl.BlockSpec((tm, wd), row))
    outs = [
        tiled(ATT_WIDTH, BF16),
        (kv_sds(KV_WIDTH), kv_spec(KV_WIDTH)),
        tiled(KV_WIDTH, BF16),
        (kv_sds(KV_WIDTH), kv_spec(KV_WIDTH)),
        tiled(KV_WIDTH, BF16),
        (jax.ShapeDtypeStruct((n // KEY_SUB, KV_WIDTH, KEY_SUB), BF16),
         pl.BlockSpec((tm // KEY_SUB, KV_WIDTH, KEY_SUB), lambda t: (t, 0, 0))),
        tiled(IDX_WIDTH, BF16),
        (kv_sds(IDX_DIM), kv_spec(IDX_DIM)),
        tiled(IDX_DIM, BF16),
        tiled(LANES, F32),
        tiled(d, BF16),
        tiled(d, BF16),
    ]
    if want_vln:
        outs.append(tiled(SGU_WIDTH, F32))
    in_specs = [
        pl.BlockSpec((tm, d), row),
        pl.BlockSpec((1, d), const),
        pl.BlockSpec(w["wa"].shape, const),
        pl.BlockSpec(w["wb"].shape, const),
        pl.BlockSpec(w["wc"].shape, const),
        pl.BlockSpec((tm, LANES), tab), pl.BlockSpec((tm, LANES), tab),
        pl.BlockSpec((tm, LANES), tab), pl.BlockSpec((tm, LANES), tab),
        pl.BlockSpec(w["sguw"].shape, lambda t: (0, 0, 0)),
        pl.BlockSpec(w["sgub"].shape, const),
        pl.BlockSpec((1, SGU_WIDTH), const),
        pl.BlockSpec((1, SGU_WIDTH), const),
        pl.BlockSpec(w["wpb"].shape, const),
    ]
    args = [x, w["g1"], w["wa"], w["wb"], w["wc"], *tabs, w["sguw"], w["sgub"], w["lng"], w["lnb"], w["wpb"]]
    aliases = {}
    if kv_bufs is not None:
        for buf, out_idx in zip(kv_bufs, (1, 3, 7)):
            aliases[len(args)] = out_idx
            in_specs.append(pl.BlockSpec(memory_space=pl.ANY))
            args.append(buf)
    return pl.pallas_call(
        functools.partial(_inproj_kernel, sgu_chunk=sgu_chunk, kv_transposed=kv_transposed,
                          n_alias=len(aliases), want_vln=want_vln),
        grid=(n // tm,),
        in_specs=in_specs,
        out_specs=[o[1] for o in outs],
        out_shape=[o[0] for o in outs],
        input_output_aliases=aliases,
        compiler_params=pltpu.CompilerParams(dimension_semantics=("arbitrary",), vmem_limit_bytes=VMEM_LIMIT),
        name="inproj",
    )(*args)


def _float_key(x):
    b = lax.bitcast_convert_type(x, jnp.int32)
    return b ^ ((b >> 31) & jnp.int32(0x7FFFFFFF))


def _key_float(k):
    return lax.bitcast_convert_type(k ^ ((k >> 31) & jnp.int32(0x7FFFFFFF)), F32)


def _fold_rows(x):
    n = x.shape[0] // SUBLANES
    ways = FOLD_WAYS if n % FOLD_WAYS == 0 else 1
    part = jnp.sum(x.reshape(n // ways, ways, SUBLANES, x.shape[1]), axis=0)
    return jnp.sum(part, axis=0)


def _topk_threshold(count_ge, snap, smin, smax, n_adm, kk):
    lo0 = _float_key(smin + 0.0)
    hx0 = _float_key(smax + 0.0) + 1

    def is_active(st):
        lo, hx, c_lo, _ = st
        return jnp.logical_and(c_lo != kk, lo + 1 < hx)

    def step(st, by_key):
        lo, hx, c_lo, c_hx = st
        active = is_active(st)
        if by_key:
            mid = (lo >> 1) + (hx >> 1) + (lo & hx & 1)
        else:
            mid = _float_key(0.5 * _key_float(lo) + 0.5 * _key_float(hx))
        mid = jnp.where(active, jnp.minimum(jnp.maximum(mid, lo + 1), hx - 1), lo)
        c = count_ge(_key_float(mid))
        up = jnp.logical_and(active, c >= kk)
        down = jnp.logical_and(active, c < kk)
        return (jnp.where(up, mid, lo), jnp.where(down, mid, hx),
                jnp.where(up, c, c_lo), jnp.where(down, c, c_hx))

    def snap_step(st):
        lo, hx, c_lo, c_hx = st
        active = is_active(st)
        first_in, last_in = snap(_key_float(lo), _key_float(hx))
        return (jnp.where(active, _float_key(first_in + 0.0), lo),
                jnp.where(active, _float_key(last_in + 0.0) + 1, hx), c_lo, c_hx)

    def n_active(st):
        return jnp.max(jnp.where(is_active(st), 1.0, 0.0))

    st = (lo0, hx0, n_adm, jnp.zeros_like(n_adm))
    for _ in range(SEARCH_FIRST_STEPS):
        st = step(st, by_key=False)
    st = snap_step(st)
    for _ in range(SEARCH_SECOND_STEPS):
        st = step(st, by_key=False)
    act = n_active(st)

    def cond(carry):
        it, act, _ = carry
        return jnp.logical_and(it < SEARCH_MAX_ROUNDS, act > 0.5)

    def body(carry):
        it, _, st = carry
        st = step(snap_step(st), by_key=True)
        for _ in range(SEARCH_ROUND_STEPS):
            st = step(st, by_key=False)
        return it + 1, n_active(st), st

    _, _, (lo, _, c_lo, c_hx) = lax.while_loop(cond, body, (jnp.int32(0), act, st))
    return _key_float(lo), c_lo, c_hx


def _demote_excess_ties(s_ref, n_chunks, thr, need):
    ri = lax.broadcasted_iota(jnp.int32, (LANES, LANES), 0)
    cj = lax.broadcasted_iota(jnp.int32, (LANES, LANES), 1)
    lower = jnp.where(cj <= ri, 1.0, 0.0).astype(BF16)

    def body(j, run):
        off = pl.multiple_of(j * LANES, LANES)
        blk = s_ref[pl.ds(off, LANES), :]
        tied = blk == thr
        tied_f = jnp.where(tied, 1.0, 0.0)
        prefix = _dot(lower, tied_f.astype(BF16)) + run
        s_ref[pl.ds(off, LANES), :] = jnp.where(jnp.logical_and(tied, prefix > need), NEG_INF, blk)
        return run + jnp.sum(tied_f, axis=0, keepdims=True)

    lax.fori_loop(0, n_chunks, body, jnp.zeros((1, LANES), F32))


def _select_mask(s_ref, n_tiles, tk, smin, smax, n_adm, topk):
    kk = jnp.minimum(n_adm, float(topk))

    def count_ge(t):
        def body(j, acc):
            off = pl.multiple_of(j * tk, tk)
            return acc + _fold_rows(jnp.where(s_ref[pl.ds(off, tk), :] >= t, 1.0, 0.0))
        acc = lax.fori_loop(0, n_tiles, body, jnp.zeros((SUBLANES, LANES), F32))
        return jnp.sum(acc, axis=0, keepdims=True)

    def snap(lo_f, hx_f):
        def body(j, carry):
            first_in, last_in = carry
            off = pl.multiple_of(j * tk, tk)
            blk = s_ref[pl.ds(off, tk), :]
            first_in = jnp.minimum(first_in, -_fold_max(jnp.where(blk >= lo_f, -blk, NEG_INF)))
            last_in = jnp.maximum(last_in, _fold_max(jnp.where(blk < hx_f, blk, NEG_INF)))
            return first_in, last_in
        first_in, last_in = lax.fori_loop(0, n_tiles, body, (jnp.full((SUBLANES, LANES), -NEG_INF, F32),
                                                             jnp.full((SUBLANES, LANES), NEG_INF, F32)))
        return jnp.min(first_in, axis=0, keepdims=True), jnp.max(last_in, axis=0, keepdims=True)

    thr, c_ge, c_gt = _topk_threshold(count_ge, snap, smin, smax, n_adm, kk)

    @pl.when(jnp.max(c_ge - kk) > 0.5)
    def _():
        _demote_excess_ties(s_ref, n_tiles * (tk // LANES), thr, kk - c_gt)

    def mask_body(j, carry):
        off = pl.multiple_of(j * tk, tk)
        s_ref[pl.ds(off, tk), :] = jnp.where(s_ref[pl.ds(off, tk), :] >= thr, 0.0, NEG_INF)
        return carry

    lax.fori_loop(0, n_tiles, mask_body, 0)


def _dsa_prompt_kernel(q_ref, qi_ref, wi_ref, k_ref, vt_ref, ki_ref, o_ref, s_scr, acc_scr, l_scr, kn_scr, lg_scr,
                       *, topk):
    i = pl.program_id(1)
    tk, ts = KEY_TILE, KEY_SUB
    n_keys = (i + 1) * Q_BLOCK
    n_tiles = (n_keys + tk - 1) // tk
    qcol = lax.broadcasted_iota(jnp.int32, (1, Q_BLOCK), 1)
    lim = ((i * Q_BLOCK + qcol) // CHUNK + 1) * CHUNK

    @pl.when(i == 0)
    def _():
        ri = lax.broadcasted_iota(jnp.int32, (KV_WIDTH, LANES), 0)
        cj = lax.broadcasted_iota(jnp.int32, (KV_WIDTH, LANES), 1)
        head_sum = jnp.where(ri // HEAD_DIM == cj, 1.0, 0.0).astype(BF16)

        def body(j, acc):
            kt = k_ref[0, pl.ds(pl.multiple_of(j * tk, tk), tk), :].astype(F32)
            return jnp.maximum(acc, _fold_max(_dot((kt * kt).astype(BF16), head_sum)))

        acc = lax.fori_loop(0, k_ref.shape[1] // tk, body, jnp.zeros((SUBLANES, LANES), F32))
        kn_scr[...] = jnp.max(acc, axis=0, keepdims=True) * KNORM_SLACK

    qt = q_ref[0].astype(F32).T.astype(BF16)
    qit = qi_ref[0].astype(F32).T.astype(BF16)
    wit = wi_ref[0].T
    qi_rhs = jnp.concatenate([qit[h * IDX_DIM:(h + 1) * IDX_DIM, :] for h in range(IDX_HEADS)], axis=1)
    zero = jnp.zeros((HEAD_DIM, Q_BLOCK), BF16)
    q_cols = []
    for h in range(N_HEADS):
        qh = qt[h * HEAD_DIM:(h + 1) * HEAD_DIM, :]
        q_cols.append(jnp.concatenate([qh, zero] if h < GQA_GROUP else [zero, qh], axis=0))
    q_rhs = jnp.concatenate(q_cols, axis=1)

    def score_tile(j, carry):
        smax, smin = carry
        for c in range(tk // ts):
            off = pl.multiple_of(j * tk + c * ts, ts)
            r = _dot(ki_ref[0, pl.ds(off, ts), :], qi_rhs)
            acc = wit[0:1, :] * jnp.maximum(r[:, 0:LANES], 0.0)
            for h in range(1, IDX_HEADS):
                acc = acc + wit[h:h + 1, :] * jnp.maximum(r[:, h * LANES:(h + 1) * LANES], 0.0)
            key = off + lax.broadcasted_iota(jnp.int32, (ts, Q_BLOCK), 0)
            ok = key < lim
            s_scr[pl.ds(off, ts), :] = jnp.where(ok, acc, NEG_INF)
            smax = jnp.maximum(smax, _fold_max(jnp.where(ok, acc, NEG_INF)))
            smin = jnp.minimum(smin, -_fold_max(jnp.where(ok, -acc, NEG_INF)))
        return smax, smin

    smax, smin = lax.fori_loop(0, n_tiles, score_tile,
                               (jnp.full((SUBLANES, Q_BLOCK), NEG_INF, F32), jnp.full((SUBLANES, Q_BLOCK), -NEG_INF, F32)))
    smax = jnp.max(smax, axis=0, keepdims=True)
    smin = jnp.min(smin, axis=0, keepdims=True)
    _select_mask(s_scr, n_tiles, tk, smin, smax, lim.astype(F32), topk)

    qsq = jnp.square(qt.astype(F32))
    kn = kn_scr[...]
    lane1 = lax.broadcasted_iota(jnp.int32, (1, LANES), 1)
    shift_rows = []
    for h in range(N_HEADS):
        qn2 = jnp.sum(_fold_rows(qsq[h * HEAD_DIM:(h + 1) * HEAD_DIM, :]), axis=0, keepdims=True)
        kmax2 = jnp.max(jnp.where(lane1 == h // GQA_GROUP, kn, 0.0))
        shift_rows.append(jnp.sqrt(qn2 * kmax2))
    shift = jnp.concatenate(shift_rows, axis=0)

    def pv_accumulate(vt, ps, alphas):
        for pair in range(N_HEADS // 2):
            g = (2 * pair) // GQA_GROUP
            o = _dot(vt[g * HEAD_DIM:(g + 1) * HEAD_DIM, :], jnp.concatenate(ps[2 * pair:2 * pair + 2], axis=1))
            for u in range(2):
                h = 2 * pair + u
                rows = slice(h * HEAD_DIM, (h + 1) * HEAD_DIM)
                prev = acc_scr[rows, :] if alphas is None else alphas[h] * acc_scr[rows, :]
                acc_scr[rows, :] = prev + o[:, u * LANES:(u + 1) * LANES]

    def logits_into(buf, sub):
        off = pl.multiple_of(jnp.minimum(sub * ts, k_ref.shape[1] - ts), ts)
        lg_scr[buf] = _dot(k_ref[0, pl.ds(off, ts), :], q_rhs)

    def softmax_pv(buf, sub, l):
        bias = s_scr[pl.ds(pl.multiple_of(sub * ts, ts), ts), :]
        l_rows, ps = [], []
        for h in range(N_HEADS):
            p = jnp.exp2(lg_scr[buf, :, h * LANES:(h + 1) * LANES] + (bias - shift[h:h + 1, :]))
            l_rows.append(l[h * SUBLANES:(h + 1) * SUBLANES, :] + _fold_rows(p))
            ps.append(p.astype(BF16))
        pv_accumulate(vt_ref[sub], ps, None)
        return jnp.concatenate(l_rows, axis=0)

    def attn_bounded(j, l):
        logits_into(1, 2 * j + 1)
        l = softmax_pv(0, 2 * j, l)
        logits_into(0, 2 * j + 2)
        return softmax_pv(1, 2 * j + 1, l)

    def attn_online(j, carry):
        m, l = carry
        off = pl.multiple_of(j * ts, ts)
        lg = _dot(k_ref[0, pl.ds(off, ts), :], q_rhs)
        bias = s_scr[pl.ds(off, ts), :]
        m_rows, l_rows, alphas, ps = [], [], [], []
        for h in range(N_HEADS):
            x = lg[:, h * LANES:(h + 1) * LANES] + bias
            m_new = jnp.maximum(m[h:h + 1, :], jnp.max(_fold_max(x), axis=0, keepdims=True))
            m_use = jnp.where(m_new > NEG_INF, m_new, 0.0)
            alpha = jnp.exp2(m[h:h + 1, :] - m_use)
            p = jnp.exp2(x - m_use)
            l_rows.append(alpha * l[h:h + 1, :] + jnp.sum(_fold_rows(p), axis=0, keepdims=True))
            m_rows.append(m_new)
            alphas.append(alpha)
            ps.append(p.astype(BF16))
        pv_accumulate(vt_ref[j], ps, alphas)
        return jnp.concatenate(m_rows, axis=0), jnp.concatenate(l_rows, axis=0)

    n_sub = (n_keys + ts - 1) // ts
    acc_scr[...] = jnp.zeros_like(acc_scr)
    logits_into(0, 0)
    l_part = lax.fori_loop(0, n_tiles, attn_bounded, jnp.zeros((N_HEADS * SUBLANES, Q_BLOCK), F32))
    l_fast = jnp.concatenate([jnp.sum(l_part[h * SUBLANES:(h + 1) * SUBLANES, :], axis=0, keepdims=True)
                              for h in range(N_HEADS)], axis=0)
    l_scr[...] = l_fast

    @pl.when(jnp.min(l_fast) < SOFTMAX_SUM_FLOOR)
    def _():
        acc_scr[...] = jnp.zeros_like(acc_scr)
        _, l_exact = lax.fori_loop(0, n_sub, attn_online,
                                   (jnp.full((N_HEADS, Q_BLOCK), NEG_INF, F32), jnp.zeros((N_HEADS, Q_BLOCK), F32)))
        l_scr[...] = l_exact

    l = l_scr[...]
    out_t = jnp.concatenate([acc_scr[h * HEAD_DIM:(h + 1) * HEAD_DIM, :] / l[h:h + 1, :] for h in range(N_HEADS)],
                            axis=0)
    o_ref[0] = out_t.T.astype(BF16)


def _fold_max(x):
    n = x.shape[0] // SUBLANES
    ways = FOLD_WAYS if n % FOLD_WAYS == 0 else 1
    part = jnp.max(x.reshape(n // ways, ways, SUBLANES, x.shape[1]), axis=0)
    return jnp.max(part, axis=0)


def _dsa_prompt(q, qi, wi, kb, vt, kib):
    b, t, _ = q.shape
    assert t % KEY_TILE == 0
    topk = min(TOPK_MAX, t // 4)
    blk = lambda w: pl.BlockSpec((1, Q_BLOCK, w), lambda bi, i: (bi, i, 0))
    whole = lambda w: pl.BlockSpec((1, t, w), lambda bi, i: (bi, 0, 0))
    return pl.pallas_call(
        functools.partial(_dsa_prompt_kernel, topk=topk),
        grid=(b, t // Q_BLOCK),
        in_specs=[blk(ATT_WIDTH), blk(IDX_WIDTH), blk(LANES), whole(KV_WIDTH),
                  pl.BlockSpec((t // KEY_SUB, KV_WIDTH, KEY_SUB), lambda bi, i: (bi, 0, 0)), whole(IDX_DIM)],
        out_specs=blk(ATT_WIDTH),
        out_shape=jax.ShapeDtypeStruct((b, t, ATT_WIDTH), BF16),
        scratch_shapes=[pltpu.VMEM((t, Q_BLOCK), F32), pltpu.VMEM((ATT_WIDTH, Q_BLOCK), F32),
                        pltpu.VMEM((N_HEADS, Q_BLOCK), F32), pltpu.VMEM((1, LANES), F32),
                        pltpu.VMEM((2, KEY_SUB, N_HEADS * Q_BLOCK), F32)],
        compiler_params=pltpu.CompilerParams(dimension_semantics=("arbitrary", "arbitrary"),
                                             vmem_limit_bytes=VMEM_LIMIT),
        name="dsa_prompt",
    )(q, qi, wi, kb, vt, kib)


def _dsa_sample_kernel(q_ref, qi_ref, wi_ref, kn_ref, vn_ref, kin_ref, ckt_ref, cvt_ref, ckit_ref, o_ref,
                       kcat, vcat_t, kicat, s_scr, *, topk):
    tq = q_ref.shape[1]
    past = ckt_ref.shape[1]
    total = kcat.shape[0]
    pad = total - past - tq
    kcat[0:past, :] = ckt_ref[...].T.astype(BF16)
    kcat[past:past + tq, :] = kn_ref[0]
    kcat[past + tq:total, :] = jnp.zeros((pad, KV_WIDTH), BF16)
    kit_wide = jnp.concatenate([ckit_ref[...], jnp.zeros((LANES - IDX_DIM, past), F32)], axis=0)
    kicat[0:past, :] = kit_wide.T[:, 0:IDX_DIM].astype(BF16)
    kicat[past:past + tq, :] = kin_ref[0]
    kicat[past + tq:total, :] = jnp.zeros((pad, IDX_DIM), BF16)
    vcat_t[:, 0:past] = cvt_ref[...].astype(BF16)
    v_new = jnp.concatenate([vn_ref[0].astype(F32), jnp.zeros((pad, KV_WIDTH), F32)], axis=0)
    vcat_t[:, past:total] = v_new.T.astype(BF16)

    lane = lax.broadcasted_iota(jnp.int32, (1, LANES), 1)
    qi = qi_ref[0]
    qi_stack = jnp.concatenate([qi[:, h * IDX_DIM:(h + 1) * IDX_DIM] for h in range(IDX_HEADS)], axis=0)
    wi_rows = jnp.concatenate([wi_ref[0], jnp.zeros((LANES - tq, LANES), F32)], axis=0).T
    w_flat = jnp.zeros((1, LANES), F32)
    for h in range(IDX_HEADS):
        row = wi_rows[h:h + 1, :]
        shifted = row if h == 0 else pltpu.roll(row, h * tq, 1)
        w_flat = jnp.where(lane // tq == h, shifted, w_flat)

    sc = w_flat * jnp.maximum(_dot_t(kicat[...], qi_stack), 0.0)
    for shift in (LANES // 2, LANES // 4, LANES // 8):
        sc = sc + pltpu.roll(sc, shift, 1)
    sc = jnp.where(lane < tq, sc, 0.0)
    for shift in (LANES // 8, LANES // 4, LANES // 2):
        sc = sc + pltpu.roll(sc, shift, 1)
    key = lax.broadcasted_iota(jnp.int32, (total, LANES), 0)
    ok = key < past + tq
    s_scr[...] = jnp.where(ok, sc, NEG_INF)
    smax = jnp.max(jnp.where(ok, sc, NEG_INF), axis=0, keepdims=True)
    smin = jnp.min(jnp.where(ok, sc, -NEG_INF), axis=0, keepdims=True)
    _select_mask(s_scr, 1, total, smin, smax, jnp.full((1, LANES), float(past + tq), F32), topk)

    q = q_ref[0]
    zero = jnp.zeros((tq, HEAD_DIM), BF16)
    q_stack = jnp.concatenate(
        [jnp.concatenate([q[:, h * HEAD_DIM:(h + 1) * HEAD_DIM], zero] if h < GQA_GROUP
                         else [zero, q[:, h * HEAD_DIM:(h + 1) * HEAD_DIM]], axis=1) for h in range(N_HEADS)], axis=0)
    lg = _dot_t(kcat[...], q_stack) + s_scr[...]
    m = jnp.max(lg, axis=0, keepdims=True)
    p = jnp.exp2(lg - m)
    l = jnp.sum(p, axis=0, keepdims=True)
    o = _dot(vcat_t[...], p.astype(BF16))
    o = jnp.where(lane < GQA_GROUP * tq, o[0:HEAD_DIM, :], o[HEAD_DIM:2 * HEAD_DIM, :]) / l
    o_t = jnp.concatenate([o, jnp.zeros((LANES - HEAD_DIM, LANES), F32)], axis=0).T
    o_ref[0] = jnp.concatenate([o_t[h * tq:(h + 1) * tq, 0:HEAD_DIM] for h in range(N_HEADS)], axis=1).astype(BF16)


def _dsa_sample(q, qi, wi, kb, vb, kib, cache_k, cache_v, cache_kidx, layer):
    b, tq, _ = q.shape
    past = cache_k.shape[3]
    assert tq * N_HEADS == LANES and tq * IDX_HEADS == LANES and past % LANES == 0
    total = past + LANES
    topk = min(TOPK_MAX, (past + tq) // 4)
    blk = lambda w: pl.BlockSpec((1, tq, w), lambda bi: (bi, 0, 0))
    cache = lambda w: pl.BlockSpec((None, None, w, past), lambda bi: (layer, bi, 0, 0))
    return pl.pallas_call(
        functools.partial(_dsa_sample_kernel, topk=topk),
        grid=(b,),
        in_specs=[blk(ATT_WIDTH), blk(IDX_WIDTH), blk(LANES), blk(KV_WIDTH), blk(KV_WIDTH), blk(IDX_DIM),
                  cache(KV_WIDTH), cache(KV_WIDTH), cache(IDX_DIM)],
        out_specs=blk(ATT_WIDTH),
        out_shape=jax.ShapeDtypeStruct((b, tq, ATT_WIDTH), BF16),
        scratch_shapes=[pltpu.VMEM((total, KV_WIDTH), BF16), pltpu.VMEM((KV_WIDTH, total), BF16),
                        pltpu.VMEM((total, IDX_DIM), BF16), pltpu.VMEM((total, LANES), F32)],
        compiler_params=pltpu.CompilerParams(dimension_semantics=("arbitrary",), vmem_limit_bytes=VMEM_LIMIT),
        name="dsa_sample",
    )(q, qi, wi, kb, vb, kib, cache_k, cache_v, cache_kidx)


def _first_lane_where(mask, lane):
    return jnp.min(jnp.where(mask, lane, LANES), axis=1, keepdims=True)


def _post_kernel(att_ref, sa_ref, mb_ref, x_ref, wpa_ref, wout_ref, g2_ref, wr_ref, br_ref,
                 h_ref, xn2_ref, gate_ref, *sort_refs):
    pa = _dot(att_ref[...], wpa_ref[...])
    merged = sa_ref[...].astype(F32) * pa + mb_ref[...].astype(F32)
    h = x_ref[...] + _dot(merged.astype(BF16), wout_ref[...])
    h_ref[...] = h
    xn2 = _rms(h, g2_ref[...])
    xh = xn2.astype(BF16)
    xn2_ref[...] = xh

    xl = (xn2 - xh.astype(F32)).astype(BF16)
    wr = wr_ref[...]
    wh = wr.astype(BF16)
    wl = (wr - wh.astype(F32)).astype(BF16)
    lg = _dot(xh, wh) + (_dot(xl, wh) + _dot(xh, wl)) + br_ref[...]

    lane = lax.broadcasted_iota(jnp.int32, lg.shape, 1)
    is_g = lane < N_GROUPS
    mg = jnp.max(jnp.where(is_g, lg, NEG_INF), axis=1, keepdims=True)
    eg = jnp.where(is_g, jnp.exp(lg - mg), 0.0)
    pg = eg / jnp.sum(eg, axis=1, keepdims=True)
    g_w = jnp.max(pg, axis=1, keepdims=True)
    g_idx = _first_lane_where(jnp.logical_and(is_g, pg == g_w), lane)

    e_lo = ROUTER_E0 + EXP_PER_GROUP * g_idx
    is_e = jnp.logical_and(lane >= e_lo, lane < e_lo + EXP_PER_GROUP)
    me = jnp.max(jnp.where(is_e, lg, NEG_INF), axis=1, keepdims=True)
    ee = jnp.where(is_e, jnp.exp(lg - me), 0.0)
    pe = ee / jnp.sum(ee, axis=1, keepdims=True)
    e1 = jnp.max(pe, axis=1, keepdims=True)
    i1 = _first_lane_where(jnp.logical_and(is_e, pe == e1), lane)
    rest = jnp.logical_and(is_e, lane != i1)
    e2 = jnp.max(jnp.where(rest, pe, -1.0), axis=1, keepdims=True)
    i2 = _first_lane_where(jnp.logical_and(rest, pe == e2), lane)
    den = e1 + e2
    gate = jnp.where(lane == i1, e1 / den, jnp.where(lane == i2, e2 / den, 0.0)) * g_w
    group_gates = [jnp.where(lane < EXP_PER_GROUP, pltpu.roll(gate, LANES - (ROUTER_E0 + EXP_PER_GROUP * g), 1), 0.0)
                   for g in range(N_GROUPS)]
    if not sort_refs:
        for g in range(N_GROUPS):
            gate_ref[:, g * LANES:(g + 1) * LANES] = group_gates[g]
        return

    pt_ref, meta_ref = sort_refs
    gate_ref[...] = (group_gates[0] + group_gates[1]) + (group_gates[2] + group_gates[3])
    tm = lg.shape[0]
    onehot = jnp.where(lane == g_idx, 1.0, 0.0)
    ri = lax.broadcasted_iota(jnp.int32, (tm, tm), 0)
    cj = lax.broadcasted_iota(jnp.int32, (tm, tm), 1)
    earlier = jnp.where(cj < ri, 1.0, 0.0).astype(BF16)
    rank = jnp.sum(_dot(earlier, onehot.astype(BF16)) * onehot, axis=1, keepdims=True)
    cnt = jnp.sum(onehot, axis=0, keepdims=True)
    cnt_pad = jnp.floor((cnt + (SORT_PAD - 1.0)) * (1.0 / SORT_PAD)) * SORT_PAD
    seg_off = pltpu.roll(cnt_pad, 1, 1) + pltpu.roll(cnt_pad, 2, 1) + pltpu.roll(cnt_pad, 3, 1)
    dest = (rank + jnp.sum(onehot * seg_off, axis=1, keepdims=True)).astype(jnp.int32)
    col = lax.broadcasted_iota(jnp.int32, pt_ref.shape, 1)
    pt_ref[...] = jnp.where(col == dest, 1.0, 0.0).astype(BF16)
    meta_ref[0] = jnp.concatenate([cnt_pad, seg_off, jnp.zeros((SUBLANES - 2, LANES), F32)], axis=0)


def _sort_rows(tm):
    return -(-(tm + N_GROUPS * SORT_PAD) // LANES) * LANES


def _post(att, sa, mb, x, w, *, tm, sort):
    n, d = x.shape
    row = lambda t: (t, 0)
    const = lambda t: (0, 0)
    gate_w = LANES if sort else N_GROUPS * LANES
    out_specs = [pl.BlockSpec((tm, d), row), pl.BlockSpec((tm, d), row), pl.BlockSpec((tm, gate_w), row)]
    out_shape = [jax.ShapeDtypeStruct((n, d), F32), jax.ShapeDtypeStruct((n, d), BF16),
                 jax.ShapeDtypeStruct((n, gate_w), F32)]
    if sort:
        out_specs += [pl.BlockSpec((tm, _sort_rows(tm)), row), pl.BlockSpec((1, SUBLANES, LANES), lambda t: (t, 0, 0))]
        out_shape += [jax.ShapeDtypeStruct((n, _sort_rows(tm)), BF16),
                      jax.ShapeDtypeStruct((n // tm, SUBLANES, LANES), F32)]
    return pl.pallas_call(
        _post_kernel,
        grid=(n // tm,),
        in_specs=[pl.BlockSpec((tm, ATT_WIDTH), row), pl.BlockSpec((tm, d), row), pl.BlockSpec((tm, d), row),
                  pl.BlockSpec((tm, d), row), pl.BlockSpec(w["wpa"].shape, const), pl.BlockSpec(w["wout"].shape, const),
                  pl.BlockSpec((1, d), const), pl.BlockSpec(w["wr"].shape, const), pl.BlockSpec((1, ROUTER_LANES), const)],
        out_specs=out_specs,
        out_shape=out_shape,
        compiler_params=pltpu.CompilerParams(dimension_semantics=("arbitrary",), vmem_limit_bytes=VMEM_LIMIT),
        name="post_attn",
    )(att, sa, mb, x, w["wpa"], w["wout"], w["g2"], w["wr"], w["br"])


def _moe_kernel(xn2_ref, gate_ref, h_ref, wg_ref, wu_ref, wd_ref, fn_ref, o_ref, acc_ref, *, final):
    g = pl.program_id(1)

    @pl.when(g == 0)
    def _():
        acc_ref[...] = jnp.zeros_like(acc_ref)

    x = xn2_ref[...]
    gate = gate_ref[...]
    act = jax.nn.silu(_dot(x, wg_ref[0])) * _dot(x, wu_ref[0])
    parts = []
    for e in range(EXP_PER_GROUP):
        parts.append((act[:, e * EXP_HIDDEN:(e + 1) * EXP_HIDDEN] * gate[:, e:e + 1]).astype(BF16))
    acc_ref[...] += _dot(jnp.concatenate(parts, axis=1), wd_ref[0])

    @pl.when(g == N_GROUPS - 1)
    def _():
        y = h_ref[...] + acc_ref[...]
        o_ref[...] = _rms(y, fn_ref[...]) if final else y


def _moe(xn2, gate, h, w, fn, *, tm, final):
    n, d = h.shape
    row = lambda t, g: (t, 0)
    wspec = lambda shp: pl.BlockSpec((1,) + shp[1:], lambda t, g: (g, 0, 0))
    return pl.pallas_call(
        functools.partial(_moe_kernel, final=final),
        grid=(n // tm, N_GROUPS),
        in_specs=[pl.BlockSpec((tm, d), row), pl.BlockSpec((tm, LANES), lambda t, g: (t, g)), pl.BlockSpec((tm, d), row),
                  wspec(w["weg"].shape), wspec(w["weu"].shape), wspec(w["wed"].shape),
                  pl.BlockSpec((1, d), lambda t, g: (0, 0))],
        out_specs=pl.BlockSpec((tm, d), row),
        out_shape=jax.ShapeDtypeStruct((n, d), F32),
        scratch_shapes=[pltpu.VMEM((tm, d), F32)],
        compiler_params=pltpu.CompilerParams(dimension_semantics=("arbitrary", "arbitrary"),
                                             vmem_limit_bytes=VMEM_LIMIT),
        name="moe",
    )(xn2, gate, h, w["weg"], w["weu"], w["wed"], fn)


def _moe_sorted_kernel(meta_ref, xn2_ref, gate_ref, h_ref, pt_ref, wg_ref, wu_ref, wd_ref, fn_ref, o_ref,
                       xs_scr, gs_scr, ys_scr, *, final):
    t = pl.program_id(0)
    rows = pt_ref.shape[1]
    pt = pt_ref[...]
    perm = pt.astype(F32).T.astype(BF16)
    xs_scr[0:rows, :] = _dot(perm, xn2_ref[...]).astype(BF16)
    xs_scr[rows:rows + MOE_CHUNK, :] = jnp.zeros((MOE_CHUNK, xs_scr.shape[1]), BF16)
    gate = gate_ref[...]
    g_hi = gate.astype(BF16)
    g_rest = gate - g_hi.astype(F32)
    g_mid = g_rest.astype(BF16)
    g_lo = (g_rest - g_mid.astype(F32)).astype(BF16)
    gs_scr[0:rows, :] = _dot(perm, g_hi) + (_dot(perm, g_mid) + _dot(perm, g_lo))
    gs_scr[rows:rows + MOE_CHUNK, :] = jnp.zeros((MOE_CHUNK, LANES), F32)
    ys_scr[...] = jnp.zeros_like(ys_scr)

    for g in range(N_GROUPS):
        cnt = meta_ref[t * 2 * N_GROUPS + g]
        seg = meta_ref[t * 2 * N_GROUPS + N_GROUPS + g]

        def chunk(c, carry, g=g, cnt=cnt, seg=seg):
            r0 = pl.multiple_of(seg + c * MOE_CHUNK, SORT_PAD)
            x = xs_scr[pl.ds(r0, MOE_CHUNK), :]
            gt = gs_scr[pl.ds(r0, MOE_CHUNK), :]
            act = jax.nn.silu(_dot(x, wg_ref[g])) * _dot(x, wu_ref[g])
            parts = [(act[:, e * EXP_HIDDEN:(e + 1) * EXP_HIDDEN] * gt[:, e:e + 1]).astype(BF16)
                     for e in range(EXP_PER_GROUP)]
            y = _dot(jnp.concatenate(parts, axis=1), wd_ref[g])
            in_seg = lax.broadcasted_iota(jnp.int32, (MOE_CHUNK, 1), 0) + c * MOE_CHUNK < cnt
            ys_scr[pl.ds(r0, MOE_CHUNK), :] = jnp.where(in_seg, y, ys_scr[pl.ds(r0, MOE_CHUNK), :])
            return carry

        lax.fori_loop(0, (cnt + MOE_CHUNK - 1) // MOE_CHUNK, chunk, 0)

    y = h_ref[...] + _dot(pt, ys_scr[0:rows, :].astype(BF16))
    o_ref[...] = _rms(y, fn_ref[...]) if final else y


def _moe_sorted(xn2, gate, h, pt, meta, w, fn, *, tm, final):
    n, d = h.shape
    rows = pt.shape[1]
    row = lambda t, m: (t, 0)
    resident = lambda shp: pl.BlockSpec(shp, lambda t, m: (0, 0, 0), pipeline_mode=pl.Buffered(1))
    grid_spec = pltpu.PrefetchScalarGridSpec(
        num_scalar_prefetch=1,
        grid=(n // tm,),
        in_specs=[pl.BlockSpec((tm, d), row), pl.BlockSpec((tm, LANES), row), pl.BlockSpec((tm, d), row),
                  pl.BlockSpec((tm, rows), row),
                  resident(w["weg"].shape), resident(w["weu"].shape), resident(w["wed"].shape),
                  pl.BlockSpec((1, d), lambda t, m: (0, 0))],
        out_specs=pl.BlockSpec((tm, d), row),
        scratch_shapes=[pltpu.VMEM((rows + MOE_CHUNK, d), BF16), pltpu.VMEM((rows + MOE_CHUNK, LANES), F32),
                        pltpu.VMEM((rows + MOE_CHUNK, d), F32)],
    )
    return pl.pallas_call(
        functools.partial(_moe_sorted_kernel, final=final),
        grid_spec=grid_spec,
        out_shape=jax.ShapeDtypeStruct((n, d), F32),
        compiler_params=pltpu.CompilerParams(dimension_semantics=("arbitrary",), vmem_limit_bytes=VMEM_LIMIT),
        name="moe_sorted",
    )(meta, xn2, gate, h, pt, w["weg"], w["weu"], w["wed"], fn)


def _pack_experts_kernel(w_ref, o_ref):
    for e in range(EXP_PER_GROUP):
        o_ref[:, e * EXP_HIDDEN:(e + 1) * EXP_HIDDEN] = w_ref[e].astype(BF16)


def _pack_experts(w):
    depth, n_groups, n_exp, d, f = w.shape
    rows = 2 * MXU_DIM
    return pl.pallas_call(
        _pack_experts_kernel,
        grid=(depth, n_groups, d // rows),
        in_specs=[pl.BlockSpec((None, None, n_exp, rows, f), lambda l, g, r: (l, g, 0, r, 0))],
        out_specs=pl.BlockSpec((None, None, rows, n_exp * f), lambda l, g, r: (l, g, r, 0)),
        out_shape=jax.ShapeDtypeStruct((depth, n_groups, d, n_exp * f), BF16),
        compiler_params=pltpu.CompilerParams(dimension_semantics=("arbitrary",) * 3, vmem_limit_bytes=VMEM_LIMIT),
        name="pack_experts",
    )(w)


def _rope_tables(pos, d):
    inv = ROPE_THETA ** (-jnp.arange(0, d, 2, dtype=F32) / d)
    ang = pos[:, None] * inv[None, :]
    cos, sin = jnp.cos(ang), jnp.sin(ang)
    reps = LANES // d
    return (jnp.tile(jnp.concatenate([cos, cos], axis=-1), (1, reps)),
            jnp.tile(jnp.concatenate([-sin, sin], axis=-1), (1, reps)))


def _layer_weights(l, norm1, w_in, sgu_ln_g, sgu_ln_b, sgu_w, sgu_b, w_pa, w_pb, w_out, norm2,
                   w_rg, b_rg, w_re, b_re, w_eg, w_eu, w_ed, sgu_chunk):
    d = w_in.shape[1]
    a_end = ATT_WIDTH + 2 * KV_WIDTH + IDX_WIDTH
    b_end = a_end + IDX_DIM + IDX_HEADS
    reps = SGU_CHUNK // sgu_chunk
    n_router = N_GROUPS + N_GROUPS * EXP_PER_GROUP
    return dict(
        g1=norm1[l][None, :],
        wa=w_in[l][:, :a_end].astype(BF16),
        wb=jnp.pad(w_in[l][:, a_end:b_end], ((0, 0), (0, LANES - (b_end - a_end)))).astype(BF16),
        wc=w_in[l][:, b_end:].astype(BF16),
        sguw=jnp.tile(sgu_w[l][:, :sgu_chunk, :sgu_chunk], (1, reps, reps)),
        sgub=jnp.repeat(jnp.tile(sgu_b[l][:, :sgu_chunk], (1, reps)).T, SGU_GDIM, axis=1),
        lng=sgu_ln_g[l][None, :], lnb=sgu_ln_b[l][None, :],
        wpb=w_pb[l].astype(BF16), wpa=w_pa[l].astype(BF16), wout=w_out[l].astype(BF16),
        g2=norm2[l][None, :],
        wr=jnp.pad(jnp.concatenate([w_rg[l], w_re[l]], axis=1), ((0, 0), (0, ROUTER_LANES - n_router))),
        br=jnp.pad(jnp.concatenate([b_rg[l], b_re[l]]), (0, ROUTER_LANES - n_router))[None, :],
        weg=w_eg[l], weu=w_eu[l],
        wed=w_ed[l].reshape(N_GROUPS, GROUP_HIDDEN, d).astype(BF16),
    )


def _token_tile(n):
    return 512 if n % 512 == 0 else n


def kernel(x_prompt, x_sample, cache_k, cache_v, cache_kidx, norm1, w_in, sgu_ln_g, sgu_ln_b, sgu_w, sgu_b,
           w_pa, w_pb, w_out, norm2, w_rg, b_rg, w_re, b_re, w_eg, w_eu, w_ed, final_norm):
    bp, tp, d = x_prompt.shape
    bs, ts, _ = x_sample.shape
    depth, _, past = cache_k.shape[:3]
    np_, ns = bp * tp, bs * ts
    tm_p, tm_s = _token_tile(tp), _token_tile(ns)
    assert tp % tm_p == 0 and tp % Q_BLOCK == 0 and SGU_CHUNK % ts == 0 and tm_s % SGU_CHUNK == 0
    assert tm_p % KEY_SUB == 0 and tm_s % KEY_SUB == 0

    pos_p = jnp.arange(tp, dtype=F32)
    pos_s = jnp.tile(past + jnp.arange(ts, dtype=F32), tm_s // ts)
    tabs_p = _rope_tables(pos_p, HEAD_DIM) + _rope_tables(pos_p, IDX_DIM)
    tabs_s = _rope_tables(pos_s, HEAD_DIM) + _rope_tables(pos_s, IDX_DIM)
    ckt = jnp.transpose(cache_k, (0, 1, 3, 4, 2)).reshape(depth, bs, KV_WIDTH, past)
    cvt = jnp.transpose(cache_v, (0, 1, 3, 4, 2)).reshape(depth, bs, KV_WIDTH, past)
    ckit = jnp.transpose(cache_kidx, (0, 1, 3, 2))
    fn = final_norm[None, :]
    weg_all, weu_all = _pack_experts(w_eg), _pack_experts(w_eu)

    hp = x_prompt.reshape(np_, d)
    hs = x_sample.reshape(ns, d)
    outs = [[] for _ in range(4)]
    kv_bufs = None
    for l in range(depth):
        wargs = (norm1, w_in, sgu_ln_g, sgu_ln_b, sgu_w, sgu_b, w_pa, w_pb, w_out, norm2,
                 w_rg, b_rg, w_re, b_re, weg_all, weu_all, w_ed)
        wp = _layer_weights(l, *wargs, sgu_chunk=SGU_CHUNK)
        ws = dict(wp)
        ws.update({k: v for k, v in _layer_weights(l, *wargs, sgu_chunk=ts).items() if k in ("sguw", "sgub")})
        final = l == depth - 1

        q, kt_all, kb, vt_all, _, vt, qi, kit_all, kib, wi, mb, sa = _inproj(
            hp, wp, tabs_p, tm=tm_p, sgu_chunk=SGU_CHUNK, want_vln=False,
            kv_layer=l, kv_bufs=kv_bufs, kv_shape=(depth, bp, tp))
        kv_bufs = (kt_all, vt_all, kit_all)
        r3 = lambda a: a.reshape(bp, tp, a.shape[-1])
        att = _dsa_prompt(r3(q), r3(qi), r3(wi), r3(kb), vt, r3(kib)).reshape(np_, ATT_WIDTH)
        h, xn2, gate, pt, meta = _post(att, sa, mb, hp, wp, tm=tm_p, sort=True)
        meta = meta[:, 0:2, 0:N_GROUPS].astype(jnp.int32).reshape(-1)
        hp = _moe_sorted(xn2, gate, h, pt, meta, wp, fn, tm=tm_p, final=final)

        q, kf, kb, vf, vb, _, qi, kif, kib, wi, mb, sa, vln = _inproj(hs, ws, tabs_s, tm=tm_s, sgu_chunk=ts,
                                                                      want_vln=True)
        r3 = lambda a: a.reshape(bs, ts, a.shape[-1])
        att = _dsa_sample(r3(q), r3(qi), r3(wi), r3(kb), r3(vb), r3(kib), ckt, cvt, ckit, l)
        h, xn2, gate = _post(att.reshape(ns, ATT_WIDTH), sa, mb, hs, ws, tm=tm_s, sort=False)
        hs = _moe(xn2, gate, h, ws, fn, tm=tm_s, final=final)
        outs[0].append(kf.reshape(bs, ts, N_KV_HEADS, HEAD_DIM))
        outs[1].append(vf.reshape(bs, ts, N_KV_HEADS, HEAD_DIM))
        outs[2].append(kif.reshape(bs, ts, IDX_DIM))
        outs[3].append(vln.reshape(bs, ts, SGU_WIDTH))

    kt_all, vt_all, kit_all = kv_bufs
    heads = lambda a: jnp.transpose(a.reshape(depth, bp, N_KV_HEADS, HEAD_DIM, tp), (0, 1, 4, 2, 3))
    return ((hp.reshape(bp, tp, d), hs.reshape(bs, ts, d), heads(kt_all), heads(vt_all),
             jnp.transpose(kit_all, (0, 1, 3, 2))) + tuple(jnp.stack(o) for o in outs))
```

```python
import functools

import jax
import jax.numpy as jnp
from jax import lax
from jax.experimental import pallas as pl
from jax.experimental.pallas import tpu as pltpu

F32 = jnp.float32
BF16 = jnp.bfloat16

CHUNK = 64
N_HEADS = 8
N_KV_HEADS = 2
HEAD_DIM = 64
GQA_GROUP = N_HEADS // N_KV_HEADS
ATT_WIDTH = N_HEADS * HEAD_DIM
KV_WIDTH = N_KV_HEADS * HEAD_DIM
IDX_HEADS = 8
IDX_DIM = 32
IDX_WIDTH = IDX_HEADS * IDX_DIM
TOPK_MAX = 256
Q_BLOCK = 128
DSA_QUERIES = 256
SGU_GROUPS = 4
SGU_CHUNK = 128
SGU_WIDTH = 512
SGU_GDIM = SGU_WIDTH // SGU_GROUPS
N_GROUPS = 4
EXP_PER_GROUP = 8
EXP_HIDDEN = 128
GROUP_HIDDEN = EXP_PER_GROUP * EXP_HIDDEN
ROPE_THETA = 10000.0
EPS = 1e-6
Q_SCALE = HEAD_DIM ** -0.5 * 1.4426950408889634

LANES = 128
SUBLANES = 8
MXU_DIM = 256
ROUTER_LANES = LANES
ROUTER_E0 = N_GROUPS
VMEM_LIMIT = 56 * 1024 * 1024
KEY_TILE = 2 * MXU_DIM
KEY_SUB = MXU_DIM
FOLD_WAYS = 4
SEARCH_FIRST_STEPS = 11
SEARCH_SECOND_STEPS = 3
SEARCH_ROUND_STEPS = 2
SEARCH_MAX_ROUNDS = 34
SORT_PAD = 16
MOE_CHUNK = 144
KNORM_SLACK = 1.0 + 2.0 ** -6
SOFTMAX_SUM_FLOOR = 2.0 ** -80
NEG_INF = float("-inf")


def _dot(a, b):
    return jnp.dot(a, b, preferred_element_type=F32)


def _dot_t(a, b):
    return lax.dot_general(a, b, (((1,), (1,)), ((), ())), preferred_element_type=F32)


def _rms(x, g):
    return x * lax.rsqrt(jnp.mean(x * x, axis=-1, keepdims=True) + EPS) * g


def _swap_halves(x, d):
    lane = lax.broadcasted_iota(jnp.int32, x.shape, 1)
    first = (lane % d) < (d // 2)
    return jnp.where(first, pltpu.roll(x, LANES - d // 2, 1), pltpu.roll(x, d // 2, 1))


def _rope(x, cos, sin_signed, d):
    return x * cos + _swap_halves(x, d) * sin_signed


def _inproj_kernel(*refs, sgu_chunk, kv_transposed, n_alias, want_vln):
    (x_ref, g1_ref, wa_ref, wb_ref, wc_ref, cq_ref, sq_ref, ci_ref, si_ref,
     sguw_ref, sgub_ref, lng_ref, lnb_ref, wpb_ref) = refs[:14]
    (q_ref, kf_ref, kb_ref, vf_ref, vb_ref, vt_ref, qi_ref, kif_ref, kib_ref, wi_ref,
     mb_ref, sa_ref) = refs[14 + n_alias:26 + n_alias]
    tm = x_ref.shape[0]
    xn = _rms(x_ref[...], g1_ref[...]).astype(BF16)
    cq, sq, ci, si = cq_ref[...], sq_ref[...], ci_ref[...], si_ref[...]

    a = _dot(xn, wa_ref[...])
    for c in range(ATT_WIDTH // LANES):
        r = _rope(a[:, c * LANES:(c + 1) * LANES], cq, sq, HEAD_DIM)
        q_ref[:, c * LANES:(c + 1) * LANES] = (r * Q_SCALE).astype(BF16)
    k = _rope(a[:, ATT_WIDTH:ATT_WIDTH + KV_WIDTH], cq, sq, HEAD_DIM)
    kb_ref[...] = k.astype(BF16)
    v = a[:, ATT_WIDTH + KV_WIDTH:ATT_WIDTH + 2 * KV_WIDTH]
    vb_ref[...] = v.astype(BF16)
    if kv_transposed:
        kf_ref[...] = k.T
        vf_ref[...] = v.T
    else:
        kf_ref[...] = k
        vf_ref[...] = v
    for c in range(tm // KEY_SUB):
        vt_ref[c] = v[c * KEY_SUB:(c + 1) * KEY_SUB, :].T.astype(BF16)
    qi0 = ATT_WIDTH + 2 * KV_WIDTH
    for c in range(IDX_WIDTH // LANES):
        r = _rope(a[:, qi0 + c * LANES:qi0 + (c + 1) * LANES], ci, si, IDX_DIM)
        qi_ref[:, c * LANES:(c + 1) * LANES] = r.astype(BF16)

    b = _dot(xn, wb_ref[...])
    ki_wide = _rope(b, ci, si, IDX_DIM)
    ki = ki_wide[:, :IDX_DIM]
    kif_ref[...] = ki_wide.T[0:IDX_DIM, :] if kv_transposed else ki
    kib_ref[...] = ki.astype(BF16)
    wi_ref[...] = pltpu.roll(b, LANES - IDX_DIM, 1) * (IDX_HEADS ** -0.5)

    u = jax.nn.gelu(_dot(xn, wc_ref[:, 0:SGU_WIDTH]))
    vs = jax.nn.gelu(_dot(xn, wc_ref[:, SGU_WIDTH:2 * SGU_WIDTH]))
    mu = jnp.mean(vs, axis=-1, keepdims=True)
    var = jnp.mean(jnp.square(vs - mu), axis=-1, keepdims=True)
    vln = (vs - mu) * lax.rsqrt(var + EPS) * lng_ref[...] + lnb_ref[...]
    if want_vln:
        refs[26 + n_alias][...] = vln
    ri = lax.broadcasted_iota(jnp.int32, (SGU_CHUNK, SGU_CHUNK), 0)
    cj = lax.broadcasted_iota(jnp.int32, (SGU_CHUNK, SGU_CHUNK), 1)
    mix_mask = (cj <= ri) & ((ri // sgu_chunk) == (cj // sgu_chunk))
    vln_b = vln.astype(BF16)
    w_eff = [jnp.where(mix_mask, sguw_ref[g], 0.0).astype(BF16) for g in range(SGU_GROUPS)]
    sg_rows = []
    for r in range(tm // SGU_CHUNK):
        rows = slice(r * SGU_CHUNK, (r + 1) * SGU_CHUNK)
        mixed = [_dot(w_eff[g], vln_b[rows, g * SGU_GDIM:(g + 1) * SGU_GDIM]) for g in range(SGU_GROUPS)]
        sg_rows.append(u[rows, :] * (jnp.concatenate(mixed, axis=1) + sgub_ref[...]))
    sg = jnp.concatenate(sg_rows, axis=0).astype(BF16)
    d = x_ref.shape[1]
    sa_ref[...] = jax.nn.sigmoid(_dot(xn, wc_ref[:, 2 * SGU_WIDTH:2 * SGU_WIDTH + d])).astype(BF16)
    gb = jax.nn.sigmoid(_dot(xn, wc_ref[:, 2 * SGU_WIDTH + d:]))
    mb_ref[...] = (gb * _dot(sg, wpb_ref[...])).astype(BF16)


def _inproj(x, w, tabs, *, tm, sgu_chunk, want_vln, kv_layer=None, kv_bufs=None, kv_shape=None):
    n, d = x.shape
    n_tab_tiles = tabs[0].shape[0] // tm
    row = lambda t: (t, 0)
    const = lambda t: (0, 0)
    tab = lambda t: (t % n_tab_tiles, 0)
    kv_transposed = kv_layer is not None
    if kv_transposed:
        depth, streams, steps = kv_shape
        per_stream = steps // tm
        kv_sds = lambda wd: jax.ShapeDtypeStruct((depth, streams, wd, steps), F32)
        kv_spec = lambda wd: pl.BlockSpec((None, None, wd, tm),
                                          lambda t: (kv_layer, t // per_stream, 0, t % per_stream))
    else:
        kv_sds = lambda wd: jax.ShapeDtypeStruct((n, wd), F32)
        kv_spec = lambda wd: pl.BlockSpec((tm, wd), row)
    tiled = lambda wd, dt: (jax.ShapeDtypeStruct((n, wd), dt), pl.BlockSpec((tm, wd), row))
    outs = [
        tiled(ATT_WIDTH, BF16),
        (kv_sds(KV_WIDTH), kv_spec(KV_WIDTH)),
        tiled(KV_WIDTH, BF16),
        (kv_sds(KV_WIDTH), kv_spec(KV_WIDTH)),
        tiled(KV_WIDTH, BF16),
        (jax.ShapeDtypeStruct((n // KEY_SUB, KV_WIDTH, KEY_SUB), BF16),
         pl.BlockSpec((tm // KEY_SUB, KV_WIDTH, KEY_SUB), lambda t: (t, 0, 0))),
        tiled(IDX_WIDTH, BF16),
        (kv_sds(IDX_DIM), kv_spec(IDX_DIM)),
        tiled(IDX_DIM, BF16),
        tiled(LANES, F32),
        tiled(d, BF16),
        tiled(d, BF16),
    ]
    if want_vln:
        outs.append(tiled(SGU_WIDTH, F32))
    in_specs = [
        pl.BlockSpec((tm, d), row),
        pl.BlockSpec((1, d), const),
        pl.BlockSpec(w["wa"].shape, const),
        pl.BlockSpec(w["wb"].shape, const),
        pl.BlockSpec(w["wc"].shape, const),
        pl.BlockSpec((tm, LANES), tab), pl.BlockSpec((tm, LANES), tab),
        pl.BlockSpec((tm, LANES), tab), pl.BlockSpec((tm, LANES), tab),
        pl.BlockSpec(w["sguw"].shape, lambda t: (0, 0, 0)),
        pl.BlockSpec(w["sgub"].shape, const),
        pl.BlockSpec((1, SGU_WIDTH), const),
        pl.BlockSpec((1, SGU_WIDTH), const),
        pl.BlockSpec(w["wpb"].shape, const),
    ]
    args = [x, w["g1"], w["wa"], w["wb"], w["wc"], *tabs, w["sguw"], w["sgub"], w["lng"], w["lnb"], w["wpb"]]
    aliases = {}
    if kv_bufs is not None:
        for buf, out_idx in zip(kv_bufs, (1, 3, 7)):
            aliases[len(args)] = out_idx
            in_specs.append(pl.BlockSpec(memory_space=pl.ANY))
            args.append(buf)
    return pl.pallas_call(
        functools.partial(_inproj_kernel, sgu_chunk=sgu_chunk, kv_transposed=kv_transposed,
                          n_alias=len(aliases), want_vln=want_vln),
        grid=(n // tm,),
        in_specs=in_specs,
        out_specs=[o[1] for o in outs],
        out_shape=[o[0] for o in outs],
        input_output_aliases=aliases,
        compiler_params=pltpu.CompilerParams(dimension_semantics=("arbitrary",), vmem_limit_bytes=VMEM_LIMIT),
        name="inproj",
    )(*args)


def _float_key(x):
    b = lax.bitcast_convert_type(x, jnp.int32)
    return b ^ ((b >> 31) & jnp.int32(0x7FFFFFFF))


def _key_float(k):
    return lax.bitcast_convert_type(k ^ ((k >> 31) & jnp.int32(0x7FFFFFFF)), F32)


def _fold_rows(x):
    n = x.shape[0] // SUBLANES
    ways = FOLD_WAYS if n % FOLD_WAYS == 0 else 1
    part = jnp.sum(x.reshape(n // ways, ways, SUBLANES, x.shape[1]), axis=0)
    return jnp.sum(part, axis=0)


def _topk_threshold(count_ge, snap, smin, smax, n_adm, kk):
    lo0 = _float_key(smin + 0.0)
    hx0 = _float_key(smax + 0.0) + 1

    def is_active(st):
        lo, hx, c_lo, _ = st
        return jnp.logical_and(c_lo != kk, lo + 1 < hx)

    def step(st, by_key):
        lo, hx, c_lo, c_hx = st
        active = is_active(st)
        if by_key:
            mid = (lo >> 1) + (hx >> 1) + (lo & hx & 1)
        else:
            mid = _float_key(0.5 * _key_float(lo) + 0.5 * _key_float(hx))
        mid = jnp.where(active, jnp.minimum(jnp.maximum(mid, lo + 1), hx - 1), lo)
        c = count_ge(_key_float(mid))
        up = jnp.logical_and(active, c >= kk)
        down = jnp.logical_and(active, c < kk)
        return (jnp.where(up, mid, lo), jnp.where(down, mid, hx),
                jnp.where(up, c, c_lo), jnp.where(down, c, c_hx))

    def snap_step(st):
        lo, hx, c_lo, c_hx = st
        active = is_active(st)
        first_in, last_in = snap(_key_float(lo), _key_float(hx))
        return (jnp.where(active, _float_key(first_in + 0.0), lo),
                jnp.where(active, _float_key(last_in + 0.0) + 1, hx), c_lo, c_hx)

    def n_active(st):
        return jnp.max(jnp.where(is_active(st), 1.0, 0.0))

    st = (lo0, hx0, n_adm, jnp.zeros_like(n_adm))
    for _ in range(SEARCH_FIRST_STEPS):
        st = step(st, by_key=False)
    st = snap_step(st)
    for _ in range(SEARCH_SECOND_STEPS):
        st = step(st, by_key=False)
    act = n_active(st)

    def cond(carry):
        it, act, _ = carry
        return jnp.logical_and(it < SEARCH_MAX_ROUNDS, act > 0.5)

    def body(carry):
        it, _, st = carry
        st = step(snap_step(st), by_key=True)
        for _ in range(SEARCH_ROUND_STEPS):
            st = step(st, by_key=False)
        return it + 1, n_active(st), st

    _, _, (lo, _, c_lo, c_hx) = lax.while_loop(cond, body, (jnp.int32(0), act, st))
    return _key_float(lo), c_lo, c_hx


def _demote_excess_ties(s_ref, n_chunks, thr, need):
    ri = lax.broadcasted_iota(jnp.int32, (LANES, LANES), 0)
    cj = lax.broadcasted_iota(jnp.int32, (LANES, LANES), 1)
    lower = jnp.where(cj <= ri, 1.0, 0.0).astype(BF16)

    def body(j, run):
        off = pl.multiple_of(j * LANES, LANES)
        blk = s_ref[pl.ds(off, LANES), :]
        tied = blk == thr
        tied_f = jnp.where(tied, 1.0, 0.0)
        prefix = _dot(lower, tied_f.astype(BF16)) + run
        s_ref[pl.ds(off, LANES), :] = jnp.where(jnp.logical_and(tied, prefix > need), NEG_INF, blk)
        return run + jnp.sum(tied_f, axis=0, keepdims=True)

    lax.fori_loop(0, n_chunks, body, jnp.zeros((1, s_ref.shape[1]), F32))


def _select_mask(s_ref, n_tiles, tk, smin, smax, n_adm, topk):
    kk = jnp.minimum(n_adm, float(topk))

    def count_ge(t):
        def body(j, acc):
            off = pl.multiple_of(j * tk, tk)
            return acc + _fold_rows(jnp.where(s_ref[pl.ds(off, tk), :] >= t, 1.0, 0.0))
        acc = lax.fori_loop(0, n_tiles, body, jnp.zeros((SUBLANES, s_ref.shape[1]), F32))
        return jnp.sum(acc, axis=0, keepdims=True)

    def snap(lo_f, hx_f):
        def body(j, carry):
            first_in, last_in = carry
            off = pl.multiple_of(j * tk, tk)
            blk = s_ref[pl.ds(off, tk), :]
            first_in = jnp.minimum(first_in, -_fold_max(jnp.where(blk >= lo_f, -blk, NEG_INF)))
            last_in = jnp.maximum(last_in, _fold_max(jnp.where(blk < hx_f, blk, NEG_INF)))
            return first_in, last_in
        first_in, last_in = lax.fori_loop(0, n_tiles, body,
                                          (jnp.full((SUBLANES, s_ref.shape[1]), -NEG_INF, F32),
                                           jnp.full((SUBLANES, s_ref.shape[1]), NEG_INF, F32)))
        return jnp.min(first_in, axis=0, keepdims=True), jnp.max(last_in, axis=0, keepdims=True)

    thr, c_ge, c_gt = _topk_threshold(count_ge, snap, smin, smax, n_adm, kk)

    @pl.when(jnp.max(c_ge - kk) > 0.5)
    def _():
        _demote_excess_ties(s_ref, n_tiles * (tk // LANES), thr, kk - c_gt)

    def mask_body(j, carry):
        off = pl.multiple_of(j * tk, tk)
        s_ref[pl.ds(off, tk), :] = jnp.where(s_ref[pl.ds(off, tk), :] >= thr, 0.0, NEG_INF)
        return carry

    lax.fori_loop(0, n_tiles, mask_body, 0)


def _dsa_prompt_kernel(q_ref, qi_ref, wi_ref, k_ref, vt_ref, ki_ref, o_ref, s_scr, acc_scr, l_scr, kn_scr, lg_scr,
                       *, topk):
    i = pl.program_id(1)
    tk, ts = KEY_TILE, KEY_SUB
    qb = q_ref.shape[1]
    n_keys = (i + 1) * qb
    n_tiles = (n_keys + tk - 1) // tk
    qcol = lax.broadcasted_iota(jnp.int32, (1, qb), 1)
    lim = ((i * qb + qcol) // CHUNK + 1) * CHUNK

    @pl.when(i == 0)
    def _():
        ri = lax.broadcasted_iota(jnp.int32, (KV_WIDTH, LANES), 0)
        cj = lax.broadcasted_iota(jnp.int32, (KV_WIDTH, LANES), 1)
        head_sum = jnp.where(ri // HEAD_DIM == cj, 1.0, 0.0).astype(BF16)

        def body(j, acc):
            kt = k_ref[0, pl.ds(pl.multiple_of(j * tk, tk), tk), :].astype(F32)
            return jnp.maximum(acc, _fold_max(_dot((kt * kt).astype(BF16), head_sum)))

        acc = lax.fori_loop(0, k_ref.shape[1] // tk, body, jnp.zeros((SUBLANES, LANES), F32))
        kn_scr[...] = jnp.max(acc, axis=0, keepdims=True) * KNORM_SLACK

    qt = q_ref[0].astype(F32).T.astype(BF16)
    qit = qi_ref[0].astype(F32).T.astype(BF16)
    wit = wi_ref[0].T
    qi_rhs = jnp.concatenate([qit[h * IDX_DIM:(h + 1) * IDX_DIM, :] for h in range(IDX_HEADS)], axis=1)
    zero = jnp.zeros((HEAD_DIM, qb), BF16)
    q_cols = []
    for h in range(N_HEADS):
        qh = qt[h * HEAD_DIM:(h + 1) * HEAD_DIM, :]
        q_cols.append(jnp.concatenate([qh, zero] if h < GQA_GROUP else [zero, qh], axis=0))
    q_rhs = jnp.concatenate(q_cols, axis=1)

    def score_tile(j, carry):
        smax, smin = carry
        for c in range(tk // ts):
            off = pl.multiple_of(j * tk + c * ts, ts)
            r = _dot(ki_ref[0, pl.ds(off, ts), :], qi_rhs)
            acc = wit[0:1, :] * jnp.maximum(r[:, 0:qb], 0.0)
            for h in range(1, IDX_HEADS):
                acc = acc + wit[h:h + 1, :] * jnp.maximum(r[:, h * qb:(h + 1) * qb], 0.0)
            key = off + lax.broadcasted_iota(jnp.int32, (ts, qb), 0)
            ok = key < lim
            s_scr[pl.ds(off, ts), :] = jnp.where(ok, acc, NEG_INF)
            smax = jnp.maximum(smax, _fold_max(jnp.where(ok, acc, NEG_INF)))
            smin = jnp.minimum(smin, -_fold_max(jnp.where(ok, -acc, NEG_INF)))
        return smax, smin

    smax, smin = lax.fori_loop(0, n_tiles, score_tile,
                               (jnp.full((SUBLANES, qb), NEG_INF, F32), jnp.full((SUBLANES, qb), -NEG_INF, F32)))
    smax = jnp.max(smax, axis=0, keepdims=True)
    smin = jnp.min(smin, axis=0, keepdims=True)
    _select_mask(s_scr, n_tiles, tk, smin, smax, lim.astype(F32), topk)

    qsq = jnp.square(qt.astype(F32))
    kn = kn_scr[...]
    lane1 = lax.broadcasted_iota(jnp.int32, (1, LANES), 1)
    shift_rows = []
    for h in range(N_HEADS):
        qn2 = jnp.sum(_fold_rows(qsq[h * HEAD_DIM:(h + 1) * HEAD_DIM, :]), axis=0, keepdims=True)
        kmax2 = jnp.max(jnp.where(lane1 == h // GQA_GROUP, kn, 0.0))
        shift_rows.append(jnp.sqrt(qn2 * kmax2))
    shift = jnp.concatenate(shift_rows, axis=0)

    def pv_accumulate(vt, ps, alphas):
        for pair in range(N_HEADS // 2):
            g = (2 * pair) // GQA_GROUP
            o = _dot(vt[g * HEAD_DIM:(g + 1) * HEAD_DIM, :], jnp.concatenate(ps[2 * pair:2 * pair + 2], axis=1))
            for u in range(2):
                h = 2 * pair + u
                rows = slice(h * HEAD_DIM, (h + 1) * HEAD_DIM)
                prev = acc_scr[rows, :] if alphas is None else alphas[h] * acc_scr[rows, :]
                acc_scr[rows, :] = prev + o[:, u * qb:(u + 1) * qb]

    def logits_into(buf, sub):
        off = pl.multiple_of(jnp.minimum(sub * ts, k_ref.shape[1] - ts), ts)
        lg_scr[buf] = _dot(k_ref[0, pl.ds(off, ts), :], q_rhs)

    def softmax_pv(buf, sub, l):
        bias = s_scr[pl.ds(pl.multiple_of(sub * ts, ts), ts), :]
        l_rows, ps = [], []
        for h in range(N_HEADS):
            p = jnp.exp2(lg_scr[buf, :, h * qb:(h + 1) * qb] + (bias - shift[h:h + 1, :]))
            l_rows.append(l[h * SUBLANES:(h + 1) * SUBLANES, :] + _fold_rows(p))
            ps.append(p.astype(BF16))
        pv_accumulate(vt_ref[sub], ps, None)
        return jnp.concatenate(l_rows, axis=0)

    def attn_bounded(j, l):
        logits_into(1, 2 * j + 1)
        l = softmax_pv(0, 2 * j, l)
        logits_into(0, 2 * j + 2)
        return softmax_pv(1, 2 * j + 1, l)

    def attn_online(j, carry):
        m, l = carry
        off = pl.multiple_of(j * ts, ts)
        lg = _dot(k_ref[0, pl.ds(off, ts), :], q_rhs)
        bias = s_scr[pl.ds(off, ts), :]
        m_rows, l_rows, alphas, ps = [], [], [], []
        for h in range(N_HEADS):
            x = lg[:, h * qb:(h + 1) * qb] + bias
            m_new = jnp.maximum(m[h:h + 1, :], jnp.max(_fold_max(x), axis=0, keepdims=True))
            m_use = jnp.where(m_new > NEG_INF, m_new, 0.0)
            alpha = jnp.exp2(m[h:h + 1, :] - m_use)
            p = jnp.exp2(x - m_use)
            l_rows.append(alpha * l[h:h + 1, :] + jnp.sum(_fold_rows(p), axis=0, keepdims=True))
            m_rows.append(m_new)
            alphas.append(alpha)
            ps.append(p.astype(BF16))
        pv_accumulate(vt_ref[j], ps, alphas)
        return jnp.concatenate(m_rows, axis=0), jnp.concatenate(l_rows, axis=0)

    n_sub = (n_keys + ts - 1) // ts
    acc_scr[...] = jnp.zeros_like(acc_scr)
    logits_into(0, 0)
    l_part = lax.fori_loop(0, n_tiles, attn_bounded, jnp.zeros((N_HEADS * SUBLANES, qb), F32))
    l_fast = jnp.concatenate([jnp.sum(l_part[h * SUBLANES:(h + 1) * SUBLANES, :], axis=0, keepdims=True)
                              for h in range(N_HEADS)], axis=0)
    l_scr[...] = l_fast

    @pl.when(jnp.min(l_fast) < SOFTMAX_SUM_FLOOR)
    def _():
        acc_scr[...] = jnp.zeros_like(acc_scr)
        _, l_exact = lax.fori_loop(0, n_sub, attn_online,
                                   (jnp.full((N_HEADS, qb), NEG_INF, F32), jnp.zeros((N_HEADS, qb), F32)))
        l_scr[...] = l_exact

    l = l_scr[...]
    out_t = jnp.concatenate([acc_scr[h * HEAD_DIM:(h + 1) * HEAD_DIM, :] / l[h:h + 1, :] for h in range(N_HEADS)],
                            axis=0)
    o_ref[0] = out_t.T.astype(BF16)


def _fold_max(x):
    n = x.shape[0] // SUBLANES
    ways = FOLD_WAYS if n % FOLD_WAYS == 0 else 1
    part = jnp.max(x.reshape(n // ways, ways, SUBLANES, x.shape[1]), axis=0)
    return jnp.max(part, axis=0)


def _dsa_prompt(q, qi, wi, kb, vt, kib):
    b, t, _ = q.shape
    qb = DSA_QUERIES
    assert t % KEY_TILE == 0 and t % qb == 0 and qb % CHUNK == 0
    topk = min(TOPK_MAX, t // 4)
    blk = lambda w: pl.BlockSpec((1, qb, w), lambda bi, i: (bi, i, 0))
    whole = lambda w: pl.BlockSpec((1, t, w), lambda bi, i: (bi, 0, 0))
    return pl.pallas_call(
        functools.partial(_dsa_prompt_kernel, topk=topk),
        grid=(b, t // qb),
        in_specs=[blk(ATT_WIDTH), blk(IDX_WIDTH), blk(LANES), whole(KV_WIDTH),
                  pl.BlockSpec((t // KEY_SUB, KV_WIDTH, KEY_SUB), lambda bi, i: (bi, 0, 0)), whole(IDX_DIM)],
        out_specs=blk(ATT_WIDTH),
        out_shape=jax.ShapeDtypeStruct((b, t, ATT_WIDTH), BF16),
        scratch_shapes=[pltpu.VMEM((t, qb), F32), pltpu.VMEM((ATT_WIDTH, qb), F32),
                        pltpu.VMEM((N_HEADS, qb), F32), pltpu.VMEM((1, LANES), F32),
                        pltpu.VMEM((2, KEY_SUB, N_HEADS * qb), F32)],
        compiler_params=pltpu.CompilerParams(dimension_semantics=("arbitrary", "arbitrary"),
                                             vmem_limit_bytes=VMEM_LIMIT),
        name="dsa_prompt",
    )(q, qi, wi, kb, vt, kib)


def _dsa_sample_kernel(q_ref, qi_ref, wi_ref, kn_ref, vn_ref, kin_ref, ckt_ref, cvt_ref, ckit_ref, o_ref,
                       kcat, vcat_t, kicat, s_scr, *, topk):
    tq = q_ref.shape[1]
    past = ckt_ref.shape[1]
    total = kcat.shape[0]
    pad = total - past - tq
    kcat[0:past, :] = ckt_ref[...].T.astype(BF16)
    kcat[past:past + tq, :] = kn_ref[0]
    kcat[past + tq:total, :] = jnp.zeros((pad, KV_WIDTH), BF16)
    kit_wide = jnp.concatenate([ckit_ref[...], jnp.zeros((LANES - IDX_DIM, past), F32)], axis=0)
    kicat[0:past, :] = kit_wide.T[:, 0:IDX_DIM].astype(BF16)
    kicat[past:past + tq, :] = kin_ref[0]
    kicat[past + tq:total, :] = jnp.zeros((pad, IDX_DIM), BF16)
    vcat_t[:, 0:past] = cvt_ref[...].astype(BF16)
    v_new = jnp.concatenate([vn_ref[0].astype(F32), jnp.zeros((pad, KV_WIDTH), F32)], axis=0)
    vcat_t[:, past:total] = v_new.T.astype(BF16)

    lane = lax.broadcasted_iota(jnp.int32, (1, LANES), 1)
    qi = qi_ref[0]
    qi_stack = jnp.concatenate([qi[:, h * IDX_DIM:(h + 1) * IDX_DIM] for h in range(IDX_HEADS)], axis=0)
    wi_rows = jnp.concatenate([wi_ref[0], jnp.zeros((LANES - tq, LANES), F32)], axis=0).T
    w_flat = jnp.zeros((1, LANES), F32)
    for h in range(IDX_HEADS):
        row = wi_rows[h:h + 1, :]
        shifted = row if h == 0 else pltpu.roll(row, h * tq, 1)
        w_flat = jnp.where(lane // tq == h, shifted, w_flat)

    sc = w_flat * jnp.maximum(_dot_t(kicat[...], qi_stack), 0.0)
    for shift in (LANES // 2, LANES // 4, LANES // 8):
        sc = sc + pltpu.roll(sc, shift, 1)
    sc = jnp.where(lane < tq, sc, 0.0)
    for shift in (LANES // 8, LANES // 4, LANES // 2):
        sc = sc + pltpu.roll(sc, shift, 1)
    key = lax.broadcasted_iota(jnp.int32, (total, LANES), 0)
    ok = key < past + tq
    s_scr[...] = jnp.where(ok, sc, NEG_INF)
    smax = jnp.max(jnp.where(ok, sc, NEG_INF), axis=0, keepdims=True)
    smin = jnp.min(jnp.where(ok, sc, -NEG_INF), axis=0, keepdims=True)
    _select_mask(s_scr, 1, total, smin, smax, jnp.full((1, LANES), float(past + tq), F32), topk)

    q = q_ref[0]
    zero = jnp.zeros((tq, HEAD_DIM), BF16)
    q_stack = jnp.concatenate(
        [jnp.concatenate([q[:, h * HEAD_DIM:(h + 1) * HEAD_DIM], zero] if h < GQA_GROUP
                         else [zero, q[:, h * HEAD_DIM:(h + 1) * HEAD_DIM]], axis=1) for h in range(N_HEADS)], axis=0)
    lg = _dot_t(kcat[...], q_stack) + s_scr[...]
    m = jnp.max(lg, axis=0, keepdims=True)
    p = jnp.exp2(lg - m)
    l = jnp.sum(p, axis=0, keepdims=True)
    o = _dot(vcat_t[...], p.astype(BF16))
    o = jnp.where(lane < GQA_GROUP * tq, o[0:HEAD_DIM, :], o[HEAD_DIM:2 * HEAD_DIM, :]) / l
    o_t = jnp.concatenate([o, jnp.zeros((LANES - HEAD_DIM, LANES), F32)], axis=0).T
    o_ref[0] = jnp.concatenate([o_t[h * tq:(h + 1) * tq, 0:HEAD_DIM] for h in range(N_HEADS)], axis=1).astype(BF16)


def _dsa_sample(q, qi, wi, kb, vb, kib, cache_k, cache_v, cache_kidx, layer):
    b, tq, _ = q.shape
    past = cache_k.shape[3]
    assert tq * N_HEADS == LANES and tq * IDX_HEADS == LANES and past % LANES == 0
    total = past + LANES
    topk = min(TOPK_MAX, (past + tq) // 4)
    blk = lambda w: pl.BlockSpec((1, tq, w), lambda bi: (bi, 0, 0))
    cache = lambda w: pl.BlockSpec((None, None, w, past), lambda bi: (layer, bi, 0, 0))
    return pl.pallas_call(
        functools.partial(_dsa_sample_kernel, topk=topk),
        grid=(b,),
        in_specs=[blk(ATT_WIDTH), blk(IDX_WIDTH), blk(LANES), blk(KV_WIDTH), blk(KV_WIDTH), blk(IDX_DIM),
                  cache(KV_WIDTH), cache(KV_WIDTH), cache(IDX_DIM)],
        out_specs=blk(ATT_WIDTH),
        out_shape=jax.ShapeDtypeStruct((b, tq, ATT_WIDTH), BF16),
        scratch_shapes=[pltpu.VMEM((total, KV_WIDTH), BF16), pltpu.VMEM((KV_WIDTH, total), BF16),
                        pltpu.VMEM((total, IDX_DIM), BF16), pltpu.VMEM((total, LANES), F32)],
        compiler_params=pltpu.CompilerParams(dimension_semantics=("arbitrary",), vmem_limit_bytes=VMEM_LIMIT),
        name="dsa_sample",
    )(q, qi, wi, kb, vb, kib, cache_k, cache_v, cache_kidx)


def _first_lane_where(mask, lane):
    return jnp.min(jnp.where(mask, lane, LANES), axis=1, keepdims=True)


def _post_kernel(att_ref, sa_ref, mb_ref, x_ref, wpa_ref, wout_ref, g2_ref, wr_ref, br_ref,
                 h_ref, xn2_ref, gate_ref, *sort_refs):
    pa = _dot(att_ref[...], wpa_ref[...])
    merged = sa_ref[...].astype(F32) * pa + mb_ref[...].astype(F32)
    h = x_ref[...] + _dot(merged.astype(BF16), wout_ref[...])
    h_ref[...] = h
    xn2 = _rms(h, g2_ref[...])
    xh = xn2.astype(BF16)
    xn2_ref[...] = xh

    xl = (xn2 - xh.astype(F32)).astype(BF16)
    wr = wr_ref[...]
    wh = wr.astype(BF16)
    wl = (wr - wh.astype(F32)).astype(BF16)
    lg = _dot(xh, wh) + (_dot(xl, wh) + _dot(xh, wl)) + br_ref[...]

    lane = lax.broadcasted_iota(jnp.int32, lg.shape, 1)
    is_g = lane < N_GROUPS
    mg = jnp.max(jnp.where(is_g, lg, NEG_INF), axis=1, keepdims=True)
    eg = jnp.where(is_g, jnp.exp(lg - mg), 0.0)
    pg = eg / jnp.sum(eg, axis=1, keepdims=True)
    g_w = jnp.max(pg, axis=1, keepdims=True)
    g_idx = _first_lane_where(jnp.logical_and(is_g, pg == g_w), lane)

    e_lo = ROUTER_E0 + EXP_PER_GROUP * g_idx
    is_e = jnp.logical_and(lane >= e_lo, lane < e_lo + EXP_PER_GROUP)
    me = jnp.max(jnp.where(is_e, lg, NEG_INF), axis=1, keepdims=True)
    ee = jnp.where(is_e, jnp.exp(lg - me), 0.0)
    pe = ee / jnp.sum(ee, axis=1, keepdims=True)
    e1 = jnp.max(pe, axis=1, keepdims=True)
    i1 = _first_lane_where(jnp.logical_and(is_e, pe == e1), lane)
    rest = jnp.logical_and(is_e, lane != i1)
    e2 = jnp.max(jnp.where(rest, pe, -1.0), axis=1, keepdims=True)
    i2 = _first_lane_where(jnp.logical_and(rest, pe == e2), lane)
    den = e1 + e2
    gate = jnp.where(lane == i1, e1 / den, jnp.where(lane == i2, e2 / den, 0.0)) * g_w
    group_gates = [jnp.where(lane < EXP_PER_GROUP, pltpu.roll(gate, LANES - (ROUTER_E0 + EXP_PER_GROUP * g), 1), 0.0)
                   for g in range(N_GROUPS)]
    if not sort_refs:
        for g in range(N_GROUPS):
            gate_ref[:, g * LANES:(g + 1) * LANES] = group_gates[g]
        return

    pt_ref, meta_ref = sort_refs
    gate_ref[...] = (group_gates[0] + group_gates[1]) + (group_gates[2] + group_gates[3])
    tm = lg.shape[0]
    onehot = jnp.where(lane == g_idx, 1.0, 0.0)
    ri = lax.broadcasted_iota(jnp.int32, (tm, tm), 0)
    cj = lax.broadcasted_iota(jnp.int32, (tm, tm), 1)
    earlier = jnp.where(cj < ri, 1.0, 0.0).astype(BF16)
    rank = jnp.sum(_dot(earlier, onehot.astype(BF16)) * onehot, axis=1, keepdims=True)
    cnt = jnp.sum(onehot, axis=0, keepdims=True)
    cnt_pad = jnp.floor((cnt + (SORT_PAD - 1.0)) * (1.0 / SORT_PAD)) * SORT_PAD
    seg_off = pltpu.roll(cnt_pad, 1, 1) + pltpu.roll(cnt_pad, 2, 1) + pltpu.roll(cnt_pad, 3, 1)
    dest = (rank + jnp.sum(onehot * seg_off, axis=1, keepdims=True)).astype(jnp.int32)
    col = lax.broadcasted_iota(jnp.int32, pt_ref.shape, 1)
    pt_ref[...] = jnp.where(col == dest, 1.0, 0.0).astype(BF16)
    meta_ref[0] = jnp.concatenate([cnt_pad, seg_off, jnp.zeros((SUBLANES - 2, LANES), F32)], axis=0)


def _sort_rows(tm):
    return -(-(tm + N_GROUPS * SORT_PAD) // LANES) * LANES


def _post(att, sa, mb, x, w, *, tm, sort):
    n, d = x.shape
    row = lambda t: (t, 0)
    const = lambda t: (0, 0)
    gate_w = LANES if sort else N_GROUPS * LANES
    out_specs = [pl.BlockSpec((tm, d), row), pl.BlockSpec((tm, d), row), pl.BlockSpec((tm, gate_w), row)]
    out_shape = [jax.ShapeDtypeStruct((n, d), F32), jax.ShapeDtypeStruct((n, d), BF16),
                 jax.ShapeDtypeStruct((n, gate_w), F32)]
    if sort:
        out_specs += [pl.BlockSpec((tm, _sort_rows(tm)), row), pl.BlockSpec((1, SUBLANES, LANES), lambda t: (t, 0, 0))]
        out_shape += [jax.ShapeDtypeStruct((n, _sort_rows(tm)), BF16),
                      jax.ShapeDtypeStruct((n // tm, SUBLANES, LANES), F32)]
    return pl.pallas_call(
        _post_kernel,
        grid=(n // tm,),
        in_specs=[pl.BlockSpec((tm, ATT_WIDTH), row), pl.BlockSpec((tm, d), row), pl.BlockSpec((tm, d), row),
                  pl.BlockSpec((tm, d), row), pl.BlockSpec(w["wpa"].shape, const), pl.BlockSpec(w["wout"].shape, const),
                  pl.BlockSpec((1, d), const), pl.BlockSpec(w["wr"].shape, const), pl.BlockSpec((1, ROUTER_LANES), const)],
        out_specs=out_specs,
        out_shape=out_shape,
        compiler_params=pltpu.CompilerParams(dimension_semantics=("arbitrary",), vmem_limit_bytes=VMEM_LIMIT),
        name="post_attn",
    )(att, sa, mb, x, w["wpa"], w["wout"], w["g2"], w["wr"], w["br"])


def _moe_kernel(xn2_ref, gate_ref, h_ref, wg_ref, wu_ref, wd_ref, fn_ref, o_ref, acc_ref, *, final):
    g = pl.program_id(1)

    @pl.when(g == 0)
    def _():
        acc_ref[...] = jnp.zeros_like(acc_ref)

    x = xn2_ref[...]
    gate = gate_ref[...]
    act = jax.nn.silu(_dot(x, wg_ref[0])) * _dot(x, wu_ref[0])
    parts = []
    for e in range(EXP_PER_GROUP):
        parts.append((act[:, e * EXP_HIDDEN:(e + 1) * EXP_HIDDEN] * gate[:, e:e + 1]).astype(BF16))
    acc_ref[...] += _dot(jnp.concatenate(parts, axis=1), wd_ref[0])

    @pl.when(g == N_GROUPS - 1)
    def _():
        y = h_ref[...] + acc_ref[...]
        o_ref[...] = _rms(y, fn_ref[...]) if final else y


def _moe(xn2, gate, h, w, fn, *, tm, final):
    n, d = h.shape
    row = lambda t, g: (t, 0)
    wspec = lambda shp: pl.BlockSpec((1,) + shp[1:], lambda t, g: (g, 0, 0))
    return pl.pallas_call(
        functools.partial(_moe_kernel, final=final),
        grid=(n // tm, N_GROUPS),
        in_specs=[pl.BlockSpec((tm, d), row), pl.BlockSpec((tm, LANES), lambda t, g: (t, g)), pl.BlockSpec((tm, d), row),
                  wspec(w["weg"].shape), wspec(w["weu"].shape), wspec(w["wed"].shape),
                  pl.BlockSpec((1, d), lambda t, g: (0, 0))],
        out_specs=pl.BlockSpec((tm, d), row),
        out_shape=jax.ShapeDtypeStruct((n, d), F32),
        scratch_shapes=[pltpu.VMEM((tm, d), F32)],
        compiler_params=pltpu.CompilerParams(dimension_semantics=("arbitrary", "arbitrary"),
                                             vmem_limit_bytes=VMEM_LIMIT),
        name="moe",
    )(xn2, gate, h, w["weg"], w["weu"], w["wed"], fn)


def _moe_sorted_kernel(meta_ref, xn2_ref, gate_ref, h_ref, pt_ref, wg_ref, wu_ref, wd_ref, fn_ref, o_ref,
                       xs_scr, gs_scr, ys_scr, *, final):
    t = pl.program_id(0)
    rows = pt_ref.shape[1]
    pt = pt_ref[...]
    perm = pt.astype(F32).T.astype(BF16)
    xs_scr[0:rows, :] = _dot(perm, xn2_ref[...]).astype(BF16)
    xs_scr[rows:rows + MOE_CHUNK, :] = jnp.zeros((MOE_CHUNK, xs_scr.shape[1]), BF16)
    gate = gate_ref[...]
    g_hi = gate.astype(BF16)
    g_rest = gate - g_hi.astype(F32)
    g_mid = g_rest.astype(BF16)
    g_lo = (g_rest - g_mid.astype(F32)).astype(BF16)
    gs_scr[0:rows, :] = _dot(perm, g_hi) + (_dot(perm, g_mid) + _dot(perm, g_lo))
    gs_scr[rows:rows + MOE_CHUNK, :] = jnp.zeros((MOE_CHUNK, LANES), F32)
    ys_scr[...] = jnp.zeros_like(ys_scr)

    for g in range(N_GROUPS):
        cnt = meta_ref[t * 2 * N_GROUPS + g]
        seg = meta_ref[t * 2 * N_GROUPS + N_GROUPS + g]

        def chunk(c, carry, g=g, cnt=cnt, seg=seg):
            r0 = pl.multiple_of(seg + c * MOE_CHUNK, SORT_PAD)
            x = xs_scr[pl.ds(r0, MOE_CHUNK), :]
            gt = gs_scr[pl.ds(r0, MOE_CHUNK), :]
            act = jax.nn.silu(_dot(x, wg_ref[g])) * _dot(x, wu_ref[g])
            parts = [(act[:, e * EXP_HIDDEN:(e + 1) * EXP_HIDDEN] * gt[:, e:e + 1]).astype(BF16)
                     for e in range(EXP_PER_GROUP)]
            y = _dot(jnp.concatenate(parts, axis=1), wd_ref[g])
            in_seg = lax.broadcasted_iota(jnp.int32, (MOE_CHUNK, 1), 0) + c * MOE_CHUNK < cnt
            ys_scr[pl.ds(r0, MOE_CHUNK), :] = jnp.where(in_seg, y, ys_scr[pl.ds(r0, MOE_CHUNK), :])
            return carry

        lax.fori_loop(0, (cnt + MOE_CHUNK - 1) // MOE_CHUNK, chunk, 0)

    y = h_ref[...] + _dot(pt, ys_scr[0:rows, :].astype(BF16))
    o_ref[...] = _rms(y, fn_ref[...]) if final else y


def _moe_sorted(xn2, gate, h, pt, meta, w, fn, *, tm, final):
    n, d = h.shape
    rows = pt.shape[1]
    row = lambda t, m: (t, 0)
    resident = lambda shp: pl.BlockSpec(shp, lambda t, m: (0, 0, 0), pipeline_mode=pl.Buffered(1))
    grid_spec = pltpu.PrefetchScalarGridSpec(
        num_scalar_prefetch=1,
        grid=(n // tm,),
        in_specs=[pl.BlockSpec((tm, d), row), pl.BlockSpec((tm, LANES), row), pl.BlockSpec((tm, d), row),
                  pl.BlockSpec((tm, rows), row),
                  resident(w["weg"].shape), resident(w["weu"].shape), resident(w["wed"].shape),
                  pl.BlockSpec((1, d), lambda t, m: (0, 0))],
        out_specs=pl.BlockSpec((tm, d), row),
        scratch_shapes=[pltpu.VMEM((rows + MOE_CHUNK, d), BF16), pltpu.VMEM((rows + MOE_CHUNK, LANES), F32),
                        pltpu.VMEM((rows + MOE_CHUNK, d), F32)],
    )
    return pl.pallas_call(
        functools.partial(_moe_sorted_kernel, final=final),
        grid_spec=grid_spec,
        out_shape=jax.ShapeDtypeStruct((n, d), F32),
        compiler_params=pltpu.CompilerParams(dimension_semantics=("arbitrary",), vmem_limit_bytes=VMEM_LIMIT),
        name="moe_sorted",
    )(meta, xn2, gate, h, pt, w["weg"], w["weu"], w["wed"], fn)


def _pack_experts_kernel(w_ref, o_ref):
    for e in range(EXP_PER_GROUP):
        o_ref[:, e * EXP_HIDDEN:(e + 1) * EXP_HIDDEN] = w_ref[e].astype(BF16)


def _pack_experts(w):
    depth, n_groups, n_exp, d, f = w.shape
    rows = 2 * MXU_DIM
    return pl.pallas_call(
        _pack_experts_kernel,
        grid=(depth, n_groups, d // rows),
        in_specs=[pl.BlockSpec((None, None, n_exp, rows, f), lambda l, g, r: (l, g, 0, r, 0))],
        out_specs=pl.BlockSpec((None, None, rows, n_exp * f), lambda l, g, r: (l, g, r, 0)),
        out_shape=jax.ShapeDtypeStruct((depth, n_groups, d, n_exp * f), BF16),
        compiler_params=pltpu.CompilerParams(dimension_semantics=("arbitrary",) * 3, vmem_limit_bytes=VMEM_LIMIT),
        name="pack_experts",
    )(w)


def _rope_tables(pos, d):
    inv = ROPE_THETA ** (-jnp.arange(0, d, 2, dtype=F32) / d)
    ang = pos[:, None] * inv[None, :]
    cos, sin = jnp.cos(ang), jnp.sin(ang)
    reps = LANES // d
    return (jnp.tile(jnp.concatenate([cos, cos], axis=-1), (1, reps)),
            jnp.tile(jnp.concatenate([-sin, sin], axis=-1), (1, reps)))


def _layer_weights(l, norm1, w_in, sgu_ln_g, sgu_ln_b, sgu_w, sgu_b, w_pa, w_pb, w_out, norm2,
                   w_rg, b_rg, w_re, b_re, w_eg, w_eu, w_ed, sgu_chunk):
    d = w_in.shape[1]
    a_end = ATT_WIDTH + 2 * KV_WIDTH + IDX_WIDTH
    b_end = a_end + IDX_DIM + IDX_HEADS
    reps = SGU_CHUNK // sgu_chunk
    n_router = N_GROUPS + N_GROUPS * EXP_PER_GROUP
    return dict(
        g1=norm1[l][None, :],
        wa=w_in[l][:, :a_end].astype(BF16),
        wb=jnp.pad(w_in[l][:, a_end:b_end], ((0, 0), (0, LANES - (b_end - a_end)))).astype(BF16),
        wc=w_in[l][:, b_end:].astype(BF16),
        sguw=jnp.tile(sgu_w[l][:, :sgu_chunk, :sgu_chunk], (1, reps, reps)),
        sgub=jnp.repeat(jnp.tile(sgu_b[l][:, :sgu_chunk], (1, reps)).T, SGU_GDIM, axis=1),
        lng=sgu_ln_g[l][None, :], lnb=sgu_ln_b[l][None, :],
        wpb=w_pb[l].astype(BF16), wpa=w_pa[l].astype(BF16), wout=w_out[l].astype(BF16),
        g2=norm2[l][None, :],
        wr=jnp.pad(jnp.concatenate([w_rg[l], w_re[l]], axis=1), ((0, 0), (0, ROUTER_LANES - n_router))),
        br=jnp.pad(jnp.concatenate([b_rg[l], b_re[l]]), (0, ROUTER_LANES - n_router))[None, :],
        weg=w_eg[l], weu=w_eu[l],
        wed=w_ed[l].reshape(N_GROUPS, GROUP_HIDDEN, d).astype(BF16),
    )


def _token_tile(n):
    return 512 if n % 512 == 0 else n


def kernel(x_prompt, x_sample, cache_k, cache_v, cache_kidx, norm1, w_in, sgu_ln_g, sgu_ln_b, sgu_w, sgu_b,
           w_pa, w_pb, w_out, norm2, w_rg, b_rg, w_re, b_re, w_eg, w_eu, w_ed, final_norm):
    bp, tp, d = x_prompt.shape
    bs, ts, _ = x_sample.shape
    depth, _, past = cache_k.shape[:3]
    np_, ns = bp * tp, bs * ts
    tm_p, tm_s = _token_tile(tp), _token_tile(ns)
    assert tp % tm_p == 0 and tp % Q_BLOCK == 0 and SGU_CHUNK % ts == 0 and tm_s % SGU_CHUNK == 0
    assert tm_p % KEY_SUB == 0 and tm_s % KEY_SUB == 0

    pos_p = jnp.arange(tp, dtype=F32)
    pos_s = jnp.tile(past + jnp.arange(ts, dtype=F32), tm_s // ts)
    tabs_p = _rope_tables(pos_p, HEAD_DIM) + _rope_tables(pos_p, IDX_DIM)
    tabs_s = _rope_tables(pos_s, HEAD_DIM) + _rope_tables(pos_s, IDX_DIM)
    ckt = jnp.transpose(cache_k, (0, 1, 3, 4, 2)).reshape(depth, bs, KV_WIDTH, past)
    cvt = jnp.transpose(cache_v, (0, 1, 3, 4, 2)).reshape(depth, bs, KV_WIDTH, past)
    ckit = jnp.transpose(cache_kidx, (0, 1, 3, 2))
    fn = final_norm[None, :]
    weg_all, weu_all = _pack_experts(w_eg), _pack_experts(w_eu)

    hp = x_prompt.reshape(np_, d)
    hs = x_sample.reshape(ns, d)
    outs = [[] for _ in range(4)]
    kv_bufs = None
    for l in range(depth):
        wargs = (norm1, w_in, sgu_ln_g, sgu_ln_b, sgu_w, sgu_b, w_pa, w_pb, w_out, norm2,
                 w_rg, b_rg, w_re, b_re, weg_all, weu_all, w_ed)
        wp = _layer_weights(l, *wargs, sgu_chunk=SGU_CHUNK)
        ws = dict(wp)
        ws.update({k: v for k, v in _layer_weights(l, *wargs, sgu_chunk=ts).items() if k in ("sguw", "sgub")})
        final = l == depth - 1

        q, kt_all, kb, vt_all, _, vt, qi, kit_all, kib, wi, mb, sa = _inproj(
            hp, wp, tabs_p, tm=tm_p, sgu_chunk=SGU_CHUNK, want_vln=False,
            kv_layer=l, kv_bufs=kv_bufs, kv_shape=(depth, bp, tp))
        kv_bufs = (kt_all, vt_all, kit_all)
        r3 = lambda a: a.reshape(bp, tp, a.shape[-1])
        att = _dsa_prompt(r3(q), r3(qi), r3(wi), r3(kb), vt, r3(kib)).reshape(np_, ATT_WIDTH)
        h, xn2, gate, pt, meta = _post(att, sa, mb, hp, wp, tm=tm_p, sort=True)
        meta = meta[:, 0:2, 0:N_GROUPS].astype(jnp.int32).reshape(-1)
        hp = _moe_sorted(xn2, gate, h, pt, meta, wp, fn, tm=tm_p, final=final)

        q, kf, kb, vf, vb, _, qi, kif, kib, wi, mb, sa, vln = _inproj(hs, ws, tabs_s, tm=tm_s, sgu_chunk=ts,
                                                                      want_vln=True)
        r3 = lambda a: a.reshape(bs, ts, a.shape[-1])
        att = _dsa_sample(r3(q), r3(qi), r3(wi), r3(kb), r3(vb), r3(kib), ckt, cvt, ckit, l)
        h, xn2, gate = _post(att.reshape(ns, ATT_WIDTH), sa, mb, hs, ws, tm=tm_s, sort=False)
        hs = _moe(xn2, gate, h, ws, fn, tm=tm_s, final=final)
        outs[0].append(kf.reshape(bs, ts, N_KV_HEADS, HEAD_DIM))
        outs[1].append(vf.reshape(bs, ts, N_KV_HEADS, HEAD_DIM))
        outs[2].append(kif.reshape(bs, ts, IDX_DIM))
        outs[3].append(vln.reshape(bs, ts, SGU_WIDTH))

    kt_all, vt_all, kit_all = kv_bufs
    heads = lambda a: jnp.transpose(a.reshape(depth, bp, N_KV_HEADS, HEAD_DIM, tp), (0, 1, 4, 2, 3))
    return ((hp.reshape(bp, tp, d), hs.reshape(bs, ts, d), heads(kt_all), heads(vt_all),
             jnp.transpose(kit_all, (0, 1, 3, 2))) + tuple(jnp.stack(o) for o in outs))
```

```python
import functools

import jax
import jax.numpy as jnp
from jax import lax
from jax.experimental import pallas as pl
from jax.experimental.pallas import tpu as pltpu

F32 = jnp.float32
BF16 = jnp.bfloat16

CHUNK = 64
N_HEADS = 8
N_KV_HEADS = 2
HEAD_DIM = 64
GQA_GROUP = N_HEADS // N_KV_HEADS
ATT_WIDTH = N_HEADS * HEAD_DIM
KV_WIDTH = N_KV_HEADS * HEAD_DIM
IDX_HEADS = 8
IDX_DIM = 32
IDX_WIDTH = IDX_HEADS * IDX_DIM
TOPK_MAX = 256
Q_BLOCK = 128
DSA_QUERIES = 256
SGU_GROUPS = 4
SGU_CHUNK = 128
SGU_WIDTH = 512
SGU_GDIM = SGU_WIDTH // SGU_GROUPS
N_GROUPS = 4
EXP_PER_GROUP = 8
EXP_HIDDEN = 128
GROUP_HIDDEN = EXP_PER_GROUP * EXP_HIDDEN
ROPE_THETA = 10000.0
EPS = 1e-6
Q_SCALE = HEAD_DIM ** -0.5 * 1.4426950408889634

LANES = 128
SUBLANES = 8
MXU_DIM = 256
ROUTER_LANES = LANES
ROUTER_E0 = N_GROUPS
VMEM_LIMIT = 56 * 1024 * 1024
KEY_TILE = 2 * MXU_DIM
KEY_SUB = MXU_DIM
FOLD_WAYS = 4
COUNT_ROWS = 64
SEARCH_FIRST_STEPS = 13
SEARCH_SECOND_STEPS = 1
SEARCH_ROUND_STEPS = 1
SEARCH_MAX_ROUNDS = 34
SORT_PAD = 16
MOE_CHUNK = 144
KNORM_SLACK = 1.0 + 2.0 ** -6
SOFTMAX_SUM_FLOOR = 2.0 ** -80
NEG_INF = float("-inf")


def _dot(a, b):
    return jnp.dot(a, b, preferred_element_type=F32)


def _dot_t(a, b):
    return lax.dot_general(a, b, (((1,), (1,)), ((), ())), preferred_element_type=F32)


def _rms(x, g):
    return x * lax.rsqrt(jnp.mean(x * x, axis=-1, keepdims=True) + EPS) * g


def _swap_halves(x, d):
    lane = lax.broadcasted_iota(jnp.int32, x.shape, 1)
    first = (lane % d) < (d // 2)
    return jnp.where(first, pltpu.roll(x, LANES - d // 2, 1), pltpu.roll(x, d // 2, 1))


def _rope(x, cos, sin_signed, d):
    return x * cos + _swap_halves(x, d) * sin_signed


def _inproj_kernel(*refs, sgu_chunk, kv_transposed, n_alias, want_vln):
    (x_ref, g1_ref, wa_ref, wb_ref, wc_ref, cq_ref, sq_ref, ci_ref, si_ref,
     sguw_ref, sgub_ref, lng_ref, lnb_ref, wpb_ref) = refs[:14]
    (q_ref, kf_ref, kb_ref, vf_ref, vb_ref, vt_ref, qi_ref, kif_ref, kib_ref, wi_ref,
     mb_ref, sa_ref) = refs[14 + n_alias:26 + n_alias]
    tm = x_ref.shape[0]
    xn = _rms(x_ref[...], g1_ref[...]).astype(BF16)
    cq, sq, ci, si = cq_ref[...], sq_ref[...], ci_ref[...], si_ref[...]

    a = _dot(xn, wa_ref[...])
    def emit_queries(ref, val):
        if kv_transposed:
            for c in range(tm // DSA_QUERIES):
                ref[c] = val[c * DSA_QUERIES:(c + 1) * DSA_QUERIES, :].T[0:ref.shape[1], :].astype(ref.dtype)
        else:
            ref[...] = val.astype(ref.dtype)

    emit_queries(q_ref, jnp.concatenate(
        [_rope(a[:, c * LANES:(c + 1) * LANES], cq, sq, HEAD_DIM) * Q_SCALE for c in range(ATT_WIDTH // LANES)], axis=1))
    k = _rope(a[:, ATT_WIDTH:ATT_WIDTH + KV_WIDTH], cq, sq, HEAD_DIM)
    kb_ref[...] = k.astype(BF16)
    v = a[:, ATT_WIDTH + KV_WIDTH:ATT_WIDTH + 2 * KV_WIDTH]
    vb_ref[...] = v.astype(BF16)
    if kv_transposed:
        kf_ref[...] = k.T
        vf_ref[...] = v.T
    else:
        kf_ref[...] = k
        vf_ref[...] = v
    for c in range(tm // KEY_SUB):
        vt_ref[c] = v[c * KEY_SUB:(c + 1) * KEY_SUB, :].T.astype(BF16)
    qi0 = ATT_WIDTH + 2 * KV_WIDTH
    emit_queries(qi_ref, jnp.concatenate(
        [_rope(a[:, qi0 + c * LANES:qi0 + (c + 1) * LANES], ci, si, IDX_DIM) for c in range(IDX_WIDTH // LANES)], axis=1))

    b = _dot(xn, wb_ref[...])
    ki_wide = _rope(b, ci, si, IDX_DIM)
    ki = ki_wide[:, :IDX_DIM]
    kif_ref[...] = ki_wide.T[0:IDX_DIM, :] if kv_transposed else ki
    kib_ref[...] = ki.astype(BF16)
    emit_queries(wi_ref, pltpu.roll(b, LANES - IDX_DIM, 1) * (IDX_HEADS ** -0.5))

    u = jax.nn.gelu(_dot(xn, wc_ref[:, 0:SGU_WIDTH]))
    vs = jax.nn.gelu(_dot(xn, wc_ref[:, SGU_WIDTH:2 * SGU_WIDTH]))
    mu = jnp.mean(vs, axis=-1, keepdims=True)
    var = jnp.mean(jnp.square(vs - mu), axis=-1, keepdims=True)
    vln = (vs - mu) * lax.rsqrt(var + EPS) * lng_ref[...] + lnb_ref[...]
    if want_vln:
        refs[26 + n_alias][...] = vln
    ri = lax.broadcasted_iota(jnp.int32, (SGU_CHUNK, SGU_CHUNK), 0)
    cj = lax.broadcasted_iota(jnp.int32, (SGU_CHUNK, SGU_CHUNK), 1)
    mix_mask = (cj <= ri) & ((ri // sgu_chunk) == (cj // sgu_chunk))
    vln_b = vln.astype(BF16)
    w_eff = [jnp.where(mix_mask, sguw_ref[g], 0.0).astype(BF16) for g in range(SGU_GROUPS)]
    sg_rows = []
    for r in range(tm // SGU_CHUNK):
        rows = slice(r * SGU_CHUNK, (r + 1) * SGU_CHUNK)
        mixed = [_dot(w_eff[g], vln_b[rows, g * SGU_GDIM:(g + 1) * SGU_GDIM]) for g in range(SGU_GROUPS)]
        sg_rows.append(u[rows, :] * (jnp.concatenate(mixed, axis=1) + sgub_ref[...]))
    sg = jnp.concatenate(sg_rows, axis=0).astype(BF16)
    d = x_ref.shape[1]
    sa_ref[...] = jax.nn.sigmoid(_dot(xn, wc_ref[:, 2 * SGU_WIDTH:2 * SGU_WIDTH + d])).astype(BF16)
    gb = jax.nn.sigmoid(_dot(xn, wc_ref[:, 2 * SGU_WIDTH + d:]))
    mb_ref[...] = (gb * _dot(sg, wpb_ref[...])).astype(BF16)


def _inproj(x, w, tabs, *, tm, sgu_chunk, want_vln, kv_layer=None, kv_bufs=None, kv_shape=None):
    n, d = x.shape
    n_tab_tiles = tabs[0].shape[0] // tm
    row = lambda t: (t, 0)
    const = lambda t: (0, 0)
    tab = lambda t: (t % n_tab_tiles, 0)
    kv_transposed = kv_layer is not None
    if kv_transposed:
        depth, streams, steps = kv_shape
        per_stream = steps // tm
        kv_sds = lambda wd: jax.ShapeDtypeStruct((depth, streams, wd, steps), F32)
        kv_spec = lambda wd: pl.BlockSpec((None, None, wd, tm),
                                          lambda t: (kv_layer, t // per_stream, 0, t % per_stream))
    else:
        kv_sds = lambda wd: jax.ShapeDtypeStruct((n, wd), F32)
        kv_spec = lambda wd: pl.BlockSpec((tm, wd), row)
    tiled = lambda wd, dt: (jax.ShapeDtypeStruct((n, wd), dt), pl.BlockSpec((tm, wd), row))
    if kv_transposed:
        qb = DSA_QUERIES
        query = lambda wd, rows, dt: (jax.ShapeDtypeStruct((n // qb, rows, qb), dt),
                                      pl.BlockSpec((tm // qb, rows, qb), lambda t: (t, 0, 0)))
    else:
        query = lambda wd, rows, dt: tiled(wd, dt)
    outs = [
        query(ATT_WIDTH, ATT_WIDTH, BF16),
        (kv_sds(KV_WIDTH), kv_spec(KV_WIDTH)),
        tiled(KV_WIDTH, BF16),
        (kv_sds(KV_WIDTH), kv_spec(KV_WIDTH)),
        tiled(KV_WIDTH, BF16),
        (jax.ShapeDtypeStruct((n // KEY_SUB, KV_WIDTH, KEY_SUB), BF16),
         pl.BlockSpec((tm // KEY_SUB, KV_WIDTH, KEY_SUB), lambda t: (t, 0, 0))),
        query(IDX_WIDTH, IDX_WIDTH, BF16),
        (kv_sds(IDX_DIM), kv_spec(IDX_DIM)),
        tiled(IDX_DIM, BF16),
        query(LANES, IDX_HEADS, F32),
        tiled(d, BF16),
        tiled(d, BF16),
    ]
    if want_vln:
        outs.append(tiled(SGU_WIDTH, F32))
    in_specs = [
        pl.BlockSpec((tm, d), row),
        pl.BlockSpec((1, d), const),
        pl.BlockSpec(w["wa"].shape, const),
        pl.BlockSpec(w["wb"].shape, const),
        pl.BlockSpec(w["wc"].shape, const),
        pl.BlockSpec((tm, LANES), tab), pl.BlockSpec((tm, LANES), tab),
        pl.BlockSpec((tm, LANES), tab), pl.BlockSpec((tm, LANES), tab),
        pl.BlockSpec(w["sguw"].shape, lambda t: (0, 0, 0)),
        pl.BlockSpec(w["sgub"].shape, const),
        pl.BlockSpec((1, SGU_WIDTH), const),
        pl.BlockSpec((1, SGU_WIDTH), const),
        pl.BlockSpec(w["wpb"].shape, const),
    ]
    args = [x, w["g1"], w["wa"], w["wb"], w["wc"], *tabs, w["sguw"], w["sgub"], w["lng"], w["lnb"], w["wpb"]]
    aliases = {}
    if kv_bufs is not None:
        for buf, out_idx in zip(kv_bufs, (1, 3, 7)):
            aliases[len(args)] = out_idx
            in_specs.append(pl.BlockSpec(memory_space=pl.ANY))
            args.append(buf)
    return pl.pallas_call(
        functools.partial(_inproj_kernel, sgu_chunk=sgu_chunk, kv_transposed=kv_transposed,
                          n_alias=len(aliases), want_vln=want_vln),
        grid=(n // tm,),
        in_specs=in_specs,
        out_specs=[o[1] for o in outs],
        out_shape=[o[0] for o in outs],
        input_output_aliases=aliases,
        compiler_params=pltpu.CompilerParams(dimension_semantics=("arbitrary",), vmem_limit_bytes=VMEM_LIMIT),
        name="inproj",
    )(*args)


def _float_key(x):
    b = lax.bitcast_convert_type(x, jnp.int32)
    return b ^ ((b >> 31) & jnp.int32(0x7FFFFFFF))


def _key_float(k):
    return lax.bitcast_convert_type(k ^ ((k >> 31) & jnp.int32(0x7FFFFFFF)), F32)


def _fold_rows(x):
    n = x.shape[0] // SUBLANES
    ways = FOLD_WAYS if n % FOLD_WAYS == 0 else 1
    part = jnp.sum(x.reshape(n // ways, ways, SUBLANES, x.shape[1]), axis=0)
    return jnp.sum(part, axis=0)


def _topk_threshold(count_ge, snap, smin, smax, n_adm, kk):
    lo0 = _float_key(smin + 0.0)
    hx0 = _float_key(smax + 0.0) + 1

    def is_active(st):
        lo, hx, c_lo, _ = st
        return jnp.logical_and(c_lo != kk, lo + 1 < hx)

    def step(st, by_key):
        lo, hx, c_lo, c_hx = st
        active = is_active(st)
        if by_key:
            mid = (lo >> 1) + (hx >> 1) + (lo & hx & 1)
        else:
            mid = _float_key(0.5 * _key_float(lo) + 0.5 * _key_float(hx))
        mid = jnp.where(active, jnp.minimum(jnp.maximum(mid, lo + 1), hx - 1), lo)
        c = count_ge(_key_float(mid))
        up = jnp.logical_and(active, c >= kk)
        down = jnp.logical_and(active, c < kk)
        return (jnp.where(up, mid, lo), jnp.where(down, mid, hx),
                jnp.where(up, c, c_lo), jnp.where(down, c, c_hx))

    def snap_step(st):
        lo, hx, c_lo, c_hx = st
        active = is_active(st)
        first_in, last_in = snap(_key_float(lo), _key_float(hx))
        return (jnp.where(active, _float_key(first_in + 0.0), lo),
                jnp.where(active, _float_key(last_in + 0.0) + 1, hx), c_lo, c_hx)

    def n_active(st):
        return jnp.max(jnp.where(is_active(st), 1.0, 0.0))

    st = (lo0, hx0, n_adm, jnp.zeros_like(n_adm))
    for _ in range(SEARCH_FIRST_STEPS):
        st = step(st, by_key=False)
    st = snap_step(st)
    for _ in range(SEARCH_SECOND_STEPS):
        st = step(st, by_key=False)
    act = n_active(st)

    def cond(carry):
        it, act, _ = carry
        return jnp.logical_and(it < SEARCH_MAX_ROUNDS, act > 0.5)

    def body(carry):
        it, _, st = carry
        st = step(snap_step(st), by_key=True)
        for _ in range(SEARCH_ROUND_STEPS):
            st = step(st, by_key=False)
        return it + 1, n_active(st), st

    _, _, (lo, _, c_lo, c_hx) = lax.while_loop(cond, body, (jnp.int32(0), act, st))
    return _key_float(lo), c_lo, c_hx


def _demote_excess_ties(s_ref, n_chunks, thr, need):
    ri = lax.broadcasted_iota(jnp.int32, (LANES, LANES), 0)
    cj = lax.broadcasted_iota(jnp.int32, (LANES, LANES), 1)
    lower = jnp.where(cj <= ri, 1.0, 0.0).astype(BF16)

    def body(j, run):
        off = pl.multiple_of(j * LANES, LANES)
        blk = s_ref[pl.ds(off, LANES), :]
        tied = blk == thr
        tied_f = jnp.where(tied, 1.0, 0.0)
        prefix = _dot(lower, tied_f.astype(BF16)) + run
        s_ref[pl.ds(off, LANES), :] = jnp.where(jnp.logical_and(tied, prefix > need), NEG_INF, blk)
        return run + jnp.sum(tied_f, axis=0, keepdims=True)

    lax.fori_loop(0, n_chunks, body, jnp.zeros((1, s_ref.shape[1]), F32))


def _select_mask(s_ref, n_tiles, tk, smin, smax, n_adm, topk):
    kk = jnp.minimum(n_adm, float(topk))

    def count_ge(t):
        def body(j, acc):
            off = pl.multiple_of(j * tk, tk)
            for r in range(0, tk, COUNT_ROWS):
                acc = acc + _fold_rows(jnp.where(s_ref[pl.ds(off + r, COUNT_ROWS), :] >= t, 1.0, 0.0))
            return acc
        acc = lax.fori_loop(0, n_tiles, body, jnp.zeros((SUBLANES, s_ref.shape[1]), F32))
        return jnp.sum(acc, axis=0, keepdims=True)

    def snap(lo_f, hx_f):
        def body(j, carry):
            first_in, last_in = carry
            off = pl.multiple_of(j * tk, tk)
            for r in range(0, tk, COUNT_ROWS):
                blk = s_ref[pl.ds(off + r, COUNT_ROWS), :]
                first_in = jnp.minimum(first_in, -_fold_max(jnp.where(blk >= lo_f, -blk, NEG_INF)))
                last_in = jnp.maximum(last_in, _fold_max(jnp.where(blk < hx_f, blk, NEG_INF)))
            return first_in, last_in
        first_in, last_in = lax.fori_loop(0, n_tiles, body,
                                          (jnp.full((SUBLANES, s_ref.shape[1]), -NEG_INF, F32),
                                           jnp.full((SUBLANES, s_ref.shape[1]), NEG_INF, F32)))
        return jnp.min(first_in, axis=0, keepdims=True), jnp.max(last_in, axis=0, keepdims=True)

    thr, c_ge, c_gt = _topk_threshold(count_ge, snap, smin, smax, n_adm, kk)

    @pl.when(jnp.max(c_ge - kk) > 0.5)
    def _():
        _demote_excess_ties(s_ref, n_tiles * (tk // LANES), thr, kk - c_gt)

    def mask_body(j, carry):
        off = pl.multiple_of(j * tk, tk)
        s_ref[pl.ds(off, tk), :] = jnp.where(s_ref[pl.ds(off, tk), :] >= thr, 0.0, NEG_INF)
        return carry

    lax.fori_loop(0, n_tiles, mask_body, 0)


def _dsa_prompt_kernel(q_ref, qi_ref, wi_ref, k_ref, vt_ref, ki_ref, o_ref, s_scr, acc_scr, l_scr, kn_scr, lg_scr,
                       *, topk):
    i = pl.program_id(1)
    tk, ts = KEY_TILE, KEY_SUB
    qb = q_ref.shape[2]
    n_keys = (i + 1) * qb
    n_tiles = (n_keys + tk - 1) // tk
    qcol = lax.broadcasted_iota(jnp.int32, (1, qb), 1)
    lim = ((i * qb + qcol) // CHUNK + 1) * CHUNK

    @pl.when(i == 0)
    def _():
        ri = lax.broadcasted_iota(jnp.int32, (KV_WIDTH, LANES), 0)
        cj = lax.broadcasted_iota(jnp.int32, (KV_WIDTH, LANES), 1)
        head_sum = jnp.where(ri // HEAD_DIM == cj, 1.0, 0.0).astype(BF16)

        def body(j, acc):
            kt = k_ref[0, pl.ds(pl.multiple_of(j * tk, tk), tk), :].astype(F32)
            return jnp.maximum(acc, _fold_max(_dot((kt * kt).astype(BF16), head_sum)))

        acc = lax.fori_loop(0, k_ref.shape[1] // tk, body, jnp.zeros((SUBLANES, LANES), F32))
        kn_scr[...] = jnp.max(acc, axis=0, keepdims=True) * KNORM_SLACK

    qt = q_ref[0]
    qit = qi_ref[0]
    wit = wi_ref[0]
    qi_rhs = jnp.concatenate([qit[h * IDX_DIM:(h + 1) * IDX_DIM, :] for h in range(IDX_HEADS)], axis=1)
    zero = jnp.zeros((HEAD_DIM, qb), BF16)
    q_cols = []
    for h in range(N_HEADS):
        qh = qt[h * HEAD_DIM:(h + 1) * HEAD_DIM, :]
        q_cols.append(jnp.concatenate([qh, zero] if h < GQA_GROUP else [zero, qh], axis=0))
    q_rhs = jnp.concatenate(q_cols, axis=1)

    def score_tile(j, carry):
        smax, smin = carry
        for c in range(tk // ts):
            off = pl.multiple_of(j * tk + c * ts, ts)
            r = _dot(ki_ref[0, pl.ds(off, ts), :], qi_rhs)
            acc = wit[0:1, :] * jnp.maximum(r[:, 0:qb], 0.0)
            for h in range(1, IDX_HEADS):
                acc = acc + wit[h:h + 1, :] * jnp.maximum(r[:, h * qb:(h + 1) * qb], 0.0)
            key = off + lax.broadcasted_iota(jnp.int32, (ts, qb), 0)
            ok = key < lim
            s_scr[pl.ds(off, ts), :] = jnp.where(ok, acc, NEG_INF)
            smax = jnp.maximum(smax, _fold_max(jnp.where(ok, acc, NEG_INF)))
            smin = jnp.minimum(smin, -_fold_max(jnp.where(ok, -acc, NEG_INF)))
        return smax, smin

    smax, smin = lax.fori_loop(0, n_tiles, score_tile,
                               (jnp.full((SUBLANES, qb), NEG_INF, F32), jnp.full((SUBLANES, qb), -NEG_INF, F32)))
    smax = jnp.max(smax, axis=0, keepdims=True)
    smin = jnp.min(smin, axis=0, keepdims=True)
    _select_mask(s_scr, n_tiles, tk, smin, smax, lim.astype(F32), topk)

    qsq = jnp.square(qt.astype(F32))
    kn = kn_scr[...]
    lane1 = lax.broadcasted_iota(jnp.int32, (1, LANES), 1)
    shift_rows = []
    for h in range(N_HEADS):
        qn2 = jnp.sum(_fold_rows(qsq[h * HEAD_DIM:(h + 1) * HEAD_DIM, :]), axis=0, keepdims=True)
        kmax2 = jnp.max(jnp.where(lane1 == h // GQA_GROUP, kn, 0.0))
        shift_rows.append(jnp.sqrt(qn2 * kmax2))
    shift = jnp.concatenate(shift_rows, axis=0)

    def pv_accumulate(vt, ps, alphas):
        for pair in range(N_HEADS // 2):
            g = (2 * pair) // GQA_GROUP
            o = _dot(vt[g * HEAD_DIM:(g + 1) * HEAD_DIM, :], jnp.concatenate(ps[2 * pair:2 * pair + 2], axis=1))
            for u in range(2):
                h = 2 * pair + u
                rows = slice(h * HEAD_DIM, (h + 1) * HEAD_DIM)
                prev = acc_scr[rows, :] if alphas is None else alphas[h] * acc_scr[rows, :]
                acc_scr[rows, :] = prev + o[:, u * qb:(u + 1) * qb]

    def logits_into(buf, sub):
        off = pl.multiple_of(jnp.minimum(sub * ts, k_ref.shape[1] - ts), ts)
        lg_scr[buf] = _dot(k_ref[0, pl.ds(off, ts), :], q_rhs)

    def softmax_pv(buf, sub, l):
        bias = s_scr[pl.ds(pl.multiple_of(sub * ts, ts), ts), :]
        l_rows, ps = [], []
        for h in range(N_HEADS):
            p = jnp.exp2(lg_scr[buf, :, h * qb:(h + 1) * qb] + (bias - shift[h:h + 1, :]))
            l_rows.append(l[h * SUBLANES:(h + 1) * SUBLANES, :] + _fold_rows(p))
            ps.append(p.astype(BF16))
        pv_accumulate(vt_ref[sub], ps, None)
        return jnp.concatenate(l_rows, axis=0)

    def attn_bounded(j, l):
        logits_into(1, 2 * j + 1)
        l = softmax_pv(0, 2 * j, l)
        logits_into(0, 2 * j + 2)
        return softmax_pv(1, 2 * j + 1, l)

    def attn_online(j, carry):
        m, l = carry
        off = pl.multiple_of(j * ts, ts)
        lg = _dot(k_ref[0, pl.ds(off, ts), :], q_rhs)
        bias = s_scr[pl.ds(off, ts), :]
        m_rows, l_rows, alphas, ps = [], [], [], []
        for h in range(N_HEADS):
            x = lg[:, h * qb:(h + 1) * qb] + bias
            m_new = jnp.maximum(m[h:h + 1, :], jnp.max(_fold_max(x), axis=0, keepdims=True))
            m_use = jnp.where(m_new > NEG_INF, m_new, 0.0)
            alpha = jnp.exp2(m[h:h + 1, :] - m_use)
            p = jnp.exp2(x - m_use)
            l_rows.append(alpha * l[h:h + 1, :] + jnp.sum(_fold_rows(p), axis=0, keepdims=True))
            m_rows.append(m_new)
            alphas.append(alpha)
            ps.append(p.astype(BF16))
        pv_accumulate(vt_ref[j], ps, alphas)
        return jnp.concatenate(m_rows, axis=0), jnp.concatenate(l_rows, axis=0)

    n_sub = (n_keys + ts - 1) // ts
    acc_scr[...] = jnp.zeros_like(acc_scr)
    logits_into(0, 0)
    l_part = lax.fori_loop(0, n_tiles, attn_bounded, jnp.zeros((N_HEADS * SUBLANES, qb), F32))
    l_fast = jnp.concatenate([jnp.sum(l_part[h * SUBLANES:(h + 1) * SUBLANES, :], axis=0, keepdims=True)
                              for h in range(N_HEADS)], axis=0)
    l_scr[...] = l_fast

    @pl.when(jnp.min(l_fast) < SOFTMAX_SUM_FLOOR)
    def _():
        acc_scr[...] = jnp.zeros_like(acc_scr)
        _, l_exact = lax.fori_loop(0, n_sub, attn_online,
                                   (jnp.full((N_HEADS, qb), NEG_INF, F32), jnp.zeros((N_HEADS, qb), F32)))
        l_scr[...] = l_exact

    l = l_scr[...]
    out_t = jnp.concatenate([acc_scr[h * HEAD_DIM:(h + 1) * HEAD_DIM, :] / l[h:h + 1, :] for h in range(N_HEADS)],
                            axis=0)
    o_ref[0] = out_t.T.astype(BF16)


def _fold_max(x):
    n = x.shape[0] // SUBLANES
    ways = FOLD_WAYS if n % FOLD_WAYS == 0 else 1
    part = jnp.max(x.reshape(n // ways, ways, SUBLANES, x.shape[1]), axis=0)
    return jnp.max(part, axis=0)


def _dsa_prompt(q, qi, wi, kb, vt, kib):
    b, t, _ = kb.shape
    qb = DSA_QUERIES
    assert t % KEY_TILE == 0 and t % qb == 0 and qb % CHUNK == 0
    topk = min(TOPK_MAX, t // 4)
    blk = lambda w: pl.BlockSpec((1, qb, w), lambda bi, i: (bi, i, 0))
    qblk = lambda rows: pl.BlockSpec((1, rows, qb), lambda bi, i: (bi * (t // qb) + i, 0, 0))
    whole = lambda w: pl.BlockSpec((1, t, w), lambda bi, i: (bi, 0, 0))
    return pl.pallas_call(
        functools.partial(_dsa_prompt_kernel, topk=topk),
        grid=(b, t // qb),
        in_specs=[qblk(ATT_WIDTH), qblk(IDX_WIDTH), qblk(IDX_HEADS), whole(KV_WIDTH),
                  pl.BlockSpec((t // KEY_SUB, KV_WIDTH, KEY_SUB), lambda bi, i: (bi, 0, 0)), whole(IDX_DIM)],
        out_specs=blk(ATT_WIDTH),
        out_shape=jax.ShapeDtypeStruct((b, t, ATT_WIDTH), BF16),
        scratch_shapes=[pltpu.VMEM((t, qb), F32), pltpu.VMEM((ATT_WIDTH, qb), F32),
                        pltpu.VMEM((N_HEADS, qb), F32), pltpu.VMEM((1, LANES), F32),
                        pltpu.VMEM((2, KEY_SUB, N_HEADS * qb), F32)],
        compiler_params=pltpu.CompilerParams(dimension_semantics=("arbitrary", "arbitrary"),
                                             vmem_limit_bytes=VMEM_LIMIT),
        name="dsa_prompt",
    )(q, qi, wi, kb, vt, kib)


def _dsa_sample_kernel(q_ref, qi_ref, wi_ref, kn_ref, vn_ref, kin_ref, ckt_ref, cvt_ref, ckit_ref, o_ref,
                       kcat, vcat_t, kicat, s_scr, *, topk):
    tq = q_ref.shape[1]
    past = ckt_ref.shape[1]
    total = kcat.shape[0]
    pad = total - past - tq
    kcat[0:past, :] = ckt_ref[...].T.astype(BF16)
    kcat[past:past + tq, :] = kn_ref[0]
    kcat[past + tq:total, :] = jnp.zeros((pad, KV_WIDTH), BF16)
    kit_wide = jnp.concatenate([ckit_ref[...], jnp.zeros((LANES - IDX_DIM, past), F32)], axis=0)
    kicat[0:past, :] = kit_wide.T[:, 0:IDX_DIM].astype(BF16)
    kicat[past:past + tq, :] = kin_ref[0]
    kicat[past + tq:total, :] = jnp.zeros((pad, IDX_DIM), BF16)
    vcat_t[:, 0:past] = cvt_ref[...].astype(BF16)
    v_new = jnp.concatenate([vn_ref[0].astype(F32), jnp.zeros((pad, KV_WIDTH), F32)], axis=0)
    vcat_t[:, past:total] = v_new.T.astype(BF16)

    lane = lax.broadcasted_iota(jnp.int32, (1, LANES), 1)
    qi = qi_ref[0]
    qi_stack = jnp.concatenate([qi[:, h * IDX_DIM:(h + 1) * IDX_DIM] for h in range(IDX_HEADS)], axis=0)
    wi_rows = jnp.concatenate([wi_ref[0], jnp.zeros((LANES - tq, LANES), F32)], axis=0).T
    w_flat = jnp.zeros((1, LANES), F32)
    for h in range(IDX_HEADS):
        row = wi_rows[h:h + 1, :]
        shifted = row if h == 0 else pltpu.roll(row, h * tq, 1)
        w_flat = jnp.where(lane // tq == h, shifted, w_flat)

    sc = w_flat * jnp.maximum(_dot_t(kicat[...], qi_stack), 0.0)
    for shift in (LANES // 2, LANES // 4, LANES // 8):
        sc = sc + pltpu.roll(sc, shift, 1)
    sc = jnp.where(lane < tq, sc, 0.0)
    for shift in (LANES // 8, LANES // 4, LANES // 2):
        sc = sc + pltpu.roll(sc, shift, 1)
    key = lax.broadcasted_iota(jnp.int32, (total, LANES), 0)
    ok = key < past + tq
    s_scr[...] = jnp.where(ok, sc, NEG_INF)
    smax = jnp.max(jnp.where(ok, sc, NEG_INF), axis=0, keepdims=True)
    smin = jnp.min(jnp.where(ok, sc, -NEG_INF), axis=0, keepdims=True)
    _select_mask(s_scr, 1, total, smin, smax, jnp.full((1, LANES), float(past + tq), F32), topk)

    q = q_ref[0]
    zero = jnp.zeros((tq, HEAD_DIM), BF16)
    q_stack = jnp.concatenate(
        [jnp.concatenate([q[:, h * HEAD_DIM:(h + 1) * HEAD_DIM], zero] if h < GQA_GROUP
                         else [zero, q[:, h * HEAD_DIM:(h + 1) * HEAD_DIM]], axis=1) for h in range(N_HEADS)], axis=0)
    lg = _dot_t(kcat[...], q_stack) + s_scr[...]
    m = jnp.max(lg, axis=0, keepdims=True)
    p = jnp.exp2(lg - m)
    l = jnp.sum(p, axis=0, keepdims=True)
    o = _dot(vcat_t[...], p.astype(BF16))
    o = jnp.where(lane < GQA_GROUP * tq, o[0:HEAD_DIM, :], o[HEAD_DIM:2 * HEAD_DIM, :]) / l
    o_t = jnp.concatenate([o, jnp.zeros((LANES - HEAD_DIM, LANES), F32)], axis=0).T
    o_ref[0] = jnp.concatenate([o_t[h * tq:(h + 1) * tq, 0:HEAD_DIM] for h in range(N_HEADS)], axis=1).astype(BF16)


def _dsa_sample(q, qi, wi, kb, vb, kib, cache_k, cache_v, cache_kidx, layer):
    b, tq, _ = q.shape
    past = cache_k.shape[3]
    assert tq * N_HEADS == LANES and tq * IDX_HEADS == LANES and past % LANES == 0
    total = past + LANES
    topk = min(TOPK_MAX, (past + tq) // 4)
    blk = lambda w: pl.BlockSpec((1, tq, w), lambda bi: (bi, 0, 0))
    cache = lambda w: pl.BlockSpec((None, None, w, past), lambda bi: (layer, bi, 0, 0))
    return pl.pallas_call(
        functools.partial(_dsa_sample_kernel, topk=topk),
        grid=(b,),
        in_specs=[blk(ATT_WIDTH), blk(IDX_WIDTH), blk(LANES), blk(KV_WIDTH), blk(KV_WIDTH), blk(IDX_DIM),
                  cache(KV_WIDTH), cache(KV_WIDTH), cache(IDX_DIM)],
        out_specs=blk(ATT_WIDTH),
        out_shape=jax.ShapeDtypeStruct((b, tq, ATT_WIDTH), BF16),
        scratch_shapes=[pltpu.VMEM((total, KV_WIDTH), BF16), pltpu.VMEM((KV_WIDTH, total), BF16),
                        pltpu.VMEM((total, IDX_DIM), BF16), pltpu.VMEM((total, LANES), F32)],
        compiler_params=pltpu.CompilerParams(dimension_semantics=("arbitrary",), vmem_limit_bytes=VMEM_LIMIT),
        name="dsa_sample",
    )(q, qi, wi, kb, vb, kib, cache_k, cache_v, cache_kidx)


def _first_lane_where(mask, lane):
    return jnp.min(jnp.where(mask, lane, LANES), axis=1, keepdims=True)


def _post_kernel(att_ref, sa_ref, mb_ref, x_ref, wpa_ref, wout_ref, g2_ref, wr_ref, br_ref,
                 h_ref, xn2_ref, gate_ref, *sort_refs):
    pa = _dot(att_ref[...], wpa_ref[...])
    merged = sa_ref[...].astype(F32) * pa + mb_ref[...].astype(F32)
    h = x_ref[...] + _dot(merged.astype(BF16), wout_ref[...])
    h_ref[...] = h
    xn2 = _rms(h, g2_ref[...])
    xh = xn2.astype(BF16)
    xn2_ref[...] = xh

    xl = (xn2 - xh.astype(F32)).astype(BF16)
    wr = wr_ref[...]
    wh = wr.astype(BF16)
    wl = (wr - wh.astype(F32)).astype(BF16)
    lg = _dot(xh, wh) + (_dot(xl, wh) + _dot(xh, wl)) + br_ref[...]

    lane = lax.broadcasted_iota(jnp.int32, lg.shape, 1)
    is_g = lane < N_GROUPS
    mg = jnp.max(jnp.where(is_g, lg, NEG_INF), axis=1, keepdims=True)
    eg = jnp.where(is_g, jnp.exp(lg - mg), 0.0)
    pg = eg / jnp.sum(eg, axis=1, keepdims=True)
    g_w = jnp.max(pg, axis=1, keepdims=True)
    g_idx = _first_lane_where(jnp.logical_and(is_g, pg == g_w), lane)

    e_lo = ROUTER_E0 + EXP_PER_GROUP * g_idx
    is_e = jnp.logical_and(lane >= e_lo, lane < e_lo + EXP_PER_GROUP)
    me = jnp.max(jnp.where(is_e, lg, NEG_INF), axis=1, keepdims=True)
    ee = jnp.where(is_e, jnp.exp(lg - me), 0.0)
    pe = ee / jnp.sum(ee, axis=1, keepdims=True)
    e1 = jnp.max(pe, axis=1, keepdims=True)
    i1 = _first_lane_where(jnp.logical_and(is_e, pe == e1), lane)
    rest = jnp.logical_and(is_e, lane != i1)
    e2 = jnp.max(jnp.where(rest, pe, -1.0), axis=1, keepdims=True)
    i2 = _first_lane_where(jnp.logical_and(rest, pe == e2), lane)
    den = e1 + e2
    gate = jnp.where(lane == i1, e1 / den, jnp.where(lane == i2, e2 / den, 0.0)) * g_w
    group_gates = [jnp.where(lane < EXP_PER_GROUP, pltpu.roll(gate, LANES - (ROUTER_E0 + EXP_PER_GROUP * g), 1), 0.0)
                   for g in range(N_GROUPS)]
    if not sort_refs:
        for g in range(N_GROUPS):
            gate_ref[:, g * LANES:(g + 1) * LANES] = group_gates[g]
        return

    pt_ref, meta_ref = sort_refs
    gate_ref[...] = (group_gates[0] + group_gates[1]) + (group_gates[2] + group_gates[3])
    tm = lg.shape[0]
    onehot = jnp.where(lane == g_idx, 1.0, 0.0)
    ri = lax.broadcasted_iota(jnp.int32, (tm, tm), 0)
    cj = lax.broadcasted_iota(jnp.int32, (tm, tm), 1)
    earlier = jnp.where(cj < ri, 1.0, 0.0).astype(BF16)
    rank = jnp.sum(_dot(earlier, onehot.astype(BF16)) * onehot, axis=1, keepdims=True)
    cnt = jnp.sum(onehot, axis=0, keepdims=True)
    cnt_pad = jnp.floor((cnt + (SORT_PAD - 1.0)) * (1.0 / SORT_PAD)) * SORT_PAD
    seg_off = pltpu.roll(cnt_pad, 1, 1) + pltpu.roll(cnt_pad, 2, 1) + pltpu.roll(cnt_pad, 3, 1)
    dest = (rank + jnp.sum(onehot * seg_off, axis=1, keepdims=True)).astype(jnp.int32)
    col = lax.broadcasted_iota(jnp.int32, pt_ref.shape, 1)
    pt_ref[...] = jnp.where(col == dest, 1.0, 0.0).astype(BF16)
    meta_ref[0] = jnp.concatenate([cnt_pad, seg_off, jnp.zeros((SUBLANES - 2, LANES), F32)], axis=0)


def _sort_rows(tm):
    return -(-(tm + N_GROUPS * SORT_PAD) // LANES) * LANES


def _post(att, sa, mb, x, w, *, tm, sort):
    n, d = x.shape
    row = lambda t: (t, 0)
    const = lambda t: (0, 0)
    gate_w = LANES if sort else N_GROUPS * LANES
    out_specs = [pl.BlockSpec((tm, d), row), pl.BlockSpec((tm, d), row), pl.BlockSpec((tm, gate_w), row)]
    out_shape = [jax.ShapeDtypeStruct((n, d), F32), jax.ShapeDtypeStruct((n, d), BF16),
                 jax.ShapeDtypeStruct((n, gate_w), F32)]
    if sort:
        out_specs += [pl.BlockSpec((tm, _sort_rows(tm)), row), pl.BlockSpec((1, SUBLANES, LANES), lambda t: (t, 0, 0))]
        out_shape += [jax.ShapeDtypeStruct((n, _sort_rows(tm)), BF16),
                      jax.ShapeDtypeStruct((n // tm, SUBLANES, LANES), F32)]
    return pl.pallas_call(
        _post_kernel,
        grid=(n // tm,),
        in_specs=[pl.BlockSpec((tm, ATT_WIDTH), row), pl.BlockSpec((tm, d), row), pl.BlockSpec((tm, d), row),
                  pl.BlockSpec((tm, d), row), pl.BlockSpec(w["wpa"].shape, const), pl.BlockSpec(w["wout"].shape, const),
                  pl.BlockSpec((1, d), const), pl.BlockSpec(w["wr"].shape, const), pl.BlockSpec((1, ROUTER_LANES), const)],
        out_specs=out_specs,
        out_shape=out_shape,
        compiler_params=pltpu.CompilerParams(dimension_semantics=("arbitrary",), vmem_limit_bytes=VMEM_LIMIT),
        name="post_attn",
    )(att, sa, mb, x, w["wpa"], w["wout"], w["g2"], w["wr"], w["br"])


def _moe_kernel(xn2_ref, gate_ref, h_ref, wg_ref, wu_ref, wd_ref, fn_ref, o_ref, acc_ref, *, final):
    g = pl.program_id(1)

    @pl.when(g == 0)
    def _():
        acc_ref[...] = jnp.zeros_like(acc_ref)

    x = xn2_ref[...]
    gate = gate_ref[...]
    act = jax.nn.silu(_dot(x, wg_ref[0])) * _dot(x, wu_ref[0])
    parts = []
    for e in range(EXP_PER_GROUP):
        parts.append((act[:, e * EXP_HIDDEN:(e + 1) * EXP_HIDDEN] * gate[:, e:e + 1]).astype(BF16))
    acc_ref[...] += _dot(jnp.concatenate(parts, axis=1), wd_ref[0])

    @pl.when(g == N_GROUPS - 1)
    def _():
        y = h_ref[...] + acc_ref[...]
        o_ref[...] = _rms(y, fn_ref[...]) if final else y


def _moe(xn2, gate, h, w, fn, *, tm, final):
    n, d = h.shape
    row = lambda t, g: (t, 0)
    wspec = lambda shp: pl.BlockSpec((1,) + shp[1:], lambda t, g: (g, 0, 0))
    return pl.pallas_call(
        functools.partial(_moe_kernel, final=final),
        grid=(n // tm, N_GROUPS),
        in_specs=[pl.BlockSpec((tm, d), row), pl.BlockSpec((tm, LANES), lambda t, g: (t, g)), pl.BlockSpec((tm, d), row),
                  wspec(w["weg"].shape), wspec(w["weu"].shape), wspec(w["wed"].shape),
                  pl.BlockSpec((1, d), lambda t, g: (0, 0))],
        out_specs=pl.BlockSpec((tm, d), row),
        out_shape=jax.ShapeDtypeStruct((n, d), F32),
        scratch_shapes=[pltpu.VMEM((tm, d), F32)],
        compiler_params=pltpu.CompilerParams(dimension_semantics=("arbitrary", "arbitrary"),
                                             vmem_limit_bytes=VMEM_LIMIT),
        name="moe",
    )(xn2, gate, h, w["weg"], w["weu"], w["wed"], fn)


def _moe_sorted_kernel(meta_ref, xn2_ref, gate_ref, h_ref, pt_ref, wg_ref, wu_ref, wd_ref, fn_ref, o_ref,
                       xs_scr, gs_scr, ys_scr, *, final):
    t = pl.program_id(0)
    rows = pt_ref.shape[1]
    pt = pt_ref[...]
    perm = pt.astype(F32).T.astype(BF16)
    xs_scr[0:rows, :] = _dot(perm, xn2_ref[...]).astype(BF16)
    xs_scr[rows:rows + MOE_CHUNK, :] = jnp.zeros((MOE_CHUNK, xs_scr.shape[1]), BF16)
    gate = gate_ref[...]
    g_hi = gate.astype(BF16)
    g_rest = gate - g_hi.astype(F32)
    g_mid = g_rest.astype(BF16)
    g_lo = (g_rest - g_mid.astype(F32)).astype(BF16)
    gs_scr[0:rows, :] = _dot(perm, g_hi) + (_dot(perm, g_mid) + _dot(perm, g_lo))
    gs_scr[rows:rows + MOE_CHUNK, :] = jnp.zeros((MOE_CHUNK, LANES), F32)
    ys_scr[...] = jnp.zeros_like(ys_scr)

    for g in range(N_GROUPS):
        cnt = meta_ref[t * 2 * N_GROUPS + g]
        seg = meta_ref[t * 2 * N_GROUPS + N_GROUPS + g]

        def chunk(c, carry, g=g, cnt=cnt, seg=seg):
            r0 = pl.multiple_of(seg + c * MOE_CHUNK, SORT_PAD)
            x = xs_scr[pl.ds(r0, MOE_CHUNK), :]
            gt = gs_scr[pl.ds(r0, MOE_CHUNK), :]
            act = jax.nn.silu(_dot(x, wg_ref[g])) * _dot(x, wu_ref[g])
            parts = [(act[:, e * EXP_HIDDEN:(e + 1) * EXP_HIDDEN] * gt[:, e:e + 1]).astype(BF16)
                     for e in range(EXP_PER_GROUP)]
            y = _dot(jnp.concatenate(parts, axis=1), wd_ref[g])
            in_seg = lax.broadcasted_iota(jnp.int32, (MOE_CHUNK, 1), 0) + c * MOE_CHUNK < cnt
            ys_scr[pl.ds(r0, MOE_CHUNK), :] = jnp.where(in_seg, y, ys_scr[pl.ds(r0, MOE_CHUNK), :])
            return carry

        lax.fori_loop(0, (cnt + MOE_CHUNK - 1) // MOE_CHUNK, chunk, 0)

    y = h_ref[...] + _dot(pt, ys_scr[0:rows, :].astype(BF16))
    o_ref[...] = _rms(y, fn_ref[...]) if final else y


def _moe_sorted(xn2, gate, h, pt, meta, w, fn, *, tm, final):
    n, d = h.shape
    rows = pt.shape[1]
    row = lambda t, m: (t, 0)
    resident = lambda shp: pl.BlockSpec(shp, lambda t, m: (0, 0, 0), pipeline_mode=pl.Buffered(1))
    grid_spec = pltpu.PrefetchScalarGridSpec(
        num_scalar_prefetch=1,
        grid=(n // tm,),
        in_specs=[pl.BlockSpec((tm, d), row), pl.BlockSpec((tm, LANES), row), pl.BlockSpec((tm, d), row),
                  pl.BlockSpec((tm, rows), row),
                  resident(w["weg"].shape), resident(w["weu"].shape), resident(w["wed"].shape),
                  pl.BlockSpec((1, d), lambda t, m: (0, 0))],
        out_specs=pl.BlockSpec((tm, d), row),
        scratch_shapes=[pltpu.VMEM((rows + MOE_CHUNK, d), BF16), pltpu.VMEM((rows + MOE_CHUNK, LANES), F32),
                        pltpu.VMEM((rows + MOE_CHUNK, d), F32)],
    )
    return pl.pallas_call(
        functools.partial(_moe_sorted_kernel, final=final),
        grid_spec=grid_spec,
        out_shape=jax.ShapeDtypeStruct((n, d), F32),
        compiler_params=pltpu.CompilerParams(dimension_semantics=("arbitrary",), vmem_limit_bytes=VMEM_LIMIT),
        name="moe_sorted",
    )(meta, xn2, gate, h, pt, w["weg"], w["weu"], w["wed"], fn)


def _pack_experts_kernel(w_ref, o_ref):
    for e in range(EXP_PER_GROUP):
        o_ref[:, e * EXP_HIDDEN:(e + 1) * EXP_HIDDEN] = w_ref[e].astype(BF16)


def _pack_experts(w):
    depth, n_groups, n_exp, d, f = w.shape
    rows = 2 * MXU_DIM
    return pl.pallas_call(
        _pack_experts_kernel,
        grid=(depth, n_groups, d // rows),
        in_specs=[pl.BlockSpec((None, None, n_exp, rows, f), lambda l, g, r: (l, g, 0, r, 0))],
        out_specs=pl.BlockSpec((None, None, rows, n_exp * f), lambda l, g, r: (l, g, r, 0)),
        out_shape=jax.ShapeDtypeStruct((depth, n_groups, d, n_exp * f), BF16),
        compiler_params=pltpu.CompilerParams(dimension_semantics=("arbitrary",) * 3, vmem_limit_bytes=VMEM_LIMIT),
        name="pack_experts",
    )(w)


def _rope_tables(pos, d):
    inv = ROPE_THETA ** (-jnp.arange(0, d, 2, dtype=F32) / d)
    ang = pos[:, None] * inv[None, :]
    cos, sin = jnp.cos(ang), jnp.sin(ang)
    reps = LANES // d
    return (jnp.tile(jnp.concatenate([cos, cos], axis=-1), (1, reps)),
            jnp.tile(jnp.concatenate([-sin, sin], axis=-1), (1, reps)))


def _layer_weights(l, norm1, w_in, sgu_ln_g, sgu_ln_b, sgu_w, sgu_b, w_pa, w_pb, w_out, norm2,
                   w_rg, b_rg, w_re, b_re, w_eg, w_eu, w_ed, sgu_chunk):
    d = w_in.shape[1]
    a_end = ATT_WIDTH + 2 * KV_WIDTH + IDX_WIDTH
    b_end = a_end + IDX_DIM + IDX_HEADS
    reps = SGU_CHUNK // sgu_chunk
    n_router = N_GROUPS + N_GROUPS * EXP_PER_GROUP
    return dict(
        g1=norm1[l][None, :],
        wa=w_in[l][:, :a_end].astype(BF16),
        wb=jnp.pad(w_in[l][:, a_end:b_end], ((0, 0), (0, LANES - (b_end - a_end)))).astype(BF16),
        wc=w_in[l][:, b_end:].astype(BF16),
        sguw=jnp.tile(sgu_w[l][:, :sgu_chunk, :sgu_chunk], (1, reps, reps)),
        sgub=jnp.repeat(jnp.tile(sgu_b[l][:, :sgu_chunk], (1, reps)).T, SGU_GDIM, axis=1),
        lng=sgu_ln_g[l][None, :], lnb=sgu_ln_b[l][None, :],
        wpb=w_pb[l].astype(BF16), wpa=w_pa[l].astype(BF16), wout=w_out[l].astype(BF16),
        g2=norm2[l][None, :],
        wr=jnp.pad(jnp.concatenate([w_rg[l], w_re[l]], axis=1), ((0, 0), (0, ROUTER_LANES - n_router))),
        br=jnp.pad(jnp.concatenate([b_rg[l], b_re[l]]), (0, ROUTER_LANES - n_router))[None, :],
        weg=w_eg[l], weu=w_eu[l],
        wed=w_ed[l].reshape(N_GROUPS, GROUP_HIDDEN, d).astype(BF16),
    )


def _token_tile(n):
    return 512 if n % 512 == 0 else n


def kernel(x_prompt, x_sample, cache_k, cache_v, cache_kidx, norm1, w_in, sgu_ln_g, sgu_ln_b, sgu_w, sgu_b,
           w_pa, w_pb, w_out, norm2, w_rg, b_rg, w_re, b_re, w_eg, w_eu, w_ed, final_norm):
    bp, tp, d = x_prompt.shape
    bs, ts, _ = x_sample.shape
    depth, _, past = cache_k.shape[:3]
    np_, ns = bp * tp, bs * ts
    tm_p, tm_s = _token_tile(tp), _token_tile(ns)
    assert tp % tm_p == 0 and tp % Q_BLOCK == 0 and SGU_CHUNK % ts == 0 and tm_s % SGU_CHUNK == 0
    assert tm_p % KEY_SUB == 0 and tm_s % KEY_SUB == 0

    pos_p = jnp.arange(tp, dtype=F32)
    pos_s = jnp.tile(past + jnp.arange(ts, dtype=F32), tm_s // ts)
    tabs_p = _rope_tables(pos_p, HEAD_DIM) + _rope_tables(pos_p, IDX_DIM)
    tabs_s = _rope_tables(pos_s, HEAD_DIM) + _rope_tables(pos_s, IDX_DIM)
    ckt = jnp.transpose(cache_k, (0, 1, 3, 4, 2)).reshape(depth, bs, KV_WIDTH, past)
    cvt = jnp.transpose(cache_v, (0, 1, 3, 4, 2)).reshape(depth, bs, KV_WIDTH, past)
    ckit = jnp.transpose(cache_kidx, (0, 1, 3, 2))
    fn = final_norm[None, :]
    weg_all, weu_all = _pack_experts(w_eg), _pack_experts(w_eu)

    hp = x_prompt.reshape(np_, d)
    hs = x_sample.reshape(ns, d)
    outs = [[] for _ in range(4)]
    kv_bufs = None
    for l in range(depth):
        wargs = (norm1, w_in, sgu_ln_g, sgu_ln_b, sgu_w, sgu_b, w_pa, w_pb, w_out, norm2,
                 w_rg, b_rg, w_re, b_re, weg_all, weu_all, w_ed)
        wp = _layer_weights(l, *wargs, sgu_chunk=SGU_CHUNK)
        ws = dict(wp)
        ws.update({k: v for k, v in _layer_weights(l, *wargs, sgu_chunk=ts).items() if k in ("sguw", "sgub")})
        final = l == depth - 1

        q, kt_all, kb, vt_all, _, vt, qi, kit_all, kib, wi, mb, sa = _inproj(
            hp, wp, tabs_p, tm=tm_p, sgu_chunk=SGU_CHUNK, want_vln=False,
            kv_layer=l, kv_bufs=kv_bufs, kv_shape=(depth, bp, tp))
        kv_bufs = (kt_all, vt_all, kit_all)
        r3 = lambda a: a.reshape(bp, tp, a.shape[-1])
        att = _dsa_prompt(q, qi, wi, r3(kb), vt, r3(kib)).reshape(np_, ATT_WIDTH)
        h, xn2, gate, pt, meta = _post(att, sa, mb, hp, wp, tm=tm_p, sort=True)
        meta = meta[:, 0:2, 0:N_GROUPS].astype(jnp.int32).reshape(-1)
        hp = _moe_sorted(xn2, gate, h, pt, meta, wp, fn, tm=tm_p, final=final)

        q, kf, kb, vf, vb, _, qi, kif, kib, wi, mb, sa, vln = _inproj(hs, ws, tabs_s, tm=tm_s, sgu_chunk=ts,
                                                                      want_vln=True)
        r3 = lambda a: a.reshape(bs, ts, a.shape[-1])
        att = _dsa_sample(r3(q), r3(qi), r3(wi), r3(kb), r3(vb), r3(kib), ckt, cvt, ckit, l)
        h, xn2, gate = _post(att.reshape(ns, ATT_WIDTH), sa, mb, hs, ws, tm=tm_s, sort=False)
        hs = _moe(xn2, gate, h, ws, fn, tm=tm_s, final=final)
        outs[0].append(kf.reshape(bs, ts, N_KV_HEADS, HEAD_DIM))
        outs[1].append(vf.reshape(bs, ts, N_KV_HEADS, HEAD_DIM))
        outs[2].append(kif.reshape(bs, ts, IDX_DIM))
        outs[3].append(vln.reshape(bs, ts, SGU_WIDTH))

    kt_all, vt_all, kit_all = kv_bufs
    heads = lambda a: jnp.transpose(a.reshape(depth, bp, N_KV_HEADS, HEAD_DIM, tp), (0, 1, 4, 2, 3))
    return ((hp.reshape(bp, tp, d), hs.reshape(bs, ts, d), heads(kt_all), heads(vt_all),
             jnp.transpose(kit_all, (0, 1, 3, 2))) + tuple(jnp.stack(o) for o in outs))
```

```python
import functools

import jax
import jax.numpy as jnp
from jax import lax
from jax.experimental import pallas as pl
from jax.experimental.pallas import tpu as pltpu

F32 = jnp.float32
BF16 = jnp.bfloat16

CHUNK = 64
N_HEADS = 8
N_KV_HEADS = 2
HEAD_DIM = 64
GQA_GROUP = N_HEADS // N_KV_HEADS
ATT_WIDTH = N_HEADS * HEAD_DIM
KV_WIDTH = N_KV_HEADS * HEAD_DIM
IDX_HEADS = 8
IDX_DIM = 32
IDX_WIDTH = IDX_HEADS * IDX_DIM
TOPK_MAX = 256
Q_BLOCK = 128
DSA_QUERIES = 256
SGU_GROUPS = 4
SGU_CHUNK = 128
SGU_WIDTH = 512
SGU_GDIM = SGU_WIDTH // SGU_GROUPS
N_GROUPS = 4
EXP_PER_GROUP = 8
EXP_HIDDEN = 128
GROUP_HIDDEN = EXP_PER_GROUP * EXP_HIDDEN
ROPE_THETA = 10000.0
EPS = 1e-6
Q_SCALE = HEAD_DIM ** -0.5 * 1.4426950408889634

LANES = 128
SUBLANES = 8
MXU_DIM = 256
ROUTER_LANES = LANES
ROUTER_E0 = N_GROUPS
VMEM_LIMIT = 56 * 1024 * 1024
KEY_TILE = 2 * MXU_DIM
KEY_SUB = MXU_DIM
FOLD_WAYS = 4
COUNT_ROWS = 64
SEARCH_FIRST_STEPS = 13
SEARCH_SECOND_STEPS = 1
SEARCH_ROUND_STEPS = 1
SEARCH_MAX_ROUNDS = 34
SORT_PAD = 16
MOE_CHUNK = 144
KNORM_SLACK = 1.0 + 2.0 ** -6
SOFTMAX_SUM_FLOOR = 2.0 ** -80
NEG_INF = float("-inf")


def _dot(a, b):
    return jnp.dot(a, b, preferred_element_type=F32)


def _dot_t(a, b):
    return lax.dot_general(a, b, (((1,), (1,)), ((), ())), preferred_element_type=F32)


def _rms(x, g):
    return x * lax.rsqrt(jnp.mean(x * x, axis=-1, keepdims=True) + EPS) * g


def _swap_halves(x, d):
    lane = lax.broadcasted_iota(jnp.int32, x.shape, 1)
    first = (lane % d) < (d // 2)
    return jnp.where(first, pltpu.roll(x, LANES - d // 2, 1), pltpu.roll(x, d // 2, 1))


def _rope(x, cos, sin_signed, d):
    return x * cos + _swap_halves(x, d) * sin_signed


def _inproj_kernel(*refs, sgu_chunk, kv_transposed, n_alias, want_vln):
    (x_ref, g1_ref, wa_ref, wb_ref, wc_ref, cq_ref, sq_ref, ci_ref, si_ref,
     sguw_ref, sgub_ref, lng_ref, lnb_ref, wpb_ref) = refs[:14]
    (q_ref, kf_ref, kb_ref, vf_ref, vb_ref, vt_ref, qi_ref, kif_ref, kib_ref, wi_ref,
     mb_ref, sa_ref) = refs[14 + n_alias:26 + n_alias]
    tm = x_ref.shape[0]
    xn = _rms(x_ref[...], g1_ref[...]).astype(BF16)
    cq, sq, ci, si = cq_ref[...], sq_ref[...], ci_ref[...], si_ref[...]

    a = _dot(xn, wa_ref[...])
    def emit_queries(ref, val):
        if kv_transposed:
            for c in range(tm // DSA_QUERIES):
                ref[c] = val[c * DSA_QUERIES:(c + 1) * DSA_QUERIES, :].T[0:ref.shape[1], :].astype(ref.dtype)
        else:
            ref[...] = val.astype(ref.dtype)

    emit_queries(q_ref, jnp.concatenate(
        [_rope(a[:, c * LANES:(c + 1) * LANES], cq, sq, HEAD_DIM) * Q_SCALE for c in range(ATT_WIDTH // LANES)], axis=1))
    k = _rope(a[:, ATT_WIDTH:ATT_WIDTH + KV_WIDTH], cq, sq, HEAD_DIM)
    kb_ref[...] = k.astype(BF16)
    v = a[:, ATT_WIDTH + KV_WIDTH:ATT_WIDTH + 2 * KV_WIDTH]
    vb_ref[...] = v.astype(BF16)
    if kv_transposed:
        kf_ref[...] = k.T
        vf_ref[...] = v.T
    else:
        kf_ref[...] = k
        vf_ref[...] = v
    for c in range(tm // KEY_SUB):
        vt_ref[c] = v[c * KEY_SUB:(c + 1) * KEY_SUB, :].T.astype(BF16)
    qi0 = ATT_WIDTH + 2 * KV_WIDTH
    emit_queries(qi_ref, jnp.concatenate(
        [_rope(a[:, qi0 + c * LANES:qi0 + (c + 1) * LANES], ci, si, IDX_DIM) for c in range(IDX_WIDTH // LANES)], axis=1))

    b = _dot(xn, wb_ref[...])
    ki_wide = _rope(b, ci, si, IDX_DIM)
    ki = ki_wide[:, :IDX_DIM]
    kif_ref[...] = ki_wide.T[0:IDX_DIM, :] if kv_transposed else ki
    kib_ref[...] = ki.astype(BF16)
    emit_queries(wi_ref, pltpu.roll(b, LANES - IDX_DIM, 1) * (IDX_HEADS ** -0.5))

    u = jax.nn.gelu(_dot(xn, wc_ref[:, 0:SGU_WIDTH]))
    vs = jax.nn.gelu(_dot(xn, wc_ref[:, SGU_WIDTH:2 * SGU_WIDTH]))
    mu = jnp.mean(vs, axis=-1, keepdims=True)
    var = jnp.mean(jnp.square(vs - mu), axis=-1, keepdims=True)
    vln = (vs - mu) * lax.rsqrt(var + EPS) * lng_ref[...] + lnb_ref[...]
    if want_vln:
        refs[26 + n_alias][...] = vln
    ri = lax.broadcasted_iota(jnp.int32, (SGU_CHUNK, SGU_CHUNK), 0)
    cj = lax.broadcasted_iota(jnp.int32, (SGU_CHUNK, SGU_CHUNK), 1)
    mix_mask = (cj <= ri) & ((ri // sgu_chunk) == (cj // sgu_chunk))
    vln_b = vln.astype(BF16)
    w_eff = [jnp.where(mix_mask, sguw_ref[g], 0.0).astype(BF16) for g in range(SGU_GROUPS)]
    sg_rows = []
    for r in range(tm // SGU_CHUNK):
        rows = slice(r * SGU_CHUNK, (r + 1) * SGU_CHUNK)
        mixed = [_dot(w_eff[g], vln_b[rows, g * SGU_GDIM:(g + 1) * SGU_GDIM]) for g in range(SGU_GROUPS)]
        sg_rows.append(u[rows, :] * (jnp.concatenate(mixed, axis=1) + sgub_ref[...]))
    sg = jnp.concatenate(sg_rows, axis=0).astype(BF16)
    d = x_ref.shape[1]
    sa_ref[...] = jax.nn.sigmoid(_dot(xn, wc_ref[:, 2 * SGU_WIDTH:2 * SGU_WIDTH + d])).astype(BF16)
    gb = jax.nn.sigmoid(_dot(xn, wc_ref[:, 2 * SGU_WIDTH + d:]))
    mb_ref[...] = (gb * _dot(sg, wpb_ref[...])).astype(BF16)


def _inproj(x, w, tabs, *, tm, sgu_chunk, want_vln, kv_layer=None, kv_bufs=None, kv_shape=None):
    n, d = x.shape
    n_tab_tiles = tabs[0].shape[0] // tm
    row = lambda t: (t, 0)
    const = lambda t: (0, 0)
    tab = lambda t: (t % n_tab_tiles, 0)
    kv_transposed = kv_layer is not None
    if kv_transposed:
        depth, streams, steps = kv_shape
        per_stream = steps // tm
        kv_sds = lambda wd: jax.ShapeDtypeStruct((depth, streams, wd, steps), F32)
        kv_spec = lambda wd: pl.BlockSpec((None, None, wd, tm),
                                          lambda t: (kv_layer, t // per_stream, 0, t % per_stream))
    else:
        kv_sds = lambda wd: jax.ShapeDtypeStruct((n, wd), F32)
        kv_spec = lambda wd: pl.BlockSpec((tm, wd), row)
    tiled = lambda wd, dt: (jax.ShapeDtypeStruct((n, wd), dt), pl.BlockSpec((tm, wd), row))
    if kv_transposed:
        qb = DSA_QUERIES
        query = lambda wd, rows, dt: (jax.ShapeDtypeStruct((n // qb, rows, qb), dt),
                                      pl.BlockSpec((tm // qb, rows, qb), lambda t: (t, 0, 0)))
    else:
        query = lambda wd, rows, dt: tiled(wd, dt)
    outs = [
        query(ATT_WIDTH, ATT_WIDTH, BF16),
        (kv_sds(KV_WIDTH), kv_spec(KV_WIDTH)),
        tiled(KV_WIDTH, BF16),
        (kv_sds(KV_WIDTH), kv_spec(KV_WIDTH)),
        tiled(KV_WIDTH, BF16),
        (jax.ShapeDtypeStruct((n // KEY_SUB, KV_WIDTH, KEY_SUB), BF16),
         pl.BlockSpec((tm // KEY_SUB, KV_WIDTH, KEY_SUB), lambda t: (t, 0, 0))),
        query(IDX_WIDTH, IDX_WIDTH, BF16),
        (kv_sds(IDX_DIM), kv_spec(IDX_DIM)),
        tiled(IDX_DIM, BF16),
        query(LANES, IDX_HEADS, F32),
        tiled(d, BF16),
        tiled(d, BF16),
    ]
    if want_vln:
        outs.append(tiled(SGU_WIDTH, F32))
    in_specs = [
        pl.BlockSpec((tm, d), row),
        pl.BlockSpec((1, d), const),
        pl.BlockSpec(w["wa"].shape, const),
        pl.BlockSpec(w["wb"].shape, const),
        pl.BlockSpec(w["wc"].shape, const),
        pl.BlockSpec((tm, LANES), tab), pl.BlockSpec((tm, LANES), tab),
        pl.BlockSpec((tm, LANES), tab), pl.BlockSpec((tm, LANES), tab),
        pl.BlockSpec(w["sguw"].shape, lambda t: (0, 0, 0)),
        pl.BlockSpec(w["sgub"].shape, const),
        pl.BlockSpec((1, SGU_WIDTH), const),
        pl.BlockSpec((1, SGU_WIDTH), const),
        pl.BlockSpec(w["wpb"].shape, const),
    ]
    args = [x, w["g1"], w["wa"], w["wb"], w["wc"], *tabs, w["sguw"], w["sgub"], w["lng"], w["lnb"], w["wpb"]]
    aliases = {}
    if kv_bufs is not None:
        for buf, out_idx in zip(kv_bufs, (1, 3, 7)):
            aliases[len(args)] = out_idx
            in_specs.append(pl.BlockSpec(memory_space=pl.ANY))
            args.append(buf)
    return pl.pallas_call(
        functools.partial(_inproj_kernel, sgu_chunk=sgu_chunk, kv_transposed=kv_transposed,
                          n_alias=len(aliases), want_vln=want_vln),
        grid=(n // tm,),
        in_specs=in_specs,
        out_specs=[o[1] for o in outs],
        out_shape=[o[0] for o in outs],
        input_output_aliases=aliases,
        compiler_params=pltpu.CompilerParams(dimension_semantics=("arbitrary",), vmem_limit_bytes=VMEM_LIMIT),
        name="inproj",
    )(*args)


def _float_key(x):
    b = lax.bitcast_convert_type(x, jnp.int32)
    return b ^ ((b >> 31) & jnp.int32(0x7FFFFFFF))


def _key_float(k):
    return lax.bitcast_convert_type(k ^ ((k >> 31) & jnp.int32(0x7FFFFFFF)), F32)


def _fold_rows(x):
    n = x.shape[0] // SUBLANES
    ways = FOLD_WAYS if n % FOLD_WAYS == 0 else 1
    part = jnp.sum(x.reshape(n // ways, ways, SUBLANES, x.shape[1]), axis=0)
    return jnp.sum(part, axis=0)


def _topk_threshold(count_ge, snap, smin, smax, n_adm, kk):
    lo0 = _float_key(smin + 0.0)
    hx0 = _float_key(smax + 0.0) + 1

    def is_active(st):
        lo, hx, c_lo, _ = st
        return jnp.logical_and(c_lo != kk, lo + 1 < hx)

    def step(st, by_key):
        lo, hx, c_lo, c_hx = st
        active = is_active(st)
        if by_key:
            mid = (lo >> 1) + (hx >> 1) + (lo & hx & 1)
        else:
            mid = _float_key(0.5 * _key_float(lo) + 0.5 * _key_float(hx))
        mid = jnp.where(active, jnp.minimum(jnp.maximum(mid, lo + 1), hx - 1), lo)
        c = count_ge(_key_float(mid))
        up = jnp.logical_and(active, c >= kk)
        down = jnp.logical_and(active, c < kk)
        return (jnp.where(up, mid, lo), jnp.where(down, mid, hx),
                jnp.where(up, c, c_lo), jnp.where(down, c, c_hx))

    def snap_step(st):
        lo, hx, c_lo, c_hx = st
        active = is_active(st)
        first_in, last_in = snap(_key_float(lo), _key_float(hx))
        return (jnp.where(active, _float_key(first_in + 0.0), lo),
                jnp.where(active, _float_key(last_in + 0.0) + 1, hx), c_lo, c_hx)

    def n_active(st):
        return jnp.max(jnp.where(is_active(st), 1.0, 0.0))

    st = (lo0, hx0, n_adm, jnp.zeros_like(n_adm))
    for _ in range(SEARCH_FIRST_STEPS):
        st = step(st, by_key=False)
    st = snap_step(st)
    for _ in range(SEARCH_SECOND_STEPS):
        st = step(st, by_key=False)
    act = n_active(st)

    def cond(carry):
        it, act, _ = carry
        return jnp.logical_and(it < SEARCH_MAX_ROUNDS, act > 0.5)

    def body(carry):
        it, _, st = carry
        st = step(snap_step(st), by_key=True)
        for _ in range(SEARCH_ROUND_STEPS):
            st = step(st, by_key=False)
        return it + 1, n_active(st), st

    _, _, (lo, _, c_lo, c_hx) = lax.while_loop(cond, body, (jnp.int32(0), act, st))
    return _key_float(lo), c_lo, c_hx


def _demote_excess_ties(s_ref, n_chunks, thr, need):
    ri = lax.broadcasted_iota(jnp.int32, (LANES, LANES), 0)
    cj = lax.broadcasted_iota(jnp.int32, (LANES, LANES), 1)
    lower = jnp.where(cj <= ri, 1.0, 0.0).astype(BF16)

    def body(j, run):
        off = pl.multiple_of(j * LANES, LANES)
        blk = s_ref[pl.ds(off, LANES), :]
        tied = blk == thr
        tied_f = jnp.where(tied, 1.0, 0.0)
        prefix = _dot(lower, tied_f.astype(BF16)) + run
        s_ref[pl.ds(off, LANES), :] = jnp.where(jnp.logical_and(tied, prefix > need), NEG_INF, blk)
        return run + jnp.sum(tied_f, axis=0, keepdims=True)

    lax.fori_loop(0, n_chunks, body, jnp.zeros((1, s_ref.shape[1]), F32))


def _select_mask(s_ref, n_tiles, tk, smin, smax, n_adm, topk, selected=0.0):
    kk = jnp.minimum(n_adm, float(topk))

    def count_ge(t):
        def body(j, acc):
            off = pl.multiple_of(j * tk, tk)
            for r in range(0, tk, COUNT_ROWS):
                acc = acc + _fold_rows(jnp.where(s_ref[pl.ds(off + r, COUNT_ROWS), :] >= t, 1.0, 0.0))
            return acc
        acc = lax.fori_loop(0, n_tiles, body, jnp.zeros((SUBLANES, s_ref.shape[1]), F32))
        return jnp.sum(acc, axis=0, keepdims=True)

    def snap(lo_f, hx_f):
        def body(j, carry):
            first_in, last_in = carry
            off = pl.multiple_of(j * tk, tk)
            for r in range(0, tk, COUNT_ROWS):
                blk = s_ref[pl.ds(off + r, COUNT_ROWS), :]
                first_in = jnp.minimum(first_in, -_fold_max(jnp.where(blk >= lo_f, -blk, NEG_INF)))
                last_in = jnp.maximum(last_in, _fold_max(jnp.where(blk < hx_f, blk, NEG_INF)))
            return first_in, last_in
        first_in, last_in = lax.fori_loop(0, n_tiles, body,
                                          (jnp.full((SUBLANES, s_ref.shape[1]), -NEG_INF, F32),
                                           jnp.full((SUBLANES, s_ref.shape[1]), NEG_INF, F32)))
        return jnp.min(first_in, axis=0, keepdims=True), jnp.max(last_in, axis=0, keepdims=True)

    thr, c_ge, c_gt = _topk_threshold(count_ge, snap, smin, smax, n_adm, kk)

    @pl.when(jnp.max(c_ge - kk) > 0.5)
    def _():
        _demote_excess_ties(s_ref, n_tiles * (tk // LANES), thr, kk - c_gt)

    def mask_body(j, carry):
        off = pl.multiple_of(j * tk, tk)
        s_ref[pl.ds(off, tk), :] = jnp.where(s_ref[pl.ds(off, tk), :] >= thr, selected, NEG_INF)
        return carry

    lax.fori_loop(0, n_tiles, mask_body, 0)


def _dsa_prompt_kernel(q_ref, qi_ref, wi_ref, k_ref, vt_ref, ki_ref, o_ref, s_scr, acc_scr, l_scr, kn_scr, lg_scr,
                       *, topk):
    i = pl.program_id(1)
    tk, ts = KEY_TILE, KEY_SUB
    qb = q_ref.shape[2]
    n_keys = (i + 1) * qb
    n_tiles = (n_keys + tk - 1) // tk
    qcol = lax.broadcasted_iota(jnp.int32, (1, qb), 1)
    lim = ((i * qb + qcol) // CHUNK + 1) * CHUNK

    @pl.when(i == 0)
    def _():
        ri = lax.broadcasted_iota(jnp.int32, (KV_WIDTH, LANES), 0)
        cj = lax.broadcasted_iota(jnp.int32, (KV_WIDTH, LANES), 1)
        head_sum = jnp.where(ri // HEAD_DIM == cj, 1.0, 0.0).astype(BF16)

        def body(j, acc):
            kt = k_ref[0, pl.ds(pl.multiple_of(j * tk, tk), tk), :].astype(F32)
            return jnp.maximum(acc, _fold_max(_dot((kt * kt).astype(BF16), head_sum)))

        acc = lax.fori_loop(0, k_ref.shape[1] // tk, body, jnp.zeros((SUBLANES, LANES), F32))
        kn_scr[...] = jnp.max(acc, axis=0, keepdims=True) * KNORM_SLACK

    qt = q_ref[0]
    qit = qi_ref[0]
    wit = wi_ref[0]
    qi_rhs = jnp.concatenate([qit[h * IDX_DIM:(h + 1) * IDX_DIM, :] for h in range(IDX_HEADS)], axis=1)
    zero = jnp.zeros((HEAD_DIM, qb), BF16)
    q_cols = []
    for h in range(N_HEADS):
        qh = qt[h * HEAD_DIM:(h + 1) * HEAD_DIM, :]
        q_cols.append(jnp.concatenate([qh, zero] if h < GQA_GROUP else [zero, qh], axis=0))
    q_rhs = jnp.concatenate(q_cols, axis=1)

    def score_tile(j, carry):
        smax, smin = carry
        for c in range(tk // ts):
            off = pl.multiple_of(j * tk + c * ts, ts)
            r = _dot(ki_ref[0, pl.ds(off, ts), :], qi_rhs)
            acc = wit[0:1, :] * jnp.maximum(r[:, 0:qb], 0.0)
            for h in range(1, IDX_HEADS):
                acc = acc + wit[h:h + 1, :] * jnp.maximum(r[:, h * qb:(h + 1) * qb], 0.0)
            key = off + lax.broadcasted_iota(jnp.int32, (ts, qb), 0)
            ok = key < lim
            s_scr[pl.ds(off, ts), :] = jnp.where(ok, acc, NEG_INF)
            smax = jnp.maximum(smax, _fold_max(jnp.where(ok, acc, NEG_INF)))
            smin = jnp.minimum(smin, -_fold_max(jnp.where(ok, -acc, NEG_INF)))
        return smax, smin

    smax, smin = lax.fori_loop(0, n_tiles, score_tile,
                               (jnp.full((SUBLANES, qb), NEG_INF, F32), jnp.full((SUBLANES, qb), -NEG_INF, F32)))
    smax = jnp.max(smax, axis=0, keepdims=True)
    smin = jnp.min(smin, axis=0, keepdims=True)
    qsq = jnp.square(qt.astype(F32))
    kn = kn_scr[...]
    lane1 = lax.broadcasted_iota(jnp.int32, (1, LANES), 1)
    shift = jnp.zeros((1, qb), F32)
    for h in range(N_HEADS):
        qn2 = jnp.sum(_fold_rows(qsq[h * HEAD_DIM:(h + 1) * HEAD_DIM, :]), axis=0, keepdims=True)
        kmax2 = jnp.max(jnp.where(lane1 == h // GQA_GROUP, kn, 0.0))
        shift = jnp.maximum(shift, jnp.sqrt(qn2 * kmax2))
    _select_mask(s_scr, n_tiles, tk, smin, smax, lim.astype(F32), topk, selected=-shift)

    def pv_accumulate(vt, ps, alphas):
        for pair in range(N_HEADS // 2):
            g = (2 * pair) // GQA_GROUP
            o = _dot(vt[g * HEAD_DIM:(g + 1) * HEAD_DIM, :], jnp.concatenate(ps[2 * pair:2 * pair + 2], axis=1))
            for u in range(2):
                h = 2 * pair + u
                rows = slice(h * HEAD_DIM, (h + 1) * HEAD_DIM)
                prev = acc_scr[rows, :] if alphas is None else alphas[h] * acc_scr[rows, :]
                acc_scr[rows, :] = prev + o[:, u * qb:(u + 1) * qb]

    def logits_into(buf, sub):
        off = pl.multiple_of(jnp.minimum(sub * ts, k_ref.shape[1] - ts), ts)
        lg_scr[buf] = _dot(k_ref[0, pl.ds(off, ts), :], q_rhs)

    ones_rows = jnp.ones((2 * SUBLANES, ts), BF16)

    def softmax_pv(buf, sub, l):
        bias = s_scr[pl.ds(pl.multiple_of(sub * ts, ts), ts), :]
        ps = [jnp.exp2(lg_scr[buf, :, h * qb:(h + 1) * qb] + bias).astype(BF16) for h in range(N_HEADS)]
        vt = vt_ref[sub]
        l_rows = []
        for pair in range(N_HEADS // 2):
            g = (2 * pair) // GQA_GROUP
            lhs = jnp.concatenate([vt[g * HEAD_DIM:(g + 1) * HEAD_DIM, :], ones_rows], axis=0)
            o = _dot(lhs, jnp.concatenate(ps[2 * pair:2 * pair + 2], axis=1))
            for u in range(2):
                h = 2 * pair + u
                rows = slice(h * HEAD_DIM, (h + 1) * HEAD_DIM)
                acc_scr[rows, :] = acc_scr[rows, :] + o[0:HEAD_DIM, u * qb:(u + 1) * qb]
                l_rows.append(l[h:h + 1, :] + o[HEAD_DIM:HEAD_DIM + 1, u * qb:(u + 1) * qb])
        return jnp.concatenate(l_rows, axis=0)

    def attn_bounded(j, l):
        logits_into(1, 2 * j + 1)
        l = softmax_pv(0, 2 * j, l)
        logits_into(0, 2 * j + 2)
        return softmax_pv(1, 2 * j + 1, l)

    def attn_online(j, carry):
        m, l = carry
        off = pl.multiple_of(j * ts, ts)
        lg = _dot(k_ref[0, pl.ds(off, ts), :], q_rhs)
        bias = s_scr[pl.ds(off, ts), :]
        m_rows, l_rows, alphas, ps = [], [], [], []
        for h in range(N_HEADS):
            x = lg[:, h * qb:(h + 1) * qb] + bias
            m_new = jnp.maximum(m[h:h + 1, :], jnp.max(_fold_max(x), axis=0, keepdims=True))
            m_use = jnp.where(m_new > NEG_INF, m_new, 0.0)
            alpha = jnp.exp2(m[h:h + 1, :] - m_use)
            p = jnp.exp2(x - m_use)
            l_rows.append(alpha * l[h:h + 1, :] + jnp.sum(_fold_rows(p), axis=0, keepdims=True))
            m_rows.append(m_new)
            alphas.append(alpha)
            ps.append(p.astype(BF16))
        pv_accumulate(vt_ref[j], ps, alphas)
        return jnp.concatenate(m_rows, axis=0), jnp.concatenate(l_rows, axis=0)

    n_sub = (n_keys + ts - 1) // ts
    acc_scr[...] = jnp.zeros_like(acc_scr)
    logits_into(0, 0)
    l_fast = lax.fori_loop(0, n_tiles, attn_bounded, jnp.zeros((N_HEADS, qb), F32))
    l_scr[...] = l_fast

    @pl.when(jnp.min(l_fast) < SOFTMAX_SUM_FLOOR)
    def _():
        acc_scr[...] = jnp.zeros_like(acc_scr)
        _, l_exact = lax.fori_loop(0, n_sub, attn_online,
                                   (jnp.full((N_HEADS, qb), NEG_INF, F32), jnp.zeros((N_HEADS, qb), F32)))
        l_scr[...] = l_exact

    l = l_scr[...]
    out_t = jnp.concatenate([acc_scr[h * HEAD_DIM:(h + 1) * HEAD_DIM, :] / l[h:h + 1, :] for h in range(N_HEADS)],
                            axis=0)
    o_ref[0] = out_t.T.astype(BF16)


def _fold_max(x):
    n = x.shape[0] // SUBLANES
    ways = FOLD_WAYS if n % FOLD_WAYS == 0 else 1
    part = jnp.max(x.reshape(n // ways, ways, SUBLANES, x.shape[1]), axis=0)
    return jnp.max(part, axis=0)


def _dsa_prompt(q, qi, wi, kb, vt, kib):
    b, t, _ = kb.shape
    qb = DSA_QUERIES
    assert t % KEY_TILE == 0 and t % qb == 0 and qb % CHUNK == 0
    topk = min(TOPK_MAX, t // 4)
    blk = lambda w: pl.BlockSpec((1, qb, w), lambda bi, i: (bi, i, 0))
    qblk = lambda rows: pl.BlockSpec((1, rows, qb), lambda bi, i: (bi * (t // qb) + i, 0, 0))
    whole = lambda w: pl.BlockSpec((1, t, w), lambda bi, i: (bi, 0, 0))
    return pl.pallas_call(
        functools.partial(_dsa_prompt_kernel, topk=topk),
        grid=(b, t // qb),
        in_specs=[qblk(ATT_WIDTH), qblk(IDX_WIDTH), qblk(IDX_HEADS), whole(KV_WIDTH),
                  pl.BlockSpec((t // KEY_SUB, KV_WIDTH, KEY_SUB), lambda bi, i: (bi, 0, 0)), whole(IDX_DIM)],
        out_specs=blk(ATT_WIDTH),
        out_shape=jax.ShapeDtypeStruct((b, t, ATT_WIDTH), BF16),
        scratch_shapes=[pltpu.VMEM((t, qb), F32), pltpu.VMEM((ATT_WIDTH, qb), F32),
                        pltpu.VMEM((N_HEADS, qb), F32), pltpu.VMEM((1, LANES), F32),
                        pltpu.VMEM((2, KEY_SUB, N_HEADS * qb), F32)],
        compiler_params=pltpu.CompilerParams(dimension_semantics=("arbitrary", "arbitrary"),
                                             vmem_limit_bytes=VMEM_LIMIT),
        name="dsa_prompt",
    )(q, qi, wi, kb, vt, kib)


def _dsa_sample_kernel(q_ref, qi_ref, wi_ref, kn_ref, vn_ref, kin_ref, ckt_ref, cvt_ref, ckit_ref, o_ref,
                       kcat, vcat_t, kicat, s_scr, *, topk):
    tq = q_ref.shape[1]
    past = ckt_ref.shape[1]
    total = kcat.shape[0]
    pad = total - past - tq
    kcat[0:past, :] = ckt_ref[...].T.astype(BF16)
    kcat[past:past + tq, :] = kn_ref[0]
    kcat[past + tq:total, :] = jnp.zeros((pad, KV_WIDTH), BF16)
    kit_wide = jnp.concatenate([ckit_ref[...], jnp.zeros((LANES - IDX_DIM, past), F32)], axis=0)
    kicat[0:past, :] = kit_wide.T[:, 0:IDX_DIM].astype(BF16)
    kicat[past:past + tq, :] = kin_ref[0]
    kicat[past + tq:total, :] = jnp.zeros((pad, IDX_DIM), BF16)
    vcat_t[:, 0:past] = cvt_ref[...].astype(BF16)
    v_new = jnp.concatenate([vn_ref[0].astype(F32), jnp.zeros((pad, KV_WIDTH), F32)], axis=0)
    vcat_t[:, past:total] = v_new.T.astype(BF16)

    lane = lax.broadcasted_iota(jnp.int32, (1, LANES), 1)
    qi = qi_ref[0]
    qi_stack = jnp.concatenate([qi[:, h * IDX_DIM:(h + 1) * IDX_DIM] for h in range(IDX_HEADS)], axis=0)
    wi_rows = jnp.concatenate([wi_ref[0], jnp.zeros((LANES - tq, LANES), F32)], axis=0).T
    w_flat = jnp.zeros((1, LANES), F32)
    for h in range(IDX_HEADS):
        row = wi_rows[h:h + 1, :]
        shifted = row if h == 0 else pltpu.roll(row, h * tq, 1)
        w_flat = jnp.where(lane // tq == h, shifted, w_flat)

    sc = w_flat * jnp.maximum(_dot_t(kicat[...], qi_stack), 0.0)
    for shift in (LANES // 2, LANES // 4, LANES // 8):
        sc = sc + pltpu.roll(sc, shift, 1)
    sc = jnp.where(lane < tq, sc, 0.0)
    for shift in (LANES // 8, LANES // 4, LANES // 2):
        sc = sc + pltpu.roll(sc, shift, 1)
    key = lax.broadcasted_iota(jnp.int32, (total, LANES), 0)
    ok = key < past + tq
    s_scr[...] = jnp.where(ok, sc, NEG_INF)
    smax = jnp.max(jnp.where(ok, sc, NEG_INF), axis=0, keepdims=True)
    smin = jnp.min(jnp.where(ok, sc, -NEG_INF), axis=0, keepdims=True)
    _select_mask(s_scr, 1, total, smin, smax, jnp.full((1, LANES), float(past + tq), F32), topk)

    q = q_ref[0]
    zero = jnp.zeros((tq, HEAD_DIM), BF16)
    q_stack = jnp.concatenate(
        [jnp.concatenate([q[:, h * HEAD_DIM:(h + 1) * HEAD_DIM], zero] if h < GQA_GROUP
                         else [zero, q[:, h * HEAD_DIM:(h + 1) * HEAD_DIM]], axis=1) for h in range(N_HEADS)], axis=0)
    lg = _dot_t(kcat[...], q_stack) + s_scr[...]
    m = jnp.max(lg, axis=0, keepdims=True)
    p = jnp.exp2(lg - m)
    l = jnp.sum(p, axis=0, keepdims=True)
    o = _dot(vcat_t[...], p.astype(BF16))
    o = jnp.where(lane < GQA_GROUP * tq, o[0:HEAD_DIM, :], o[HEAD_DIM:2 * HEAD_DIM, :]) / l
    o_t = jnp.concatenate([o, jnp.zeros((LANES - HEAD_DIM, LANES), F32)], axis=0).T
    o_ref[0] = jnp.concatenate([o_t[h * tq:(h + 1) * tq, 0:HEAD_DIM] for h in range(N_HEADS)], axis=1).astype(BF16)


def _dsa_sample(q, qi, wi, kb, vb, kib, cache_k, cache_v, cache_kidx, layer):
    b, tq, _ = q.shape
    past = cache_k.shape[3]
    assert tq * N_HEADS == LANES and tq * IDX_HEADS == LANES and past % LANES == 0
    total = past + LANES
    topk = min(TOPK_MAX, (past + tq) // 4)
    blk = lambda w: pl.BlockSpec((1, tq, w), lambda bi: (bi, 0, 0))
    cache = lambda w: pl.BlockSpec((None, None, w, past), lambda bi: (layer, bi, 0, 0))
    return pl.pallas_call(
        functools.partial(_dsa_sample_kernel, topk=topk),
        grid=(b,),
        in_specs=[blk(ATT_WIDTH), blk(IDX_WIDTH), blk(LANES), blk(KV_WIDTH), blk(KV_WIDTH), blk(IDX_DIM),
                  cache(KV_WIDTH), cache(KV_WIDTH), cache(IDX_DIM)],
        out_specs=blk(ATT_WIDTH),
        out_shape=jax.ShapeDtypeStruct((b, tq, ATT_WIDTH), BF16),
        scratch_shapes=[pltpu.VMEM((total, KV_WIDTH), BF16), pltpu.VMEM((KV_WIDTH, total), BF16),
                        pltpu.VMEM((total, IDX_DIM), BF16), pltpu.VMEM((total, LANES), F32)],
        compiler_params=pltpu.CompilerParams(dimension_semantics=("arbitrary",), vmem_limit_bytes=VMEM_LIMIT),
        name="dsa_sample",
    )(q, qi, wi, kb, vb, kib, cache_k, cache_v, cache_kidx)


def _first_lane_where(mask, lane):
    return jnp.min(jnp.where(mask, lane, LANES), axis=1, keepdims=True)


def _post_kernel(att_ref, sa_ref, mb_ref, x_ref, wpa_ref, wout_ref, g2_ref, wr_ref, br_ref,
                 h_ref, xn2_ref, gate_ref, *sort_refs):
    pa = _dot(att_ref[...], wpa_ref[...])
    merged = sa_ref[...].astype(F32) * pa + mb_ref[...].astype(F32)
    h = x_ref[...] + _dot(merged.astype(BF16), wout_ref[...])
    h_ref[...] = h
    xn2 = _rms(h, g2_ref[...])
    xh = xn2.astype(BF16)
    xn2_ref[...] = xh

    xl = (xn2 - xh.astype(F32)).astype(BF16)
    wr = wr_ref[...]
    wh = wr.astype(BF16)
    wl = (wr - wh.astype(F32)).astype(BF16)
    lg = _dot(xh, wh) + (_dot(xl, wh) + _dot(xh, wl)) + br_ref[...]

    lane = lax.broadcasted_iota(jnp.int32, lg.shape, 1)
    is_g = lane < N_GROUPS
    mg = jnp.max(jnp.where(is_g, lg, NEG_INF), axis=1, keepdims=True)
    eg = jnp.where(is_g, jnp.exp(lg - mg), 0.0)
    pg = eg / jnp.sum(eg, axis=1, keepdims=True)
    g_w = jnp.max(pg, axis=1, keepdims=True)
    g_idx = _first_lane_where(jnp.logical_and(is_g, pg == g_w), lane)

    e_lo = ROUTER_E0 + EXP_PER_GROUP * g_idx
    is_e = jnp.logical_and(lane >= e_lo, lane < e_lo + EXP_PER_GROUP)
    me = jnp.max(jnp.where(is_e, lg, NEG_INF), axis=1, keepdims=True)
    ee = jnp.where(is_e, jnp.exp(lg - me), 0.0)
    pe = ee / jnp.sum(ee, axis=1, keepdims=True)
    e1 = jnp.max(pe, axis=1, keepdims=True)
    i1 = _first_lane_where(jnp.logical_and(is_e, pe == e1), lane)
    rest = jnp.logical_and(is_e, lane != i1)
    e2 = jnp.max(jnp.where(rest, pe, -1.0), axis=1, keepdims=True)
    i2 = _first_lane_where(jnp.logical_and(rest, pe == e2), lane)
    den = e1 + e2
    gate = jnp.where(lane == i1, e1 / den, jnp.where(lane == i2, e2 / den, 0.0)) * g_w
    group_gates = [jnp.where(lane < EXP_PER_GROUP, pltpu.roll(gate, LANES - (ROUTER_E0 + EXP_PER_GROUP * g), 1), 0.0)
                   for g in range(N_GROUPS)]
    if not sort_refs:
        for g in range(N_GROUPS):
            gate_ref[:, g * LANES:(g + 1) * LANES] = group_gates[g]
        return

    pt_ref, meta_ref = sort_refs
    gate_ref[...] = (group_gates[0] + group_gates[1]) + (group_gates[2] + group_gates[3])
    tm = lg.shape[0]
    onehot = jnp.where(lane == g_idx, 1.0, 0.0)
    ri = lax.broadcasted_iota(jnp.int32, (tm, tm), 0)
    cj = lax.broadcasted_iota(jnp.int32, (tm, tm), 1)
    earlier = jnp.where(cj < ri, 1.0, 0.0).astype(BF16)
    rank = jnp.sum(_dot(earlier, onehot.astype(BF16)) * onehot, axis=1, keepdims=True)
    cnt = jnp.sum(onehot, axis=0, keepdims=True)
    cnt_pad = jnp.floor((cnt + (SORT_PAD - 1.0)) * (1.0 / SORT_PAD)) * SORT_PAD
    seg_off = pltpu.roll(cnt_pad, 1, 1) + pltpu.roll(cnt_pad, 2, 1) + pltpu.roll(cnt_pad, 3, 1)
    dest = (rank + jnp.sum(onehot * seg_off, axis=1, keepdims=True)).astype(jnp.int32)
    col = lax.broadcasted_iota(jnp.int32, pt_ref.shape, 1)
    pt_ref[...] = jnp.where(col == dest, 1.0, 0.0).astype(BF16)
    meta_ref[0] = jnp.concatenate([cnt_pad, seg_off, jnp.zeros((SUBLANES - 2, LANES), F32)], axis=0)


def _sort_rows(tm):
    return -(-(tm + N_GROUPS * SORT_PAD) // LANES) * LANES


def _post(att, sa, mb, x, w, *, tm, sort):
    n, d = x.shape
    row = lambda t: (t, 0)
    const = lambda t: (0, 0)
    gate_w = LANES if sort else N_GROUPS * LANES
    out_specs = [pl.BlockSpec((tm, d), row), pl.BlockSpec((tm, d), row), pl.BlockSpec((tm, gate_w), row)]
    out_shape = [jax.ShapeDtypeStruct((n, d), F32), jax.ShapeDtypeStruct((n, d), BF16),
                 jax.ShapeDtypeStruct((n, gate_w), F32)]
    if sort:
        out_specs += [pl.BlockSpec((tm, _sort_rows(tm)), row), pl.BlockSpec((1, SUBLANES, LANES), lambda t: (t, 0, 0))]
        out_shape += [jax.ShapeDtypeStruct((n, _sort_rows(tm)), BF16),
                      jax.ShapeDtypeStruct((n // tm, SUBLANES, LANES), F32)]
    return pl.pallas_call(
        _post_kernel,
        grid=(n // tm,),
        in_specs=[pl.BlockSpec((tm, ATT_WIDTH), row), pl.BlockSpec((tm, d), row), pl.BlockSpec((tm, d), row),
                  pl.BlockSpec((tm, d), row), pl.BlockSpec(w["wpa"].shape, const), pl.BlockSpec(w["wout"].shape, const),
                  pl.BlockSpec((1, d), const), pl.BlockSpec(w["wr"].shape, const), pl.BlockSpec((1, ROUTER_LANES), const)],
        out_specs=out_specs,
        out_shape=out_shape,
        compiler_params=pltpu.CompilerParams(dimension_semantics=("arbitrary",), vmem_limit_bytes=VMEM_LIMIT),
        name="post_attn",
    )(att, sa, mb, x, w["wpa"], w["wout"], w["g2"], w["wr"], w["br"])


def _moe_kernel(xn2_ref, gate_ref, h_ref, wg_ref, wu_ref, wd_ref, fn_ref, o_ref, acc_ref, *, final):
    g = pl.program_id(1)

    @pl.when(g == 0)
    def _():
        acc_ref[...] = jnp.zeros_like(acc_ref)

    x = xn2_ref[...]
    gate = gate_ref[...]
    act = jax.nn.silu(_dot(x, wg_ref[0])) * _dot(x, wu_ref[0])
    parts = []
    for e in range(EXP_PER_GROUP):
        parts.append((act[:, e * EXP_HIDDEN:(e + 1) * EXP_HIDDEN] * gate[:, e:e + 1]).astype(BF16))
    acc_ref[...] += _dot(jnp.concatenate(parts, axis=1), wd_ref[0])

    @pl.when(g == N_GROUPS - 1)
    def _():
        y = h_ref[...] + acc_ref[...]
        o_ref[...] = _rms(y, fn_ref[...]) if final else y


def _moe(xn2, gate, h, w, fn, *, tm, final):
    n, d = h.shape
    row = lambda t, g: (t, 0)
    wspec = lambda shp: pl.BlockSpec((1,) + shp[1:], lambda t, g: (g, 0, 0))
    return pl.pallas_call(
        functools.partial(_moe_kernel, final=final),
        grid=(n // tm, N_GROUPS),
        in_specs=[pl.BlockSpec((tm, d), row), pl.BlockSpec((tm, LANES), lambda t, g: (t, g)), pl.BlockSpec((tm, d), row),
                  wspec(w["weg"].shape), wspec(w["weu"].shape), wspec(w["wed"].shape),
                  pl.BlockSpec((1, d), lambda t, g: (0, 0))],
        out_specs=pl.BlockSpec((tm, d), row),
        out_shape=jax.ShapeDtypeStruct((n, d), F32),
        scratch_shapes=[pltpu.VMEM((tm, d), F32)],
        compiler_params=pltpu.CompilerParams(dimension_semantics=("arbitrary", "arbitrary"),
                                             vmem_limit_bytes=VMEM_LIMIT),
        name="moe",
    )(xn2, gate, h, w["weg"], w["weu"], w["wed"], fn)


def _moe_sorted_kernel(meta_ref, xn2_ref, gate_ref, h_ref, pt_ref, wg_ref, wu_ref, wd_ref, fn_ref, o_ref,
                       xs_scr, gs_scr, ys_scr, *, final):
    t = pl.program_id(0)
    rows = pt_ref.shape[1]
    pt = pt_ref[...]
    perm = pt.astype(F32).T.astype(BF16)
    xs_scr[0:rows, :] = _dot(perm, xn2_ref[...]).astype(BF16)
    xs_scr[rows:rows + MOE_CHUNK, :] = jnp.zeros((MOE_CHUNK, xs_scr.shape[1]), BF16)
    gate = gate_ref[...]
    g_hi = gate.astype(BF16)
    g_rest = gate - g_hi.astype(F32)
    g_mid = g_rest.astype(BF16)
    g_lo = (g_rest - g_mid.astype(F32)).astype(BF16)
    gs_scr[0:rows, :] = _dot(perm, g_hi) + (_dot(perm, g_mid) + _dot(perm, g_lo))
    gs_scr[rows:rows + MOE_CHUNK, :] = jnp.zeros((MOE_CHUNK, LANES), F32)
    ys_scr[...] = jnp.zeros_like(ys_scr)

    for g in range(N_GROUPS):
        cnt = meta_ref[t * 2 * N_GROUPS + g]
        seg = meta_ref[t * 2 * N_GROUPS + N_GROUPS + g]

        def chunk(c, carry, g=g, cnt=cnt, seg=seg):
            r0 = pl.multiple_of(seg + c * MOE_CHUNK, SORT_PAD)
            x = xs_scr[pl.ds(r0, MOE_CHUNK), :]
            gt = gs_scr[pl.ds(r0, MOE_CHUNK), :]
            act = jax.nn.silu(_dot(x, wg_ref[g])) * _dot(x, wu_ref[g])
            parts = [(act[:, e * EXP_HIDDEN:(e + 1) * EXP_HIDDEN] * gt[:, e:e + 1]).astype(BF16)
                     for e in range(EXP_PER_GROUP)]
            y = _dot(jnp.concatenate(parts, axis=1), wd_ref[g])
            in_seg = lax.broadcasted_iota(jnp.int32, (MOE_CHUNK, 1), 0) + c * MOE_CHUNK < cnt
            ys_scr[pl.ds(r0, MOE_CHUNK), :] = jnp.where(in_seg, y, ys_scr[pl.ds(r0, MOE_CHUNK), :])
            return carry

        lax.fori_loop(0, (cnt + MOE_CHUNK - 1) // MOE_CHUNK, chunk, 0)

    y = h_ref[...] + _dot(pt, ys_scr[0:rows, :].astype(BF16))
    o_ref[...] = _rms(y, fn_ref[...]) if final else y


def _moe_sorted(xn2, gate, h, pt, meta, w, fn, *, tm, final):
    n, d = h.shape
    rows = pt.shape[1]
    row = lambda t, m: (t, 0)
    resident = lambda shp: pl.BlockSpec(shp, lambda t, m: (0, 0, 0), pipeline_mode=pl.Buffered(1))
    grid_spec = pltpu.PrefetchScalarGridSpec(
        num_scalar_prefetch=1,
        grid=(n // tm,),
        in_specs=[pl.BlockSpec((tm, d), row), pl.BlockSpec((tm, LANES), row), pl.BlockSpec((tm, d), row),
                  pl.BlockSpec((tm, rows), row),
                  resident(w["weg"].shape), resident(w["weu"].shape), resident(w["wed"].shape),
                  pl.BlockSpec((1, d), lambda t, m: (0, 0))],
        out_specs=pl.BlockSpec((tm, d), row),
        scratch_shapes=[pltpu.VMEM((rows + MOE_CHUNK, d), BF16), pltpu.VMEM((rows + MOE_CHUNK, LANES), F32),
                        pltpu.VMEM((rows + MOE_CHUNK, d), F32)],
    )
    return pl.pallas_call(
        functools.partial(_moe_sorted_kernel, final=final),
        grid_spec=grid_spec,
        out_shape=jax.ShapeDtypeStruct((n, d), F32),
        compiler_params=pltpu.CompilerParams(dimension_semantics=("arbitrary",), vmem_limit_bytes=VMEM_LIMIT),
        name="moe_sorted",
    )(meta, xn2, gate, h, pt, w["weg"], w["weu"], w["wed"], fn)


def _pack_experts_kernel(w_ref, o_ref):
    for e in range(EXP_PER_GROUP):
        o_ref[:, e * EXP_HIDDEN:(e + 1) * EXP_HIDDEN] = w_ref[e].astype(BF16)


def _pack_experts(w):
    depth, n_groups, n_exp, d, f = w.shape
    rows = 2 * MXU_DIM
    return pl.pallas_call(
        _pack_experts_kernel,
        grid=(depth, n_groups, d // rows),
        in_specs=[pl.BlockSpec((None, None, n_exp, rows, f), lambda l, g, r: (l, g, 0, r, 0))],
        out_specs=pl.BlockSpec((None, None, rows, n_exp * f), lambda l, g, r: (l, g, r, 0)),
        out_shape=jax.ShapeDtypeStruct((depth, n_groups, d, n_exp * f), BF16),
        compiler_params=pltpu.CompilerParams(dimension_semantics=("arbitrary",) * 3, vmem_limit_bytes=VMEM_LIMIT),
        name="pack_experts",
    )(w)


def _rope_tables(pos, d):
    inv = ROPE_THETA ** (-jnp.arange(0, d, 2, dtype=F32) / d)
    ang = pos[:, None] * inv[None, :]
    cos, sin = jnp.cos(ang), jnp.sin(ang)
    reps = LANES // d
    return (jnp.tile(jnp.concatenate([cos, cos], axis=-1), (1, reps)),
            jnp.tile(jnp.concatenate([-sin, sin], axis=-1), (1, reps)))


def _layer_weights(l, norm1, w_in, sgu_ln_g, sgu_ln_b, sgu_w, sgu_b, w_pa, w_pb, w_out, norm2,
                   w_rg, b_rg, w_re, b_re, w_eg, w_eu, w_ed, sgu_chunk):
    d = w_in.shape[1]
    a_end = ATT_WIDTH + 2 * KV_WIDTH + IDX_WIDTH
    b_end = a_end + IDX_DIM + IDX_HEADS
    reps = SGU_CHUNK // sgu_chunk
    n_router = N_GROUPS + N_GROUPS * EXP_PER_GROUP
    return dict(
        g1=norm1[l][None, :],
        wa=w_in[l][:, :a_end].astype(BF16),
        wb=jnp.pad(w_in[l][:, a_end:b_end], ((0, 0), (0, LANES - (b_end - a_end)))).astype(BF16),
        wc=w_in[l][:, b_end:].astype(BF16),
        sguw=jnp.tile(sgu_w[l][:, :sgu_chunk, :sgu_chunk], (1, reps, reps)),
        sgub=jnp.repeat(jnp.tile(sgu_b[l][:, :sgu_chunk], (1, reps)).T, SGU_GDIM, axis=1),
        lng=sgu_ln_g[l][None, :], lnb=sgu_ln_b[l][None, :],
        wpb=w_pb[l].astype(BF16), wpa=w_pa[l].astype(BF16), wout=w_out[l].astype(BF16),
        g2=norm2[l][None, :],
        wr=jnp.pad(jnp.concatenate([w_rg[l], w_re[l]], axis=1), ((0, 0), (0, ROUTER_LANES - n_router))),
        br=jnp.pad(jnp.concatenate([b_rg[l], b_re[l]]), (0, ROUTER_LANES - n_router))[None, :],
        weg=w_eg[l], weu=w_eu[l],
        wed=w_ed[l].reshape(N_GROUPS, GROUP_HIDDEN, d).astype(BF16),
    )


def _token_tile(n):
    return 512 if n % 512 == 0 else n


def kernel(x_prompt, x_sample, cache_k, cache_v, cache_kidx, norm1, w_in, sgu_ln_g, sgu_ln_b, sgu_w, sgu_b,
           w_pa, w_pb, w_out, norm2, w_rg, b_rg, w_re, b_re, w_eg, w_eu, w_ed, final_norm):
    bp, tp, d = x_prompt.shape
    bs, ts, _ = x_sample.shape
    depth, _, past = cache_k.shape[:3]
    np_, ns = bp * tp, bs * ts
    tm_p, tm_s = _token_tile(tp), _token_tile(ns)
    assert tp % tm_p == 0 and tp % Q_BLOCK == 0 and SGU_CHUNK % ts == 0 and tm_s % SGU_CHUNK == 0
    assert tm_p % KEY_SUB == 0 and tm_s % KEY_SUB == 0

    pos_p = jnp.arange(tp, dtype=F32)
    pos_s = jnp.tile(past + jnp.arange(ts, dtype=F32), tm_s // ts)
    tabs_p = _rope_tables(pos_p, HEAD_DIM) + _rope_tables(pos_p, IDX_DIM)
    tabs_s = _rope_tables(pos_s, HEAD_DIM) + _rope_tables(pos_s, IDX_DIM)
    ckt = jnp.transpose(cache_k, (0, 1, 3, 4, 2)).reshape(depth, bs, KV_WIDTH, past)
    cvt = jnp.transpose(cache_v, (0, 1, 3, 4, 2)).reshape(depth, bs, KV_WIDTH, past)
    ckit = jnp.transpose(cache_kidx, (0, 1, 3, 2))
    fn = final_norm[None, :]
    weg_all, weu_all = _pack_experts(w_eg), _pack_experts(w_eu)

    hp = x_prompt.reshape(np_, d)
    hs = x_sample.reshape(ns, d)
    outs = [[] for _ in range(4)]
    kv_bufs = None
    for l in range(depth):
        wargs = (norm1, w_in, sgu_ln_g, sgu_ln_b, sgu_w, sgu_b, w_pa, w_pb, w_out, norm2,
                 w_rg, b_rg, w_re, b_re, weg_all, weu_all, w_ed)
        wp = _layer_weights(l, *wargs, sgu_chunk=SGU_CHUNK)
        ws = dict(wp)
        ws.update({k: v for k, v in _layer_weights(l, *wargs, sgu_chunk=ts).items() if k in ("sguw", "sgub")})
        final = l == depth - 1

        q, kt_all, kb, vt_all, _, vt, qi, kit_all, kib, wi, mb, sa = _inproj(
            hp, wp, tabs_p, tm=tm_p, sgu_chunk=SGU_CHUNK, want_vln=False,
            kv_layer=l, kv_bufs=kv_bufs, kv_shape=(depth, bp, tp))
        kv_bufs = (kt_all, vt_all, kit_all)
        r3 = lambda a: a.reshape(bp, tp, a.shape[-1])
        att = _dsa_prompt(q, qi, wi, r3(kb), vt, r3(kib)).reshape(np_, ATT_WIDTH)
        h, xn2, gate, pt, meta = _post(att, sa, mb, hp, wp, tm=tm_p, sort=True)
        meta = meta[:, 0:2, 0:N_GROUPS].astype(jnp.int32).reshape(-1)
        hp = _moe_sorted(xn2, gate, h, pt, meta, wp, fn, tm=tm_p, final=final)

        q, kf, kb, vf, vb, _, qi, kif, kib, wi, mb, sa, vln = _inproj(hs, ws, tabs_s, tm=tm_s, sgu_chunk=ts,
                                                                      want_vln=True)
        r3 = lambda a: a.reshape(bs, ts, a.shape[-1])
        att = _dsa_sample(r3(q), r3(qi), r3(wi), r3(kb), r3(vb), r3(kib), ckt, cvt, ckit, l)
        h, xn2, gate = _post(att.reshape(ns, ATT_WIDTH), sa, mb, hs, ws, tm=tm_s, sort=False)
        hs = _moe(xn2, gate, h, ws, fn, tm=tm_s, final=final)
        outs[0].append(kf.reshape(bs, ts, N_KV_HEADS, HEAD_DIM))
        outs[1].append(vf.reshape(bs, ts, N_KV_HEADS, HEAD_DIM))
        outs[2].append(kif.reshape(bs, ts, IDX_DIM))
        outs[3].append(vln.reshape(bs, ts, SGU_WIDTH))

    kt_all, vt_all, kit_all = kv_bufs
    heads = lambda a: jnp.transpose(a.reshape(depth, bp, N_KV_HEADS, HEAD_DIM, tp), (0, 1, 4, 2, 3))
    return ((hp.reshape(bp, tp, d), hs.reshape(bs, ts, d), heads(kt_all), heads(vt_all),
             jnp.transpose(kit_all, (0, 1, 3, 2))) + tuple(jnp.stack(o) for o in outs))
```

```python
import functools

import jax
import jax.numpy as jnp
from jax import lax
from jax.experimental import pallas as pl
from jax.experimental.pallas import tpu as pltpu

F32 = jnp.float32
BF16 = jnp.bfloat16

CHUNK = 64
N_HEADS = 8
N_KV_HEADS = 2
HEAD_DIM = 64
GQA_GROUP = N_HEADS // N_KV_HEADS
ATT_WIDTH = N_HEADS * HEAD_DIM
KV_WIDTH = N_KV_HEADS * HEAD_DIM
IDX_HEADS = 8
IDX_DIM = 32
IDX_WIDTH = IDX_HEADS * IDX_DIM
TOPK_MAX = 256
Q_BLOCK = 128
DSA_QUERIES = 256
SGU_GROUPS = 4
SGU_CHUNK = 128
SGU_WIDTH = 512
SGU_GDIM = SGU_WIDTH // SGU_GROUPS
N_GROUPS = 4
EXP_PER_GROUP = 8
EXP_HIDDEN = 128
GROUP_HIDDEN = EXP_PER_GROUP * EXP_HIDDEN
ROPE_THETA = 10000.0
EPS = 1e-6
Q_SCALE = HEAD_DIM ** -0.5 * 1.4426950408889634

LANES = 128
SUBLANES = 8
MXU_DIM = 256
ROUTER_LANES = LANES
ROUTER_E0 = N_GROUPS
VMEM_LIMIT = 56 * 1024 * 1024
KEY_TILE = 2 * MXU_DIM
KEY_SUB = MXU_DIM
FOLD_WAYS = 4
COUNT_ROWS = 64
SEARCH_FIRST_STEPS = 17
SEARCH_SECOND_STEPS = 0
SEARCH_ROUND_STEPS = 1
SEARCH_MAX_ROUNDS = 34
SORT_PAD = 16
MOE_CHUNK = 144
KNORM_SLACK = 1.0 + 2.0 ** -6
SOFTMAX_SUM_FLOOR = 2.0 ** -80
MAX_SUBNORMAL_BITS = 0x007FFFFF
NEG_INF = float("-inf")


def _dot(a, b):
    return jnp.dot(a, b, preferred_element_type=F32)


def _dot_t(a, b):
    return lax.dot_general(a, b, (((1,), (1,)), ((), ())), preferred_element_type=F32)


def _rms(x, g):
    return x * lax.rsqrt(jnp.mean(x * x, axis=-1, keepdims=True) + EPS) * g


def _swap_halves(x, d):
    lane = lax.broadcasted_iota(jnp.int32, x.shape, 1)
    first = (lane % d) < (d // 2)
    return jnp.where(first, pltpu.roll(x, LANES - d // 2, 1), pltpu.roll(x, d // 2, 1))


def _rope(x, cos, sin_signed, d):
    return x * cos + _swap_halves(x, d) * sin_signed


def _inproj_kernel(*refs, sgu_chunk, kv_transposed, n_alias, want_vln):
    (x_ref, g1_ref, wa_ref, wb_ref, wc_ref, cq_ref, sq_ref, ci_ref, si_ref,
     sguw_ref, sgub_ref, lng_ref, lnb_ref, wpb_ref) = refs[:14]
    (q_ref, kf_ref, kb_ref, vf_ref, vb_ref, vt_ref, qi_ref, kif_ref, kib_ref, wi_ref,
     mb_ref, sa_ref) = refs[14 + n_alias:26 + n_alias]
    tm = x_ref.shape[0]
    xn = _rms(x_ref[...], g1_ref[...]).astype(BF16)
    cq, sq, ci, si = cq_ref[...], sq_ref[...], ci_ref[...], si_ref[...]

    a = _dot(xn, wa_ref[...])
    def emit_queries(ref, val):
        if kv_transposed:
            for c in range(tm // DSA_QUERIES):
                ref[c] = val[c * DSA_QUERIES:(c + 1) * DSA_QUERIES, :].T[0:ref.shape[1], :].astype(ref.dtype)
        else:
            ref[...] = val.astype(ref.dtype)

    emit_queries(q_ref, jnp.concatenate(
        [_rope(a[:, c * LANES:(c + 1) * LANES], cq, sq, HEAD_DIM) * Q_SCALE for c in range(ATT_WIDTH // LANES)], axis=1))
    k = _rope(a[:, ATT_WIDTH:ATT_WIDTH + KV_WIDTH], cq, sq, HEAD_DIM)
    kb_ref[...] = k.astype(BF16)
    v = a[:, ATT_WIDTH + KV_WIDTH:ATT_WIDTH + 2 * KV_WIDTH]
    vb_ref[...] = v.astype(BF16)
    if kv_transposed:
        kf_ref[...] = k.T
        vf_ref[...] = v.T
    else:
        kf_ref[...] = k
        vf_ref[...] = v
    for c in range(tm // KEY_SUB):
        vt_ref[c] = v[c * KEY_SUB:(c + 1) * KEY_SUB, :].T.astype(BF16)
    qi0 = ATT_WIDTH + 2 * KV_WIDTH
    emit_queries(qi_ref, jnp.concatenate(
        [_rope(a[:, qi0 + c * LANES:qi0 + (c + 1) * LANES], ci, si, IDX_DIM) for c in range(IDX_WIDTH // LANES)], axis=1))

    b = _dot(xn, wb_ref[...])
    ki_wide = _rope(b, ci, si, IDX_DIM)
    ki = ki_wide[:, :IDX_DIM]
    kif_ref[...] = ki_wide.T[0:IDX_DIM, :] if kv_transposed else ki
    kib_ref[...] = ki.astype(BF16)
    emit_queries(wi_ref, pltpu.roll(b, LANES - IDX_DIM, 1) * (IDX_HEADS ** -0.5))

    u = jax.nn.gelu(_dot(xn, wc_ref[:, 0:SGU_WIDTH]))
    vs = jax.nn.gelu(_dot(xn, wc_ref[:, SGU_WIDTH:2 * SGU_WIDTH]))
    mu = jnp.mean(vs, axis=-1, keepdims=True)
    var = jnp.mean(jnp.square(vs - mu), axis=-1, keepdims=True)
    vln = (vs - mu) * lax.rsqrt(var + EPS) * lng_ref[...] + lnb_ref[...]
    if want_vln:
        refs[26 + n_alias][...] = vln
    ri = lax.broadcasted_iota(jnp.int32, (SGU_CHUNK, SGU_CHUNK), 0)
    cj = lax.broadcasted_iota(jnp.int32, (SGU_CHUNK, SGU_CHUNK), 1)
    mix_mask = (cj <= ri) & ((ri // sgu_chunk) == (cj // sgu_chunk))
    vln_b = vln.astype(BF16)
    w_eff = [jnp.where(mix_mask, sguw_ref[g], 0.0).astype(BF16) for g in range(SGU_GROUPS)]
    sg_rows = []
    for r in range(tm // SGU_CHUNK):
        rows = slice(r * SGU_CHUNK, (r + 1) * SGU_CHUNK)
        mixed = [_dot(w_eff[g], vln_b[rows, g * SGU_GDIM:(g + 1) * SGU_GDIM]) for g in range(SGU_GROUPS)]
        sg_rows.append(u[rows, :] * (jnp.concatenate(mixed, axis=1) + sgub_ref[...]))
    sg = jnp.concatenate(sg_rows, axis=0).astype(BF16)
    d = x_ref.shape[1]
    sa_ref[...] = jax.nn.sigmoid(_dot(xn, wc_ref[:, 2 * SGU_WIDTH:2 * SGU_WIDTH + d])).astype(BF16)
    gb = jax.nn.sigmoid(_dot(xn, wc_ref[:, 2 * SGU_WIDTH + d:]))
    mb_ref[...] = (gb * _dot(sg, wpb_ref[...])).astype(BF16)


def _inproj(x, w, tabs, *, tm, sgu_chunk, want_vln, kv_layer=None, kv_bufs=None, kv_shape=None):
    n, d = x.shape
    n_tab_tiles = tabs[0].shape[0] // tm
    row = lambda t: (t, 0)
    const = lambda t: (0, 0)
    tab = lambda t: (t % n_tab_tiles, 0)
    kv_transposed = kv_layer is not None
    if kv_transposed:
        depth, streams, steps = kv_shape
        per_stream = steps // tm
        kv_sds = lambda wd: jax.ShapeDtypeStruct((depth, streams, wd, steps), F32)
        kv_spec = lambda wd: pl.BlockSpec((None, None, wd, tm),
                                          lambda t: (kv_layer, t // per_stream, 0, t % per_stream))
    else:
        kv_sds = lambda wd: jax.ShapeDtypeStruct((n, wd), F32)
        kv_spec = lambda wd: pl.BlockSpec((tm, wd), row)
    tiled = lambda wd, dt: (jax.ShapeDtypeStruct((n, wd), dt), pl.BlockSpec((tm, wd), row))
    if kv_transposed:
        qb = DSA_QUERIES
        query = lambda wd, rows, dt: (jax.ShapeDtypeStruct((n // qb, rows, qb), dt),
                                      pl.BlockSpec((tm // qb, rows, qb), lambda t: (t, 0, 0)))
    else:
        query = lambda wd, rows, dt: tiled(wd, dt)
    outs = [
        query(ATT_WIDTH, ATT_WIDTH, BF16),
        (kv_sds(KV_WIDTH), kv_spec(KV_WIDTH)),
        tiled(KV_WIDTH, BF16),
        (kv_sds(KV_WIDTH), kv_spec(KV_WIDTH)),
        tiled(KV_WIDTH, BF16),
        (jax.ShapeDtypeStruct((n // KEY_SUB, KV_WIDTH, KEY_SUB), BF16),
         pl.BlockSpec((tm // KEY_SUB, KV_WIDTH, KEY_SUB), lambda t: (t, 0, 0))),
        query(IDX_WIDTH, IDX_WIDTH, BF16),
        (kv_sds(IDX_DIM), kv_spec(IDX_DIM)),
        tiled(IDX_DIM, BF16),
        query(LANES, IDX_HEADS, F32),
        tiled(d, BF16),
        tiled(d, BF16),
    ]
    if want_vln:
        outs.append(tiled(SGU_WIDTH, F32))
    in_specs = [
        pl.BlockSpec((tm, d), row),
        pl.BlockSpec((1, d), const),
        pl.BlockSpec(w["wa"].shape, const),
        pl.BlockSpec(w["wb"].shape, const),
        pl.BlockSpec(w["wc"].shape, const),
        pl.BlockSpec((tm, LANES), tab), pl.BlockSpec((tm, LANES), tab),
        pl.BlockSpec((tm, LANES), tab), pl.BlockSpec((tm, LANES), tab),
        pl.BlockSpec(w["sguw"].shape, lambda t: (0, 0, 0)),
        pl.BlockSpec(w["sgub"].shape, const),
        pl.BlockSpec((1, SGU_WIDTH), const),
        pl.BlockSpec((1, SGU_WIDTH), const),
        pl.BlockSpec(w["wpb"].shape, const),
    ]
    args = [x, w["g1"], w["wa"], w["wb"], w["wc"], *tabs, w["sguw"], w["sgub"], w["lng"], w["lnb"], w["wpb"]]
    aliases = {}
    if kv_bufs is not None:
        for buf, out_idx in zip(kv_bufs, (1, 3, 7)):
            aliases[len(args)] = out_idx
            in_specs.append(pl.BlockSpec(memory_space=pl.ANY))
            args.append(buf)
    return pl.pallas_call(
        functools.partial(_inproj_kernel, sgu_chunk=sgu_chunk, kv_transposed=kv_transposed,
                          n_alias=len(aliases), want_vln=want_vln),
        grid=(n // tm,),
        in_specs=in_specs,
        out_specs=[o[1] for o in outs],
        out_shape=[o[0] for o in outs],
        input_output_aliases=aliases,
        compiler_params=pltpu.CompilerParams(dimension_semantics=("arbitrary",), vmem_limit_bytes=VMEM_LIMIT),
        name="inproj",
    )(*args)


def _float_key(x):
    b = lax.bitcast_convert_type(x, jnp.int32)
    k = b ^ ((b >> 31) & jnp.int32(0x7FFFFFFF))
    return jnp.where(k > MAX_SUBNORMAL_BITS, k - MAX_SUBNORMAL_BITS,
                     jnp.where(k < -MAX_SUBNORMAL_BITS - 1, k + MAX_SUBNORMAL_BITS, jnp.clip(k, -1, 0)))


def _key_float(k):
    k = jnp.where(k > 0, k + MAX_SUBNORMAL_BITS, jnp.where(k < -1, k - MAX_SUBNORMAL_BITS, k))
    return lax.bitcast_convert_type(k ^ ((k >> 31) & jnp.int32(0x7FFFFFFF)), F32)


def _fold_rows(x):
    n = x.shape[0] // SUBLANES
    ways = FOLD_WAYS if n % FOLD_WAYS == 0 else 1
    part = jnp.sum(x.reshape(n // ways, ways, SUBLANES, x.shape[1]), axis=0)
    return jnp.sum(part, axis=0)


def _topk_threshold(count_ge, snap, smin, smax, n_adm, kk):
    lo0 = _float_key(smin + 0.0)
    hx0 = _float_key(smax + 0.0) + 1

    def is_active(st):
        lo, hx, c_lo, _ = st
        return jnp.logical_and(c_lo != kk, lo + 1 < hx)

    def step(st, by_key):
        lo, hx, c_lo, c_hx = st
        active = is_active(st)
        if by_key:
            mid = (lo >> 1) + (hx >> 1) + (lo & hx & 1)
        else:
            mid = _float_key(0.5 * _key_float(lo) + 0.5 * _key_float(hx))
        mid = jnp.where(active, jnp.minimum(jnp.maximum(mid, lo + 1), hx - 1), lo)
        c = count_ge(_key_float(mid))
        up = jnp.logical_and(active, c >= kk)
        down = jnp.logical_and(active, c < kk)
        return (jnp.where(up, mid, lo), jnp.where(down, mid, hx),
                jnp.where(up, c, c_lo), jnp.where(down, c, c_hx))

    def snap_step(st):
        lo, hx, c_lo, c_hx = st
        active = is_active(st)
        first_in, last_in = snap(_key_float(lo), _key_float(hx))
        lo_n = jnp.where(active, _float_key(first_in + 0.0), lo)
        hx_n = jnp.where(active, _float_key(last_in + 0.0) + 1, hx)
        few = jnp.logical_and(active, c_lo - c_hx <= 2.0)
        same = first_in == last_in
        take_last = kk - c_hx <= 1.0
        thr_key = _float_key(jnp.where(take_last, last_in, first_in) + 0.0)
        c_ge = jnp.where(jnp.logical_and(take_last, jnp.logical_not(same)), c_hx + 1.0, c_lo)
        c_gt = jnp.where(jnp.logical_or(take_last, same), c_hx, c_hx + 1.0)
        return (jnp.where(few, thr_key, lo_n), jnp.where(few, thr_key + 1, hx_n),
                jnp.where(few, c_ge, c_lo), jnp.where(few, c_gt, c_hx))

    def n_active(st):
        return jnp.max(jnp.where(is_active(st), 1.0, 0.0))

    st = (lo0, hx0, n_adm, jnp.zeros_like(n_adm))
    for _ in range(SEARCH_FIRST_STEPS):
        st = step(st, by_key=False)
    st = snap_step(st)
    for _ in range(SEARCH_SECOND_STEPS):
        st = step(st, by_key=False)
    act = n_active(st)

    def cond(carry):
        it, act, _ = carry
        return jnp.logical_and(it < SEARCH_MAX_ROUNDS, act > 0.5)

    def body(carry):
        it, _, st = carry
        st = step(snap_step(st), by_key=True)
        for _ in range(SEARCH_ROUND_STEPS):
            st = step(st, by_key=False)
        return it + 1, n_active(st), st

    _, _, (lo, _, c_lo, c_hx) = lax.while_loop(cond, body, (jnp.int32(0), act, st))
    return _key_float(lo), c_lo, c_hx


def _demote_excess_ties(s_ref, n_chunks, thr, need):
    ri = lax.broadcasted_iota(jnp.int32, (LANES, LANES), 0)
    cj = lax.broadcasted_iota(jnp.int32, (LANES, LANES), 1)
    lower = jnp.where(cj <= ri, 1.0, 0.0).astype(BF16)

    def body(j, run):
        off = pl.multiple_of(j * LANES, LANES)
        blk = s_ref[pl.ds(off, LANES), :]
        tied = blk == thr
        tied_f = jnp.where(tied, 1.0, 0.0)
        prefix = _dot(lower, tied_f.astype(BF16)) + run
        s_ref[pl.ds(off, LANES), :] = jnp.where(jnp.logical_and(tied, prefix > need), NEG_INF, blk)
        return run + jnp.sum(tied_f, axis=0, keepdims=True)

    lax.fori_loop(0, n_chunks, body, jnp.zeros((1, s_ref.shape[1]), F32))


def _select_mask(s_ref, n_tiles, tk, smin, smax, n_adm, topk, selected=0.0):
    kk = jnp.minimum(n_adm, float(topk))

    def count_ge(t):
        def body(j, acc):
            off = pl.multiple_of(j * tk, tk)
            for r in range(0, tk, COUNT_ROWS):
                acc = acc + _fold_rows(jnp.where(s_ref[pl.ds(off + r, COUNT_ROWS), :] >= t, 1.0, 0.0))
            return acc
        acc = lax.fori_loop(0, n_tiles, body, jnp.zeros((SUBLANES, s_ref.shape[1]), F32))
        return jnp.sum(acc, axis=0, keepdims=True)

    def snap(lo_f, hx_f):
        def body(j, carry):
            first_in, last_in = carry
            off = pl.multiple_of(j * tk, tk)
            for r in range(0, tk, COUNT_ROWS):
                blk = s_ref[pl.ds(off + r, COUNT_ROWS), :]
                first_in = jnp.minimum(first_in, -_fold_max(jnp.where(blk >= lo_f, -blk, NEG_INF)))
                last_in = jnp.maximum(last_in, _fold_max(jnp.where(blk < hx_f, blk, NEG_INF)))
            return first_in, last_in
        first_in, last_in = lax.fori_loop(0, n_tiles, body,
                                          (jnp.full((SUBLANES, s_ref.shape[1]), -NEG_INF, F32),
                                           jnp.full((SUBLANES, s_ref.shape[1]), NEG_INF, F32)))
        return jnp.min(first_in, axis=0, keepdims=True), jnp.max(last_in, axis=0, keepdims=True)

    thr, c_ge, c_gt = _topk_threshold(count_ge, snap, smin, smax, n_adm, kk)

    @pl.when(jnp.max(c_ge - kk) > 0.5)
    def _():
        _demote_excess_ties(s_ref, n_tiles * (tk // LANES), thr, kk - c_gt)

    def mask_body(j, carry):
        off = pl.multiple_of(j * tk, tk)
        s_ref[pl.ds(off, tk), :] = jnp.where(s_ref[pl.ds(off, tk), :] >= thr, selected, NEG_INF)
        return carry

    lax.fori_loop(0, n_tiles, mask_body, 0)


def _dsa_prompt_kernel(q_ref, qi_ref, wi_ref, k_ref, vt_ref, ki_ref, o_ref, s_scr, acc_scr, l_scr, kn_scr, lg_scr,
                       *, topk):
    i = pl.program_id(1)
    tk, ts = KEY_TILE, KEY_SUB
    qb = q_ref.shape[2]
    n_keys = (i + 1) * qb
    n_tiles = (n_keys + tk - 1) // tk
    qcol = lax.broadcasted_iota(jnp.int32, (1, qb), 1)
    lim = ((i * qb + qcol) // CHUNK + 1) * CHUNK

    @pl.when(i == 0)
    def _():
        ri = lax.broadcasted_iota(jnp.int32, (KV_WIDTH, LANES), 0)
        cj = lax.broadcasted_iota(jnp.int32, (KV_WIDTH, LANES), 1)
        head_sum = jnp.where(ri // HEAD_DIM == cj, 1.0, 0.0).astype(BF16)

        def body(j, acc):
            kt = k_ref[0, pl.ds(pl.multiple_of(j * tk, tk), tk), :].astype(F32)
            return jnp.maximum(acc, _fold_max(_dot((kt * kt).astype(BF16), head_sum)))

        acc = lax.fori_loop(0, k_ref.shape[1] // tk, body, jnp.zeros((SUBLANES, LANES), F32))
        kn_scr[...] = jnp.max(acc, axis=0, keepdims=True) * KNORM_SLACK

    qt = q_ref[0]
    qit = qi_ref[0]
    wit = wi_ref[0]
    qi_rhs = jnp.concatenate([qit[h * IDX_DIM:(h + 1) * IDX_DIM, :] for h in range(IDX_HEADS)], axis=1)
    zero = jnp.zeros((HEAD_DIM, qb), BF16)
    q_cols = []
    for h in range(N_HEADS):
        qh = qt[h * HEAD_DIM:(h + 1) * HEAD_DIM, :]
        q_cols.append(jnp.concatenate([qh, zero] if h < GQA_GROUP else [zero, qh], axis=0))
    q_rhs = jnp.concatenate(q_cols, axis=1)

    def score_tile(j, carry):
        smax, smin = carry
        for c in range(tk // ts):
            off = pl.multiple_of(j * tk + c * ts, ts)
            r = _dot(ki_ref[0, pl.ds(off, ts), :], qi_rhs)
            acc = wit[0:1, :] * jnp.maximum(r[:, 0:qb], 0.0)
            for h in range(1, IDX_HEADS):
                acc = acc + wit[h:h + 1, :] * jnp.maximum(r[:, h * qb:(h + 1) * qb], 0.0)
            key = off + lax.broadcasted_iota(jnp.int32, (ts, qb), 0)
            ok = key < lim
            s_scr[pl.ds(off, ts), :] = jnp.where(ok, acc, NEG_INF)
            smax = jnp.maximum(smax, _fold_max(jnp.where(ok, acc, NEG_INF)))
            smin = jnp.minimum(smin, -_fold_max(jnp.where(ok, -acc, NEG_INF)))
        return smax, smin

    smax, smin = lax.fori_loop(0, n_tiles, score_tile,
                               (jnp.full((SUBLANES, qb), NEG_INF, F32), jnp.full((SUBLANES, qb), -NEG_INF, F32)))
    smax = jnp.max(smax, axis=0, keepdims=True)
    smin = jnp.min(smin, axis=0, keepdims=True)
    qsq = jnp.square(qt.astype(F32))
    kn = kn_scr[...]
    lane1 = lax.broadcasted_iota(jnp.int32, (1, LANES), 1)
    shift = jnp.zeros((1, qb), F32)
    for h in range(N_HEADS):
        qn2 = jnp.sum(_fold_rows(qsq[h * HEAD_DIM:(h + 1) * HEAD_DIM, :]), axis=0, keepdims=True)
        kmax2 = jnp.max(jnp.where(lane1 == h // GQA_GROUP, kn, 0.0))
        shift = jnp.maximum(shift, jnp.sqrt(qn2 * kmax2))
    _select_mask(s_scr, n_tiles, tk, smin, smax, lim.astype(F32), topk, selected=-shift)

    def pv_accumulate(vt, ps, alphas):
        for pair in range(N_HEADS // 2):
            g = (2 * pair) // GQA_GROUP
            o = _dot(vt[g * HEAD_DIM:(g + 1) * HEAD_DIM, :], jnp.concatenate(ps[2 * pair:2 * pair + 2], axis=1))
            for u in range(2):
                h = 2 * pair + u
                rows = slice(h * HEAD_DIM, (h + 1) * HEAD_DIM)
                prev = acc_scr[rows, :] if alphas is None else alphas[h] * acc_scr[rows, :]
                acc_scr[rows, :] = prev + o[:, u * qb:(u + 1) * qb]

    def logits_into(buf, sub):
        off = pl.multiple_of(jnp.minimum(sub * ts, k_ref.shape[1] - ts), ts)
        lg_scr[buf] = _dot(k_ref[0, pl.ds(off, ts), :], q_rhs)

    ones_rows = jnp.ones((2 * SUBLANES, ts), BF16)

    def softmax_pv(buf, sub, l):
        bias = s_scr[pl.ds(pl.multiple_of(sub * ts, ts), ts), :]
        ps = [jnp.exp2(lg_scr[buf, :, h * qb:(h + 1) * qb] + bias).astype(BF16) for h in range(N_HEADS)]
        vt = vt_ref[sub]
        l_rows = []
        for pair in range(N_HEADS // 2):
            g = (2 * pair) // GQA_GROUP
            lhs = jnp.concatenate([vt[g * HEAD_DIM:(g + 1) * HEAD_DIM, :], ones_rows], axis=0)
            o = _dot(lhs, jnp.concatenate(ps[2 * pair:2 * pair + 2], axis=1))
            for u in range(2):
                h = 2 * pair + u
                rows = slice(h * HEAD_DIM, (h + 1) * HEAD_DIM)
                acc_scr[rows, :] = acc_scr[rows, :] + o[0:HEAD_DIM, u * qb:(u + 1) * qb]
                l_rows.append(l[h:h + 1, :] + o[HEAD_DIM:HEAD_DIM + 1, u * qb:(u + 1) * qb])
        return jnp.concatenate(l_rows, axis=0)

    def attn_bounded(j, l):
        logits_into(1, 2 * j + 1)
        l = softmax_pv(0, 2 * j, l)
        logits_into(0, 2 * j + 2)
        return softmax_pv(1, 2 * j + 1, l)

    def attn_online(j, carry):
        m, l = carry
        off = pl.multiple_of(j * ts, ts)
        lg = _dot(k_ref[0, pl.ds(off, ts), :], q_rhs)
        bias = s_scr[pl.ds(off, ts), :]
        m_rows, l_rows, alphas, ps = [], [], [], []
        for h in range(N_HEADS):
            x = lg[:, h * qb:(h + 1) * qb] + bias
            m_new = jnp.maximum(m[h:h + 1, :], jnp.max(_fold_max(x), axis=0, keepdims=True))
            m_use = jnp.where(m_new > NEG_INF, m_new, 0.0)
            alpha = jnp.exp2(m[h:h + 1, :] - m_use)
            p = jnp.exp2(x - m_use)
            l_rows.append(alpha * l[h:h + 1, :] + jnp.sum(_fold_rows(p), axis=0, keepdims=True))
            m_rows.append(m_new)
            alphas.append(alpha)
            ps.append(p.astype(BF16))
        pv_accumulate(vt_ref[j], ps, alphas)
        return jnp.concatenate(m_rows, axis=0), jnp.concatenate(l_rows, axis=0)

    n_sub = (n_keys + ts - 1) // ts
    acc_scr[...] = jnp.zeros_like(acc_scr)
    logits_into(0, 0)
    l_fast = lax.fori_loop(0, n_tiles, attn_bounded, jnp.zeros((N_HEADS, qb), F32))
    l_scr[...] = l_fast

    @pl.when(jnp.min(l_fast) < SOFTMAX_SUM_FLOOR)
    def _():
        acc_scr[...] = jnp.zeros_like(acc_scr)
        _, l_exact = lax.fori_loop(0, n_sub, attn_online,
                                   (jnp.full((N_HEADS, qb), NEG_INF, F32), jnp.zeros((N_HEADS, qb), F32)))
        l_scr[...] = l_exact

    l = l_scr[...]
    out_t = jnp.concatenate([acc_scr[h * HEAD_DIM:(h + 1) * HEAD_DIM, :] / l[h:h + 1, :] for h in range(N_HEADS)],
                            axis=0)
    o_ref[0] = out_t.T.astype(BF16)


def _fold_max(x):
    n = x.shape[0] // SUBLANES
    ways = FOLD_WAYS if n % FOLD_WAYS == 0 else 1
    part = jnp.max(x.reshape(n // ways, ways, SUBLANES, x.shape[1]), axis=0)
    return jnp.max(part, axis=0)


def _dsa_prompt(q, qi, wi, kb, vt, kib):
    b, t, _ = kb.shape
    qb = DSA_QUERIES
    assert t % KEY_TILE == 0 and t % qb == 0 and qb % CHUNK == 0
    topk = min(TOPK_MAX, t // 4)
    blk = lambda w: pl.BlockSpec((1, qb, w), lambda bi, i: (bi, i, 0))
    qblk = lambda rows: pl.BlockSpec((1, rows, qb), lambda bi, i: (bi * (t // qb) + i, 0, 0))
    whole = lambda w: pl.BlockSpec((1, t, w), lambda bi, i: (bi, 0, 0))
    return pl.pallas_call(
        functools.partial(_dsa_prompt_kernel, topk=topk),
        grid=(b, t // qb),
        in_specs=[qblk(ATT_WIDTH), qblk(IDX_WIDTH), qblk(IDX_HEADS), whole(KV_WIDTH),
                  pl.BlockSpec((t // KEY_SUB, KV_WIDTH, KEY_SUB), lambda bi, i: (bi, 0, 0)), whole(IDX_DIM)],
        out_specs=blk(ATT_WIDTH),
        out_shape=jax.ShapeDtypeStruct((b, t, ATT_WIDTH), BF16),
        scratch_shapes=[pltpu.VMEM((t, qb), F32), pltpu.VMEM((ATT_WIDTH, qb), F32),
                        pltpu.VMEM((N_HEADS, qb), F32), pltpu.VMEM((1, LANES), F32),
                        pltpu.VMEM((2, KEY_SUB, N_HEADS * qb), F32)],
        compiler_params=pltpu.CompilerParams(dimension_semantics=("arbitrary", "arbitrary"),
                                             vmem_limit_bytes=VMEM_LIMIT),
        name="dsa_prompt",
    )(q, qi, wi, kb, vt, kib)


def _dsa_sample_kernel(q_ref, qi_ref, wi_ref, kn_ref, vn_ref, kin_ref, ckt_ref, cvt_ref, ckit_ref, o_ref,
                       kcat, vcat_t, kicat, s_scr, *, topk):
    tq = q_ref.shape[1]
    past = ckt_ref.shape[1]
    total = kcat.shape[0]
    pad = total - past - tq
    kcat[0:past, :] = ckt_ref[...].T.astype(BF16)
    kcat[past:past + tq, :] = kn_ref[0]
    kcat[past + tq:total, :] = jnp.zeros((pad, KV_WIDTH), BF16)
    kit_wide = jnp.concatenate([ckit_ref[...], jnp.zeros((LANES - IDX_DIM, past), F32)], axis=0)
    kicat[0:past, :] = kit_wide.T[:, 0:IDX_DIM].astype(BF16)
    kicat[past:past + tq, :] = kin_ref[0]
    kicat[past + tq:total, :] = jnp.zeros((pad, IDX_DIM), BF16)
    vcat_t[:, 0:past] = cvt_ref[...].astype(BF16)
    v_new = jnp.concatenate([vn_ref[0].astype(F32), jnp.zeros((pad, KV_WIDTH), F32)], axis=0)
    vcat_t[:, past:total] = v_new.T.astype(BF16)

    lane = lax.broadcasted_iota(jnp.int32, (1, LANES), 1)
    qi = qi_ref[0]
    qi_stack = jnp.concatenate([qi[:, h * IDX_DIM:(h + 1) * IDX_DIM] for h in range(IDX_HEADS)], axis=0)
    wi_rows = jnp.concatenate([wi_ref[0], jnp.zeros((LANES - tq, LANES), F32)], axis=0).T
    w_flat = jnp.zeros((1, LANES), F32)
    for h in range(IDX_HEADS):
        row = wi_rows[h:h + 1, :]
        shifted = row if h == 0 else pltpu.roll(row, h * tq, 1)
        w_flat = jnp.where(lane // tq == h, shifted, w_flat)

    sc = w_flat * jnp.maximum(_dot_t(kicat[...], qi_stack), 0.0)
    for shift in (LANES // 2, LANES // 4, LANES // 8):
        sc = sc + pltpu.roll(sc, shift, 1)
    sc = jnp.where(lane < tq, sc, 0.0)
    for shift in (LANES // 8, LANES // 4, LANES // 2):
        sc = sc + pltpu.roll(sc, shift, 1)
    key = lax.broadcasted_iota(jnp.int32, (total, LANES), 0)
    ok = key < past + tq
    s_scr[...] = jnp.where(ok, sc, NEG_INF)
    smax = jnp.max(jnp.where(ok, sc, NEG_INF), axis=0, keepdims=True)
    smin = jnp.min(jnp.where(ok, sc, -NEG_INF), axis=0, keepdims=True)
    _select_mask(s_scr, 1, total, smin, smax, jnp.full((1, LANES), float(past + tq), F32), topk)

    q = q_ref[0]
    zero = jnp.zeros((tq, HEAD_DIM), BF16)
    q_stack = jnp.concatenate(
        [jnp.concatenate([q[:, h * HEAD_DIM:(h + 1) * HEAD_DIM], zero] if h < GQA_GROUP
                         else [zero, q[:, h * HEAD_DIM:(h + 1) * HEAD_DIM]], axis=1) for h in range(N_HEADS)], axis=0)
    lg = _dot_t(kcat[...], q_stack) + s_scr[...]
    m = jnp.max(lg, axis=0, keepdims=True)
    p = jnp.exp2(lg - m)
    l = jnp.sum(p, axis=0, keepdims=True)
    o = _dot(vcat_t[...], p.astype(BF16))
    o = jnp.where(lane < GQA_GROUP * tq, o[0:HEAD_DIM, :], o[HEAD_DIM:2 * HEAD_DIM, :]) / l
    o_t = jnp.concatenate([o, jnp.zeros((LANES - HEAD_DIM, LANES), F32)], axis=0).T
    o_ref[0] = jnp.concatenate([o_t[h * tq:(h + 1) * tq, 0:HEAD_DIM] for h in range(N_HEADS)], axis=1).astype(BF16)


def _dsa_sample(q, qi, wi, kb, vb, kib, cache_k, cache_v, cache_kidx, layer):
    b, tq, _ = q.shape
    past = cache_k.shape[3]
    assert tq * N_HEADS == LANES and tq * IDX_HEADS == LANES and past % LANES == 0
    total = past + LANES
    topk = min(TOPK_MAX, (past + tq) // 4)
    blk = lambda w: pl.BlockSpec((1, tq, w), lambda bi: (bi, 0, 0))
    cache = lambda w: pl.BlockSpec((None, None, w, past), lambda bi: (layer, bi, 0, 0))
    return pl.pallas_call(
        functools.partial(_dsa_sample_kernel, topk=topk),
        grid=(b,),
        in_specs=[blk(ATT_WIDTH), blk(IDX_WIDTH), blk(LANES), blk(KV_WIDTH), blk(KV_WIDTH), blk(IDX_DIM),
                  cache(KV_WIDTH), cache(KV_WIDTH), cache(IDX_DIM)],
        out_specs=blk(ATT_WIDTH),
        out_shape=jax.ShapeDtypeStruct((b, tq, ATT_WIDTH), BF16),
        scratch_shapes=[pltpu.VMEM((total, KV_WIDTH), BF16), pltpu.VMEM((KV_WIDTH, total), BF16),
                        pltpu.VMEM((total, IDX_DIM), BF16), pltpu.VMEM((total, LANES), F32)],
        compiler_params=pltpu.CompilerParams(dimension_semantics=("arbitrary",), vmem_limit_bytes=VMEM_LIMIT),
        name="dsa_sample",
    )(q, qi, wi, kb, vb, kib, cache_k, cache_v, cache_kidx)


def _first_lane_where(mask, lane):
    return jnp.min(jnp.where(mask, lane, LANES), axis=1, keepdims=True)


def _post_kernel(att_ref, sa_ref, mb_ref, x_ref, wpa_ref, wout_ref, g2_ref, wr_ref, br_ref,
                 h_ref, xn2_ref, gate_ref, *sort_refs):
    pa = _dot(att_ref[...], wpa_ref[...])
    merged = sa_ref[...].astype(F32) * pa + mb_ref[...].astype(F32)
    h = x_ref[...] + _dot(merged.astype(BF16), wout_ref[...])
    h_ref[...] = h
    xn2 = _rms(h, g2_ref[...])
    xh = xn2.astype(BF16)
    xn2_ref[...] = xh

    xl = (xn2 - xh.astype(F32)).astype(BF16)
    wr = wr_ref[...]
    wh = wr.astype(BF16)
    wl = (wr - wh.astype(F32)).astype(BF16)
    lg = _dot(xh, wh) + (_dot(xl, wh) + _dot(xh, wl)) + br_ref[...]

    lane = lax.broadcasted_iota(jnp.int32, lg.shape, 1)
    is_g = lane < N_GROUPS
    mg = jnp.max(jnp.where(is_g, lg, NEG_INF), axis=1, keepdims=True)
    eg = jnp.where(is_g, jnp.exp(lg - mg), 0.0)
    pg = eg / jnp.sum(eg, axis=1, keepdims=True)
    g_w = jnp.max(pg, axis=1, keepdims=True)
    g_idx = _first_lane_where(jnp.logical_and(is_g, pg == g_w), lane)

    e_lo = ROUTER_E0 + EXP_PER_GROUP * g_idx
    is_e = jnp.logical_and(lane >= e_lo, lane < e_lo + EXP_PER_GROUP)
    me = jnp.max(jnp.where(is_e, lg, NEG_INF), axis=1, keepdims=True)
    ee = jnp.where(is_e, jnp.exp(lg - me), 0.0)
    pe = ee / jnp.sum(ee, axis=1, keepdims=True)
    e1 = jnp.max(pe, axis=1, keepdims=True)
    i1 = _first_lane_where(jnp.logical_and(is_e, pe == e1), lane)
    rest = jnp.logical_and(is_e, lane != i1)
    e2 = jnp.max(jnp.where(rest, pe, -1.0), axis=1, keepdims=True)
    i2 = _first_lane_where(jnp.logical_and(rest, pe == e2), lane)
    den = e1 + e2
    gate = jnp.where(lane == i1, e1 / den, jnp.where(lane == i2, e2 / den, 0.0)) * g_w
    group_gates = [jnp.where(lane < EXP_PER_GROUP, pltpu.roll(gate, LANES - (ROUTER_E0 + EXP_PER_GROUP * g), 1), 0.0)
                   for g in range(N_GROUPS)]
    if not sort_refs:
        for g in range(N_GROUPS):
            gate_ref[:, g * LANES:(g + 1) * LANES] = group_gates[g]
        return

    pt_ref, meta_ref = sort_refs
    gate_ref[...] = (group_gates[0] + group_gates[1]) + (group_gates[2] + group_gates[3])
    tm = lg.shape[0]
    onehot = jnp.where(lane == g_idx, 1.0, 0.0)
    ri = lax.broadcasted_iota(jnp.int32, (tm, tm), 0)
    cj = lax.broadcasted_iota(jnp.int32, (tm, tm), 1)
    earlier = jnp.where(cj < ri, 1.0, 0.0).astype(BF16)
    rank = jnp.sum(_dot(earlier, onehot.astype(BF16)) * onehot, axis=1, keepdims=True)
    cnt = jnp.sum(onehot, axis=0, keepdims=True)
    cnt_pad = jnp.floor((cnt + (SORT_PAD - 1.0)) * (1.0 / SORT_PAD)) * SORT_PAD
    seg_off = pltpu.roll(cnt_pad, 1, 1) + pltpu.roll(cnt_pad, 2, 1) + pltpu.roll(cnt_pad, 3, 1)
    dest = (rank + jnp.sum(onehot * seg_off, axis=1, keepdims=True)).astype(jnp.int32)
    col = lax.broadcasted_iota(jnp.int32, pt_ref.shape, 1)
    pt_ref[...] = jnp.where(col == dest, 1.0, 0.0).astype(BF16)
    meta_ref[0] = jnp.concatenate([cnt_pad, seg_off, jnp.zeros((SUBLANES - 2, LANES), F32)], axis=0)


def _sort_rows(tm):
    return -(-(tm + N_GROUPS * SORT_PAD) // LANES) * LANES


def _post(att, sa, mb, x, w, *, tm, sort):
    n, d = x.shape
    row = lambda t: (t, 0)
    const = lambda t: (0, 0)
    gate_w = LANES if sort else N_GROUPS * LANES
    out_specs = [pl.BlockSpec((tm, d), row), pl.BlockSpec((tm, d), row), pl.BlockSpec((tm, gate_w), row)]
    out_shape = [jax.ShapeDtypeStruct((n, d), F32), jax.ShapeDtypeStruct((n, d), BF16),
                 jax.ShapeDtypeStruct((n, gate_w), F32)]
    if sort:
        out_specs += [pl.BlockSpec((tm, _sort_rows(tm)), row), pl.BlockSpec((1, SUBLANES, LANES), lambda t: (t, 0, 0))]
        out_shape += [jax.ShapeDtypeStruct((n, _sort_rows(tm)), BF16),
                      jax.ShapeDtypeStruct((n // tm, SUBLANES, LANES), F32)]
    return pl.pallas_call(
        _post_kernel,
        grid=(n // tm,),
        in_specs=[pl.BlockSpec((tm, ATT_WIDTH), row), pl.BlockSpec((tm, d), row), pl.BlockSpec((tm, d), row),
                  pl.BlockSpec((tm, d), row), pl.BlockSpec(w["wpa"].shape, const), pl.BlockSpec(w["wout"].shape, const),
                  pl.BlockSpec((1, d), const), pl.BlockSpec(w["wr"].shape, const), pl.BlockSpec((1, ROUTER_LANES), const)],
        out_specs=out_specs,
        out_shape=out_shape,
        compiler_params=pltpu.CompilerParams(dimension_semantics=("arbitrary",), vmem_limit_bytes=VMEM_LIMIT),
        name="post_attn",
    )(att, sa, mb, x, w["wpa"], w["wout"], w["g2"], w["wr"], w["br"])


def _moe_kernel(xn2_ref, gate_ref, h_ref, wg_ref, wu_ref, wd_ref, fn_ref, o_ref, acc_ref, *, final):
    g = pl.program_id(1)

    @pl.when(g == 0)
    def _():
        acc_ref[...] = jnp.zeros_like(acc_ref)

    x = xn2_ref[...]
    gate = gate_ref[...]
    act = jax.nn.silu(_dot(x, wg_ref[0])) * _dot(x, wu_ref[0])
    parts = []
    for e in range(EXP_PER_GROUP):
        parts.append((act[:, e * EXP_HIDDEN:(e + 1) * EXP_HIDDEN] * gate[:, e:e + 1]).astype(BF16))
    acc_ref[...] += _dot(jnp.concatenate(parts, axis=1), wd_ref[0])

    @pl.when(g == N_GROUPS - 1)
    def _():
        y = h_ref[...] + acc_ref[...]
        o_ref[...] = _rms(y, fn_ref[...]) if final else y


def _moe(xn2, gate, h, w, fn, *, tm, final):
    n, d = h.shape
    row = lambda t, g: (t, 0)
    wspec = lambda shp: pl.BlockSpec((1,) + shp[1:], lambda t, g: (g, 0, 0))
    return pl.pallas_call(
        functools.partial(_moe_kernel, final=final),
        grid=(n // tm, N_GROUPS),
        in_specs=[pl.BlockSpec((tm, d), row), pl.BlockSpec((tm, LANES), lambda t, g: (t, g)), pl.BlockSpec((tm, d), row),
                  wspec(w["weg"].shape), wspec(w["weu"].shape), wspec(w["wed"].shape),
                  pl.BlockSpec((1, d), lambda t, g: (0, 0))],
        out_specs=pl.BlockSpec((tm, d), row),
        out_shape=jax.ShapeDtypeStruct((n, d), F32),
        scratch_shapes=[pltpu.VMEM((tm, d), F32)],
        compiler_params=pltpu.CompilerParams(dimension_semantics=("arbitrary", "arbitrary"),
                                             vmem_limit_bytes=VMEM_LIMIT),
        name="moe",
    )(xn2, gate, h, w["weg"], w["weu"], w["wed"], fn)


def _moe_sorted_kernel(meta_ref, xn2_ref, gate_ref, h_ref, pt_ref, wg_ref, wu_ref, wd_ref, fn_ref, o_ref,
                       xs_scr, gs_scr, ys_scr, *, final):
    t = pl.program_id(0)
    rows = pt_ref.shape[1]
    pt = pt_ref[...]
    perm = pt.astype(F32).T.astype(BF16)
    xs_scr[0:rows, :] = _dot(perm, xn2_ref[...]).astype(BF16)
    xs_scr[rows:rows + MOE_CHUNK, :] = jnp.zeros((MOE_CHUNK, xs_scr.shape[1]), BF16)
    gate = gate_ref[...]
    g_hi = gate.astype(BF16)
    g_rest = gate - g_hi.astype(F32)
    g_mid = g_rest.astype(BF16)
    g_lo = (g_rest - g_mid.astype(F32)).astype(BF16)
    gs_scr[0:rows, :] = _dot(perm, g_hi) + (_dot(perm, g_mid) + _dot(perm, g_lo))
    gs_scr[rows:rows + MOE_CHUNK, :] = jnp.zeros((MOE_CHUNK, LANES), F32)
    ys_scr[...] = jnp.zeros_like(ys_scr)

    for g in range(N_GROUPS):
        cnt = meta_ref[t * 2 * N_GROUPS + g]
        seg = meta_ref[t * 2 * N_GROUPS + N_GROUPS + g]

        def chunk(c, carry, g=g, cnt=cnt, seg=seg):
            r0 = pl.multiple_of(seg + c * MOE_CHUNK, SORT_PAD)
            x = xs_scr[pl.ds(r0, MOE_CHUNK), :]
            gt = gs_scr[pl.ds(r0, MOE_CHUNK), :]
            act = jax.nn.silu(_dot(x, wg_ref[g])) * _dot(x, wu_ref[g])
            parts = [(act[:, e * EXP_HIDDEN:(e + 1) * EXP_HIDDEN] * gt[:, e:e + 1]).astype(BF16)
                     for e in range(EXP_PER_GROUP)]
            y = _dot(jnp.concatenate(parts, axis=1), wd_ref[g])
            in_seg = lax.broadcasted_iota(jnp.int32, (MOE_CHUNK, 1), 0) + c * MOE_CHUNK < cnt
            ys_scr[pl.ds(r0, MOE_CHUNK), :] = jnp.where(in_seg, y, ys_scr[pl.ds(r0, MOE_CHUNK), :])
            return carry

        lax.fori_loop(0, (cnt + MOE_CHUNK - 1) // MOE_CHUNK, chunk, 0)

    y = h_ref[...] + _dot(pt, ys_scr[0:rows, :].astype(BF16))
    o_ref[...] = _rms(y, fn_ref[...]) if final else y


def _moe_sorted(xn2, gate, h, pt, meta, w, fn, *, tm, final):
    n, d = h.shape
    rows = pt.shape[1]
    row = lambda t, m: (t, 0)
    resident = lambda shp: pl.BlockSpec(shp, lambda t, m: (0, 0, 0), pipeline_mode=pl.Buffered(1))
    grid_spec = pltpu.PrefetchScalarGridSpec(
        num_scalar_prefetch=1,
        grid=(n // tm,),
        in_specs=[pl.BlockSpec((tm, d), row), pl.BlockSpec((tm, LANES), row), pl.BlockSpec((tm, d), row),
                  pl.BlockSpec((tm, rows), row),
                  resident(w["weg"].shape), resident(w["weu"].shape), resident(w["wed"].shape),
                  pl.BlockSpec((1, d), lambda t, m: (0, 0))],
        out_specs=pl.BlockSpec((tm, d), row),
        scratch_shapes=[pltpu.VMEM((rows + MOE_CHUNK, d), BF16), pltpu.VMEM((rows + MOE_CHUNK, LANES), F32),
                        pltpu.VMEM((rows + MOE_CHUNK, d), F32)],
    )
    return pl.pallas_call(
        functools.partial(_moe_sorted_kernel, final=final),
        grid_spec=grid_spec,
        out_shape=jax.ShapeDtypeStruct((n, d), F32),
        compiler_params=pltpu.CompilerParams(dimension_semantics=("arbitrary",), vmem_limit_bytes=VMEM_LIMIT),
        name="moe_sorted",
    )(meta, xn2, gate, h, pt, w["weg"], w["weu"], w["wed"], fn)


def _pack_experts_kernel(w_ref, o_ref):
    for e in range(EXP_PER_GROUP):
        o_ref[:, e * EXP_HIDDEN:(e + 1) * EXP_HIDDEN] = w_ref[e].astype(BF16)


def _pack_experts(w):
    depth, n_groups, n_exp, d, f = w.shape
    rows = 2 * MXU_DIM
    return pl.pallas_call(
        _pack_experts_kernel,
        grid=(depth, n_groups, d // rows),
        in_specs=[pl.BlockSpec((None, None, n_exp, rows, f), lambda l, g, r: (l, g, 0, r, 0))],
        out_specs=pl.BlockSpec((None, None, rows, n_exp * f), lambda l, g, r: (l, g, r, 0)),
        out_shape=jax.ShapeDtypeStruct((depth, n_groups, d, n_exp * f), BF16),
        compiler_params=pltpu.CompilerParams(dimension_semantics=("arbitrary",) * 3, vmem_limit_bytes=VMEM_LIMIT),
        name="pack_experts",
    )(w)


def _rope_tables(pos, d):
    inv = ROPE_THETA ** (-jnp.arange(0, d, 2, dtype=F32) / d)
    ang = pos[:, None] * inv[None, :]
    cos, sin = jnp.cos(ang), jnp.sin(ang)
    reps = LANES // d
    return (jnp.tile(jnp.concatenate([cos, cos], axis=-1), (1, reps)),
            jnp.tile(jnp.concatenate([-sin, sin], axis=-1), (1, reps)))


def _layer_weights(l, norm1, w_in, sgu_ln_g, sgu_ln_b, sgu_w, sgu_b, w_pa, w_pb, w_out, norm2,
                   w_rg, b_rg, w_re, b_re, w_eg, w_eu, w_ed, sgu_chunk):
    d = w_in.shape[1]
    a_end = ATT_WIDTH + 2 * KV_WIDTH + IDX_WIDTH
    b_end = a_end + IDX_DIM + IDX_HEADS
    reps = SGU_CHUNK // sgu_chunk
    n_router = N_GROUPS + N_GROUPS * EXP_PER_GROUP
    return dict(
        g1=norm1[l][None, :],
        wa=w_in[l][:, :a_end].astype(BF16),
        wb=jnp.pad(w_in[l][:, a_end:b_end], ((0, 0), (0, LANES - (b_end - a_end)))).astype(BF16),
        wc=w_in[l][:, b_end:].astype(BF16),
        sguw=jnp.tile(sgu_w[l][:, :sgu_chunk, :sgu_chunk], (1, reps, reps)),
        sgub=jnp.repeat(jnp.tile(sgu_b[l][:, :sgu_chunk], (1, reps)).T, SGU_GDIM, axis=1),
        lng=sgu_ln_g[l][None, :], lnb=sgu_ln_b[l][None, :],
        wpb=w_pb[l].astype(BF16), wpa=w_pa[l].astype(BF16), wout=w_out[l].astype(BF16),
        g2=norm2[l][None, :],
        wr=jnp.pad(jnp.concatenate([w_rg[l], w_re[l]], axis=1), ((0, 0), (0, ROUTER_LANES - n_router))),
        br=jnp.pad(jnp.concatenate([b_rg[l], b_re[l]]), (0, ROUTER_LANES - n_router))[None, :],
        weg=w_eg[l], weu=w_eu[l],
        wed=w_ed[l].reshape(N_GROUPS, GROUP_HIDDEN, d).astype(BF16),
    )


def _token_tile(n):
    return 512 if n % 512 == 0 else n


def kernel(x_prompt, x_sample, cache_k, cache_v, cache_kidx, norm1, w_in, sgu_ln_g, sgu_ln_b, sgu_w, sgu_b,
           w_pa, w_pb, w_out, norm2, w_rg, b_rg, w_re, b_re, w_eg, w_eu, w_ed, final_norm):
    bp, tp, d = x_prompt.shape
    bs, ts, _ = x_sample.shape
    depth, _, past = cache_k.shape[:3]
    np_, ns = bp * tp, bs * ts
    tm_p, tm_s = _token_tile(tp), _token_tile(ns)
    assert tp % tm_p == 0 and tp % Q_BLOCK == 0 and SGU_CHUNK % ts == 0 and tm_s % SGU_CHUNK == 0
    assert tm_p % KEY_SUB == 0 and tm_s % KEY_SUB == 0

    pos_p = jnp.arange(tp, dtype=F32)
    pos_s = jnp.tile(past + jnp.arange(ts, dtype=F32), tm_s // ts)
    tabs_p = _rope_tables(pos_p, HEAD_DIM) + _rope_tables(pos_p, IDX_DIM)
    tabs_s = _rope_tables(pos_s, HEAD_DIM) + _rope_tables(pos_s, IDX_DIM)
    ckt = jnp.transpose(cache_k, (0, 1, 3, 4, 2)).reshape(depth, bs, KV_WIDTH, past)
    cvt = jnp.transpose(cache_v, (0, 1, 3, 4, 2)).reshape(depth, bs, KV_WIDTH, past)
    ckit = jnp.transpose(cache_kidx, (0, 1, 3, 2))
    fn = final_norm[None, :]
    weg_all, weu_all = _pack_experts(w_eg), _pack_experts(w_eu)

    hp = x_prompt.reshape(np_, d)
    hs = x_sample.reshape(ns, d)
    outs = [[] for _ in range(4)]
    kv_bufs = None
    for l in range(depth):
        wargs = (norm1, w_in, sgu_ln_g, sgu_ln_b, sgu_w, sgu_b, w_pa, w_pb, w_out, norm2,
                 w_rg, b_rg, w_re, b_re, weg_all, weu_all, w_ed)
        wp = _layer_weights(l, *wargs, sgu_chunk=SGU_CHUNK)
        ws = dict(wp)
        ws.update({k: v for k, v in _layer_weights(l, *wargs, sgu_chunk=ts).items() if k in ("sguw", "sgub")})
        final = l == depth - 1

        q, kt_all, kb, vt_all, _, vt, qi, kit_all, kib, wi, mb, sa = _inproj(
            hp, wp, tabs_p, tm=tm_p, sgu_chunk=SGU_CHUNK, want_vln=False,
            kv_layer=l, kv_bufs=kv_bufs, kv_shape=(depth, bp, tp))
        kv_bufs = (kt_all, vt_all, kit_all)
        r3 = lambda a: a.reshape(bp, tp, a.shape[-1])
        att = _dsa_prompt(q, qi, wi, r3(kb), vt, r3(kib)).reshape(np_, ATT_WIDTH)
        h, xn2, gate, pt, meta = _post(att, sa, mb, hp, wp, tm=tm_p, sort=True)
        meta = meta[:, 0:2, 0:N_GROUPS].astype(jnp.int32).reshape(-1)
        hp = _moe_sorted(xn2, gate, h, pt, meta, wp, fn, tm=tm_p, final=final)

        q, kf, kb, vf, vb, _, qi, kif, kib, wi, mb, sa, vln = _inproj(hs, ws, tabs_s, tm=tm_s, sgu_chunk=ts,
                                                                      want_vln=True)
        r3 = lambda a: a.reshape(bs, ts, a.shape[-1])
        att = _dsa_sample(r3(q), r3(qi), r3(wi), r3(kb), r3(vb), r3(kib), ckt, cvt, ckit, l)
        h, xn2, gate = _post(att.reshape(ns, ATT_WIDTH), sa, mb, hs, ws, tm=tm_s, sort=False)
        hs = _moe(xn2, gate, h, ws, fn, tm=tm_s, final=final)
        outs[0].append(kf.reshape(bs, ts, N_KV_HEADS, HEAD_DIM))
        outs[1].append(vf.reshape(bs, ts, N_KV_HEADS, HEAD_DIM))
        outs[2].append(kif.reshape(bs, ts, IDX_DIM))
        outs[3].append(vln.reshape(bs, ts, SGU_WIDTH))

    kt_all, vt_all, kit_all = kv_bufs
    heads = lambda a: jnp.transpose(a.reshape(depth, bp, N_KV_HEADS, HEAD_DIM, tp), (0, 1, 4, 2, 3))
    return ((hp.reshape(bp, tp, d), hs.reshape(bs, ts, d), heads(kt_all), heads(vt_all),
             jnp.transpose(kit_all, (0, 1, 3, 2))) + tuple(jnp.stack(o) for o in outs))
```

```python
import functools

import jax
import jax.numpy as jnp
from jax import lax
from jax.experimental import pallas as pl
from jax.experimental.pallas import tpu as pltpu

F32 = jnp.float32
BF16 = jnp.bfloat16

CHUNK = 64
N_HEADS = 8
N_KV_HEADS = 2
HEAD_DIM = 64
GQA_GROUP = N_HEADS // N_KV_HEADS
ATT_WIDTH = N_HEADS * HEAD_DIM
KV_WIDTH = N_KV_HEADS * HEAD_DIM
IDX_HEADS = 8
IDX_DIM = 32
IDX_WIDTH = IDX_HEADS * IDX_DIM
TOPK_MAX = 256
Q_BLOCK = 128
DSA_QUERIES = 256
SGU_GROUPS = 4
SGU_CHUNK = 128
SGU_WIDTH = 512
SGU_GDIM = SGU_WIDTH // SGU_GROUPS
N_GROUPS = 4
EXP_PER_GROUP = 8
EXP_HIDDEN = 128
GROUP_HIDDEN = EXP_PER_GROUP * EXP_HIDDEN
ROPE_THETA = 10000.0
EPS = 1e-6
Q_SCALE = HEAD_DIM ** -0.5 * 1.4426950408889634

LANES = 128
SUBLANES = 8
MXU_DIM = 256
ROUTER_LANES = LANES
ROUTER_E0 = N_GROUPS
VMEM_LIMIT = 56 * 1024 * 1024
KEY_TILE = 2 * MXU_DIM
KEY_SUB = MXU_DIM
FOLD_WAYS = 4
COUNT_ROWS = 64
SEARCH_FIRST_STEPS = 17
SEARCH_SECOND_STEPS = 0
SEARCH_ROUND_STEPS = 1
SEARCH_MAX_ROUNDS = 34
SORT_PAD = 16
MOE_CHUNK = 144
KNORM_SLACK = 1.0 + 2.0 ** -6
SOFTMAX_SUM_FLOOR = 2.0 ** -80
MAX_SUBNORMAL_BITS = 0x007FFFFF
NEG_INF = float("-inf")


def _dot(a, b):
    return jnp.dot(a, b, preferred_element_type=F32)


def _dot_t(a, b):
    return lax.dot_general(a, b, (((1,), (1,)), ((), ())), preferred_element_type=F32)


def _rms(x, g):
    return x * lax.rsqrt(jnp.mean(x * x, axis=-1, keepdims=True) + EPS) * g


def _swap_halves(x, d):
    lane = lax.broadcasted_iota(jnp.int32, x.shape, 1)
    first = (lane % d) < (d // 2)
    return jnp.where(first, pltpu.roll(x, LANES - d // 2, 1), pltpu.roll(x, d // 2, 1))


def _rope(x, cos, sin_signed, d):
    return x * cos + _swap_halves(x, d) * sin_signed


def _inproj_kernel(*refs, sgu_chunk, kv_transposed, n_alias, want_vln):
    (x_ref, g1_ref, wa_ref, wb_ref, wc_ref, cq_ref, sq_ref, ci_ref, si_ref,
     sguw_ref, sgub_ref, lng_ref, lnb_ref, wpb_ref) = refs[:14]
    (q_ref, kf_ref, kb_ref, vf_ref, vb_ref, vt_ref, qi_ref, kif_ref, kib_ref, wi_ref,
     mb_ref, sa_ref) = refs[14 + n_alias:26 + n_alias]
    tm = x_ref.shape[0]
    xn = _rms(x_ref[...], g1_ref[...]).astype(BF16)
    cq, sq, ci, si = cq_ref[...], sq_ref[...], ci_ref[...], si_ref[...]

    a = _dot(xn, wa_ref[...])
    def emit_queries(ref, val):
        if kv_transposed:
            for c in range(tm // DSA_QUERIES):
                ref[c] = val[c * DSA_QUERIES:(c + 1) * DSA_QUERIES, :].T[0:ref.shape[1], :].astype(ref.dtype)
        else:
            ref[...] = val.astype(ref.dtype)

    emit_queries(q_ref, jnp.concatenate(
        [_rope(a[:, c * LANES:(c + 1) * LANES], cq, sq, HEAD_DIM) * Q_SCALE for c in range(ATT_WIDTH // LANES)], axis=1))
    k = _rope(a[:, ATT_WIDTH:ATT_WIDTH + KV_WIDTH], cq, sq, HEAD_DIM)
    kb_ref[...] = k.astype(BF16)
    v = a[:, ATT_WIDTH + KV_WIDTH:ATT_WIDTH + 2 * KV_WIDTH]
    vb_ref[...] = v.astype(BF16)
    if kv_transposed:
        kf_ref[...] = k.T
        vf_ref[...] = v.T
    else:
        kf_ref[...] = k
        vf_ref[...] = v
    for c in range(tm // KEY_SUB):
        vt_ref[c] = v[c * KEY_SUB:(c + 1) * KEY_SUB, :].T.astype(BF16)
    qi0 = ATT_WIDTH + 2 * KV_WIDTH
    emit_queries(qi_ref, jnp.concatenate(
        [_rope(a[:, qi0 + c * LANES:qi0 + (c + 1) * LANES], ci, si, IDX_DIM) for c in range(IDX_WIDTH // LANES)], axis=1))

    b = _dot(xn, wb_ref[...])
    ki_wide = _rope(b, ci, si, IDX_DIM)
    ki = ki_wide[:, :IDX_DIM]
    kif_ref[...] = ki_wide.T[0:IDX_DIM, :] if kv_transposed else ki
    kib_ref[...] = ki.astype(BF16)
    emit_queries(wi_ref, pltpu.roll(b, LANES - IDX_DIM, 1) * (IDX_HEADS ** -0.5))

    u = jax.nn.gelu(_dot(xn, wc_ref[:, 0:SGU_WIDTH]))
    vs = jax.nn.gelu(_dot(xn, wc_ref[:, SGU_WIDTH:2 * SGU_WIDTH]))
    mu = jnp.mean(vs, axis=-1, keepdims=True)
    var = jnp.mean(jnp.square(vs - mu), axis=-1, keepdims=True)
    vln = (vs - mu) * lax.rsqrt(var + EPS) * lng_ref[...] + lnb_ref[...]
    if want_vln:
        refs[26 + n_alias][...] = vln
    ri = lax.broadcasted_iota(jnp.int32, (SGU_CHUNK, SGU_CHUNK), 0)
    cj = lax.broadcasted_iota(jnp.int32, (SGU_CHUNK, SGU_CHUNK), 1)
    mix_mask = (cj <= ri) & ((ri // sgu_chunk) == (cj // sgu_chunk))
    vln_b = vln.astype(BF16)
    w_eff = [jnp.where(mix_mask, sguw_ref[g], 0.0).astype(BF16) for g in range(SGU_GROUPS)]
    sg_rows = []
    for r in range(tm // SGU_CHUNK):
        rows = slice(r * SGU_CHUNK, (r + 1) * SGU_CHUNK)
        mixed = [_dot(w_eff[g], vln_b[rows, g * SGU_GDIM:(g + 1) * SGU_GDIM]) for g in range(SGU_GROUPS)]
        sg_rows.append(u[rows, :] * (jnp.concatenate(mixed, axis=1) + sgub_ref[...]))
    sg = jnp.concatenate(sg_rows, axis=0).astype(BF16)
    d = x_ref.shape[1]
    sa_ref[...] = jax.nn.sigmoid(_dot(xn, wc_ref[:, 2 * SGU_WIDTH:2 * SGU_WIDTH + d])).astype(BF16)
    gb = jax.nn.sigmoid(_dot(xn, wc_ref[:, 2 * SGU_WIDTH + d:]))
    mb_ref[...] = (gb * _dot(sg, wpb_ref[...])).astype(BF16)


def _inproj(x, w, tabs, *, tm, sgu_chunk, want_vln, kv_layer=None, kv_bufs=None, kv_shape=None):
    n, d = x.shape
    n_tab_tiles = tabs[0].shape[0] // tm
    row = lambda t: (t, 0)
    const = lambda t: (0, 0)
    tab = lambda t: (t % n_tab_tiles, 0)
    kv_transposed = kv_layer is not None
    if kv_transposed:
        depth, streams, steps = kv_shape
        per_stream = steps // tm
        kv_sds = lambda wd: jax.ShapeDtypeStruct((depth, streams, wd, steps), F32)
        kv_spec = lambda wd: pl.BlockSpec((None, None, wd, tm),
                                          lambda t: (kv_layer, t // per_stream, 0, t % per_stream))
    else:
        kv_sds = lambda wd: jax.ShapeDtypeStruct((n, wd), F32)
        kv_spec = lambda wd: pl.BlockSpec((tm, wd), row)
    tiled = lambda wd, dt: (jax.ShapeDtypeStruct((n, wd), dt), pl.BlockSpec((tm, wd), row))
    if kv_transposed:
        qb = DSA_QUERIES
        query = lambda wd, rows, dt: (jax.ShapeDtypeStruct((n // qb, rows, qb), dt),
                                      pl.BlockSpec((tm // qb, rows, qb), lambda t: (t, 0, 0)))
    else:
        query = lambda wd, rows, dt: tiled(wd, dt)
    outs = [
        query(ATT_WIDTH, ATT_WIDTH, BF16),
        (kv_sds(KV_WIDTH), kv_spec(KV_WIDTH)),
        tiled(KV_WIDTH, BF16),
        (kv_sds(KV_WIDTH), kv_spec(KV_WIDTH)),
        tiled(KV_WIDTH, BF16),
        (jax.ShapeDtypeStruct((n // KEY_SUB, KV_WIDTH, KEY_SUB), BF16),
         pl.BlockSpec((tm // KEY_SUB, KV_WIDTH, KEY_SUB), lambda t: (t, 0, 0))),
        query(IDX_WIDTH, IDX_WIDTH, BF16),
        (kv_sds(IDX_DIM), kv_spec(IDX_DIM)),
        tiled(IDX_DIM, BF16),
        query(LANES, IDX_HEADS, F32),
        tiled(d, BF16),
        tiled(d, BF16),
    ]
    if want_vln:
        outs.append(tiled(SGU_WIDTH, F32))
    in_specs = [
        pl.BlockSpec((tm, d), row),
        pl.BlockSpec((1, d), const),
        pl.BlockSpec(w["wa"].shape, const),
        pl.BlockSpec(w["wb"].shape, const),
        pl.BlockSpec(w["wc"].shape, const),
        pl.BlockSpec((tm, LANES), tab), pl.BlockSpec((tm, LANES), tab),
        pl.BlockSpec((tm, LANES), tab), pl.BlockSpec((tm, LANES), tab),
        pl.BlockSpec(w["sguw"].shape, lambda t: (0, 0, 0)),
        pl.BlockSpec(w["sgub"].shape, const),
        pl.BlockSpec((1, SGU_WIDTH), const),
        pl.BlockSpec((1, SGU_WIDTH), const),
        pl.BlockSpec(w["wpb"].shape, const),
    ]
    args = [x, w["g1"], w["wa"], w["wb"], w["wc"], *tabs, w["sguw"], w["sgub"], w["lng"], w["lnb"], w["wpb"]]
    aliases = {}
    if kv_bufs is not None:
        for buf, out_idx in zip(kv_bufs, (1, 3, 7)):
            aliases[len(args)] = out_idx
            in_specs.append(pl.BlockSpec(memory_space=pl.ANY))
            args.append(buf)
    return pl.pallas_call(
        functools.partial(_inproj_kernel, sgu_chunk=sgu_chunk, kv_transposed=kv_transposed,
                          n_alias=len(aliases), want_vln=want_vln),
        grid=(n // tm,),
        in_specs=in_specs,
        out_specs=[o[1] for o in outs],
        out_shape=[o[0] for o in outs],
        input_output_aliases=aliases,
        compiler_params=pltpu.CompilerParams(dimension_semantics=("arbitrary",), vmem_limit_bytes=VMEM_LIMIT),
        name="inproj",
    )(*args)


def _float_key(x):
    b = lax.bitcast_convert_type(x, jnp.int32)
    k = b ^ ((b >> 31) & jnp.int32(0x7FFFFFFF))
    return jnp.where(k > MAX_SUBNORMAL_BITS, k - MAX_SUBNORMAL_BITS,
                     jnp.where(k < -MAX_SUBNORMAL_BITS - 1, k + MAX_SUBNORMAL_BITS, jnp.clip(k, -1, 0)))


def _key_float(k):
    k = jnp.where(k > 0, k + MAX_SUBNORMAL_BITS, jnp.where(k < -1, k - MAX_SUBNORMAL_BITS, k))
    return lax.bitcast_convert_type(k ^ ((k >> 31) & jnp.int32(0x7FFFFFFF)), F32)


def _fold_rows(x):
    n = x.shape[0] // SUBLANES
    ways = FOLD_WAYS if n % FOLD_WAYS == 0 else 1
    part = jnp.sum(x.reshape(n // ways, ways, SUBLANES, x.shape[1]), axis=0)
    return jnp.sum(part, axis=0)


def _topk_threshold(count_ge, snap, smin, smax, n_adm, kk):
    lo0 = _float_key(smin + 0.0)
    hx0 = _float_key(smax + 0.0) + 1

    def is_active(st):
        lo, hx, c_lo, _ = st
        return jnp.logical_and(c_lo != kk, lo + 1 < hx)

    def step(st, by_key):
        lo, hx, c_lo, c_hx = st
        active = is_active(st)
        if by_key:
            mid = (lo >> 1) + (hx >> 1) + (lo & hx & 1)
        else:
            mid = _float_key(0.5 * _key_float(lo) + 0.5 * _key_float(hx))
        mid = jnp.where(active, jnp.minimum(jnp.maximum(mid, lo + 1), hx - 1), lo)
        c = count_ge(_key_float(mid))
        up = jnp.logical_and(active, c >= kk)
        down = jnp.logical_and(active, c < kk)
        return (jnp.where(up, mid, lo), jnp.where(down, mid, hx),
                jnp.where(up, c, c_lo), jnp.where(down, c, c_hx))

    def snap_step(st):
        lo, hx, c_lo, c_hx = st
        active = is_active(st)
        first_in, last_in = snap(_key_float(lo), _key_float(hx))
        lo_n = jnp.where(active, _float_key(first_in + 0.0), lo)
        hx_n = jnp.where(active, _float_key(last_in + 0.0) + 1, hx)
        few = jnp.logical_and(active, c_lo - c_hx <= 2.0)
        same = first_in == last_in
        take_last = kk - c_hx <= 1.0
        thr_key = _float_key(jnp.where(take_last, last_in, first_in) + 0.0)
        c_ge = jnp.where(jnp.logical_and(take_last, jnp.logical_not(same)), c_hx + 1.0, c_lo)
        c_gt = jnp.where(jnp.logical_or(take_last, same), c_hx, c_hx + 1.0)
        return (jnp.where(few, thr_key, lo_n), jnp.where(few, thr_key + 1, hx_n),
                jnp.where(few, c_ge, c_lo), jnp.where(few, c_gt, c_hx))

    def n_active(st):
        return jnp.max(jnp.where(is_active(st), 1.0, 0.0))

    st = (lo0, hx0, n_adm, jnp.zeros_like(n_adm))
    for _ in range(SEARCH_FIRST_STEPS):
        st = step(st, by_key=False)
    st = snap_step(st)
    for _ in range(SEARCH_SECOND_STEPS):
        st = step(st, by_key=False)
    act = n_active(st)

    def cond(carry):
        it, act, _ = carry
        return jnp.logical_and(it < SEARCH_MAX_ROUNDS, act > 0.5)

    def body(carry):
        it, _, st = carry
        st = step(snap_step(st), by_key=True)
        for _ in range(SEARCH_ROUND_STEPS):
            st = step(st, by_key=False)
        return it + 1, n_active(st), st

    _, _, (lo, _, c_lo, c_hx) = lax.while_loop(cond, body, (jnp.int32(0), act, st))
    return _key_float(lo), c_lo, c_hx


def _demote_excess_ties(s_ref, n_chunks, thr, need):
    ri = lax.broadcasted_iota(jnp.int32, (LANES, LANES), 0)
    cj = lax.broadcasted_iota(jnp.int32, (LANES, LANES), 1)
    lower = jnp.where(cj <= ri, 1.0, 0.0).astype(BF16)

    def body(j, run):
        off = pl.multiple_of(j * LANES, LANES)
        blk = s_ref[pl.ds(off, LANES), :]
        tied = blk == thr
        tied_f = jnp.where(tied, 1.0, 0.0)
        prefix = _dot(lower, tied_f.astype(BF16)) + run
        s_ref[pl.ds(off, LANES), :] = jnp.where(jnp.logical_and(tied, prefix > need), NEG_INF, blk)
        return run + jnp.sum(tied_f, axis=0, keepdims=True)

    lax.fori_loop(0, n_chunks, body, jnp.zeros((1, s_ref.shape[1]), F32))


def _select_mask(s_ref, n_full, n_half, tk, smin, smax, n_adm, topk, selected=0.0):
    kk = jnp.minimum(n_adm, float(topk))
    half = tk // 2

    def over_rows(visit, carry):
        carry = lax.fori_loop(0, n_full, lambda j, c: visit(pl.multiple_of(j * tk, tk), tk, c), carry)
        return lax.fori_loop(0, n_half, lambda j, c: visit(pl.multiple_of(n_full * tk, half), half, c), carry)

    def count_ge(t):
        def visit(off, rows, acc):
            for r in range(0, rows, COUNT_ROWS):
                acc = acc + _fold_rows(jnp.where(s_ref[pl.ds(off + r, COUNT_ROWS), :] >= t, 1.0, 0.0))
            return acc
        acc = over_rows(visit, jnp.zeros((SUBLANES, s_ref.shape[1]), F32))
        return jnp.sum(acc, axis=0, keepdims=True)

    def snap(lo_f, hx_f):
        def visit(off, rows, carry):
            first_in, last_in = carry
            for r in range(0, rows, COUNT_ROWS):
                blk = s_ref[pl.ds(off + r, COUNT_ROWS), :]
                first_in = jnp.minimum(first_in, -_fold_max(jnp.where(blk >= lo_f, -blk, NEG_INF)))
                last_in = jnp.maximum(last_in, _fold_max(jnp.where(blk < hx_f, blk, NEG_INF)))
            return first_in, last_in
        first_in, last_in = over_rows(visit, (jnp.full((SUBLANES, s_ref.shape[1]), -NEG_INF, F32),
                                              jnp.full((SUBLANES, s_ref.shape[1]), NEG_INF, F32)))
        return jnp.min(first_in, axis=0, keepdims=True), jnp.max(last_in, axis=0, keepdims=True)

    thr, c_ge, c_gt = _topk_threshold(count_ge, snap, smin, smax, n_adm, kk)

    @pl.when(jnp.max(c_ge - kk) > 0.5)
    def _():
        _demote_excess_ties(s_ref, (n_full * tk + n_half * half) // LANES, thr, kk - c_gt)

    def mask_rows(off, rows, carry):
        s_ref[pl.ds(off, rows), :] = jnp.where(s_ref[pl.ds(off, rows), :] >= thr, selected, NEG_INF)
        return carry

    over_rows(mask_rows, 0)


def _dsa_prompt_kernel(q_ref, qi_ref, wi_ref, k_ref, vt_ref, ki_ref, o_ref, s_scr, acc_scr, l_scr, kn_scr, lg_scr,
                       *, topk):
    i = pl.program_id(1)
    tk, ts = KEY_TILE, KEY_SUB
    qb = q_ref.shape[2]
    n_keys = (i + 1) * qb
    qcol = lax.broadcasted_iota(jnp.int32, (1, qb), 1)
    lim = ((i * qb + qcol) // CHUNK + 1) * CHUNK

    @pl.when(i == 0)
    def _():
        ri = lax.broadcasted_iota(jnp.int32, (KV_WIDTH, LANES), 0)
        cj = lax.broadcasted_iota(jnp.int32, (KV_WIDTH, LANES), 1)
        head_sum = jnp.where(ri // HEAD_DIM == cj, 1.0, 0.0).astype(BF16)

        def body(j, acc):
            kt = k_ref[0, pl.ds(pl.multiple_of(j * tk, tk), tk), :].astype(F32)
            return jnp.maximum(acc, _fold_max(_dot((kt * kt).astype(BF16), head_sum)))

        acc = lax.fori_loop(0, k_ref.shape[1] // tk, body, jnp.zeros((SUBLANES, LANES), F32))
        kn_scr[...] = jnp.max(acc, axis=0, keepdims=True) * KNORM_SLACK

    qt = q_ref[0]
    qit = qi_ref[0]
    wit = wi_ref[0]
    qi_rhs = jnp.concatenate([qit[h * IDX_DIM:(h + 1) * IDX_DIM, :] for h in range(IDX_HEADS)], axis=1)
    zero = jnp.zeros((HEAD_DIM, qb), BF16)
    q_cols = []
    for h in range(N_HEADS):
        qh = qt[h * HEAD_DIM:(h + 1) * HEAD_DIM, :]
        q_cols.append(jnp.concatenate([qh, zero] if h < GQA_GROUP else [zero, qh], axis=0))
    q_rhs = jnp.concatenate(q_cols, axis=1)

    def score_sub(sub, carry):
        smax, smin = carry
        off = pl.multiple_of(sub * ts, ts)
        r = _dot(ki_ref[0, pl.ds(off, ts), :], qi_rhs)
        acc = wit[0:1, :] * jnp.maximum(r[:, 0:qb], 0.0)
        for h in range(1, IDX_HEADS):
            acc = acc + wit[h:h + 1, :] * jnp.maximum(r[:, h * qb:(h + 1) * qb], 0.0)
        key = off + lax.broadcasted_iota(jnp.int32, (ts, qb), 0)
        ok = key < lim
        s_scr[pl.ds(off, ts), :] = jnp.where(ok, acc, NEG_INF)
        smax = jnp.maximum(smax, _fold_max(jnp.where(ok, acc, NEG_INF)))
        smin = jnp.minimum(smin, -_fold_max(jnp.where(ok, -acc, NEG_INF)))
        return smax, smin

    n_sub = n_keys // ts
    n_full, n_half = n_sub // 2, n_sub % 2
    carry = (jnp.full((SUBLANES, qb), NEG_INF, F32), jnp.full((SUBLANES, qb), -NEG_INF, F32))
    carry = lax.fori_loop(0, n_full, lambda j, c: score_sub(2 * j + 1, score_sub(2 * j, c)), carry)
    smax, smin = lax.fori_loop(0, n_half, lambda j, c: score_sub(2 * n_full, c), carry)
    smax = jnp.max(smax, axis=0, keepdims=True)
    smin = jnp.min(smin, axis=0, keepdims=True)
    qsq = jnp.square(qt.astype(F32))
    kn = kn_scr[...]
    lane1 = lax.broadcasted_iota(jnp.int32, (1, LANES), 1)
    shift = jnp.zeros((1, qb), F32)
    for h in range(N_HEADS):
        qn2 = jnp.sum(_fold_rows(qsq[h * HEAD_DIM:(h + 1) * HEAD_DIM, :]), axis=0, keepdims=True)
        kmax2 = jnp.max(jnp.where(lane1 == h // GQA_GROUP, kn, 0.0))
        shift = jnp.maximum(shift, jnp.sqrt(qn2 * kmax2))
    _select_mask(s_scr, n_full, n_half, tk, smin, smax, lim.astype(F32), topk, selected=-shift)

    def pv_accumulate(vt, ps, alphas):
        for pair in range(N_HEADS // 2):
            g = (2 * pair) // GQA_GROUP
            o = _dot(vt[g * HEAD_DIM:(g + 1) * HEAD_DIM, :], jnp.concatenate(ps[2 * pair:2 * pair + 2], axis=1))
            for u in range(2):
                h = 2 * pair + u
                rows = slice(h * HEAD_DIM, (h + 1) * HEAD_DIM)
                prev = acc_scr[rows, :] if alphas is None else alphas[h] * acc_scr[rows, :]
                acc_scr[rows, :] = prev + o[:, u * qb:(u + 1) * qb]

    def logits_into(buf, sub):
        off = pl.multiple_of(jnp.minimum(sub * ts, k_ref.shape[1] - ts), ts)
        lg_scr[buf] = _dot(k_ref[0, pl.ds(off, ts), :], q_rhs)

    ones_rows = jnp.ones((2 * SUBLANES, ts), BF16)

    def softmax_pv(buf, sub, l):
        bias = s_scr[pl.ds(pl.multiple_of(sub * ts, ts), ts), :]
        ps = [jnp.exp2(lg_scr[buf, :, h * qb:(h + 1) * qb] + bias).astype(BF16) for h in range(N_HEADS)]
        vt = vt_ref[sub]
        l_rows = []
        for pair in range(N_HEADS // 2):
            g = (2 * pair) // GQA_GROUP
            lhs = jnp.concatenate([vt[g * HEAD_DIM:(g + 1) * HEAD_DIM, :], ones_rows], axis=0)
            o = _dot(lhs, jnp.concatenate(ps[2 * pair:2 * pair + 2], axis=1))
            for u in range(2):
                h = 2 * pair + u
                rows = slice(h * HEAD_DIM, (h + 1) * HEAD_DIM)
                acc_scr[rows, :] = acc_scr[rows, :] + o[0:HEAD_DIM, u * qb:(u + 1) * qb]
                l_rows.append(l[h:h + 1, :] + o[HEAD_DIM:HEAD_DIM + 1, u * qb:(u + 1) * qb])
        return jnp.concatenate(l_rows, axis=0)

    def attn_bounded(j, l):
        logits_into(1, 2 * j + 1)
        l = softmax_pv(0, 2 * j, l)
        logits_into(0, 2 * j + 2)
        return softmax_pv(1, 2 * j + 1, l)

    def attn_online(j, carry):
        m, l = carry
        off = pl.multiple_of(j * ts, ts)
        lg = _dot(k_ref[0, pl.ds(off, ts), :], q_rhs)
        bias = s_scr[pl.ds(off, ts), :]
        m_rows, l_rows, alphas, ps = [], [], [], []
        for h in range(N_HEADS):
            x = lg[:, h * qb:(h + 1) * qb] + bias
            m_new = jnp.maximum(m[h:h + 1, :], jnp.max(_fold_max(x), axis=0, keepdims=True))
            m_use = jnp.where(m_new > NEG_INF, m_new, 0.0)
            alpha = jnp.exp2(m[h:h + 1, :] - m_use)
            p = jnp.exp2(x - m_use)
            l_rows.append(alpha * l[h:h + 1, :] + jnp.sum(_fold_rows(p), axis=0, keepdims=True))
            m_rows.append(m_new)
            alphas.append(alpha)
            ps.append(p.astype(BF16))
        pv_accumulate(vt_ref[j], ps, alphas)
        return jnp.concatenate(m_rows, axis=0), jnp.concatenate(l_rows, axis=0)

    acc_scr[...] = jnp.zeros_like(acc_scr)
    logits_into(0, 0)
    l_fast = lax.fori_loop(0, n_full, attn_bounded, jnp.zeros((N_HEADS, qb), F32))
    l_fast = lax.fori_loop(0, n_half, lambda j, l: softmax_pv(0, 2 * n_full, l), l_fast)
    l_scr[...] = l_fast

    @pl.when(jnp.min(l_fast) < SOFTMAX_SUM_FLOOR)
    def _():
        acc_scr[...] = jnp.zeros_like(acc_scr)
        _, l_exact = lax.fori_loop(0, n_sub, attn_online,
                                   (jnp.full((N_HEADS, qb), NEG_INF, F32), jnp.zeros((N_HEADS, qb), F32)))
        l_scr[...] = l_exact

    l = l_scr[...]
    out_t = jnp.concatenate([acc_scr[h * HEAD_DIM:(h + 1) * HEAD_DIM, :] / l[h:h + 1, :] for h in range(N_HEADS)],
                            axis=0)
    o_ref[0] = out_t.T.astype(BF16)


def _fold_max(x):
    n = x.shape[0] // SUBLANES
    ways = FOLD_WAYS if n % FOLD_WAYS == 0 else 1
    part = jnp.max(x.reshape(n // ways, ways, SUBLANES, x.shape[1]), axis=0)
    return jnp.max(part, axis=0)


def _dsa_prompt(q, qi, wi, kb, vt, kib):
    b, t, _ = kb.shape
    qb = DSA_QUERIES
    assert t % KEY_TILE == 0 and t % qb == 0 and qb % CHUNK == 0
    topk = min(TOPK_MAX, t // 4)
    blk = lambda w: pl.BlockSpec((1, qb, w), lambda bi, i: (bi, i, 0))
    qblk = lambda rows: pl.BlockSpec((1, rows, qb), lambda bi, i: (bi * (t // qb) + i, 0, 0))
    whole = lambda w: pl.BlockSpec((1, t, w), lambda bi, i: (bi, 0, 0))
    return pl.pallas_call(
        functools.partial(_dsa_prompt_kernel, topk=topk),
        grid=(b, t // qb),
        in_specs=[qblk(ATT_WIDTH), qblk(IDX_WIDTH), qblk(IDX_HEADS), whole(KV_WIDTH),
                  pl.BlockSpec((t // KEY_SUB, KV_WIDTH, KEY_SUB), lambda bi, i: (bi, 0, 0)), whole(IDX_DIM)],
        out_specs=blk(ATT_WIDTH),
        out_shape=jax.ShapeDtypeStruct((b, t, ATT_WIDTH), BF16),
        scratch_shapes=[pltpu.VMEM((t, qb), F32), pltpu.VMEM((ATT_WIDTH, qb), F32),
                        pltpu.VMEM((N_HEADS, qb), F32), pltpu.VMEM((1, LANES), F32),
                        pltpu.VMEM((2, KEY_SUB, N_HEADS * qb), F32)],
        compiler_params=pltpu.CompilerParams(dimension_semantics=("arbitrary", "arbitrary"),
                                             vmem_limit_bytes=VMEM_LIMIT),
        name="dsa_prompt",
    )(q, qi, wi, kb, vt, kib)


def _dsa_sample_kernel(q_ref, qi_ref, wi_ref, kn_ref, vn_ref, kin_ref, ckt_ref, cvt_ref, ckit_ref, o_ref,
                       kcat, vcat_t, kicat, s_scr, *, topk):
    tq = q_ref.shape[1]
    past = ckt_ref.shape[1]
    total = kcat.shape[0]
    pad = total - past - tq
    kcat[0:past, :] = ckt_ref[...].T.astype(BF16)
    kcat[past:past + tq, :] = kn_ref[0]
    kcat[past + tq:total, :] = jnp.zeros((pad, KV_WIDTH), BF16)
    kit_wide = jnp.concatenate([ckit_ref[...], jnp.zeros((LANES - IDX_DIM, past), F32)], axis=0)
    kicat[0:past, :] = kit_wide.T[:, 0:IDX_DIM].astype(BF16)
    kicat[past:past + tq, :] = kin_ref[0]
    kicat[past + tq:total, :] = jnp.zeros((pad, IDX_DIM), BF16)
    vcat_t[:, 0:past] = cvt_ref[...].astype(BF16)
    v_new = jnp.concatenate([vn_ref[0].astype(F32), jnp.zeros((pad, KV_WIDTH), F32)], axis=0)
    vcat_t[:, past:total] = v_new.T.astype(BF16)

    lane = lax.broadcasted_iota(jnp.int32, (1, LANES), 1)
    qi = qi_ref[0]
    qi_stack = jnp.concatenate([qi[:, h * IDX_DIM:(h + 1) * IDX_DIM] for h in range(IDX_HEADS)], axis=0)
    wi_rows = jnp.concatenate([wi_ref[0], jnp.zeros((LANES - tq, LANES), F32)], axis=0).T
    w_flat = jnp.zeros((1, LANES), F32)
    for h in range(IDX_HEADS):
        row = wi_rows[h:h + 1, :]
        shifted = row if h == 0 else pltpu.roll(row, h * tq, 1)
        w_flat = jnp.where(lane // tq == h, shifted, w_flat)

    sc = w_flat * jnp.maximum(_dot_t(kicat[...], qi_stack), 0.0)
    for shift in (LANES // 2, LANES // 4, LANES // 8):
        sc = sc + pltpu.roll(sc, shift, 1)
    sc = jnp.where(lane < tq, sc, 0.0)
    for shift in (LANES // 8, LANES // 4, LANES // 2):
        sc = sc + pltpu.roll(sc, shift, 1)
    key = lax.broadcasted_iota(jnp.int32, (total, LANES), 0)
    ok = key < past + tq
    s_scr[...] = jnp.where(ok, sc, NEG_INF)
    smax = jnp.max(jnp.where(ok, sc, NEG_INF), axis=0, keepdims=True)
    smin = jnp.min(jnp.where(ok, sc, -NEG_INF), axis=0, keepdims=True)
    _select_mask(s_scr, 1, 0, total, smin, smax, jnp.full((1, LANES), float(past + tq), F32), topk)

    q = q_ref[0]
    zero = jnp.zeros((tq, HEAD_DIM), BF16)
    q_stack = jnp.concatenate(
        [jnp.concatenate([q[:, h * HEAD_DIM:(h + 1) * HEAD_DIM], zero] if h < GQA_GROUP
                         else [zero, q[:, h * HEAD_DIM:(h + 1) * HEAD_DIM]], axis=1) for h in range(N_HEADS)], axis=0)
    lg = _dot_t(kcat[...], q_stack) + s_scr[...]
    m = jnp.max(lg, axis=0, keepdims=True)
    p = jnp.exp2(lg - m)
    l = jnp.sum(p, axis=0, keepdims=True)
    o = _dot(vcat_t[...], p.astype(BF16))
    o = jnp.where(lane < GQA_GROUP * tq, o[0:HEAD_DIM, :], o[HEAD_DIM:2 * HEAD_DIM, :]) / l
    o_t = jnp.concatenate([o, jnp.zeros((LANES - HEAD_DIM, LANES), F32)], axis=0).T
    o_ref[0] = jnp.concatenate([o_t[h * tq:(h + 1) * tq, 0:HEAD_DIM] for h in range(N_HEADS)], axis=1).astype(BF16)


def _dsa_sample(q, qi, wi, kb, vb, kib, cache_k, cache_v, cache_kidx, layer):
    b, tq, _ = q.shape
    past = cache_k.shape[3]
    assert tq * N_HEADS == LANES and tq * IDX_HEADS == LANES and past % LANES == 0
    total = past + LANES
    topk = min(TOPK_MAX, (past + tq) // 4)
    blk = lambda w: pl.BlockSpec((1, tq, w), lambda bi: (bi, 0, 0))
    cache = lambda w: pl.BlockSpec((None, None, w, past), lambda bi: (layer, bi, 0, 0))
    return pl.pallas_call(
        functools.partial(_dsa_sample_kernel, topk=topk),
        grid=(b,),
        in_specs=[blk(ATT_WIDTH), blk(IDX_WIDTH), blk(LANES), blk(KV_WIDTH), blk(KV_WIDTH), blk(IDX_DIM),
                  cache(KV_WIDTH), cache(KV_WIDTH), cache(IDX_DIM)],
        out_specs=blk(ATT_WIDTH),
        out_shape=jax.ShapeDtypeStruct((b, tq, ATT_WIDTH), BF16),
        scratch_shapes=[pltpu.VMEM((total, KV_WIDTH), BF16), pltpu.VMEM((KV_WIDTH, total), BF16),
                        pltpu.VMEM((total, IDX_DIM), BF16), pltpu.VMEM((total, LANES), F32)],
        compiler_params=pltpu.CompilerParams(dimension_semantics=("arbitrary",), vmem_limit_bytes=VMEM_LIMIT),
        name="dsa_sample",
    )(q, qi, wi, kb, vb, kib, cache_k, cache_v, cache_kidx)


def _first_lane_where(mask, lane):
    return jnp.min(jnp.where(mask, lane, LANES), axis=1, keepdims=True)


def _post_kernel(att_ref, sa_ref, mb_ref, x_ref, wpa_ref, wout_ref, g2_ref, wr_ref, br_ref,
                 h_ref, xn2_ref, gate_ref, *sort_refs):
    pa = _dot(att_ref[...], wpa_ref[...])
    merged = sa_ref[...].astype(F32) * pa + mb_ref[...].astype(F32)
    h = x_ref[...] + _dot(merged.astype(BF16), wout_ref[...])
    h_ref[...] = h
    xn2 = _rms(h, g2_ref[...])
    xh = xn2.astype(BF16)
    xn2_ref[...] = xh

    xl = (xn2 - xh.astype(F32)).astype(BF16)
    wr = wr_ref[...]
    wh = wr.astype(BF16)
    wl = (wr - wh.astype(F32)).astype(BF16)
    lg = _dot(xh, wh) + (_dot(xl, wh) + _dot(xh, wl)) + br_ref[...]

    lane = lax.broadcasted_iota(jnp.int32, lg.shape, 1)
    is_g = lane < N_GROUPS
    mg = jnp.max(jnp.where(is_g, lg, NEG_INF), axis=1, keepdims=True)
    eg = jnp.where(is_g, jnp.exp(lg - mg), 0.0)
    pg = eg / jnp.sum(eg, axis=1, keepdims=True)
    g_w = jnp.max(pg, axis=1, keepdims=True)
    g_idx = _first_lane_where(jnp.logical_and(is_g, pg == g_w), lane)

    e_lo = ROUTER_E0 + EXP_PER_GROUP * g_idx
    is_e = jnp.logical_and(lane >= e_lo, lane < e_lo + EXP_PER_GROUP)
    me = jnp.max(jnp.where(is_e, lg, NEG_INF), axis=1, keepdims=True)
    ee = jnp.where(is_e, jnp.exp(lg - me), 0.0)
    pe = ee / jnp.sum(ee, axis=1, keepdims=True)
    e1 = jnp.max(pe, axis=1, keepdims=True)
    i1 = _first_lane_where(jnp.logical_and(is_e, pe == e1), lane)
    rest = jnp.logical_and(is_e, lane != i1)
    e2 = jnp.max(jnp.where(rest, pe, -1.0), axis=1, keepdims=True)
    i2 = _first_lane_where(jnp.logical_and(rest, pe == e2), lane)
    den = e1 + e2
    gate = jnp.where(lane == i1, e1 / den, jnp.where(lane == i2, e2 / den, 0.0)) * g_w
    group_gates = [jnp.where(lane < EXP_PER_GROUP, pltpu.roll(gate, LANES - (ROUTER_E0 + EXP_PER_GROUP * g), 1), 0.0)
                   for g in range(N_GROUPS)]
    if not sort_refs:
        for g in range(N_GROUPS):
            gate_ref[:, g * LANES:(g + 1) * LANES] = group_gates[g]
        return

    pt_ref, meta_ref = sort_refs
    gate_ref[...] = (group_gates[0] + group_gates[1]) + (group_gates[2] + group_gates[3])
    tm = lg.shape[0]
    onehot = jnp.where(lane == g_idx, 1.0, 0.0)
    ri = lax.broadcasted_iota(jnp.int32, (tm, tm), 0)
    cj = lax.broadcasted_iota(jnp.int32, (tm, tm), 1)
    earlier = jnp.where(cj < ri, 1.0, 0.0).astype(BF16)
    rank = jnp.sum(_dot(earlier, onehot.astype(BF16)) * onehot, axis=1, keepdims=True)
    cnt = jnp.sum(onehot, axis=0, keepdims=True)
    cnt_pad = jnp.floor((cnt + (SORT_PAD - 1.0)) * (1.0 / SORT_PAD)) * SORT_PAD
    seg_off = pltpu.roll(cnt_pad, 1, 1) + pltpu.roll(cnt_pad, 2, 1) + pltpu.roll(cnt_pad, 3, 1)
    dest = (rank + jnp.sum(onehot * seg_off, axis=1, keepdims=True)).astype(jnp.int32)
    col = lax.broadcasted_iota(jnp.int32, pt_ref.shape, 1)
    pt_ref[...] = jnp.where(col == dest, 1.0, 0.0).astype(BF16)
    meta_ref[0] = jnp.concatenate([cnt_pad, seg_off, jnp.zeros((SUBLANES - 2, LANES), F32)], axis=0)


def _sort_rows(tm):
    return -(-(tm + N_GROUPS * SORT_PAD) // LANES) * LANES


def _post(att, sa, mb, x, w, *, tm, sort):
    n, d = x.shape
    row = lambda t: (t, 0)
    const = lambda t: (0, 0)
    gate_w = LANES if sort else N_GROUPS * LANES
    out_specs = [pl.BlockSpec((tm, d), row), pl.BlockSpec((tm, d), row), pl.BlockSpec((tm, gate_w), row)]
    out_shape = [jax.ShapeDtypeStruct((n, d), F32), jax.ShapeDtypeStruct((n, d), BF16),
                 jax.ShapeDtypeStruct((n, gate_w), F32)]
    if sort:
        out_specs += [pl.BlockSpec((tm, _sort_rows(tm)), row), pl.BlockSpec((1, SUBLANES, LANES), lambda t: (t, 0, 0))]
        out_shape += [jax.ShapeDtypeStruct((n, _sort_rows(tm)), BF16),
                      jax.ShapeDtypeStruct((n // tm, SUBLANES, LANES), F32)]
    return pl.pallas_call(
        _post_kernel,
        grid=(n // tm,),
        in_specs=[pl.BlockSpec((tm, ATT_WIDTH), row), pl.BlockSpec((tm, d), row), pl.BlockSpec((tm, d), row),
                  pl.BlockSpec((tm, d), row), pl.BlockSpec(w["wpa"].shape, const), pl.BlockSpec(w["wout"].shape, const),
                  pl.BlockSpec((1, d), const), pl.BlockSpec(w["wr"].shape, const), pl.BlockSpec((1, ROUTER_LANES), const)],
        out_specs=out_specs,
        out_shape=out_shape,
        compiler_params=pltpu.CompilerParams(dimension_semantics=("arbitrary",), vmem_limit_bytes=VMEM_LIMIT),
        name="post_attn",
    )(att, sa, mb, x, w["wpa"], w["wout"], w["g2"], w["wr"], w["br"])


def _moe_kernel(xn2_ref, gate_ref, h_ref, wg_ref, wu_ref, wd_ref, fn_ref, o_ref, acc_ref, *, final):
    g = pl.program_id(1)

    @pl.when(g == 0)
    def _():
        acc_ref[...] = jnp.zeros_like(acc_ref)

    x = xn2_ref[...]
    gate = gate_ref[...]
    act = jax.nn.silu(_dot(x, wg_ref[0])) * _dot(x, wu_ref[0])
    parts = []
    for e in range(EXP_PER_GROUP):
        parts.append((act[:, e * EXP_HIDDEN:(e + 1) * EXP_HIDDEN] * gate[:, e:e + 1]).astype(BF16))
    acc_ref[...] += _dot(jnp.concatenate(parts, axis=1), wd_ref[0])

    @pl.when(g == N_GROUPS - 1)
    def _():
        y = h_ref[...] + acc_ref[...]
        o_ref[...] = _rms(y, fn_ref[...]) if final else y


def _moe(xn2, gate, h, w, fn, *, tm, final):
    n, d = h.shape
    row = lambda t, g: (t, 0)
    wspec = lambda shp: pl.BlockSpec((1,) + shp[1:], lambda t, g: (g, 0, 0))
    return pl.pallas_call(
        functools.partial(_moe_kernel, final=final),
        grid=(n // tm, N_GROUPS),
        in_specs=[pl.BlockSpec((tm, d), row), pl.BlockSpec((tm, LANES), lambda t, g: (t, g)), pl.BlockSpec((tm, d), row),
                  wspec(w["weg"].shape), wspec(w["weu"].shape), wspec(w["wed"].shape),
                  pl.BlockSpec((1, d), lambda t, g: (0, 0))],
        out_specs=pl.BlockSpec((tm, d), row),
        out_shape=jax.ShapeDtypeStruct((n, d), F32),
        scratch_shapes=[pltpu.VMEM((tm, d), F32)],
        compiler_params=pltpu.CompilerParams(dimension_semantics=("arbitrary", "arbitrary"),
                                             vmem_limit_bytes=VMEM_LIMIT),
        name="moe",
    )(xn2, gate, h, w["weg"], w["weu"], w["wed"], fn)


def _moe_sorted_kernel(meta_ref, xn2_ref, gate_ref, h_ref, pt_ref, wg_ref, wu_ref, wd_ref, fn_ref, o_ref,
                       xs_scr, gs_scr, ys_scr, *, final):
    t = pl.program_id(0)
    rows = pt_ref.shape[1]
    pt = pt_ref[...]
    perm = pt.astype(F32).T.astype(BF16)
    xs_scr[0:rows, :] = _dot(perm, xn2_ref[...]).astype(BF16)
    xs_scr[rows:rows + MOE_CHUNK, :] = jnp.zeros((MOE_CHUNK, xs_scr.shape[1]), BF16)
    gate = gate_ref[...]
    g_hi = gate.astype(BF16)
    g_rest = gate - g_hi.astype(F32)
    g_mid = g_rest.astype(BF16)
    g_lo = (g_rest - g_mid.astype(F32)).astype(BF16)
    gs_scr[0:rows, :] = _dot(perm, g_hi) + (_dot(perm, g_mid) + _dot(perm, g_lo))
    gs_scr[rows:rows + MOE_CHUNK, :] = jnp.zeros((MOE_CHUNK, LANES), F32)
    ys_scr[...] = jnp.zeros_like(ys_scr)

    for g in range(N_GROUPS):
        cnt = meta_ref[t * 2 * N_GROUPS + g]
        seg = meta_ref[t * 2 * N_GROUPS + N_GROUPS + g]

        def chunk(c, carry, g=g, cnt=cnt, seg=seg):
            r0 = pl.multiple_of(seg + c * MOE_CHUNK, SORT_PAD)
            x = xs_scr[pl.ds(r0, MOE_CHUNK), :]
            gt = gs_scr[pl.ds(r0, MOE_CHUNK), :]
            act = jax.nn.silu(_dot(x, wg_ref[g])) * _dot(x, wu_ref[g])
            parts = [(act[:, e * EXP_HIDDEN:(e + 1) * EXP_HIDDEN] * gt[:, e:e + 1]).astype(BF16)
                     for e in range(EXP_PER_GROUP)]
            y = _dot(jnp.concatenate(parts, axis=1), wd_ref[g])
            in_seg = lax.broadcasted_iota(jnp.int32, (MOE_CHUNK, 1), 0) + c * MOE_CHUNK < cnt
            ys_scr[pl.ds(r0, MOE_CHUNK), :] = jnp.where(in_seg, y, ys_scr[pl.ds(r0, MOE_CHUNK), :])
            return carry

        lax.fori_loop(0, (cnt + MOE_CHUNK - 1) // MOE_CHUNK, chunk, 0)

    y = h_ref[...] + _dot(pt, ys_scr[0:rows, :].astype(BF16))
    o_ref[...] = _rms(y, fn_ref[...]) if final else y


def _moe_sorted(xn2, gate, h, pt, meta, w, fn, *, tm, final):
    n, d = h.shape
    rows = pt.shape[1]
    row = lambda t, m: (t, 0)
    resident = lambda shp: pl.BlockSpec(shp, lambda t, m: (0, 0, 0), pipeline_mode=pl.Buffered(1))
    grid_spec = pltpu.PrefetchScalarGridSpec(
        num_scalar_prefetch=1,
        grid=(n // tm,),
        in_specs=[pl.BlockSpec((tm, d), row), pl.BlockSpec((tm, LANES), row), pl.BlockSpec((tm, d), row),
                  pl.BlockSpec((tm, rows), row),
                  resident(w["weg"].shape), resident(w["weu"].shape), resident(w["wed"].shape),
                  pl.BlockSpec((1, d), lambda t, m: (0, 0))],
        out_specs=pl.BlockSpec((tm, d), row),
        scratch_shapes=[pltpu.VMEM((rows + MOE_CHUNK, d), BF16), pltpu.VMEM((rows + MOE_CHUNK, LANES), F32),
                        pltpu.VMEM((rows + MOE_CHUNK, d), F32)],
    )
    return pl.pallas_call(
        functools.partial(_moe_sorted_kernel, final=final),
        grid_spec=grid_spec,
        out_shape=jax.ShapeDtypeStruct((n, d), F32),
        compiler_params=pltpu.CompilerParams(dimension_semantics=("arbitrary",), vmem_limit_bytes=VMEM_LIMIT),
        name="moe_sorted",
    )(meta, xn2, gate, h, pt, w["weg"], w["weu"], w["wed"], fn)


def _pack_experts_kernel(w_ref, o_ref):
    for e in range(EXP_PER_GROUP):
        o_ref[:, e * EXP_HIDDEN:(e + 1) * EXP_HIDDEN] = w_ref[e].astype(BF16)


def _pack_experts(w):
    depth, n_groups, n_exp, d, f = w.shape
    rows = 2 * MXU_DIM
    return pl.pallas_call(
        _pack_experts_kernel,
        grid=(depth, n_groups, d // rows),
        in_specs=[pl.BlockSpec((None, None, n_exp, rows, f), lambda l, g, r: (l, g, 0, r, 0))],
        out_specs=pl.BlockSpec((None, None, rows, n_exp * f), lambda l, g, r: (l, g, r, 0)),
        out_shape=jax.ShapeDtypeStruct((depth, n_groups, d, n_exp * f), BF16),
        compiler_params=pltpu.CompilerParams(dimension_semantics=("arbitrary",) * 3, vmem_limit_bytes=VMEM_LIMIT),
        name="pack_experts",
    )(w)


def _rope_tables(pos, d):
    inv = ROPE_THETA ** (-jnp.arange(0, d, 2, dtype=F32) / d)
    ang = pos[:, None] * inv[None, :]
    cos, sin = jnp.cos(ang), jnp.sin(ang)
    reps = LANES // d
    return (jnp.tile(jnp.concatenate([cos, cos], axis=-1), (1, reps)),
            jnp.tile(jnp.concatenate([-sin, sin], axis=-1), (1, reps)))


def _layer_weights(l, norm1, w_in, sgu_ln_g, sgu_ln_b, sgu_w, sgu_b, w_pa, w_pb, w_out, norm2,
                   w_rg, b_rg, w_re, b_re, w_eg, w_eu, w_ed, sgu_chunk):
    d = w_in.shape[1]
    a_end = ATT_WIDTH + 2 * KV_WIDTH + IDX_WIDTH
    b_end = a_end + IDX_DIM + IDX_HEADS
    reps = SGU_CHUNK // sgu_chunk
    n_router = N_GROUPS + N_GROUPS * EXP_PER_GROUP
    return dict(
        g1=norm1[l][None, :],
        wa=w_in[l][:, :a_end].astype(BF16),
        wb=jnp.pad(w_in[l][:, a_end:b_end], ((0, 0), (0, LANES - (b_end - a_end)))).astype(BF16),
        wc=w_in[l][:, b_end:].astype(BF16),
        sguw=jnp.tile(sgu_w[l][:, :sgu_chunk, :sgu_chunk], (1, reps, reps)),
        sgub=jnp.repeat(jnp.tile(sgu_b[l][:, :sgu_chunk], (1, reps)).T, SGU_GDIM, axis=1),
        lng=sgu_ln_g[l][None, :], lnb=sgu_ln_b[l][None, :],
        wpb=w_pb[l].astype(BF16), wpa=w_pa[l].astype(BF16), wout=w_out[l].astype(BF16),
        g2=norm2[l][None, :],
        wr=jnp.pad(jnp.concatenate([w_rg[l], w_re[l]], axis=1), ((0, 0), (0, ROUTER_LANES - n_router))),
        br=jnp.pad(jnp.concatenate([b_rg[l], b_re[l]]), (0, ROUTER_LANES - n_router))[None, :],
        weg=w_eg[l], weu=w_eu[l],
        wed=w_ed[l].reshape(N_GROUPS, GROUP_HIDDEN, d).astype(BF16),
    )


def _token_tile(n):
    return 512 if n % 512 == 0 else n


def kernel(x_prompt, x_sample, cache_k, cache_v, cache_kidx, norm1, w_in, sgu_ln_g, sgu_ln_b, sgu_w, sgu_b,
           w_pa, w_pb, w_out, norm2, w_rg, b_rg, w_re, b_re, w_eg, w_eu, w_ed, final_norm):
    bp, tp, d = x_prompt.shape
    bs, ts, _ = x_sample.shape
    depth, _, past = cache_k.shape[:3]
    np_, ns = bp * tp, bs * ts
    tm_p, tm_s = _token_tile(tp), _token_tile(ns)
    assert tp % tm_p == 0 and tp % Q_BLOCK == 0 and SGU_CHUNK % ts == 0 and tm_s % SGU_CHUNK == 0
    assert tm_p % KEY_SUB == 0 and tm_s % KEY_SUB == 0

    pos_p = jnp.arange(tp, dtype=F32)
    pos_s = jnp.tile(past + jnp.arange(ts, dtype=F32), tm_s // ts)
    tabs_p = _rope_tables(pos_p, HEAD_DIM) + _rope_tables(pos_p, IDX_DIM)
    tabs_s = _rope_tables(pos_s, HEAD_DIM) + _rope_tables(pos_s, IDX_DIM)
    ckt = jnp.transpose(cache_k, (0, 1, 3, 4, 2)).reshape(depth, bs, KV_WIDTH, past)
    cvt = jnp.transpose(cache_v, (0, 1, 3, 4, 2)).reshape(depth, bs, KV_WIDTH, past)
    ckit = jnp.transpose(cache_kidx, (0, 1, 3, 2))
    fn = final_norm[None, :]
    weg_all, weu_all = _pack_experts(w_eg), _pack_experts(w_eu)

    hp = x_prompt.reshape(np_, d)
    hs = x_sample.reshape(ns, d)
    outs = [[] for _ in range(4)]
    kv_bufs = None
    for l in range(depth):
        wargs = (norm1, w_in, sgu_ln_g, sgu_ln_b, sgu_w, sgu_b, w_pa, w_pb, w_out, norm2,
                 w_rg, b_rg, w_re, b_re, weg_all, weu_all, w_ed)
        wp = _layer_weights(l, *wargs, sgu_chunk=SGU_CHUNK)
        ws = dict(wp)
        ws.update({k: v for k, v in _layer_weights(l, *wargs, sgu_chunk=ts).items() if k in ("sguw", "sgub")})
        final = l == depth - 1

        q, kt_all, kb, vt_all, _, vt, qi, kit_all, kib, wi, mb, sa = _inproj(
            hp, wp, tabs_p, tm=tm_p, sgu_chunk=SGU_CHUNK, want_vln=False,
            kv_layer=l, kv_bufs=kv_bufs, kv_shape=(depth, bp, tp))
        kv_bufs = (kt_all, vt_all, kit_all)
        r3 = lambda a: a.reshape(bp, tp, a.shape[-1])
        att = _dsa_prompt(q, qi, wi, r3(kb), vt, r3(kib)).reshape(np_, ATT_WIDTH)
        h, xn2, gate, pt, meta = _post(att, sa, mb, hp, wp, tm=tm_p, sort=True)
        meta = meta[:, 0:2, 0:N_GROUPS].astype(jnp.int32).reshape(-1)
        hp = _moe_sorted(xn2, gate, h, pt, meta, wp, fn, tm=tm_p, final=final)

        q, kf, kb, vf, vb, _, qi, kif, kib, wi, mb, sa, vln = _inproj(hs, ws, tabs_s, tm=tm_s, sgu_chunk=ts,
                                                                      want_vln=True)
        r3 = lambda a: a.reshape(bs, ts, a.shape[-1])
        att = _dsa_sample(r3(q), r3(qi), r3(wi), r3(kb), r3(vb), r3(kib), ckt, cvt, ckit, l)
        h, xn2, gate = _post(att.reshape(ns, ATT_WIDTH), sa, mb, hs, ws, tm=tm_s, sort=False)
        hs = _moe(xn2, gate, h, ws, fn, tm=tm_s, final=final)
        outs[0].append(kf.reshape(bs, ts, N_KV_HEADS, HEAD_DIM))
        outs[1].append(vf.reshape(bs, ts, N_KV_HEADS, HEAD_DIM))
        outs[2].append(kif.reshape(bs, ts, IDX_DIM))
        outs[3].append(vln.reshape(bs, ts, SGU_WIDTH))

    kt_all, vt_all, kit_all = kv_bufs
    heads = lambda a: jnp.transpose(a.reshape(depth, bp, N_KV_HEADS, HEAD_DIM, tp), (0, 1, 4, 2, 3))
    return ((hp.reshape(bp, tp, d), hs.reshape(bs, ts, d), heads(kt_all), heads(vt_all),
             jnp.transpose(kit_all, (0, 1, 3, 2))) + tuple(jnp.stack(o) for o in outs))
```

```python
import functools

import jax
import jax.numpy as jnp
from jax import lax
from jax.experimental import pallas as pl
from jax.experimental.pallas import tpu as pltpu

F32 = jnp.float32
BF16 = jnp.bfloat16

CHUNK = 64
N_HEADS = 8
N_KV_HEADS = 2
HEAD_DIM = 64
GQA_GROUP = N_HEADS // N_KV_HEADS
ATT_WIDTH = N_HEADS * HEAD_DIM
KV_WIDTH = N_KV_HEADS * HEAD_DIM
IDX_HEADS = 8
IDX_DIM = 32
IDX_WIDTH = IDX_HEADS * IDX_DIM
TOPK_MAX = 256
Q_BLOCK = 128
DSA_QUERIES = 256
SGU_GROUPS = 4
SGU_CHUNK = 128
SGU_WIDTH = 512
SGU_GDIM = SGU_WIDTH // SGU_GROUPS
N_GROUPS = 4
EXP_PER_GROUP = 8
EXP_HIDDEN = 128
GROUP_HIDDEN = EXP_PER_GROUP * EXP_HIDDEN
ROPE_THETA = 10000.0
EPS = 1e-6
Q_SCALE = HEAD_DIM ** -0.5 * 1.4426950408889634

LANES = 128
SUBLANES = 8
MXU_DIM = 256
ROUTER_LANES = LANES
ROUTER_E0 = N_GROUPS
VMEM_LIMIT = 56 * 1024 * 1024
KEY_TILE = 2 * MXU_DIM
KEY_SUB = MXU_DIM
FOLD_WAYS = 4
COUNT_ROWS = 64
SEARCH_FIRST_STEPS = 17
SEARCH_SECOND_STEPS = 0
SEARCH_ROUND_STEPS = 1
SEARCH_MAX_ROUNDS = 34
SORT_PAD = 16
MOE_CHUNK = 144
KNORM_SLACK = 1.0 + 2.0 ** -6
SOFTMAX_SUM_FLOOR = 2.0 ** -80
MAX_SUBNORMAL_BITS = 0x007FFFFF
NEG_INF = float("-inf")


def _dot(a, b):
    return jnp.dot(a, b, preferred_element_type=F32)


def _dot_t(a, b):
    return lax.dot_general(a, b, (((1,), (1,)), ((), ())), preferred_element_type=F32)


def _rms(x, g):
    return x * lax.rsqrt(jnp.mean(x * x, axis=-1, keepdims=True) + EPS) * g


def _swap_halves(x, d):
    lane = lax.broadcasted_iota(jnp.int32, x.shape, 1)
    first = (lane % d) < (d // 2)
    return jnp.where(first, pltpu.roll(x, LANES - d // 2, 1), pltpu.roll(x, d // 2, 1))


def _rope(x, cos, sin_signed, d):
    return x * cos + _swap_halves(x, d) * sin_signed


def _inproj_kernel(*refs, sgu_chunk, kv_transposed, n_alias, want_vln):
    (x_ref, g1_ref, wa_ref, wb_ref, wc_ref, cq_ref, sq_ref, ci_ref, si_ref,
     sguw_ref, sgub_ref, lng_ref, lnb_ref, wpb_ref) = refs[:14]
    (q_ref, kf_ref, kb_ref, vf_ref, vb_ref, vt_ref, qi_ref, kif_ref, kib_ref, wi_ref,
     mb_ref, sa_ref) = refs[14 + n_alias:26 + n_alias]
    tm = x_ref.shape[0]
    xn = _rms(x_ref[...], g1_ref[...]).astype(BF16)
    cq, sq, ci, si = cq_ref[...], sq_ref[...], ci_ref[...], si_ref[...]

    a = _dot(xn, wa_ref[...])
    def emit_queries(ref, val):
        if kv_transposed:
            for c in range(tm // DSA_QUERIES):
                ref[c] = val[c * DSA_QUERIES:(c + 1) * DSA_QUERIES, :].T[0:ref.shape[1], :].astype(ref.dtype)
        else:
            ref[...] = val.astype(ref.dtype)

    emit_queries(q_ref, jnp.concatenate(
        [_rope(a[:, c * LANES:(c + 1) * LANES], cq, sq, HEAD_DIM) * Q_SCALE for c in range(ATT_WIDTH // LANES)], axis=1))
    k = _rope(a[:, ATT_WIDTH:ATT_WIDTH + KV_WIDTH], cq, sq, HEAD_DIM)
    kb_ref[...] = k.astype(BF16)
    v = a[:, ATT_WIDTH + KV_WIDTH:ATT_WIDTH + 2 * KV_WIDTH]
    vb_ref[...] = v.astype(BF16)
    if kv_transposed:
        kf_ref[...] = k.T
        vf_ref[...] = v.T
    else:
        kf_ref[...] = k
        vf_ref[...] = v
    for c in range(tm // KEY_SUB):
        vt_ref[c] = v[c * KEY_SUB:(c + 1) * KEY_SUB, :].T.astype(BF16)
    qi0 = ATT_WIDTH + 2 * KV_WIDTH
    emit_queries(qi_ref, jnp.concatenate(
        [_rope(a[:, qi0 + c * LANES:qi0 + (c + 1) * LANES], ci, si, IDX_DIM) for c in range(IDX_WIDTH // LANES)], axis=1))

    b = _dot(xn, wb_ref[...])
    ki_wide = _rope(b, ci, si, IDX_DIM)
    ki = ki_wide[:, :IDX_DIM]
    kif_ref[...] = ki_wide.T[0:IDX_DIM, :] if kv_transposed else ki
    kib_ref[...] = ki.astype(BF16)
    emit_queries(wi_ref, pltpu.roll(b, LANES - IDX_DIM, 1) * (IDX_HEADS ** -0.5))

    u = jax.nn.gelu(_dot(xn, wc_ref[:, 0:SGU_WIDTH]))
    vs = jax.nn.gelu(_dot(xn, wc_ref[:, SGU_WIDTH:2 * SGU_WIDTH]))
    mu = jnp.mean(vs, axis=-1, keepdims=True)
    var = jnp.mean(jnp.square(vs - mu), axis=-1, keepdims=True)
    vln = (vs - mu) * lax.rsqrt(var + EPS) * lng_ref[...] + lnb_ref[...]
    if want_vln:
        refs[26 + n_alias][...] = vln
    ri = lax.broadcasted_iota(jnp.int32, (SGU_CHUNK, SGU_CHUNK), 0)
    cj = lax.broadcasted_iota(jnp.int32, (SGU_CHUNK, SGU_CHUNK), 1)
    mix_mask = (cj <= ri) & ((ri // sgu_chunk) == (cj // sgu_chunk))
    vln_b = vln.astype(BF16)
    w_eff = [jnp.where(mix_mask, sguw_ref[g], 0.0).astype(BF16) for g in range(SGU_GROUPS)]
    sg_rows = []
    for r in range(tm // SGU_CHUNK):
        rows = slice(r * SGU_CHUNK, (r + 1) * SGU_CHUNK)
        mixed = [_dot(w_eff[g], vln_b[rows, g * SGU_GDIM:(g + 1) * SGU_GDIM]) for g in range(SGU_GROUPS)]
        sg_rows.append(u[rows, :] * (jnp.concatenate(mixed, axis=1) + sgub_ref[...]))
    sg = jnp.concatenate(sg_rows, axis=0).astype(BF16)
    d = x_ref.shape[1]
    sa_ref[...] = jax.nn.sigmoid(_dot(xn, wc_ref[:, 2 * SGU_WIDTH:2 * SGU_WIDTH + d])).astype(BF16)
    gb = jax.nn.sigmoid(_dot(xn, wc_ref[:, 2 * SGU_WIDTH + d:]))
    mb_ref[...] = (gb * _dot(sg, wpb_ref[...])).astype(BF16)


def _inproj(x, w, tabs, *, tm, sgu_chunk, want_vln, kv_layer=None, kv_bufs=None, kv_shape=None):
    n, d = x.shape
    n_tab_tiles = tabs[0].shape[0] // tm
    row = lambda t: (t, 0)
    const = lambda t: (0, 0)
    tab = lambda t: (t % n_tab_tiles, 0)
    kv_transposed = kv_layer is not None
    if kv_transposed:
        depth, streams, steps = kv_shape
        per_stream = steps // tm
        kv_sds = lambda wd: jax.ShapeDtypeStruct((depth, streams, wd, steps), F32)
        kv_spec = lambda wd: pl.BlockSpec((None, None, wd, tm),
                                          lambda t: (kv_layer, t // per_stream, 0, t % per_stream))
    else:
        kv_sds = lambda wd: jax.ShapeDtypeStruct((n, wd), F32)
        kv_spec = lambda wd: pl.BlockSpec((tm, wd), row)
    tiled = lambda wd, dt: (jax.ShapeDtypeStruct((n, wd), dt), pl.BlockSpec((tm, wd), row))
    if kv_transposed:
        qb = DSA_QUERIES
        query = lambda wd, rows, dt: (jax.ShapeDtypeStruct((n // qb, rows, qb), dt),
                                      pl.BlockSpec((tm // qb, rows, qb), lambda t: (t, 0, 0)))
    else:
        query = lambda wd, rows, dt: tiled(wd, dt)
    outs = [
        query(ATT_WIDTH, ATT_WIDTH, BF16),
        (kv_sds(KV_WIDTH), kv_spec(KV_WIDTH)),
        tiled(KV_WIDTH, BF16),
        (kv_sds(KV_WIDTH), kv_spec(KV_WIDTH)),
        tiled(KV_WIDTH, BF16),
        (jax.ShapeDtypeStruct((n // KEY_SUB, KV_WIDTH, KEY_SUB), BF16),
         pl.BlockSpec((tm // KEY_SUB, KV_WIDTH, KEY_SUB), lambda t: (t, 0, 0))),
        query(IDX_WIDTH, IDX_WIDTH, BF16),
        (kv_sds(IDX_DIM), kv_spec(IDX_DIM)),
        tiled(IDX_DIM, BF16),
        query(LANES, IDX_HEADS, F32),
        tiled(d, BF16),
        tiled(d, BF16),
    ]
    if want_vln:
        outs.append(tiled(SGU_WIDTH, F32))
    in_specs = [
        pl.BlockSpec((tm, d), row),
        pl.BlockSpec((1, d), const),
        pl.BlockSpec(w["wa"].shape, const, pipeline_mode=pl.Buffered(1)),
        pl.BlockSpec(w["wb"].shape, const, pipeline_mode=pl.Buffered(1)),
        pl.BlockSpec(w["wc"].shape, const, pipeline_mode=pl.Buffered(1)),
        pl.BlockSpec((tm, LANES), tab), pl.BlockSpec((tm, LANES), tab),
        pl.BlockSpec((tm, LANES), tab), pl.BlockSpec((tm, LANES), tab),
        pl.BlockSpec(w["sguw"].shape, lambda t: (0, 0, 0)),
        pl.BlockSpec(w["sgub"].shape, const),
        pl.BlockSpec((1, SGU_WIDTH), const),
        pl.BlockSpec((1, SGU_WIDTH), const),
        pl.BlockSpec(w["wpb"].shape, const),
    ]
    args = [x, w["g1"], w["wa"], w["wb"], w["wc"], *tabs, w["sguw"], w["sgub"], w["lng"], w["lnb"], w["wpb"]]
    aliases = {}
    if kv_bufs is not None:
        for buf, out_idx in zip(kv_bufs, (1, 3, 7)):
            aliases[len(args)] = out_idx
            in_specs.append(pl.BlockSpec(memory_space=pl.ANY))
            args.append(buf)
    return pl.pallas_call(
        functools.partial(_inproj_kernel, sgu_chunk=sgu_chunk, kv_transposed=kv_transposed,
                          n_alias=len(aliases), want_vln=want_vln),
        grid=(n // tm,),
        in_specs=in_specs,
        out_specs=[o[1] for o in outs],
        out_shape=[o[0] for o in outs],
        input_output_aliases=aliases,
        compiler_params=pltpu.CompilerParams(dimension_semantics=("arbitrary",), vmem_limit_bytes=VMEM_LIMIT),
        name="inproj",
    )(*args)


def _float_key(x):
    b = lax.bitcast_convert_type(x, jnp.int32)
    k = b ^ ((b >> 31) & jnp.int32(0x7FFFFFFF))
    return jnp.where(k > MAX_SUBNORMAL_BITS, k - MAX_SUBNORMAL_BITS,
                     jnp.where(k < -MAX_SUBNORMAL_BITS - 1, k + MAX_SUBNORMAL_BITS, jnp.clip(k, -1, 0)))


def _key_float(k):
    k = jnp.where(k > 0, k + MAX_SUBNORMAL_BITS, jnp.where(k < -1, k - MAX_SUBNORMAL_BITS, k))
    return lax.bitcast_convert_type(k ^ ((k >> 31) & jnp.int32(0x7FFFFFFF)), F32)


def _fold_rows(x):
    n = x.shape[0] // SUBLANES
    ways = FOLD_WAYS if n % FOLD_WAYS == 0 else 1
    part = jnp.sum(x.reshape(n // ways, ways, SUBLANES, x.shape[1]), axis=0)
    return jnp.sum(part, axis=0)


def _topk_threshold(count_ge, snap, smin, smax, n_adm, kk):
    lo0 = _float_key(smin + 0.0)
    hx0 = _float_key(smax + 0.0) + 1

    def is_active(st):
        lo, hx, c_lo, _ = st
        return jnp.logical_and(c_lo != kk, lo + 1 < hx)

    def step(st, by_key):
        lo, hx, c_lo, c_hx = st
        active = is_active(st)
        if by_key:
            mid = (lo >> 1) + (hx >> 1) + (lo & hx & 1)
        else:
            mid = _float_key(0.5 * _key_float(lo) + 0.5 * _key_float(hx))
        mid = jnp.where(active, jnp.minimum(jnp.maximum(mid, lo + 1), hx - 1), lo)
        c = count_ge(_key_float(mid))
        up = jnp.logical_and(active, c >= kk)
        down = jnp.logical_and(active, c < kk)
        return (jnp.where(up, mid, lo), jnp.where(down, mid, hx),
                jnp.where(up, c, c_lo), jnp.where(down, c, c_hx))

    def snap_step(st):
        lo, hx, c_lo, c_hx = st
        active = is_active(st)
        first_in, last_in = snap(_key_float(lo), _key_float(hx))
        lo_n = jnp.where(active, _float_key(first_in + 0.0), lo)
        hx_n = jnp.where(active, _float_key(last_in + 0.0) + 1, hx)
        few = jnp.logical_and(active, c_lo - c_hx <= 2.0)
        same = first_in == last_in
        take_last = kk - c_hx <= 1.0
        thr_key = _float_key(jnp.where(take_last, last_in, first_in) + 0.0)
        c_ge = jnp.where(jnp.logical_and(take_last, jnp.logical_not(same)), c_hx + 1.0, c_lo)
        c_gt = jnp.where(jnp.logical_or(take_last, same), c_hx, c_hx + 1.0)
        return (jnp.where(few, thr_key, lo_n), jnp.where(few, thr_key + 1, hx_n),
                jnp.where(few, c_ge, c_lo), jnp.where(few, c_gt, c_hx))

    def n_active(st):
        return jnp.max(jnp.where(is_active(st), 1.0, 0.0))

    st = (lo0, hx0, n_adm, jnp.zeros_like(n_adm))
    for _ in range(SEARCH_FIRST_STEPS):
        st = step(st, by_key=False)
    st = snap_step(st)
    for _ in range(SEARCH_SECOND_STEPS):
        st = step(st, by_key=False)
    act = n_active(st)

    def cond(carry):
        it, act, _ = carry
        return jnp.logical_and(it < SEARCH_MAX_ROUNDS, act > 0.5)

    def body(carry):
        it, _, st = carry
        st = step(snap_step(st), by_key=True)
        for _ in range(SEARCH_ROUND_STEPS):
            st = step(st, by_key=False)
        return it + 1, n_active(st), st

    _, _, (lo, _, c_lo, c_hx) = lax.while_loop(cond, body, (jnp.int32(0), act, st))
    return _key_float(lo), c_lo, c_hx


def _demote_excess_ties(s_ref, n_chunks, thr, need):
    ri = lax.broadcasted_iota(jnp.int32, (LANES, LANES), 0)
    cj = lax.broadcasted_iota(jnp.int32, (LANES, LANES), 1)
    lower = jnp.where(cj <= ri, 1.0, 0.0).astype(BF16)

    def body(j, run):
        off = pl.multiple_of(j * LANES, LANES)
        blk = s_ref[pl.ds(off, LANES), :]
        tied = blk == thr
        tied_f = jnp.where(tied, 1.0, 0.0)
        prefix = _dot(lower, tied_f.astype(BF16)) + run
        s_ref[pl.ds(off, LANES), :] = jnp.where(jnp.logical_and(tied, prefix > need), NEG_INF, blk)
        return run + jnp.sum(tied_f, axis=0, keepdims=True)

    lax.fori_loop(0, n_chunks, body, jnp.zeros((1, s_ref.shape[1]), F32))


def _select_mask(s_ref, n_full, n_half, tk, smin, smax, n_adm, topk, selected=0.0):
    kk = jnp.minimum(n_adm, float(topk))
    half = tk // 2

    def over_rows(visit, carry):
        carry = lax.fori_loop(0, n_full, lambda j, c: visit(pl.multiple_of(j * tk, tk), tk, c), carry)
        return lax.fori_loop(0, n_half, lambda j, c: visit(pl.multiple_of(n_full * tk, half), half, c), carry)

    def count_ge(t):
        def visit(off, rows, acc):
            for r in range(0, rows, COUNT_ROWS):
                acc = acc + _fold_rows(jnp.where(s_ref[pl.ds(off + r, COUNT_ROWS), :] >= t, 1.0, 0.0))
            return acc
        acc = over_rows(visit, jnp.zeros((SUBLANES, s_ref.shape[1]), F32))
        return jnp.sum(acc, axis=0, keepdims=True)

    def snap(lo_f, hx_f):
        def visit(off, rows, carry):
            first_in, last_in = carry
            for r in range(0, rows, COUNT_ROWS):
                blk = s_ref[pl.ds(off + r, COUNT_ROWS), :]
                first_in = jnp.minimum(first_in, -_fold_max(jnp.where(blk >= lo_f, -blk, NEG_INF)))
                last_in = jnp.maximum(last_in, _fold_max(jnp.where(blk < hx_f, blk, NEG_INF)))
            return first_in, last_in
        first_in, last_in = over_rows(visit, (jnp.full((SUBLANES, s_ref.shape[1]), -NEG_INF, F32),
                                              jnp.full((SUBLANES, s_ref.shape[1]), NEG_INF, F32)))
        return jnp.min(first_in, axis=0, keepdims=True), jnp.max(last_in, axis=0, keepdims=True)

    thr, c_ge, c_gt = _topk_threshold(count_ge, snap, smin, smax, n_adm, kk)

    @pl.when(jnp.max(c_ge - kk) > 0.5)
    def _():
        _demote_excess_ties(s_ref, (n_full * tk + n_half * half) // LANES, thr, kk - c_gt)

    def mask_rows(off, rows, carry):
        s_ref[pl.ds(off, rows), :] = jnp.where(s_ref[pl.ds(off, rows), :] >= thr, selected, NEG_INF)
        return carry

    over_rows(mask_rows, 0)


def _dsa_prompt_kernel(q_ref, qi_ref, wi_ref, k_ref, vt_ref, ki_ref, o_ref, s_scr, acc_scr, l_scr, kn_scr, lg_scr,
                       *, topk):
    i = pl.program_id(1)
    tk, ts = KEY_TILE, KEY_SUB
    qb = q_ref.shape[2]
    n_keys = (i + 1) * qb
    qcol = lax.broadcasted_iota(jnp.int32, (1, qb), 1)
    lim = ((i * qb + qcol) // CHUNK + 1) * CHUNK

    @pl.when(i == 0)
    def _():
        ri = lax.broadcasted_iota(jnp.int32, (KV_WIDTH, LANES), 0)
        cj = lax.broadcasted_iota(jnp.int32, (KV_WIDTH, LANES), 1)
        head_sum = jnp.where(ri // HEAD_DIM == cj, 1.0, 0.0).astype(BF16)

        def body(j, acc):
            kt = k_ref[0, pl.ds(pl.multiple_of(j * tk, tk), tk), :].astype(F32)
            return jnp.maximum(acc, _fold_max(_dot((kt * kt).astype(BF16), head_sum)))

        acc = lax.fori_loop(0, k_ref.shape[1] // tk, body, jnp.zeros((SUBLANES, LANES), F32))
        kn_scr[...] = jnp.max(acc, axis=0, keepdims=True) * KNORM_SLACK

    qt = q_ref[0]
    qit = qi_ref[0]
    wit = wi_ref[0]
    qi_rhs = jnp.concatenate([qit[h * IDX_DIM:(h + 1) * IDX_DIM, :] for h in range(IDX_HEADS)], axis=1)
    zero = jnp.zeros((HEAD_DIM, qb), BF16)
    q_cols = []
    for h in range(N_HEADS):
        qh = qt[h * HEAD_DIM:(h + 1) * HEAD_DIM, :]
        q_cols.append(jnp.concatenate([qh, zero] if h < GQA_GROUP else [zero, qh], axis=0))
    q_rhs = jnp.concatenate(q_cols, axis=1)

    def score_sub(sub, carry):
        smax, smin = carry
        off = pl.multiple_of(sub * ts, ts)
        r = _dot(ki_ref[0, pl.ds(off, ts), :], qi_rhs)
        acc = wit[0:1, :] * jnp.maximum(r[:, 0:qb], 0.0)
        for h in range(1, IDX_HEADS):
            acc = acc + wit[h:h + 1, :] * jnp.maximum(r[:, h * qb:(h + 1) * qb], 0.0)
        key = off + lax.broadcasted_iota(jnp.int32, (ts, qb), 0)
        ok = key < lim
        s_scr[pl.ds(off, ts), :] = jnp.where(ok, acc, NEG_INF)
        smax = jnp.maximum(smax, _fold_max(jnp.where(ok, acc, NEG_INF)))
        smin = jnp.minimum(smin, -_fold_max(jnp.where(ok, -acc, NEG_INF)))
        return smax, smin

    n_sub = n_keys // ts
    n_full, n_half = n_sub // 2, n_sub % 2
    carry = (jnp.full((SUBLANES, qb), NEG_INF, F32), jnp.full((SUBLANES, qb), -NEG_INF, F32))
    carry = lax.fori_loop(0, n_full, lambda j, c: score_sub(2 * j + 1, score_sub(2 * j, c)), carry)
    smax, smin = lax.fori_loop(0, n_half, lambda j, c: score_sub(2 * n_full, c), carry)
    smax = jnp.max(smax, axis=0, keepdims=True)
    smin = jnp.min(smin, axis=0, keepdims=True)
    qsq = jnp.square(qt.astype(F32))
    kn = kn_scr[...]
    lane1 = lax.broadcasted_iota(jnp.int32, (1, LANES), 1)
    shift = jnp.zeros((1, qb), F32)
    for h in range(N_HEADS):
        qn2 = jnp.sum(_fold_rows(qsq[h * HEAD_DIM:(h + 1) * HEAD_DIM, :]), axis=0, keepdims=True)
        kmax2 = jnp.max(jnp.where(lane1 == h // GQA_GROUP, kn, 0.0))
        shift = jnp.maximum(shift, jnp.sqrt(qn2 * kmax2))
    _select_mask(s_scr, n_full, n_half, tk, smin, smax, lim.astype(F32), topk, selected=-shift)

    def pv_accumulate(vt, ps, alphas):
        for pair in range(N_HEADS // 2):
            g = (2 * pair) // GQA_GROUP
            o = _dot(vt[g * HEAD_DIM:(g + 1) * HEAD_DIM, :], jnp.concatenate(ps[2 * pair:2 * pair + 2], axis=1))
            for u in range(2):
                h = 2 * pair + u
                rows = slice(h * HEAD_DIM, (h + 1) * HEAD_DIM)
                prev = acc_scr[rows, :] if alphas is None else alphas[h] * acc_scr[rows, :]
                acc_scr[rows, :] = prev + o[:, u * qb:(u + 1) * qb]

    def logits_into(buf, sub):
        off = pl.multiple_of(jnp.minimum(sub * ts, k_ref.shape[1] - ts), ts)
        lg_scr[buf] = _dot(k_ref[0, pl.ds(off, ts), :], q_rhs)

    ones_rows = jnp.ones((2 * SUBLANES, ts), BF16)

    def softmax_pv(buf, sub, l):
        bias = s_scr[pl.ds(pl.multiple_of(sub * ts, ts), ts), :]
        ps = [jnp.exp2(lg_scr[buf, :, h * qb:(h + 1) * qb] + bias).astype(BF16) for h in range(N_HEADS)]
        vt = vt_ref[sub]
        l_rows = []
        for pair in range(N_HEADS // 2):
            g = (2 * pair) // GQA_GROUP
            lhs = jnp.concatenate([vt[g * HEAD_DIM:(g + 1) * HEAD_DIM, :], ones_rows], axis=0)
            o = _dot(lhs, jnp.concatenate(ps[2 * pair:2 * pair + 2], axis=1))
            for u in range(2):
                h = 2 * pair + u
                rows = slice(h * HEAD_DIM, (h + 1) * HEAD_DIM)
                acc_scr[rows, :] = acc_scr[rows, :] + o[0:HEAD_DIM, u * qb:(u + 1) * qb]
                l_rows.append(l[h:h + 1, :] + o[HEAD_DIM:HEAD_DIM + 1, u * qb:(u + 1) * qb])
        return jnp.concatenate(l_rows, axis=0)

    def attn_bounded(j, l):
        logits_into(1, 2 * j + 1)
        l = softmax_pv(0, 2 * j, l)
        logits_into(0, 2 * j + 2)
        return softmax_pv(1, 2 * j + 1, l)

    def attn_online(j, carry):
        m, l = carry
        off = pl.multiple_of(j * ts, ts)
        lg = _dot(k_ref[0, pl.ds(off, ts), :], q_rhs)
        bias = s_scr[pl.ds(off, ts), :]
        m_rows, l_rows, alphas, ps = [], [], [], []
        for h in range(N_HEADS):
            x = lg[:, h * qb:(h + 1) * qb] + bias
            m_new = jnp.maximum(m[h:h + 1, :], jnp.max(_fold_max(x), axis=0, keepdims=True))
            m_use = jnp.where(m_new > NEG_INF, m_new, 0.0)
            alpha = jnp.exp2(m[h:h + 1, :] - m_use)
            p = jnp.exp2(x - m_use)
            l_rows.append(alpha * l[h:h + 1, :] + jnp.sum(_fold_rows(p), axis=0, keepdims=True))
            m_rows.append(m_new)
            alphas.append(alpha)
            ps.append(p.astype(BF16))
        pv_accumulate(vt_ref[j], ps, alphas)
        return jnp.concatenate(m_rows, axis=0), jnp.concatenate(l_rows, axis=0)

    acc_scr[...] = jnp.zeros_like(acc_scr)
    logits_into(0, 0)
    l_fast = lax.fori_loop(0, n_full, attn_bounded, jnp.zeros((N_HEADS, qb), F32))
    l_fast = lax.fori_loop(0, n_half, lambda j, l: softmax_pv(0, 2 * n_full, l), l_fast)
    l_scr[...] = l_fast

    @pl.when(jnp.min(l_fast) < SOFTMAX_SUM_FLOOR)
    def _():
        acc_scr[...] = jnp.zeros_like(acc_scr)
        _, l_exact = lax.fori_loop(0, n_sub, attn_online,
                                   (jnp.full((N_HEADS, qb), NEG_INF, F32), jnp.zeros((N_HEADS, qb), F32)))
        l_scr[...] = l_exact

    l = l_scr[...]
    out_t = jnp.concatenate([acc_scr[h * HEAD_DIM:(h + 1) * HEAD_DIM, :] / l[h:h + 1, :] for h in range(N_HEADS)],
                            axis=0)
    o_ref[0] = out_t.T.astype(BF16)


def _fold_max(x):
    n = x.shape[0] // SUBLANES
    ways = FOLD_WAYS if n % FOLD_WAYS == 0 else 1
    part = jnp.max(x.reshape(n // ways, ways, SUBLANES, x.shape[1]), axis=0)
    return jnp.max(part, axis=0)


def _dsa_prompt(q, qi, wi, kb, vt, kib):
    b, t, _ = kb.shape
    qb = DSA_QUERIES
    assert t % KEY_TILE == 0 and t % qb == 0 and qb % CHUNK == 0
    topk = min(TOPK_MAX, t // 4)
    blk = lambda w: pl.BlockSpec((1, qb, w), lambda bi, i: (bi, i, 0))
    qblk = lambda rows: pl.BlockSpec((1, rows, qb), lambda bi, i: (bi * (t // qb) + i, 0, 0))
    whole = lambda w: pl.BlockSpec((1, t, w), lambda bi, i: (bi, 0, 0))
    return pl.pallas_call(
        functools.partial(_dsa_prompt_kernel, topk=topk),
        grid=(b, t // qb),
        in_specs=[qblk(ATT_WIDTH), qblk(IDX_WIDTH), qblk(IDX_HEADS), whole(KV_WIDTH),
                  pl.BlockSpec((t // KEY_SUB, KV_WIDTH, KEY_SUB), lambda bi, i: (bi, 0, 0)), whole(IDX_DIM)],
        out_specs=blk(ATT_WIDTH),
        out_shape=jax.ShapeDtypeStruct((b, t, ATT_WIDTH), BF16),
        scratch_shapes=[pltpu.VMEM((t, qb), F32), pltpu.VMEM((ATT_WIDTH, qb), F32),
                        pltpu.VMEM((N_HEADS, qb), F32), pltpu.VMEM((1, LANES), F32),
                        pltpu.VMEM((2, KEY_SUB, N_HEADS * qb), F32)],
        compiler_params=pltpu.CompilerParams(dimension_semantics=("arbitrary", "arbitrary"),
                                             vmem_limit_bytes=VMEM_LIMIT),
        name="dsa_prompt",
    )(q, qi, wi, kb, vt, kib)


def _dsa_sample_kernel(q_ref, qi_ref, wi_ref, kn_ref, vn_ref, kin_ref, ckt_ref, cvt_ref, ckit_ref, o_ref,
                       kcat, vcat_t, kicat, s_scr, *, topk):
    tq = q_ref.shape[1]
    past = ckt_ref.shape[1]
    total = kcat.shape[0]
    pad = total - past - tq
    kcat[0:past, :] = ckt_ref[...].T.astype(BF16)
    kcat[past:past + tq, :] = kn_ref[0]
    kcat[past + tq:total, :] = jnp.zeros((pad, KV_WIDTH), BF16)
    kit_wide = jnp.concatenate([ckit_ref[...], jnp.zeros((LANES - IDX_DIM, past), F32)], axis=0)
    kicat[0:past, :] = kit_wide.T[:, 0:IDX_DIM].astype(BF16)
    kicat[past:past + tq, :] = kin_ref[0]
    kicat[past + tq:total, :] = jnp.zeros((pad, IDX_DIM), BF16)
    vcat_t[:, 0:past] = cvt_ref[...].astype(BF16)
    v_new = jnp.concatenate([vn_ref[0].astype(F32), jnp.zeros((pad, KV_WIDTH), F32)], axis=0)
    vcat_t[:, past:total] = v_new.T.astype(BF16)

    lane = lax.broadcasted_iota(jnp.int32, (1, LANES), 1)
    qi = qi_ref[0]
    qi_stack = jnp.concatenate([qi[:, h * IDX_DIM:(h + 1) * IDX_DIM] for h in range(IDX_HEADS)], axis=0)
    wi_rows = jnp.concatenate([wi_ref[0], jnp.zeros((LANES - tq, LANES), F32)], axis=0).T
    w_flat = jnp.zeros((1, LANES), F32)
    for h in range(IDX_HEADS):
        row = wi_rows[h:h + 1, :]
        shifted = row if h == 0 else pltpu.roll(row, h * tq, 1)
        w_flat = jnp.where(lane // tq == h, shifted, w_flat)

    sc = w_flat * jnp.maximum(_dot_t(kicat[...], qi_stack), 0.0)
    for shift in (LANES // 2, LANES // 4, LANES // 8):
        sc = sc + pltpu.roll(sc, shift, 1)
    sc = jnp.where(lane < tq, sc, 0.0)
    for shift in (LANES // 8, LANES // 4, LANES // 2):
        sc = sc + pltpu.roll(sc, shift, 1)
    key = lax.broadcasted_iota(jnp.int32, (total, LANES), 0)
    ok = key < past + tq
    s_scr[...] = jnp.where(ok, sc, NEG_INF)
    smax = jnp.max(jnp.where(ok, sc, NEG_INF), axis=0, keepdims=True)
    smin = jnp.min(jnp.where(ok, sc, -NEG_INF), axis=0, keepdims=True)
    _select_mask(s_scr, 1, 0, total, smin, smax, jnp.full((1, LANES), float(past + tq), F32), topk)

    q = q_ref[0]
    zero = jnp.zeros((tq, HEAD_DIM), BF16)
    q_stack = jnp.concatenate(
        [jnp.concatenate([q[:, h * HEAD_DIM:(h + 1) * HEAD_DIM], zero] if h < GQA_GROUP
                         else [zero, q[:, h * HEAD_DIM:(h + 1) * HEAD_DIM]], axis=1) for h in range(N_HEADS)], axis=0)
    lg = _dot_t(kcat[...], q_stack) + s_scr[...]
    m = jnp.max(lg, axis=0, keepdims=True)
    p = jnp.exp2(lg - m)
    l = jnp.sum(p, axis=0, keepdims=True)
    o = _dot(vcat_t[...], p.astype(BF16))
    o = jnp.where(lane < GQA_GROUP * tq, o[0:HEAD_DIM, :], o[HEAD_DIM:2 * HEAD_DIM, :]) / l
    o_t = jnp.concatenate([o, jnp.zeros((LANES - HEAD_DIM, LANES), F32)], axis=0).T
    o_ref[0] = jnp.concatenate([o_t[h * tq:(h + 1) * tq, 0:HEAD_DIM] for h in range(N_HEADS)], axis=1).astype(BF16)


def _dsa_sample(q, qi, wi, kb, vb, kib, cache_k, cache_v, cache_kidx, layer):
    b, tq, _ = q.shape
    past = cache_k.shape[3]
    assert tq * N_HEADS == LANES and tq * IDX_HEADS == LANES and past % LANES == 0
    total = past + LANES
    topk = min(TOPK_MAX, (past + tq) // 4)
    blk = lambda w: pl.BlockSpec((1, tq, w), lambda bi: (bi, 0, 0))
    cache = lambda w: pl.BlockSpec((None, None, w, past), lambda bi: (layer, bi, 0, 0))
    return pl.pallas_call(
        functools.partial(_dsa_sample_kernel, topk=topk),
        grid=(b,),
        in_specs=[blk(ATT_WIDTH), blk(IDX_WIDTH), blk(LANES), blk(KV_WIDTH), blk(KV_WIDTH), blk(IDX_DIM),
                  cache(KV_WIDTH), cache(KV_WIDTH), cache(IDX_DIM)],
        out_specs=blk(ATT_WIDTH),
        out_shape=jax.ShapeDtypeStruct((b, tq, ATT_WIDTH), BF16),
        scratch_shapes=[pltpu.VMEM((total, KV_WIDTH), BF16), pltpu.VMEM((KV_WIDTH, total), BF16),
                        pltpu.VMEM((total, IDX_DIM), BF16), pltpu.VMEM((total, LANES), F32)],
        compiler_params=pltpu.CompilerParams(dimension_semantics=("arbitrary",), vmem_limit_bytes=VMEM_LIMIT),
        name="dsa_sample",
    )(q, qi, wi, kb, vb, kib, cache_k, cache_v, cache_kidx)


def _first_lane_where(mask, lane):
    return jnp.min(jnp.where(mask, lane, LANES), axis=1, keepdims=True)


def _post_kernel(att_ref, sa_ref, mb_ref, x_ref, wpa_ref, wout_ref, g2_ref, wr_ref, br_ref,
                 h_ref, xn2_ref, gate_ref, *sort_refs):
    pa = _dot(att_ref[...], wpa_ref[...])
    merged = sa_ref[...].astype(F32) * pa + mb_ref[...].astype(F32)
    h = x_ref[...] + _dot(merged.astype(BF16), wout_ref[...])
    h_ref[...] = h
    xn2 = _rms(h, g2_ref[...])
    xh = xn2.astype(BF16)
    xn2_ref[...] = xh

    xl = (xn2 - xh.astype(F32)).astype(BF16)
    wr = wr_ref[...]
    wh = wr.astype(BF16)
    wl = (wr - wh.astype(F32)).astype(BF16)
    lg = _dot(xh, wh) + (_dot(xl, wh) + _dot(xh, wl)) + br_ref[...]

    lane = lax.broadcasted_iota(jnp.int32, lg.shape, 1)
    is_g = lane < N_GROUPS
    mg = jnp.max(jnp.where(is_g, lg, NEG_INF), axis=1, keepdims=True)
    eg = jnp.where(is_g, jnp.exp(lg - mg), 0.0)
    pg = eg / jnp.sum(eg, axis=1, keepdims=True)
    g_w = jnp.max(pg, axis=1, keepdims=True)
    g_idx = _first_lane_where(jnp.logical_and(is_g, pg == g_w), lane)

    e_lo = ROUTER_E0 + EXP_PER_GROUP * g_idx
    is_e = jnp.logical_and(lane >= e_lo, lane < e_lo + EXP_PER_GROUP)
    me = jnp.max(jnp.where(is_e, lg, NEG_INF), axis=1, keepdims=True)
    ee = jnp.where(is_e, jnp.exp(lg - me), 0.0)
    pe = ee / jnp.sum(ee, axis=1, keepdims=True)
    e1 = jnp.max(pe, axis=1, keepdims=True)
    i1 = _first_lane_where(jnp.logical_and(is_e, pe == e1), lane)
    rest = jnp.logical_and(is_e, lane != i1)
    e2 = jnp.max(jnp.where(rest, pe, -1.0), axis=1, keepdims=True)
    i2 = _first_lane_where(jnp.logical_and(rest, pe == e2), lane)
    den = e1 + e2
    gate = jnp.where(lane == i1, e1 / den, jnp.where(lane == i2, e2 / den, 0.0)) * g_w
    group_gates = [jnp.where(lane < EXP_PER_GROUP, pltpu.roll(gate, LANES - (ROUTER_E0 + EXP_PER_GROUP * g), 1), 0.0)
                   for g in range(N_GROUPS)]
    if not sort_refs:
        for g in range(N_GROUPS):
            gate_ref[:, g * LANES:(g + 1) * LANES] = group_gates[g]
        return

    pt_ref, meta_ref = sort_refs
    gate_ref[...] = (group_gates[0] + group_gates[1]) + (group_gates[2] + group_gates[3])
    tm = lg.shape[0]
    onehot = jnp.where(lane == g_idx, 1.0, 0.0)
    ri = lax.broadcasted_iota(jnp.int32, (tm, tm), 0)
    cj = lax.broadcasted_iota(jnp.int32, (tm, tm), 1)
    earlier = jnp.where(cj < ri, 1.0, 0.0).astype(BF16)
    rank = jnp.sum(_dot(earlier, onehot.astype(BF16)) * onehot, axis=1, keepdims=True)
    cnt = jnp.sum(onehot, axis=0, keepdims=True)
    cnt_pad = jnp.floor((cnt + (SORT_PAD - 1.0)) * (1.0 / SORT_PAD)) * SORT_PAD
    seg_off = pltpu.roll(cnt_pad, 1, 1) + pltpu.roll(cnt_pad, 2, 1) + pltpu.roll(cnt_pad, 3, 1)
    dest = (rank + jnp.sum(onehot * seg_off, axis=1, keepdims=True)).astype(jnp.int32)
    col = lax.broadcasted_iota(jnp.int32, pt_ref.shape, 1)
    pt_ref[...] = jnp.where(col == dest, 1.0, 0.0).astype(BF16)
    meta_ref[0] = jnp.concatenate([cnt_pad, seg_off, jnp.zeros((SUBLANES - 2, LANES), F32)], axis=0)


def _sort_rows(tm):
    return -(-(tm + N_GROUPS * SORT_PAD) // LANES) * LANES


def _post(att, sa, mb, x, w, *, tm, sort):
    n, d = x.shape
    row = lambda t: (t, 0)
    const = lambda t: (0, 0)
    gate_w = LANES if sort else N_GROUPS * LANES
    out_specs = [pl.BlockSpec((tm, d), row), pl.BlockSpec((tm, d), row), pl.BlockSpec((tm, gate_w), row)]
    out_shape = [jax.ShapeDtypeStruct((n, d), F32), jax.ShapeDtypeStruct((n, d), BF16),
                 jax.ShapeDtypeStruct((n, gate_w), F32)]
    if sort:
        out_specs += [pl.BlockSpec((tm, _sort_rows(tm)), row), pl.BlockSpec((1, SUBLANES, LANES), lambda t: (t, 0, 0))]
        out_shape += [jax.ShapeDtypeStruct((n, _sort_rows(tm)), BF16),
                      jax.ShapeDtypeStruct((n // tm, SUBLANES, LANES), F32)]
    return pl.pallas_call(
        _post_kernel,
        grid=(n // tm,),
        in_specs=[pl.BlockSpec((tm, ATT_WIDTH), row), pl.BlockSpec((tm, d), row), pl.BlockSpec((tm, d), row),
                  pl.BlockSpec((tm, d), row), pl.BlockSpec(w["wpa"].shape, const), pl.BlockSpec(w["wout"].shape, const),
                  pl.BlockSpec((1, d), const), pl.BlockSpec(w["wr"].shape, const), pl.BlockSpec((1, ROUTER_LANES), const)],
        out_specs=out_specs,
        out_shape=out_shape,
        compiler_params=pltpu.CompilerParams(dimension_semantics=("arbitrary",), vmem_limit_bytes=VMEM_LIMIT),
        name="post_attn",
    )(att, sa, mb, x, w["wpa"], w["wout"], w["g2"], w["wr"], w["br"])


def _moe_kernel(xn2_ref, gate_ref, h_ref, wg_ref, wu_ref, wd_ref, fn_ref, o_ref, acc_ref, *, final):
    g = pl.program_id(1)

    @pl.when(g == 0)
    def _():
        acc_ref[...] = jnp.zeros_like(acc_ref)

    x = xn2_ref[...]
    gate = gate_ref[...]
    act = jax.nn.silu(_dot(x, wg_ref[0])) * _dot(x, wu_ref[0])
    parts = []
    for e in range(EXP_PER_GROUP):
        parts.append((act[:, e * EXP_HIDDEN:(e + 1) * EXP_HIDDEN] * gate[:, e:e + 1]).astype(BF16))
    acc_ref[...] += _dot(jnp.concatenate(parts, axis=1), wd_ref[0])

    @pl.when(g == N_GROUPS - 1)
    def _():
        y = h_ref[...] + acc_ref[...]
        o_ref[...] = _rms(y, fn_ref[...]) if final else y


def _moe(xn2, gate, h, w, fn, *, tm, final):
    n, d = h.shape
    row = lambda t, g: (t, 0)
    wspec = lambda shp: pl.BlockSpec((1,) + shp[1:], lambda t, g: (g, 0, 0))
    return pl.pallas_call(
        functools.partial(_moe_kernel, final=final),
        grid=(n // tm, N_GROUPS),
        in_specs=[pl.BlockSpec((tm, d), row), pl.BlockSpec((tm, LANES), lambda t, g: (t, g)), pl.BlockSpec((tm, d), row),
                  wspec(w["weg"].shape), wspec(w["weu"].shape), wspec(w["wed"].shape),
                  pl.BlockSpec((1, d), lambda t, g: (0, 0))],
        out_specs=pl.BlockSpec((tm, d), row),
        out_shape=jax.ShapeDtypeStruct((n, d), F32),
        scratch_shapes=[pltpu.VMEM((tm, d), F32)],
        compiler_params=pltpu.CompilerParams(dimension_semantics=("arbitrary", "arbitrary"),
                                             vmem_limit_bytes=VMEM_LIMIT),
        name="moe",
    )(xn2, gate, h, w["weg"], w["weu"], w["wed"], fn)


def _moe_sorted_kernel(meta_ref, xn2_ref, gate_ref, h_ref, pt_ref, wg_ref, wu_ref, wd_ref, fn_ref, o_ref,
                       xs_scr, gs_scr, ys_scr, *, final):
    t = pl.program_id(0)
    rows = pt_ref.shape[1]
    pt = pt_ref[...]
    perm = pt.astype(F32).T.astype(BF16)
    xs_scr[0:rows, :] = _dot(perm, xn2_ref[...]).astype(BF16)
    xs_scr[rows:rows + MOE_CHUNK, :] = jnp.zeros((MOE_CHUNK, xs_scr.shape[1]), BF16)
    gate = gate_ref[...]
    g_hi = gate.astype(BF16)
    g_rest = gate - g_hi.astype(F32)
    g_mid = g_rest.astype(BF16)
    g_lo = (g_rest - g_mid.astype(F32)).astype(BF16)
    gs_scr[0:rows, :] = _dot(perm, g_hi) + (_dot(perm, g_mid) + _dot(perm, g_lo))
    gs_scr[rows:rows + MOE_CHUNK, :] = jnp.zeros((MOE_CHUNK, LANES), F32)
    ys_scr[...] = jnp.zeros_like(ys_scr)

    for g in range(N_GROUPS):
        cnt = meta_ref[t * 2 * N_GROUPS + g]
        seg = meta_ref[t * 2 * N_GROUPS + N_GROUPS + g]

        def chunk(c, carry, g=g, cnt=cnt, seg=seg):
            r0 = pl.multiple_of(seg + c * MOE_CHUNK, SORT_PAD)
            x = xs_scr[pl.ds(r0, MOE_CHUNK), :]
            gt = gs_scr[pl.ds(r0, MOE_CHUNK), :]
            act = jax.nn.silu(_dot(x, wg_ref[g])) * _dot(x, wu_ref[g])
            parts = [(act[:, e * EXP_HIDDEN:(e + 1) * EXP_HIDDEN] * gt[:, e:e + 1]).astype(BF16)
                     for e in range(EXP_PER_GROUP)]
            y = _dot(jnp.concatenate(parts, axis=1), wd_ref[g])
            in_seg = lax.broadcasted_iota(jnp.int32, (MOE_CHUNK, 1), 0) + c * MOE_CHUNK < cnt
            ys_scr[pl.ds(r0, MOE_CHUNK), :] = jnp.where(in_seg, y, ys_scr[pl.ds(r0, MOE_CHUNK), :])
            return carry

        lax.fori_loop(0, (cnt + MOE_CHUNK - 1) // MOE_CHUNK, chunk, 0)

    y = h_ref[...] + _dot(pt, ys_scr[0:rows, :].astype(BF16))
    o_ref[...] = _rms(y, fn_ref[...]) if final else y


def _moe_sorted(xn2, gate, h, pt, meta, w, fn, *, tm, final):
    n, d = h.shape
    rows = pt.shape[1]
    row = lambda t, m: (t, 0)
    resident = lambda shp: pl.BlockSpec(shp, lambda t, m: (0, 0, 0), pipeline_mode=pl.Buffered(1))
    grid_spec = pltpu.PrefetchScalarGridSpec(
        num_scalar_prefetch=1,
        grid=(n // tm,),
        in_specs=[pl.BlockSpec((tm, d), row), pl.BlockSpec((tm, LANES), row), pl.BlockSpec((tm, d), row),
                  pl.BlockSpec((tm, rows), row),
                  resident(w["weg"].shape), resident(w["weu"].shape), resident(w["wed"].shape),
                  pl.BlockSpec((1, d), lambda t, m: (0, 0))],
        out_specs=pl.BlockSpec((tm, d), row),
        scratch_shapes=[pltpu.VMEM((rows + MOE_CHUNK, d), BF16), pltpu.VMEM((rows + MOE_CHUNK, LANES), F32),
                        pltpu.VMEM((rows + MOE_CHUNK, d), F32)],
    )
    return pl.pallas_call(
        functools.partial(_moe_sorted_kernel, final=final),
        grid_spec=grid_spec,
        out_shape=jax.ShapeDtypeStruct((n, d), F32),
        compiler_params=pltpu.CompilerParams(dimension_semantics=("arbitrary",), vmem_limit_bytes=VMEM_LIMIT),
        name="moe_sorted",
    )(meta, xn2, gate, h, pt, w["weg"], w["weu"], w["wed"], fn)


def _pack_experts_kernel(w_ref, o_ref):
    for e in range(EXP_PER_GROUP):
        o_ref[:, e * EXP_HIDDEN:(e + 1) * EXP_HIDDEN] = w_ref[e].astype(BF16)


def _pack_experts(w):
    depth, n_groups, n_exp, d, f = w.shape
    rows = 2 * MXU_DIM
    return pl.pallas_call(
        _pack_experts_kernel,
        grid=(depth, n_groups, d // rows),
        in_specs=[pl.BlockSpec((None, None, n_exp, rows, f), lambda l, g, r: (l, g, 0, r, 0))],
        out_specs=pl.BlockSpec((None, None, rows, n_exp * f), lambda l, g, r: (l, g, r, 0)),
        out_shape=jax.ShapeDtypeStruct((depth, n_groups, d, n_exp * f), BF16),
        compiler_params=pltpu.CompilerParams(dimension_semantics=("arbitrary",) * 3, vmem_limit_bytes=VMEM_LIMIT),
        name="pack_experts",
    )(w)


def _rope_tables(pos, d):
    inv = ROPE_THETA ** (-jnp.arange(0, d, 2, dtype=F32) / d)
    ang = pos[:, None] * inv[None, :]
    cos, sin = jnp.cos(ang), jnp.sin(ang)
    reps = LANES // d
    return (jnp.tile(jnp.concatenate([cos, cos], axis=-1), (1, reps)),
            jnp.tile(jnp.concatenate([-sin, sin], axis=-1), (1, reps)))


def _layer_weights(l, norm1, w_in, sgu_ln_g, sgu_ln_b, sgu_w, sgu_b, w_pa, w_pb, w_out, norm2,
                   w_rg, b_rg, w_re, b_re, w_eg, w_eu, w_ed, sgu_chunk):
    d = w_in.shape[1]
    a_end = ATT_WIDTH + 2 * KV_WIDTH + IDX_WIDTH
    b_end = a_end + IDX_DIM + IDX_HEADS
    reps = SGU_CHUNK // sgu_chunk
    n_router = N_GROUPS + N_GROUPS * EXP_PER_GROUP
    return dict(
        g1=norm1[l][None, :],
        wa=w_in[l][:, :a_end].astype(BF16),
        wb=jnp.pad(w_in[l][:, a_end:b_end], ((0, 0), (0, LANES - (b_end - a_end)))).astype(BF16),
        wc=w_in[l][:, b_end:].astype(BF16),
        sguw=jnp.tile(sgu_w[l][:, :sgu_chunk, :sgu_chunk], (1, reps, reps)),
        sgub=jnp.repeat(jnp.tile(sgu_b[l][:, :sgu_chunk], (1, reps)).T, SGU_GDIM, axis=1),
        lng=sgu_ln_g[l][None, :], lnb=sgu_ln_b[l][None, :],
        wpb=w_pb[l].astype(BF16), wpa=w_pa[l].astype(BF16), wout=w_out[l].astype(BF16),
        g2=norm2[l][None, :],
        wr=jnp.pad(jnp.concatenate([w_rg[l], w_re[l]], axis=1), ((0, 0), (0, ROUTER_LANES - n_router))),
        br=jnp.pad(jnp.concatenate([b_rg[l], b_re[l]]), (0, ROUTER_LANES - n_router))[None, :],
        weg=w_eg[l], weu=w_eu[l],
        wed=w_ed[l].reshape(N_GROUPS, GROUP_HIDDEN, d).astype(BF16),
    )


def _token_tile(n):
    return 512 if n % 512 == 0 else n


def kernel(x_prompt, x_sample, cache_k, cache_v, cache_kidx, norm1, w_in, sgu_ln_g, sgu_ln_b, sgu_w, sgu_b,
           w_pa, w_pb, w_out, norm2, w_rg, b_rg, w_re, b_re, w_eg, w_eu, w_ed, final_norm):
    bp, tp, d = x_prompt.shape
    bs, ts, _ = x_sample.shape
    depth, _, past = cache_k.shape[:3]
    np_, ns = bp * tp, bs * ts
    tm_p, tm_s = _token_tile(tp), _token_tile(ns)
    assert tp % tm_p == 0 and tp % Q_BLOCK == 0 and SGU_CHUNK % ts == 0 and tm_s % SGU_CHUNK == 0
    assert tm_p % KEY_SUB == 0 and tm_s % KEY_SUB == 0

    pos_p = jnp.arange(tp, dtype=F32)
    pos_s = jnp.tile(past + jnp.arange(ts, dtype=F32), tm_s // ts)
    tabs_p = _rope_tables(pos_p, HEAD_DIM) + _rope_tables(pos_p, IDX_DIM)
    tabs_s = _rope_tables(pos_s, HEAD_DIM) + _rope_tables(pos_s, IDX_DIM)
    ckt = jnp.transpose(cache_k, (0, 1, 3, 4, 2)).reshape(depth, bs, KV_WIDTH, past)
    cvt = jnp.transpose(cache_v, (0, 1, 3, 4, 2)).reshape(depth, bs, KV_WIDTH, past)
    ckit = jnp.transpose(cache_kidx, (0, 1, 3, 2))
    fn = final_norm[None, :]
    weg_all, weu_all = _pack_experts(w_eg), _pack_experts(w_eu)

    hp = x_prompt.reshape(np_, d)
    hs = x_sample.reshape(ns, d)
    outs = [[] for _ in range(4)]
    kv_bufs = None
    for l in range(depth):
        wargs = (norm1, w_in, sgu_ln_g, sgu_ln_b, sgu_w, sgu_b, w_pa, w_pb, w_out, norm2,
                 w_rg, b_rg, w_re, b_re, weg_all, weu_all, w_ed)
        wp = _layer_weights(l, *wargs, sgu_chunk=SGU_CHUNK)
        ws = dict(wp)
        ws.update({k: v for k, v in _layer_weights(l, *wargs, sgu_chunk=ts).items() if k in ("sguw", "sgub")})
        final = l == depth - 1

        q, kt_all, kb, vt_all, _, vt, qi, kit_all, kib, wi, mb, sa = _inproj(
            hp, wp, tabs_p, tm=2 * tm_p, sgu_chunk=SGU_CHUNK, want_vln=False,
            kv_layer=l, kv_bufs=kv_bufs, kv_shape=(depth, bp, tp))
        kv_bufs = (kt_all, vt_all, kit_all)
        r3 = lambda a: a.reshape(bp, tp, a.shape[-1])
        att = _dsa_prompt(q, qi, wi, r3(kb), vt, r3(kib)).reshape(np_, ATT_WIDTH)
        h, xn2, gate, pt, meta = _post(att, sa, mb, hp, wp, tm=tm_p, sort=True)
        meta = meta[:, 0:2, 0:N_GROUPS].astype(jnp.int32).reshape(-1)
        hp = _moe_sorted(xn2, gate, h, pt, meta, wp, fn, tm=tm_p, final=final)

        q, kf, kb, vf, vb, _, qi, kif, kib, wi, mb, sa, vln = _inproj(hs, ws, tabs_s, tm=tm_s, sgu_chunk=ts,
                                                                      want_vln=True)
        r3 = lambda a: a.reshape(bs, ts, a.shape[-1])
        att = _dsa_sample(r3(q), r3(qi), r3(wi), r3(kb), r3(vb), r3(kib), ckt, cvt, ckit, l)
        h, xn2, gate = _post(att.reshape(ns, ATT_WIDTH), sa, mb, hs, ws, tm=tm_s, sort=False)
        hs = _moe(xn2, gate, h, ws, fn, tm=tm_s, final=final)
        outs[0].append(kf.reshape(bs, ts, N_KV_HEADS, HEAD_DIM))
        outs[1].append(vf.reshape(bs, ts, N_KV_HEADS, HEAD_DIM))
        outs[2].append(kif.reshape(bs, ts, IDX_DIM))
        outs[3].append(vln.reshape(bs, ts, SGU_WIDTH))

    kt_all, vt_all, kit_all = kv_bufs
    heads = lambda a: jnp.transpose(a.reshape(depth, bp, N_KV_HEADS, HEAD_DIM, tp), (0, 1, 4, 2, 3))
    return ((hp.reshape(bp, tp, d), hs.reshape(bs, ts, d), heads(kt_all), heads(vt_all),
             jnp.transpose(kit_all, (0, 1, 3, 2))) + tuple(jnp.stack(o) for o in outs))
```

```python
import functools

import jax
import jax.numpy as jnp
from jax import lax
from jax.experimental import pallas as pl
from jax.experimental.pallas import tpu as pltpu

F32 = jnp.float32
BF16 = jnp.bfloat16

CHUNK = 64
N_HEADS = 8
N_KV_HEADS = 2
HEAD_DIM = 64
GQA_GROUP = N_HEADS // N_KV_HEADS
ATT_WIDTH = N_HEADS * HEAD_DIM
KV_WIDTH = N_KV_HEADS * HEAD_DIM
IDX_HEADS = 8
IDX_DIM = 32
IDX_WIDTH = IDX_HEADS * IDX_DIM
TOPK_MAX = 256
Q_BLOCK = 128
DSA_QUERIES = 256
SGU_GROUPS = 4
SGU_CHUNK = 128
SGU_WIDTH = 512
SGU_GDIM = SGU_WIDTH // SGU_GROUPS
N_GROUPS = 4
EXP_PER_GROUP = 8
EXP_HIDDEN = 128
GROUP_HIDDEN = EXP_PER_GROUP * EXP_HIDDEN
ROPE_THETA = 10000.0
EPS = 1e-6
Q_SCALE = HEAD_DIM ** -0.5 * 1.4426950408889634

LANES = 128
SUBLANES = 8
MXU_DIM = 256
ROUTER_LANES = LANES
ROUTER_E0 = N_GROUPS
VMEM_LIMIT = 56 * 1024 * 1024
KEY_TILE = 2 * MXU_DIM
KEY_SUB = MXU_DIM
FOLD_WAYS = 4
COUNT_ROWS = 64
SEARCH_FIRST_STEPS = 17
SEARCH_SECOND_STEPS = 0
SEARCH_ROUND_STEPS = 1
SEARCH_MAX_ROUNDS = 34
SORT_PAD = 16
MOE_CHUNK = 144
KNORM_SLACK = 1.0 + 2.0 ** -6
SOFTMAX_SUM_FLOOR = 2.0 ** -80
MAX_SUBNORMAL_BITS = 0x007FFFFF
NEG_INF = float("-inf")


def _dot(a, b):
    return jnp.dot(a, b, preferred_element_type=F32)


def _dot_t(a, b):
    return lax.dot_general(a, b, (((1,), (1,)), ((), ())), preferred_element_type=F32)


def _rms(x, g):
    return x * lax.rsqrt(jnp.mean(x * x, axis=-1, keepdims=True) + EPS) * g


def _swap_halves(x, d):
    lane = lax.broadcasted_iota(jnp.int32, x.shape, 1)
    first = (lane % d) < (d // 2)
    return jnp.where(first, pltpu.roll(x, LANES - d // 2, 1), pltpu.roll(x, d // 2, 1))


def _rope(x, cos, sin_signed, d):
    return x * cos + _swap_halves(x, d) * sin_signed


def _inproj_kernel(*refs, sgu_chunk, kv_transposed, n_alias, want_vln):
    (x_ref, g1_ref, wa_ref, wb_ref, wc_ref, cq_ref, sq_ref, ci_ref, si_ref,
     sguw_ref, sgub_ref, lng_ref, lnb_ref, wpb_ref) = refs[:14]
    (q_ref, kf_ref, kb_ref, vf_ref, vb_ref, vt_ref, qi_ref, kif_ref, kib_ref, wi_ref,
     mb_ref, sa_ref) = refs[14 + n_alias:26 + n_alias]
    tm = x_ref.shape[0]
    xn = _rms(x_ref[...], g1_ref[...]).astype(BF16)
    cq, sq, ci, si = cq_ref[...], sq_ref[...], ci_ref[...], si_ref[...]

    a = _dot(xn, wa_ref[...])
    def emit_queries(ref, val):
        if kv_transposed:
            for c in range(tm // DSA_QUERIES):
                ref[c] = val[c * DSA_QUERIES:(c + 1) * DSA_QUERIES, :].T[0:ref.shape[1], :].astype(ref.dtype)
        else:
            ref[...] = val.astype(ref.dtype)

    emit_queries(q_ref, jnp.concatenate(
        [_rope(a[:, c * LANES:(c + 1) * LANES], cq, sq, HEAD_DIM) * Q_SCALE for c in range(ATT_WIDTH // LANES)], axis=1))
    k = _rope(a[:, ATT_WIDTH:ATT_WIDTH + KV_WIDTH], cq, sq, HEAD_DIM)
    kb_ref[...] = k.astype(BF16)
    v = a[:, ATT_WIDTH + KV_WIDTH:ATT_WIDTH + 2 * KV_WIDTH]
    vb_ref[...] = v.astype(BF16)
    if kv_transposed:
        kf_ref[...] = k.T
        vf_ref[...] = v.T
    else:
        kf_ref[...] = k
        vf_ref[...] = v
    for c in range(tm // KEY_SUB):
        vt_ref[c] = v[c * KEY_SUB:(c + 1) * KEY_SUB, :].T.astype(BF16)
    qi0 = ATT_WIDTH + 2 * KV_WIDTH
    emit_queries(qi_ref, jnp.concatenate(
        [_rope(a[:, qi0 + c * LANES:qi0 + (c + 1) * LANES], ci, si, IDX_DIM) for c in range(IDX_WIDTH // LANES)], axis=1))

    b = _dot(xn, wb_ref[...])
    ki_wide = _rope(b, ci, si, IDX_DIM)
    ki = ki_wide[:, :IDX_DIM]
    kif_ref[...] = ki_wide.T[0:IDX_DIM, :] if kv_transposed else ki
    kib_ref[...] = ki.astype(BF16)
    emit_queries(wi_ref, pltpu.roll(b, LANES - IDX_DIM, 1) * (IDX_HEADS ** -0.5))

    u = jax.nn.gelu(_dot(xn, wc_ref[:, 0:SGU_WIDTH]))
    vs = jax.nn.gelu(_dot(xn, wc_ref[:, SGU_WIDTH:2 * SGU_WIDTH]))
    mu = jnp.mean(vs, axis=-1, keepdims=True)
    var = jnp.mean(jnp.square(vs - mu), axis=-1, keepdims=True)
    vln = (vs - mu) * lax.rsqrt(var + EPS) * lng_ref[...] + lnb_ref[...]
    if want_vln:
        refs[26 + n_alias][...] = vln
    ri = lax.broadcasted_iota(jnp.int32, (SGU_CHUNK, SGU_CHUNK), 0)
    cj = lax.broadcasted_iota(jnp.int32, (SGU_CHUNK, SGU_CHUNK), 1)
    mix_mask = (cj <= ri) & ((ri // sgu_chunk) == (cj // sgu_chunk))
    vln_b = vln.astype(BF16)
    w_eff = [jnp.where(mix_mask, sguw_ref[g], 0.0).astype(BF16) for g in range(SGU_GROUPS)]
    sg_rows = []
    for r in range(tm // SGU_CHUNK):
        rows = slice(r * SGU_CHUNK, (r + 1) * SGU_CHUNK)
        mixed = [_dot(w_eff[g], vln_b[rows, g * SGU_GDIM:(g + 1) * SGU_GDIM]) for g in range(SGU_GROUPS)]
        sg_rows.append(u[rows, :] * (jnp.concatenate(mixed, axis=1) + sgub_ref[...]))
    sg = jnp.concatenate(sg_rows, axis=0).astype(BF16)
    d = x_ref.shape[1]
    sa_ref[...] = jax.nn.sigmoid(_dot(xn, wc_ref[:, 2 * SGU_WIDTH:2 * SGU_WIDTH + d])).astype(BF16)
    gb = jax.nn.sigmoid(_dot(xn, wc_ref[:, 2 * SGU_WIDTH + d:]))
    mb_ref[...] = (gb * _dot(sg, wpb_ref[...])).astype(BF16)


def _inproj(x, w, tabs, *, tm, sgu_chunk, want_vln, kv_layer=None, kv_bufs=None, kv_shape=None):
    n, d = x.shape
    n_tab_tiles = tabs[0].shape[0] // tm
    row = lambda t: (t, 0)
    const = lambda t: (0, 0)
    tab = lambda t: (t % n_tab_tiles, 0)
    kv_transposed = kv_layer is not None
    if kv_transposed:
        depth, streams, steps = kv_shape
        per_stream = steps // tm
        kv_sds = lambda wd: jax.ShapeDtypeStruct((depth, streams, wd, steps), F32)
        kv_spec = lambda wd: pl.BlockSpec((None, None, wd, tm),
                                          lambda t: (kv_layer, t // per_stream, 0, t % per_stream))
    else:
        kv_sds = lambda wd: jax.ShapeDtypeStruct((n, wd), F32)
        kv_spec = lambda wd: pl.BlockSpec((tm, wd), row)
    tiled = lambda wd, dt: (jax.ShapeDtypeStruct((n, wd), dt), pl.BlockSpec((tm, wd), row))
    if kv_transposed:
        qb = DSA_QUERIES
        query = lambda wd, rows, dt: (jax.ShapeDtypeStruct((n // qb, rows, qb), dt),
                                      pl.BlockSpec((tm // qb, rows, qb), lambda t: (t, 0, 0)))
    else:
        query = lambda wd, rows, dt: tiled(wd, dt)
    outs = [
        query(ATT_WIDTH, ATT_WIDTH, BF16),
        (kv_sds(KV_WIDTH), kv_spec(KV_WIDTH)),
        tiled(KV_WIDTH, BF16),
        (kv_sds(KV_WIDTH), kv_spec(KV_WIDTH)),
        tiled(KV_WIDTH, BF16),
        (jax.ShapeDtypeStruct((n // KEY_SUB, KV_WIDTH, KEY_SUB), BF16),
         pl.BlockSpec((tm // KEY_SUB, KV_WIDTH, KEY_SUB), lambda t: (t, 0, 0))),
        query(IDX_WIDTH, IDX_WIDTH, BF16),
        (kv_sds(IDX_DIM), kv_spec(IDX_DIM)),
        tiled(IDX_DIM, BF16),
        query(LANES, IDX_HEADS, F32),
        tiled(d, BF16),
        tiled(d, BF16),
    ]
    if want_vln:
        outs.append(tiled(SGU_WIDTH, F32))
    in_specs = [
        pl.BlockSpec((tm, d), row),
        pl.BlockSpec((1, d), const),
        pl.BlockSpec(w["wa"].shape, const, pipeline_mode=pl.Buffered(1)),
        pl.BlockSpec(w["wb"].shape, const, pipeline_mode=pl.Buffered(1)),
        pl.BlockSpec(w["wc"].shape, const, pipeline_mode=pl.Buffered(1)),
        pl.BlockSpec((tm, LANES), tab), pl.BlockSpec((tm, LANES), tab),
        pl.BlockSpec((tm, LANES), tab), pl.BlockSpec((tm, LANES), tab),
        pl.BlockSpec(w["sguw"].shape, lambda t: (0, 0, 0)),
        pl.BlockSpec(w["sgub"].shape, const),
        pl.BlockSpec((1, SGU_WIDTH), const),
        pl.BlockSpec((1, SGU_WIDTH), const),
        pl.BlockSpec(w["wpb"].shape, const),
    ]
    args = [x, w["g1"], w["wa"], w["wb"], w["wc"], *tabs, w["sguw"], w["sgub"], w["lng"], w["lnb"], w["wpb"]]
    aliases = {}
    if kv_bufs is not None:
        for buf, out_idx in zip(kv_bufs, (1, 3, 7)):
            aliases[len(args)] = out_idx
            in_specs.append(pl.BlockSpec(memory_space=pl.ANY))
            args.append(buf)
    return pl.pallas_call(
        functools.partial(_inproj_kernel, sgu_chunk=sgu_chunk, kv_transposed=kv_transposed,
                          n_alias=len(aliases), want_vln=want_vln),
        grid=(n // tm,),
        in_specs=in_specs,
        out_specs=[o[1] for o in outs],
        out_shape=[o[0] for o in outs],
        input_output_aliases=aliases,
        compiler_params=pltpu.CompilerParams(dimension_semantics=("arbitrary",), vmem_limit_bytes=VMEM_LIMIT),
        name="inproj",
    )(*args)


def _float_key(x):
    b = lax.bitcast_convert_type(x, jnp.int32)
    k = b ^ ((b >> 31) & jnp.int32(0x7FFFFFFF))
    return jnp.where(k > MAX_SUBNORMAL_BITS, k - MAX_SUBNORMAL_BITS,
                     jnp.where(k < -MAX_SUBNORMAL_BITS - 1, k + MAX_SUBNORMAL_BITS, jnp.clip(k, -1, 0)))


def _key_float(k):
    k = jnp.where(k > 0, k + MAX_SUBNORMAL_BITS, jnp.where(k < -1, k - MAX_SUBNORMAL_BITS, k))
    return lax.bitcast_convert_type(k ^ ((k >> 31) & jnp.int32(0x7FFFFFFF)), F32)


def _fold_rows(x):
    n = x.shape[0] // SUBLANES
    ways = FOLD_WAYS if n % FOLD_WAYS == 0 else 1
    part = jnp.sum(x.reshape(n // ways, ways, SUBLANES, x.shape[1]), axis=0)
    return jnp.sum(part, axis=0)


def _topk_threshold(count_ge, snap, smin, smax, n_adm, kk):
    lo0 = _float_key(smin + 0.0)
    hx0 = _float_key(smax + 0.0) + 1

    def is_active(st):
        lo, hx, c_lo, _ = st
        return jnp.logical_and(c_lo != kk, lo + 1 < hx)

    def step(st, by_key):
        lo, hx, c_lo, c_hx = st
        active = is_active(st)
        if by_key:
            mid = (lo >> 1) + (hx >> 1) + (lo & hx & 1)
        else:
            mid = _float_key(0.5 * _key_float(lo) + 0.5 * _key_float(hx))
        mid = jnp.where(active, jnp.minimum(jnp.maximum(mid, lo + 1), hx - 1), lo)
        c = count_ge(_key_float(mid))
        up = jnp.logical_and(active, c >= kk)
        down = jnp.logical_and(active, c < kk)
        return (jnp.where(up, mid, lo), jnp.where(down, mid, hx),
                jnp.where(up, c, c_lo), jnp.where(down, c, c_hx))

    def snap_step(st):
        lo, hx, c_lo, c_hx = st
        active = is_active(st)
        first_in, last_in = snap(_key_float(lo), _key_float(hx))
        lo_n = jnp.where(active, _float_key(first_in + 0.0), lo)
        hx_n = jnp.where(active, _float_key(last_in + 0.0) + 1, hx)
        few = jnp.logical_and(active, c_lo - c_hx <= 2.0)
        same = first_in == last_in
        take_last = kk - c_hx <= 1.0
        thr_key = _float_key(jnp.where(take_last, last_in, first_in) + 0.0)
        c_ge = jnp.where(jnp.logical_and(take_last, jnp.logical_not(same)), c_hx + 1.0, c_lo)
        c_gt = jnp.where(jnp.logical_or(take_last, same), c_hx, c_hx + 1.0)
        return (jnp.where(few, thr_key, lo_n), jnp.where(few, thr_key + 1, hx_n),
                jnp.where(few, c_ge, c_lo), jnp.where(few, c_gt, c_hx))

    def n_active(st):
        return jnp.max(jnp.where(is_active(st), 1.0, 0.0))

    st = (lo0, hx0, n_adm, jnp.zeros_like(n_adm))
    for _ in range(SEARCH_FIRST_STEPS):
        st = step(st, by_key=False)
    st = snap_step(st)
    for _ in range(SEARCH_SECOND_STEPS):
        st = step(st, by_key=False)
    act = n_active(st)

    def cond(carry):
        it, act, _ = carry
        return jnp.logical_and(it < SEARCH_MAX_ROUNDS, act > 0.5)

    def body(carry):
        it, _, st = carry
        st = step(snap_step(st), by_key=True)
        for _ in range(SEARCH_ROUND_STEPS):
            st = step(st, by_key=False)
        return it + 1, n_active(st), st

    _, _, (lo, _, c_lo, c_hx) = lax.while_loop(cond, body, (jnp.int32(0), act, st))
    return _key_float(lo), c_lo, c_hx


def _demote_excess_ties(s_ref, n_chunks, thr, need):
    ri = lax.broadcasted_iota(jnp.int32, (LANES, LANES), 0)
    cj = lax.broadcasted_iota(jnp.int32, (LANES, LANES), 1)
    lower = jnp.where(cj <= ri, 1.0, 0.0).astype(BF16)

    def body(j, run):
        off = pl.multiple_of(j * LANES, LANES)
        blk = s_ref[pl.ds(off, LANES), :]
        tied = blk == thr
        tied_f = jnp.where(tied, 1.0, 0.0)
        prefix = _dot(lower, tied_f.astype(BF16)) + run
        s_ref[pl.ds(off, LANES), :] = jnp.where(jnp.logical_and(tied, prefix > need), NEG_INF, blk)
        return run + jnp.sum(tied_f, axis=0, keepdims=True)

    lax.fori_loop(0, n_chunks, body, jnp.zeros((1, s_ref.shape[1]), F32))


def _select_mask(s_ref, n_full, n_half, tk, smin, smax, n_adm, topk, selected=0.0):
    kk = jnp.minimum(n_adm, float(topk))
    half = tk // 2

    def over_rows(visit, carry):
        carry = lax.fori_loop(0, n_full, lambda j, c: visit(pl.multiple_of(j * tk, tk), tk, c), carry)
        return lax.fori_loop(0, n_half, lambda j, c: visit(pl.multiple_of(n_full * tk, half), half, c), carry)

    def count_ge(t):
        def visit(off, rows, acc):
            for r in range(0, rows, COUNT_ROWS):
                acc = acc + _fold_rows(jnp.where(s_ref[pl.ds(off + r, COUNT_ROWS), :] >= t, 1.0, 0.0))
            return acc
        acc = over_rows(visit, jnp.zeros((SUBLANES, s_ref.shape[1]), F32))
        return jnp.sum(acc, axis=0, keepdims=True)

    def snap(lo_f, hx_f):
        def visit(off, rows, carry):
            first_in, last_in = carry
            for r in range(0, rows, COUNT_ROWS):
                blk = s_ref[pl.ds(off + r, COUNT_ROWS), :]
                first_in = jnp.minimum(first_in, -_fold_max(jnp.where(blk >= lo_f, -blk, NEG_INF)))
                last_in = jnp.maximum(last_in, _fold_max(jnp.where(blk < hx_f, blk, NEG_INF)))
            return first_in, last_in
        first_in, last_in = over_rows(visit, (jnp.full((SUBLANES, s_ref.shape[1]), -NEG_INF, F32),
                                              jnp.full((SUBLANES, s_ref.shape[1]), NEG_INF, F32)))
        return jnp.min(first_in, axis=0, keepdims=True), jnp.max(last_in, axis=0, keepdims=True)

    thr, c_ge, c_gt = _topk_threshold(count_ge, snap, smin, smax, n_adm, kk)

    @pl.when(jnp.max(c_ge - kk) > 0.5)
    def _():
        _demote_excess_ties(s_ref, (n_full * tk + n_half * half) // LANES, thr, kk - c_gt)

    def mask_rows(off, rows, carry):
        s_ref[pl.ds(off, rows), :] = jnp.where(s_ref[pl.ds(off, rows), :] >= thr, selected, NEG_INF)
        return carry

    over_rows(mask_rows, 0)


def _dsa_prompt_kernel(q_ref, qi_ref, wi_ref, k_ref, vt_ref, ki_ref, o_ref, s_scr, acc_scr, l_scr, kn_scr, lg_scr,
                       *, topk):
    i = pl.program_id(1)
    tk, ts = KEY_TILE, KEY_SUB
    qb = q_ref.shape[2]
    n_keys = (i + 1) * qb
    qcol = lax.broadcasted_iota(jnp.int32, (1, qb), 1)
    lim = ((i * qb + qcol) // CHUNK + 1) * CHUNK

    @pl.when(i == 0)
    def _():
        ri = lax.broadcasted_iota(jnp.int32, (KV_WIDTH, LANES), 0)
        cj = lax.broadcasted_iota(jnp.int32, (KV_WIDTH, LANES), 1)
        head_sum = jnp.where(ri // HEAD_DIM == cj, 1.0, 0.0).astype(BF16)

        def body(j, acc):
            kt = k_ref[0, pl.ds(pl.multiple_of(j * tk, tk), tk), :].astype(F32)
            return jnp.maximum(acc, _fold_max(_dot((kt * kt).astype(BF16), head_sum)))

        acc = lax.fori_loop(0, k_ref.shape[1] // tk, body, jnp.zeros((SUBLANES, LANES), F32))
        kn_scr[...] = jnp.max(acc, axis=0, keepdims=True) * KNORM_SLACK

    qt = q_ref[0]
    qit = qi_ref[0]
    wit = wi_ref[0]
    qi_rhs = jnp.concatenate([qit[h * IDX_DIM:(h + 1) * IDX_DIM, :] for h in range(IDX_HEADS)], axis=1)
    zero = jnp.zeros((HEAD_DIM, qb), BF16)
    q_cols = []
    for h in range(N_HEADS):
        qh = qt[h * HEAD_DIM:(h + 1) * HEAD_DIM, :]
        q_cols.append(jnp.concatenate([qh, zero] if h < GQA_GROUP else [zero, qh], axis=0))
    q_rhs = jnp.concatenate(q_cols, axis=1)

    def score_sub(sub, carry):
        smax, smin = carry
        off = pl.multiple_of(sub * ts, ts)
        r = _dot(ki_ref[0, pl.ds(off, ts), :], qi_rhs)
        acc = wit[0:1, :] * jnp.maximum(r[:, 0:qb], 0.0)
        for h in range(1, IDX_HEADS):
            acc = acc + wit[h:h + 1, :] * jnp.maximum(r[:, h * qb:(h + 1) * qb], 0.0)
        key = off + lax.broadcasted_iota(jnp.int32, (ts, qb), 0)
        ok = key < lim
        s_scr[pl.ds(off, ts), :] = jnp.where(ok, acc, NEG_INF)
        smax = jnp.maximum(smax, _fold_max(jnp.where(ok, acc, NEG_INF)))
        smin = jnp.minimum(smin, -_fold_max(jnp.where(ok, -acc, NEG_INF)))
        return smax, smin

    n_sub = n_keys // ts
    n_full, n_half = n_sub // 2, n_sub % 2
    carry = (jnp.full((SUBLANES, qb), NEG_INF, F32), jnp.full((SUBLANES, qb), -NEG_INF, F32))
    carry = lax.fori_loop(0, n_full, lambda j, c: score_sub(2 * j + 1, score_sub(2 * j, c)), carry)
    smax, smin = lax.fori_loop(0, n_half, lambda j, c: score_sub(2 * n_full, c), carry)
    smax = jnp.max(smax, axis=0, keepdims=True)
    smin = jnp.min(smin, axis=0, keepdims=True)
    qsq = jnp.square(qt.astype(F32))
    kn = kn_scr[...]
    lane1 = lax.broadcasted_iota(jnp.int32, (1, LANES), 1)
    shift = jnp.zeros((1, qb), F32)
    for h in range(N_HEADS):
        qn2 = jnp.sum(_fold_rows(qsq[h * HEAD_DIM:(h + 1) * HEAD_DIM, :]), axis=0, keepdims=True)
        kmax2 = jnp.max(jnp.where(lane1 == h // GQA_GROUP, kn, 0.0))
        shift = jnp.maximum(shift, jnp.sqrt(qn2 * kmax2))
    _select_mask(s_scr, n_full, n_half, tk, smin, smax, lim.astype(F32), topk, selected=-shift)

    def pv_accumulate(vt, ps, alphas):
        for pair in range(N_HEADS // 2):
            g = (2 * pair) // GQA_GROUP
            o = _dot(vt[g * HEAD_DIM:(g + 1) * HEAD_DIM, :], jnp.concatenate(ps[2 * pair:2 * pair + 2], axis=1))
            for u in range(2):
                h = 2 * pair + u
                rows = slice(h * HEAD_DIM, (h + 1) * HEAD_DIM)
                prev = acc_scr[rows, :] if alphas is None else alphas[h] * acc_scr[rows, :]
                acc_scr[rows, :] = prev + o[:, u * qb:(u + 1) * qb]

    def logits_into(buf, sub):
        off = pl.multiple_of(jnp.minimum(sub * ts, k_ref.shape[1] - ts), ts)
        lg_scr[buf] = _dot(k_ref[0, pl.ds(off, ts), :], q_rhs)

    ones_rows = jnp.ones((2 * SUBLANES, ts), BF16)

    def softmax_pv(buf, sub, l):
        bias = s_scr[pl.ds(pl.multiple_of(sub * ts, ts), ts), :]
        ps = [jnp.exp2(lg_scr[buf, :, h * qb:(h + 1) * qb] + bias).astype(BF16) for h in range(N_HEADS)]
        vt = vt_ref[sub]
        l_rows = []
        for pair in range(N_HEADS // 2):
            g = (2 * pair) // GQA_GROUP
            lhs = jnp.concatenate([vt[g * HEAD_DIM:(g + 1) * HEAD_DIM, :], ones_rows], axis=0)
            o = _dot(lhs, jnp.concatenate(ps[2 * pair:2 * pair + 2], axis=1))
            for u in range(2):
                h = 2 * pair + u
                rows = slice(h * HEAD_DIM, (h + 1) * HEAD_DIM)
                acc_scr[rows, :] = acc_scr[rows, :] + o[0:HEAD_DIM, u * qb:(u + 1) * qb]
                l_rows.append(l[h:h + 1, :] + o[HEAD_DIM:HEAD_DIM + 1, u * qb:(u + 1) * qb])
        return jnp.concatenate(l_rows, axis=0)

    def attn_bounded(j, l):
        logits_into(1, 2 * j + 1)
        l = softmax_pv(0, 2 * j, l)
        logits_into(0, 2 * j + 2)
        return softmax_pv(1, 2 * j + 1, l)

    def attn_online(j, carry):
        m, l = carry
        off = pl.multiple_of(j * ts, ts)
        lg = _dot(k_ref[0, pl.ds(off, ts), :], q_rhs)
        bias = s_scr[pl.ds(off, ts), :]
        m_rows, l_rows, alphas, ps = [], [], [], []
        for h in range(N_HEADS):
            x = lg[:, h * qb:(h + 1) * qb] + bias
            m_new = jnp.maximum(m[h:h + 1, :], jnp.max(_fold_max(x), axis=0, keepdims=True))
            m_use = jnp.where(m_new > NEG_INF, m_new, 0.0)
            alpha = jnp.exp2(m[h:h + 1, :] - m_use)
            p = jnp.exp2(x - m_use)
            l_rows.append(alpha * l[h:h + 1, :] + jnp.sum(_fold_rows(p), axis=0, keepdims=True))
            m_rows.append(m_new)
            alphas.append(alpha)
            ps.append(p.astype(BF16))
        pv_accumulate(vt_ref[j], ps, alphas)
        return jnp.concatenate(m_rows, axis=0), jnp.concatenate(l_rows, axis=0)

    acc_scr[...] = jnp.zeros_like(acc_scr)
    logits_into(0, 0)
    l_fast = lax.fori_loop(0, n_full, attn_bounded, jnp.zeros((N_HEADS, qb), F32))
    l_fast = lax.fori_loop(0, n_half, lambda j, l: softmax_pv(0, 2 * n_full, l), l_fast)
    l_scr[...] = l_fast

    @pl.when(jnp.min(l_fast) < SOFTMAX_SUM_FLOOR)
    def _():
        acc_scr[...] = jnp.zeros_like(acc_scr)
        _, l_exact = lax.fori_loop(0, n_sub, attn_online,
                                   (jnp.full((N_HEADS, qb), NEG_INF, F32), jnp.zeros((N_HEADS, qb), F32)))
        l_scr[...] = l_exact

    l = l_scr[...]
    out_t = jnp.concatenate([acc_scr[h * HEAD_DIM:(h + 1) * HEAD_DIM, :] / l[h:h + 1, :] for h in range(N_HEADS)],
                            axis=0)
    o_ref[0] = out_t.T.astype(BF16)


def _fold_max(x):
    n = x.shape[0] // SUBLANES
    ways = FOLD_WAYS if n % FOLD_WAYS == 0 else 1
    part = jnp.max(x.reshape(n // ways, ways, SUBLANES, x.shape[1]), axis=0)
    return jnp.max(part, axis=0)


def _dsa_prompt(q, qi, wi, kb, vt, kib):
    b, t, _ = kb.shape
    qb = DSA_QUERIES
    assert t % KEY_TILE == 0 and t % qb == 0 and qb % CHUNK == 0
    topk = min(TOPK_MAX, t // 4)
    blk = lambda w: pl.BlockSpec((1, qb, w), lambda bi, i: (bi, i, 0))
    qblk = lambda rows: pl.BlockSpec((1, rows, qb), lambda bi, i: (bi * (t // qb) + i, 0, 0))
    whole = lambda w: pl.BlockSpec((1, t, w), lambda bi, i: (bi, 0, 0))
    return pl.pallas_call(
        functools.partial(_dsa_prompt_kernel, topk=topk),
        grid=(b, t // qb),
        in_specs=[qblk(ATT_WIDTH), qblk(IDX_WIDTH), qblk(IDX_HEADS), whole(KV_WIDTH),
                  pl.BlockSpec((t // KEY_SUB, KV_WIDTH, KEY_SUB), lambda bi, i: (bi, 0, 0)), whole(IDX_DIM)],
        out_specs=blk(ATT_WIDTH),
        out_shape=jax.ShapeDtypeStruct((b, t, ATT_WIDTH), BF16),
        scratch_shapes=[pltpu.VMEM((t, qb), F32), pltpu.VMEM((ATT_WIDTH, qb), F32),
                        pltpu.VMEM((N_HEADS, qb), F32), pltpu.VMEM((1, LANES), F32),
                        pltpu.VMEM((2, KEY_SUB, N_HEADS * qb), F32)],
        compiler_params=pltpu.CompilerParams(dimension_semantics=("arbitrary", "arbitrary"),
                                             vmem_limit_bytes=VMEM_LIMIT),
        name="dsa_prompt",
    )(q, qi, wi, kb, vt, kib)


def _dsa_sample_kernel(q_ref, qi_ref, wi_ref, kn_ref, vn_ref, kin_ref, ckt_ref, cvt_ref, ckit_ref, o_ref,
                       kcat, vcat_t, kicat, s_scr, *, topk):
    tq = q_ref.shape[1]
    past = ckt_ref.shape[1]
    total = kcat.shape[0]
    pad = total - past - tq
    kcat[0:past, :] = ckt_ref[...].T.astype(BF16)
    kcat[past:past + tq, :] = kn_ref[0]
    kcat[past + tq:total, :] = jnp.zeros((pad, KV_WIDTH), BF16)
    kit_wide = jnp.concatenate([ckit_ref[...], jnp.zeros((LANES - IDX_DIM, past), F32)], axis=0)
    kicat[0:past, :] = kit_wide.T[:, 0:IDX_DIM].astype(BF16)
    kicat[past:past + tq, :] = kin_ref[0]
    kicat[past + tq:total, :] = jnp.zeros((pad, IDX_DIM), BF16)
    vcat_t[:, 0:past] = cvt_ref[...].astype(BF16)
    v_new = jnp.concatenate([vn_ref[0].astype(F32), jnp.zeros((pad, KV_WIDTH), F32)], axis=0)
    vcat_t[:, past:total] = v_new.T.astype(BF16)

    lane = lax.broadcasted_iota(jnp.int32, (1, LANES), 1)
    qi = qi_ref[0]
    qi_stack = jnp.concatenate([qi[:, h * IDX_DIM:(h + 1) * IDX_DIM] for h in range(IDX_HEADS)], axis=0)
    wi_rows = jnp.concatenate([wi_ref[0], jnp.zeros((LANES - tq, LANES), F32)], axis=0).T
    w_flat = jnp.zeros((1, LANES), F32)
    for h in range(IDX_HEADS):
        row = wi_rows[h:h + 1, :]
        shifted = row if h == 0 else pltpu.roll(row, h * tq, 1)
        w_flat = jnp.where(lane // tq == h, shifted, w_flat)

    sc = w_flat * jnp.maximum(_dot_t(kicat[...], qi_stack), 0.0)
    for shift in (LANES // 2, LANES // 4, LANES // 8):
        sc = sc + pltpu.roll(sc, shift, 1)
    sc = jnp.where(lane < tq, sc, 0.0)
    for shift in (LANES // 8, LANES // 4, LANES // 2):
        sc = sc + pltpu.roll(sc, shift, 1)
    key = lax.broadcasted_iota(jnp.int32, (total, LANES), 0)
    ok = key < past + tq
    s_scr[...] = jnp.where(ok, sc, NEG_INF)
    smax = jnp.max(jnp.where(ok, sc, NEG_INF), axis=0, keepdims=True)
    smin = jnp.min(jnp.where(ok, sc, -NEG_INF), axis=0, keepdims=True)
    _select_mask(s_scr, 1, 0, total, smin, smax, jnp.full((1, LANES), float(past + tq), F32), topk)

    q = q_ref[0]
    zero = jnp.zeros((tq, HEAD_DIM), BF16)
    q_stack = jnp.concatenate(
        [jnp.concatenate([q[:, h * HEAD_DIM:(h + 1) * HEAD_DIM], zero] if h < GQA_GROUP
                         else [zero, q[:, h * HEAD_DIM:(h + 1) * HEAD_DIM]], axis=1) for h in range(N_HEADS)], axis=0)
    lg = _dot_t(kcat[...], q_stack) + s_scr[...]
    m = jnp.max(lg, axis=0, keepdims=True)
    p = jnp.exp2(lg - m)
    l = jnp.sum(p, axis=0, keepdims=True)
    o = _dot(vcat_t[...], p.astype(BF16))
    o = jnp.where(lane < GQA_GROUP * tq, o[0:HEAD_DIM, :], o[HEAD_DIM:2 * HEAD_DIM, :]) / l
    o_t = jnp.concatenate([o, jnp.zeros((LANES - HEAD_DIM, LANES), F32)], axis=0).T
    o_ref[0] = jnp.concatenate([o_t[h * tq:(h + 1) * tq, 0:HEAD_DIM] for h in range(N_HEADS)], axis=1).astype(BF16)


def _dsa_sample(q, qi, wi, kb, vb, kib, cache_k, cache_v, cache_kidx, layer):
    b, tq, _ = q.shape
    past = cache_k.shape[3]
    assert tq * N_HEADS == LANES and tq * IDX_HEADS == LANES and past % LANES == 0
    total = past + LANES
    topk = min(TOPK_MAX, (past + tq) // 4)
    blk = lambda w: pl.BlockSpec((1, tq, w), lambda bi: (bi, 0, 0))
    cache = lambda w: pl.BlockSpec((None, None, w, past), lambda bi: (layer, bi, 0, 0))
    return pl.pallas_call(
        functools.partial(_dsa_sample_kernel, topk=topk),
        grid=(b,),
        in_specs=[blk(ATT_WIDTH), blk(IDX_WIDTH), blk(LANES), blk(KV_WIDTH), blk(KV_WIDTH), blk(IDX_DIM),
                  cache(KV_WIDTH), cache(KV_WIDTH), cache(IDX_DIM)],
        out_specs=blk(ATT_WIDTH),
        out_shape=jax.ShapeDtypeStruct((b, tq, ATT_WIDTH), BF16),
        scratch_shapes=[pltpu.VMEM((total, KV_WIDTH), BF16), pltpu.VMEM((KV_WIDTH, total), BF16),
                        pltpu.VMEM((total, IDX_DIM), BF16), pltpu.VMEM((total, LANES), F32)],
        compiler_params=pltpu.CompilerParams(dimension_semantics=("arbitrary",), vmem_limit_bytes=VMEM_LIMIT),
        name="dsa_sample",
    )(q, qi, wi, kb, vb, kib, cache_k, cache_v, cache_kidx)


def _first_lane_where(mask, lane):
    return jnp.min(jnp.where(mask, lane, LANES), axis=1, keepdims=True)


def _post_kernel(att_ref, sa_ref, mb_ref, x_ref, wpa_ref, wout_ref, g2_ref, wr_ref, br_ref,
                 h_ref, xn2_ref, gate_ref, *sort_refs):
    pa = _dot(att_ref[...], wpa_ref[...])
    merged = sa_ref[...].astype(F32) * pa + mb_ref[...].astype(F32)
    h = x_ref[...] + _dot(merged.astype(BF16), wout_ref[...])
    h_ref[...] = h
    xn2 = _rms(h, g2_ref[...])
    xh = xn2.astype(BF16)
    xn2_ref[...] = xh

    xl = (xn2 - xh.astype(F32)).astype(BF16)
    wr = wr_ref[...]
    wh = wr.astype(BF16)
    wl = (wr - wh.astype(F32)).astype(BF16)
    lg = _dot(xh, wh) + (_dot(xl, wh) + _dot(xh, wl)) + br_ref[...]

    lane = lax.broadcasted_iota(jnp.int32, lg.shape, 1)
    is_g = lane < N_GROUPS
    mg = jnp.max(jnp.where(is_g, lg, NEG_INF), axis=1, keepdims=True)
    eg = jnp.where(is_g, jnp.exp(lg - mg), 0.0)
    pg = eg / jnp.sum(eg, axis=1, keepdims=True)
    g_w = jnp.max(pg, axis=1, keepdims=True)
    g_idx = _first_lane_where(jnp.logical_and(is_g, pg == g_w), lane)

    e_lo = ROUTER_E0 + EXP_PER_GROUP * g_idx
    is_e = jnp.logical_and(lane >= e_lo, lane < e_lo + EXP_PER_GROUP)
    me = jnp.max(jnp.where(is_e, lg, NEG_INF), axis=1, keepdims=True)
    ee = jnp.where(is_e, jnp.exp(lg - me), 0.0)
    pe = ee / jnp.sum(ee, axis=1, keepdims=True)
    e1 = jnp.max(pe, axis=1, keepdims=True)
    i1 = _first_lane_where(jnp.logical_and(is_e, pe == e1), lane)
    rest = jnp.logical_and(is_e, lane != i1)
    e2 = jnp.max(jnp.where(rest, pe, -1.0), axis=1, keepdims=True)
    i2 = _first_lane_where(jnp.logical_and(rest, pe == e2), lane)
    den = e1 + e2
    gate = jnp.where(lane == i1, e1 / den, jnp.where(lane == i2, e2 / den, 0.0)) * g_w
    group_gates = [jnp.where(lane < EXP_PER_GROUP, pltpu.roll(gate, LANES - (ROUTER_E0 + EXP_PER_GROUP * g), 1), 0.0)
                   for g in range(N_GROUPS)]
    if not sort_refs:
        for g in range(N_GROUPS):
            gate_ref[:, g * LANES:(g + 1) * LANES] = group_gates[g]
        return

    pt_ref, meta_ref = sort_refs
    gate_ref[...] = (group_gates[0] + group_gates[1]) + (group_gates[2] + group_gates[3])
    tm = lg.shape[0]
    onehot = jnp.where(lane == g_idx, 1.0, 0.0)
    ri = lax.broadcasted_iota(jnp.int32, (tm, tm), 0)
    cj = lax.broadcasted_iota(jnp.int32, (tm, tm), 1)
    earlier = jnp.where(cj < ri, 1.0, 0.0).astype(BF16)
    rank = jnp.sum(_dot(earlier, onehot.astype(BF16)) * onehot, axis=1, keepdims=True)
    cnt = jnp.sum(onehot, axis=0, keepdims=True)
    cnt_pad = jnp.floor((cnt + (SORT_PAD - 1.0)) * (1.0 / SORT_PAD)) * SORT_PAD
    seg_off = pltpu.roll(cnt_pad, 1, 1) + pltpu.roll(cnt_pad, 2, 1) + pltpu.roll(cnt_pad, 3, 1)
    dest = (rank + jnp.sum(onehot * seg_off, axis=1, keepdims=True)).astype(jnp.int32)
    col = lax.broadcasted_iota(jnp.int32, pt_ref.shape, 1)
    pt_ref[...] = jnp.where(col == dest, 1.0, 0.0).astype(BF16)
    meta_ref[0] = jnp.concatenate([cnt_pad, seg_off, jnp.zeros((SUBLANES - 2, LANES), F32)], axis=0)


def _sort_rows(tm):
    return -(-(tm + N_GROUPS * SORT_PAD) // LANES) * LANES


def _post(att, sa, mb, x, w, *, tm, sort):
    n, d = x.shape
    row = lambda t: (t, 0)
    const = lambda t: (0, 0)
    gate_w = LANES if sort else N_GROUPS * LANES
    out_specs = [pl.BlockSpec((tm, d), row), pl.BlockSpec((tm, d), row), pl.BlockSpec((tm, gate_w), row)]
    out_shape = [jax.ShapeDtypeStruct((n, d), F32), jax.ShapeDtypeStruct((n, d), BF16),
                 jax.ShapeDtypeStruct((n, gate_w), F32)]
    if sort:
        out_specs += [pl.BlockSpec((tm, _sort_rows(tm)), row), pl.BlockSpec((1, SUBLANES, LANES), lambda t: (t, 0, 0))]
        out_shape += [jax.ShapeDtypeStruct((n, _sort_rows(tm)), BF16),
                      jax.ShapeDtypeStruct((n // tm, SUBLANES, LANES), F32)]
    return pl.pallas_call(
        _post_kernel,
        grid=(n // tm,),
        in_specs=[pl.BlockSpec((tm, ATT_WIDTH), row), pl.BlockSpec((tm, d), row), pl.BlockSpec((tm, d), row),
                  pl.BlockSpec((tm, d), row), pl.BlockSpec(w["wpa"].shape, const), pl.BlockSpec(w["wout"].shape, const),
                  pl.BlockSpec((1, d), const), pl.BlockSpec(w["wr"].shape, const), pl.BlockSpec((1, ROUTER_LANES), const)],
        out_specs=out_specs,
        out_shape=out_shape,
        compiler_params=pltpu.CompilerParams(dimension_semantics=("arbitrary",), vmem_limit_bytes=VMEM_LIMIT),
        name="post_attn",
    )(att, sa, mb, x, w["wpa"], w["wout"], w["g2"], w["wr"], w["br"])


def _moe_kernel(xn2_ref, gate_ref, h_ref, wg_ref, wu_ref, wd_ref, fn_ref, o_ref, acc_ref, *, final):
    g = pl.program_id(1)

    @pl.when(g == 0)
    def _():
        acc_ref[...] = jnp.zeros_like(acc_ref)

    x = xn2_ref[...]
    gate = gate_ref[...]
    act = jax.nn.silu(_dot(x, wg_ref[0])) * _dot(x, wu_ref[0])
    parts = []
    for e in range(EXP_PER_GROUP):
        parts.append((act[:, e * EXP_HIDDEN:(e + 1) * EXP_HIDDEN] * gate[:, e:e + 1]).astype(BF16))
    acc_ref[...] += _dot(jnp.concatenate(parts, axis=1), wd_ref[0])

    @pl.when(g == N_GROUPS - 1)
    def _():
        y = h_ref[...] + acc_ref[...]
        o_ref[...] = _rms(y, fn_ref[...]) if final else y


def _moe(xn2, gate, h, w, fn, *, tm, final):
    n, d = h.shape
    row = lambda t, g: (t, 0)
    wspec = lambda shp: pl.BlockSpec((1,) + shp[1:], lambda t, g: (g, 0, 0))
    return pl.pallas_call(
        functools.partial(_moe_kernel, final=final),
        grid=(n // tm, N_GROUPS),
        in_specs=[pl.BlockSpec((tm, d), row), pl.BlockSpec((tm, LANES), lambda t, g: (t, g)), pl.BlockSpec((tm, d), row),
                  wspec(w["weg"].shape), wspec(w["weu"].shape), wspec(w["wed"].shape),
                  pl.BlockSpec((1, d), lambda t, g: (0, 0))],
        out_specs=pl.BlockSpec((tm, d), row),
        out_shape=jax.ShapeDtypeStruct((n, d), F32),
        scratch_shapes=[pltpu.VMEM((tm, d), F32)],
        compiler_params=pltpu.CompilerParams(dimension_semantics=("arbitrary", "arbitrary"),
                                             vmem_limit_bytes=VMEM_LIMIT),
        name="moe",
    )(xn2, gate, h, w["weg"], w["weu"], w["wed"], fn)


def _moe_sorted_kernel(meta_ref, xn2_ref, gate_ref, h_ref, pt_ref, wg_ref, wu_ref, wd_ref, fn_ref, o_ref,
                       xs_scr, gs_scr, ys_scr, *, final):
    t = pl.program_id(0)
    rows = pt_ref.shape[1]
    pt = pt_ref[...]
    perm = pt.astype(F32).T.astype(BF16)
    xs_scr[0:rows, :] = _dot(perm, xn2_ref[...]).astype(BF16)
    xs_scr[rows:rows + MOE_CHUNK, :] = jnp.zeros((MOE_CHUNK, xs_scr.shape[1]), BF16)
    gate = gate_ref[...]
    g_hi = gate.astype(BF16)
    g_rest = gate - g_hi.astype(F32)
    g_mid = g_rest.astype(BF16)
    g_lo = (g_rest - g_mid.astype(F32)).astype(BF16)
    gs_scr[0:rows, :] = _dot(perm, g_hi) + (_dot(perm, g_mid) + _dot(perm, g_lo))
    gs_scr[rows:rows + MOE_CHUNK, :] = jnp.zeros((MOE_CHUNK, LANES), F32)
    tokens = xn2_ref.shape[0]
    ys_scr[tokens:, :] = jnp.zeros((ys_scr.shape[0] - tokens, ys_scr.shape[1]), F32)

    for g in range(N_GROUPS):
        cnt = meta_ref[t * 2 * N_GROUPS + g]
        seg = meta_ref[t * 2 * N_GROUPS + N_GROUPS + g]

        def chunk(c, carry, g=g, cnt=cnt, seg=seg):
            r0 = pl.multiple_of(seg + c * MOE_CHUNK, SORT_PAD)
            x = xs_scr[pl.ds(r0, MOE_CHUNK), :]
            gt = gs_scr[pl.ds(r0, MOE_CHUNK), :]
            act = jax.nn.silu(_dot(x, wg_ref[g])) * _dot(x, wu_ref[g])
            parts = [(act[:, e * EXP_HIDDEN:(e + 1) * EXP_HIDDEN] * gt[:, e:e + 1]).astype(BF16)
                     for e in range(EXP_PER_GROUP)]
            ys_scr[pl.ds(r0, MOE_CHUNK), :] = _dot(jnp.concatenate(parts, axis=1), wd_ref[g])
            return carry

        lax.fori_loop(0, (cnt + MOE_CHUNK - 1) // MOE_CHUNK, chunk, 0)

    y = h_ref[...] + _dot(pt, ys_scr[0:rows, :].astype(BF16))
    o_ref[...] = _rms(y, fn_ref[...]) if final else y


def _moe_sorted(xn2, gate, h, pt, meta, w, fn, *, tm, final):
    n, d = h.shape
    rows = pt.shape[1]
    row = lambda t, m: (t, 0)
    resident = lambda shp: pl.BlockSpec(shp, lambda t, m: (0, 0, 0), pipeline_mode=pl.Buffered(1))
    grid_spec = pltpu.PrefetchScalarGridSpec(
        num_scalar_prefetch=1,
        grid=(n // tm,),
        in_specs=[pl.BlockSpec((tm, d), row), pl.BlockSpec((tm, LANES), row), pl.BlockSpec((tm, d), row),
                  pl.BlockSpec((tm, rows), row),
                  resident(w["weg"].shape), resident(w["weu"].shape), resident(w["wed"].shape),
                  pl.BlockSpec((1, d), lambda t, m: (0, 0))],
        out_specs=pl.BlockSpec((tm, d), row),
        scratch_shapes=[pltpu.VMEM((rows + MOE_CHUNK, d), BF16), pltpu.VMEM((rows + MOE_CHUNK, LANES), F32),
                        pltpu.VMEM((rows + MOE_CHUNK, d), F32)],
    )
    return pl.pallas_call(
        functools.partial(_moe_sorted_kernel, final=final),
        grid_spec=grid_spec,
        out_shape=jax.ShapeDtypeStruct((n, d), F32),
        compiler_params=pltpu.CompilerParams(dimension_semantics=("arbitrary",), vmem_limit_bytes=VMEM_LIMIT),
        name="moe_sorted",
    )(meta, xn2, gate, h, pt, w["weg"], w["weu"], w["wed"], fn)


def _pack_experts_kernel(w_ref, o_ref):
    for e in range(EXP_PER_GROUP):
        o_ref[:, e * EXP_HIDDEN:(e + 1) * EXP_HIDDEN] = w_ref[e].astype(BF16)


def _pack_experts(w):
    depth, n_groups, n_exp, d, f = w.shape
    rows = 2 * MXU_DIM
    return pl.pallas_call(
        _pack_experts_kernel,
        grid=(depth, n_groups, d // rows),
        in_specs=[pl.BlockSpec((None, None, n_exp, rows, f), lambda l, g, r: (l, g, 0, r, 0))],
        out_specs=pl.BlockSpec((None, None, rows, n_exp * f), lambda l, g, r: (l, g, r, 0)),
        out_shape=jax.ShapeDtypeStruct((depth, n_groups, d, n_exp * f), BF16),
        compiler_params=pltpu.CompilerParams(dimension_semantics=("arbitrary",) * 3, vmem_limit_bytes=VMEM_LIMIT),
        name="pack_experts",
    )(w)


def _rope_tables(pos, d):
    inv = ROPE_THETA ** (-jnp.arange(0, d, 2, dtype=F32) / d)
    ang = pos[:, None] * inv[None, :]
    cos, sin = jnp.cos(ang), jnp.sin(ang)
    reps = LANES // d
    return (jnp.tile(jnp.concatenate([cos, cos], axis=-1), (1, reps)),
            jnp.tile(jnp.concatenate([-sin, sin], axis=-1), (1, reps)))


def _layer_weights(l, norm1, w_in, sgu_ln_g, sgu_ln_b, sgu_w, sgu_b, w_pa, w_pb, w_out, norm2,
                   w_rg, b_rg, w_re, b_re, w_eg, w_eu, w_ed, sgu_chunk):
    d = w_in.shape[1]
    a_end = ATT_WIDTH + 2 * KV_WIDTH + IDX_WIDTH
    b_end = a_end + IDX_DIM + IDX_HEADS
    reps = SGU_CHUNK // sgu_chunk
    n_router = N_GROUPS + N_GROUPS * EXP_PER_GROUP
    return dict(
        g1=norm1[l][None, :],
        wa=w_in[l][:, :a_end].astype(BF16),
        wb=jnp.pad(w_in[l][:, a_end:b_end], ((0, 0), (0, LANES - (b_end - a_end)))).astype(BF16),
        wc=w_in[l][:, b_end:].astype(BF16),
        sguw=jnp.tile(sgu_w[l][:, :sgu_chunk, :sgu_chunk], (1, reps, reps)),
        sgub=jnp.repeat(jnp.tile(sgu_b[l][:, :sgu_chunk], (1, reps)).T, SGU_GDIM, axis=1),
        lng=sgu_ln_g[l][None, :], lnb=sgu_ln_b[l][None, :],
        wpb=w_pb[l].astype(BF16), wpa=w_pa[l].astype(BF16), wout=w_out[l].astype(BF16),
        g2=norm2[l][None, :],
        wr=jnp.pad(jnp.concatenate([w_rg[l], w_re[l]], axis=1), ((0, 0), (0, ROUTER_LANES - n_router))),
        br=jnp.pad(jnp.concatenate([b_rg[l], b_re[l]]), (0, ROUTER_LANES - n_router))[None, :],
        weg=w_eg[l], weu=w_eu[l],
        wed=w_ed[l].reshape(N_GROUPS, GROUP_HIDDEN, d).astype(BF16),
    )


def _token_tile(n):
    return 512 if n % 512 == 0 else n


def kernel(x_prompt, x_sample, cache_k, cache_v, cache_kidx, norm1, w_in, sgu_ln_g, sgu_ln_b, sgu_w, sgu_b,
           w_pa, w_pb, w_out, norm2, w_rg, b_rg, w_re, b_re, w_eg, w_eu, w_ed, final_norm):
    bp, tp, d = x_prompt.shape
    bs, ts, _ = x_sample.shape
    depth, _, past = cache_k.shape[:3]
    np_, ns = bp * tp, bs * ts
    tm_p, tm_s = _token_tile(tp), _token_tile(ns)
    assert tp % tm_p == 0 and tp % Q_BLOCK == 0 and SGU_CHUNK % ts == 0 and tm_s % SGU_CHUNK == 0
    assert tm_p % KEY_SUB == 0 and tm_s % KEY_SUB == 0

    pos_p = jnp.arange(tp, dtype=F32)
    pos_s = jnp.tile(past + jnp.arange(ts, dtype=F32), tm_s // ts)
    tabs_p = _rope_tables(pos_p, HEAD_DIM) + _rope_tables(pos_p, IDX_DIM)
    tabs_s = _rope_tables(pos_s, HEAD_DIM) + _rope_tables(pos_s, IDX_DIM)
    ckt = jnp.transpose(cache_k, (0, 1, 3, 4, 2)).reshape(depth, bs, KV_WIDTH, past)
    cvt = jnp.transpose(cache_v, (0, 1, 3, 4, 2)).reshape(depth, bs, KV_WIDTH, past)
    ckit = jnp.transpose(cache_kidx, (0, 1, 3, 2))
    fn = final_norm[None, :]
    weg_all, weu_all = _pack_experts(w_eg), _pack_experts(w_eu)

    hp = x_prompt.reshape(np_, d)
    hs = x_sample.reshape(ns, d)
    outs = [[] for _ in range(4)]
    kv_bufs = None
    for l in range(depth):
        wargs = (norm1, w_in, sgu_ln_g, sgu_ln_b, sgu_w, sgu_b, w_pa, w_pb, w_out, norm2,
                 w_rg, b_rg, w_re, b_re, weg_all, weu_all, w_ed)
        wp = _layer_weights(l, *wargs, sgu_chunk=SGU_CHUNK)
        ws = dict(wp)
        ws.update({k: v for k, v in _layer_weights(l, *wargs, sgu_chunk=ts).items() if k in ("sguw", "sgub")})
        final = l == depth - 1

        q, kt_all, kb, vt_all, _, vt, qi, kit_all, kib, wi, mb, sa = _inproj(
            hp, wp, tabs_p, tm=2 * tm_p, sgu_chunk=SGU_CHUNK, want_vln=False,
            kv_layer=l, kv_bufs=kv_bufs, kv_shape=(depth, bp, tp))
        kv_bufs = (kt_all, vt_all, kit_all)
        r3 = lambda a: a.reshape(bp, tp, a.shape[-1])
        att = _dsa_prompt(q, qi, wi, r3(kb), vt, r3(kib)).reshape(np_, ATT_WIDTH)
        h, xn2, gate, pt, meta = _post(att, sa, mb, hp, wp, tm=tm_p, sort=True)
        meta = meta[:, 0:2, 0:N_GROUPS].astype(jnp.int32).reshape(-1)
        hp = _moe_sorted(xn2, gate, h, pt, meta, wp, fn, tm=tm_p, final=final)

        q, kf, kb, vf, vb, _, qi, kif, kib, wi, mb, sa, vln = _inproj(hs, ws, tabs_s, tm=tm_s, sgu_chunk=ts,
                                                                      want_vln=True)
        r3 = lambda a: a.reshape(bs, ts, a.shape[-1])
        att = _dsa_sample(r3(q), r3(qi), r3(wi), r3(kb), r3(vb), r3(kib), ckt, cvt, ckit, l)
        h, xn2, gate = _post(att.reshape(ns, ATT_WIDTH), sa, mb, hs, ws, tm=tm_s, sort=False)
        hs = _moe(xn2, gate, h, ws, fn, tm=tm_s, final=final)
        outs[0].append(kf.reshape(bs, ts, N_KV_HEADS, HEAD_DIM))
        outs[1].append(vf.reshape(bs, ts, N_KV_HEADS, HEAD_DIM))
        outs[2].append(kif.reshape(bs, ts, IDX_DIM))
        outs[3].append(vln.reshape(bs, ts, SGU_WIDTH))

    kt_all, vt_all, kit_all = kv_bufs
    heads = lambda a: jnp.transpose(a.reshape(depth, bp, N_KV_HEADS, HEAD_DIM, tp), (0, 1, 4, 2, 3))
    return ((hp.reshape(bp, tp, d), hs.reshape(bs, ts, d), heads(kt_all), heads(vt_all),
             jnp.transpose(kit_all, (0, 1, 3, 2))) + tuple(jnp.stack(o) for o in outs))
```

```python
import functools

import jax
import jax.numpy as jnp
from jax import lax
from jax.experimental import pallas as pl
from jax.experimental.pallas import tpu as pltpu

F32 = jnp.float32
BF16 = jnp.bfloat16

CHUNK = 64
N_HEADS = 8
N_KV_HEADS = 2
HEAD_DIM = 64
GQA_GROUP = N_HEADS // N_KV_HEADS
ATT_WIDTH = N_HEADS * HEAD_DIM
KV_WIDTH = N_KV_HEADS * HEAD_DIM
IDX_HEADS = 8
IDX_DIM = 32
IDX_WIDTH = IDX_HEADS * IDX_DIM
TOPK_MAX = 256
Q_BLOCK = 128
DSA_QUERIES = 256
SGU_GROUPS = 4
SGU_CHUNK = 128
SGU_WIDTH = 512
SGU_GDIM = SGU_WIDTH // SGU_GROUPS
N_GROUPS = 4
EXP_PER_GROUP = 8
EXP_HIDDEN = 128
GROUP_HIDDEN = EXP_PER_GROUP * EXP_HIDDEN
ROPE_THETA = 10000.0
EPS = 1e-6
Q_SCALE = HEAD_DIM ** -0.5 * 1.4426950408889634

LANES = 128
SUBLANES = 8
MXU_DIM = 256
ROUTER_LANES = LANES
ROUTER_E0 = N_GROUPS
VMEM_LIMIT = 56 * 1024 * 1024
KEY_TILE = 2 * MXU_DIM
KEY_SUB = MXU_DIM
FOLD_WAYS = 4
COUNT_ROWS = 64
SEARCH_FIRST_STEPS = 17
SEARCH_SECOND_STEPS = 0
SEARCH_ROUND_STEPS = 1
SEARCH_MAX_ROUNDS = 34
SORT_TOKENS = 512
SORT_PAD = 16
MOE_CHUNK = 144
KNORM_SLACK = 1.0 + 2.0 ** -6
SOFTMAX_SUM_FLOOR = 2.0 ** -80
MAX_SUBNORMAL_BITS = 0x007FFFFF
NEG_INF = float("-inf")


def _dot(a, b):
    return jnp.dot(a, b, preferred_element_type=F32)


def _dot_t(a, b):
    return lax.dot_general(a, b, (((1,), (1,)), ((), ())), preferred_element_type=F32)


def _rms(x, g):
    return x * lax.rsqrt(jnp.mean(x * x, axis=-1, keepdims=True) + EPS) * g


def _swap_halves(x, d):
    lane = lax.broadcasted_iota(jnp.int32, x.shape, 1)
    first = (lane % d) < (d // 2)
    return jnp.where(first, pltpu.roll(x, LANES - d // 2, 1), pltpu.roll(x, d // 2, 1))


def _rope(x, cos, sin_signed, d):
    return x * cos + _swap_halves(x, d) * sin_signed


def _inproj_kernel(*refs, sgu_chunk, kv_transposed, n_alias, want_vln):
    (x_ref, g1_ref, wa_ref, wb_ref, wc_ref, cq_ref, sq_ref, ci_ref, si_ref,
     sguw_ref, sgub_ref, lng_ref, lnb_ref, wpb_ref) = refs[:14]
    (q_ref, kf_ref, kb_ref, vf_ref, vb_ref, vt_ref, qi_ref, kif_ref, kib_ref, wi_ref,
     mb_ref, sa_ref) = refs[14 + n_alias:26 + n_alias]
    tm = x_ref.shape[0]
    xn = _rms(x_ref[...], g1_ref[...]).astype(BF16)
    cq, sq, ci, si = cq_ref[...], sq_ref[...], ci_ref[...], si_ref[...]

    a = _dot(xn, wa_ref[...])
    def emit_queries(ref, val):
        if kv_transposed:
            for c in range(tm // DSA_QUERIES):
                ref[c] = val[c * DSA_QUERIES:(c + 1) * DSA_QUERIES, :].T[0:ref.shape[1], :].astype(ref.dtype)
        else:
            ref[...] = val.astype(ref.dtype)

    emit_queries(q_ref, jnp.concatenate(
        [_rope(a[:, c * LANES:(c + 1) * LANES], cq, sq, HEAD_DIM) * Q_SCALE for c in range(ATT_WIDTH // LANES)], axis=1))
    k = _rope(a[:, ATT_WIDTH:ATT_WIDTH + KV_WIDTH], cq, sq, HEAD_DIM)
    kb_ref[...] = k.astype(BF16)
    v = a[:, ATT_WIDTH + KV_WIDTH:ATT_WIDTH + 2 * KV_WIDTH]
    vb_ref[...] = v.astype(BF16)
    if kv_transposed:
        kf_ref[...] = k.T
        vf_ref[...] = v.T
    else:
        kf_ref[...] = k
        vf_ref[...] = v
    for c in range(tm // KEY_SUB):
        vt_ref[c] = v[c * KEY_SUB:(c + 1) * KEY_SUB, :].T.astype(BF16)
    qi0 = ATT_WIDTH + 2 * KV_WIDTH
    emit_queries(qi_ref, jnp.concatenate(
        [_rope(a[:, qi0 + c * LANES:qi0 + (c + 1) * LANES], ci, si, IDX_DIM) for c in range(IDX_WIDTH // LANES)], axis=1))

    b = _dot(xn, wb_ref[...])
    ki_wide = _rope(b, ci, si, IDX_DIM)
    ki = ki_wide[:, :IDX_DIM]
    kif_ref[...] = ki_wide.T[0:IDX_DIM, :] if kv_transposed else ki
    kib_ref[...] = ki.astype(BF16)
    emit_queries(wi_ref, pltpu.roll(b, LANES - IDX_DIM, 1) * (IDX_HEADS ** -0.5))

    u = jax.nn.gelu(_dot(xn, wc_ref[:, 0:SGU_WIDTH]))
    vs = jax.nn.gelu(_dot(xn, wc_ref[:, SGU_WIDTH:2 * SGU_WIDTH]))
    mu = jnp.mean(vs, axis=-1, keepdims=True)
    var = jnp.mean(jnp.square(vs - mu), axis=-1, keepdims=True)
    vln = (vs - mu) * lax.rsqrt(var + EPS) * lng_ref[...] + lnb_ref[...]
    if want_vln:
        refs[26 + n_alias][...] = vln
    ri = lax.broadcasted_iota(jnp.int32, (SGU_CHUNK, SGU_CHUNK), 0)
    cj = lax.broadcasted_iota(jnp.int32, (SGU_CHUNK, SGU_CHUNK), 1)
    mix_mask = (cj <= ri) & ((ri // sgu_chunk) == (cj // sgu_chunk))
    vln_b = vln.astype(BF16)
    w_eff = [jnp.where(mix_mask, sguw_ref[g], 0.0).astype(BF16) for g in range(SGU_GROUPS)]
    sg_rows = []
    for r in range(tm // SGU_CHUNK):
        rows = slice(r * SGU_CHUNK, (r + 1) * SGU_CHUNK)
        mixed = [_dot(w_eff[g], vln_b[rows, g * SGU_GDIM:(g + 1) * SGU_GDIM]) for g in range(SGU_GROUPS)]
        sg_rows.append(u[rows, :] * (jnp.concatenate(mixed, axis=1) + sgub_ref[...]))
    sg = jnp.concatenate(sg_rows, axis=0).astype(BF16)
    d = x_ref.shape[1]
    sa_ref[...] = jax.nn.sigmoid(_dot(xn, wc_ref[:, 2 * SGU_WIDTH:2 * SGU_WIDTH + d])).astype(BF16)
    gb = jax.nn.sigmoid(_dot(xn, wc_ref[:, 2 * SGU_WIDTH + d:]))
    mb_ref[...] = (gb * _dot(sg, wpb_ref[...])).astype(BF16)


def _inproj(x, w, tabs, *, tm, sgu_chunk, want_vln, kv_layer=None, kv_bufs=None, kv_shape=None):
    n, d = x.shape
    n_tab_tiles = tabs[0].shape[0] // tm
    row = lambda t: (t, 0)
    const = lambda t: (0, 0)
    tab = lambda t: (t % n_tab_tiles, 0)
    kv_transposed = kv_layer is not None
    if kv_transposed:
        depth, streams, steps = kv_shape
        per_stream = steps // tm
        kv_sds = lambda wd: jax.ShapeDtypeStruct((depth, streams, wd, steps), F32)
        kv_spec = lambda wd: pl.BlockSpec((None, None, wd, tm),
                                          lambda t: (kv_layer, t // per_stream, 0, t % per_stream))
    else:
        kv_sds = lambda wd: jax.ShapeDtypeStruct((n, wd), F32)
        kv_spec = lambda wd: pl.BlockSpec((tm, wd), row)
    tiled = lambda wd, dt: (jax.ShapeDtypeStruct((n, wd), dt), pl.BlockSpec((tm, wd), row))
    if kv_transposed:
        qb = DSA_QUERIES
        query = lambda wd, rows, dt: (jax.ShapeDtypeStruct((n // qb, rows, qb), dt),
                                      pl.BlockSpec((tm // qb, rows, qb), lambda t: (t, 0, 0)))
    else:
        query = lambda wd, rows, dt: tiled(wd, dt)
    outs = [
        query(ATT_WIDTH, ATT_WIDTH, BF16),
        (kv_sds(KV_WIDTH), kv_spec(KV_WIDTH)),
        tiled(KV_WIDTH, BF16),
        (kv_sds(KV_WIDTH), kv_spec(KV_WIDTH)),
        tiled(KV_WIDTH, BF16),
        (jax.ShapeDtypeStruct((n // KEY_SUB, KV_WIDTH, KEY_SUB), BF16),
         pl.BlockSpec((tm // KEY_SUB, KV_WIDTH, KEY_SUB), lambda t: (t, 0, 0))),
        query(IDX_WIDTH, IDX_WIDTH, BF16),
        (kv_sds(IDX_DIM), kv_spec(IDX_DIM)),
        tiled(IDX_DIM, BF16),
        query(LANES, IDX_HEADS, F32),
        tiled(d, BF16),
        tiled(d, BF16),
    ]
    if want_vln:
        outs.append(tiled(SGU_WIDTH, F32))
    in_specs = [
        pl.BlockSpec((tm, d), row),
        pl.BlockSpec((1, d), const),
        pl.BlockSpec(w["wa"].shape, const, pipeline_mode=pl.Buffered(1)),
        pl.BlockSpec(w["wb"].shape, const, pipeline_mode=pl.Buffered(1)),
        pl.BlockSpec(w["wc"].shape, const, pipeline_mode=pl.Buffered(1)),
        pl.BlockSpec((tm, LANES), tab), pl.BlockSpec((tm, LANES), tab),
        pl.BlockSpec((tm, LANES), tab), pl.BlockSpec((tm, LANES), tab),
        pl.BlockSpec(w["sguw"].shape, lambda t: (0, 0, 0)),
        pl.BlockSpec(w["sgub"].shape, const),
        pl.BlockSpec((1, SGU_WIDTH), const),
        pl.BlockSpec((1, SGU_WIDTH), const),
        pl.BlockSpec(w["wpb"].shape, const),
    ]
    args = [x, w["g1"], w["wa"], w["wb"], w["wc"], *tabs, w["sguw"], w["sgub"], w["lng"], w["lnb"], w["wpb"]]
    aliases = {}
    if kv_bufs is not None:
        for buf, out_idx in zip(kv_bufs, (1, 3, 7)):
            aliases[len(args)] = out_idx
            in_specs.append(pl.BlockSpec(memory_space=pl.ANY))
            args.append(buf)
    return pl.pallas_call(
        functools.partial(_inproj_kernel, sgu_chunk=sgu_chunk, kv_transposed=kv_transposed,
                          n_alias=len(aliases), want_vln=want_vln),
        grid=(n // tm,),
        in_specs=in_specs,
        out_specs=[o[1] for o in outs],
        out_shape=[o[0] for o in outs],
        input_output_aliases=aliases,
        compiler_params=pltpu.CompilerParams(dimension_semantics=("arbitrary",), vmem_limit_bytes=VMEM_LIMIT),
        name="inproj",
    )(*args)


def _float_key(x):
    b = lax.bitcast_convert_type(x, jnp.int32)
    k = b ^ ((b >> 31) & jnp.int32(0x7FFFFFFF))
    return jnp.where(k > MAX_SUBNORMAL_BITS, k - MAX_SUBNORMAL_BITS,
                     jnp.where(k < -MAX_SUBNORMAL_BITS - 1, k + MAX_SUBNORMAL_BITS, jnp.clip(k, -1, 0)))


def _key_float(k):
    k = jnp.where(k > 0, k + MAX_SUBNORMAL_BITS, jnp.where(k < -1, k - MAX_SUBNORMAL_BITS, k))
    return lax.bitcast_convert_type(k ^ ((k >> 31) & jnp.int32(0x7FFFFFFF)), F32)


def _fold_rows(x):
    n = x.shape[0] // SUBLANES
    ways = FOLD_WAYS if n % FOLD_WAYS == 0 else 1
    part = jnp.sum(x.reshape(n // ways, ways, SUBLANES, x.shape[1]), axis=0)
    return jnp.sum(part, axis=0)


def _topk_threshold(count_ge, snap, smin, smax, n_adm, kk):
    lo0 = _float_key(smin + 0.0)
    hx0 = _float_key(smax + 0.0) + 1

    def is_active(st):
        lo, hx, c_lo, _ = st
        return jnp.logical_and(c_lo != kk, lo + 1 < hx)

    def step(st, by_key):
        lo, hx, c_lo, c_hx = st
        active = is_active(st)
        if by_key:
            mid = (lo >> 1) + (hx >> 1) + (lo & hx & 1)
        else:
            mid = _float_key(0.5 * _key_float(lo) + 0.5 * _key_float(hx))
        mid = jnp.where(active, jnp.minimum(jnp.maximum(mid, lo + 1), hx - 1), lo)
        c = count_ge(_key_float(mid))
        up = jnp.logical_and(active, c >= kk)
        down = jnp.logical_and(active, c < kk)
        return (jnp.where(up, mid, lo), jnp.where(down, mid, hx),
                jnp.where(up, c, c_lo), jnp.where(down, c, c_hx))

    def snap_step(st):
        lo, hx, c_lo, c_hx = st
        active = is_active(st)
        first_in, last_in = snap(_key_float(lo), _key_float(hx))
        lo_n = jnp.where(active, _float_key(first_in + 0.0), lo)
        hx_n = jnp.where(active, _float_key(last_in + 0.0) + 1, hx)
        few = jnp.logical_and(active, c_lo - c_hx <= 2.0)
        same = first_in == last_in
        take_last = kk - c_hx <= 1.0
        thr_key = _float_key(jnp.where(take_last, last_in, first_in) + 0.0)
        c_ge = jnp.where(jnp.logical_and(take_last, jnp.logical_not(same)), c_hx + 1.0, c_lo)
        c_gt = jnp.where(jnp.logical_or(take_last, same), c_hx, c_hx + 1.0)
        return (jnp.where(few, thr_key, lo_n), jnp.where(few, thr_key + 1, hx_n),
                jnp.where(few, c_ge, c_lo), jnp.where(few, c_gt, c_hx))

    def n_active(st):
        return jnp.max(jnp.where(is_active(st), 1.0, 0.0))

    st = (lo0, hx0, n_adm, jnp.zeros_like(n_adm))
    for _ in range(SEARCH_FIRST_STEPS):
        st = step(st, by_key=False)
    st = snap_step(st)
    for _ in range(SEARCH_SECOND_STEPS):
        st = step(st, by_key=False)
    act = n_active(st)

    def cond(carry):
        it, act, _ = carry
        return jnp.logical_and(it < SEARCH_MAX_ROUNDS, act > 0.5)

    def body(carry):
        it, _, st = carry
        st = step(snap_step(st), by_key=True)
        for _ in range(SEARCH_ROUND_STEPS):
            st = step(st, by_key=False)
        return it + 1, n_active(st), st

    _, _, (lo, _, c_lo, c_hx) = lax.while_loop(cond, body, (jnp.int32(0), act, st))
    return _key_float(lo), c_lo, c_hx


def _demote_excess_ties(s_ref, n_chunks, thr, need):
    ri = lax.broadcasted_iota(jnp.int32, (LANES, LANES), 0)
    cj = lax.broadcasted_iota(jnp.int32, (LANES, LANES), 1)
    lower = jnp.where(cj <= ri, 1.0, 0.0).astype(BF16)

    def body(j, run):
        off = pl.multiple_of(j * LANES, LANES)
        blk = s_ref[pl.ds(off, LANES), :]
        tied = blk == thr
        tied_f = jnp.where(tied, 1.0, 0.0)
        prefix = _dot(lower, tied_f.astype(BF16)) + run
        s_ref[pl.ds(off, LANES), :] = jnp.where(jnp.logical_and(tied, prefix > need), NEG_INF, blk)
        return run + jnp.sum(tied_f, axis=0, keepdims=True)

    lax.fori_loop(0, n_chunks, body, jnp.zeros((1, s_ref.shape[1]), F32))


def _select_mask(s_ref, n_full, n_half, tk, smin, smax, n_adm, topk, selected=0.0):
    kk = jnp.minimum(n_adm, float(topk))
    half = tk // 2

    def over_rows(visit, carry):
        carry = lax.fori_loop(0, n_full, lambda j, c: visit(pl.multiple_of(j * tk, tk), tk, c), carry)
        return lax.fori_loop(0, n_half, lambda j, c: visit(pl.multiple_of(n_full * tk, half), half, c), carry)

    def count_ge(t):
        def visit(off, rows, acc):
            for r in range(0, rows, COUNT_ROWS):
                acc = acc + _fold_rows(jnp.where(s_ref[pl.ds(off + r, COUNT_ROWS), :] >= t, 1.0, 0.0))
            return acc
        acc = over_rows(visit, jnp.zeros((SUBLANES, s_ref.shape[1]), F32))
        return jnp.sum(acc, axis=0, keepdims=True)

    def snap(lo_f, hx_f):
        def visit(off, rows, carry):
            first_in, last_in = carry
            for r in range(0, rows, COUNT_ROWS):
                blk = s_ref[pl.ds(off + r, COUNT_ROWS), :]
                first_in = jnp.minimum(first_in, -_fold_max(jnp.where(blk >= lo_f, -blk, NEG_INF)))
                last_in = jnp.maximum(last_in, _fold_max(jnp.where(blk < hx_f, blk, NEG_INF)))
            return first_in, last_in
        first_in, last_in = over_rows(visit, (jnp.full((SUBLANES, s_ref.shape[1]), -NEG_INF, F32),
                                              jnp.full((SUBLANES, s_ref.shape[1]), NEG_INF, F32)))
        return jnp.min(first_in, axis=0, keepdims=True), jnp.max(last_in, axis=0, keepdims=True)

    thr, c_ge, c_gt = _topk_threshold(count_ge, snap, smin, smax, n_adm, kk)

    @pl.when(jnp.max(c_ge - kk) > 0.5)
    def _():
        _demote_excess_ties(s_ref, (n_full * tk + n_half * half) // LANES, thr, kk - c_gt)

    def mask_rows(off, rows, carry):
        s_ref[pl.ds(off, rows), :] = jnp.where(s_ref[pl.ds(off, rows), :] >= thr, selected, NEG_INF)
        return carry

    over_rows(mask_rows, 0)


def _dsa_prompt_kernel(q_ref, qi_ref, wi_ref, k_ref, vt_ref, ki_ref, o_ref, s_scr, acc_scr, l_scr, kn_scr, lg_scr,
                       *, topk):
    i = pl.program_id(1)
    tk, ts = KEY_TILE, KEY_SUB
    qb = q_ref.shape[2]
    n_keys = (i + 1) * qb
    qcol = lax.broadcasted_iota(jnp.int32, (1, qb), 1)
    lim = ((i * qb + qcol) // CHUNK + 1) * CHUNK

    @pl.when(i == 0)
    def _():
        ri = lax.broadcasted_iota(jnp.int32, (KV_WIDTH, LANES), 0)
        cj = lax.broadcasted_iota(jnp.int32, (KV_WIDTH, LANES), 1)
        head_sum = jnp.where(ri // HEAD_DIM == cj, 1.0, 0.0).astype(BF16)

        def body(j, acc):
            kt = k_ref[0, pl.ds(pl.multiple_of(j * tk, tk), tk), :].astype(F32)
            return jnp.maximum(acc, _fold_max(_dot((kt * kt).astype(BF16), head_sum)))

        acc = lax.fori_loop(0, k_ref.shape[1] // tk, body, jnp.zeros((SUBLANES, LANES), F32))
        kn_scr[...] = jnp.max(acc, axis=0, keepdims=True) * KNORM_SLACK

    qt = q_ref[0]
    qit = qi_ref[0]
    wit = wi_ref[0]
    qi_rhs = jnp.concatenate([qit[h * IDX_DIM:(h + 1) * IDX_DIM, :] for h in range(IDX_HEADS)], axis=1)
    zero = jnp.zeros((HEAD_DIM, qb), BF16)
    q_cols = []
    for h in range(N_HEADS):
        qh = qt[h * HEAD_DIM:(h + 1) * HEAD_DIM, :]
        q_cols.append(jnp.concatenate([qh, zero] if h < GQA_GROUP else [zero, qh], axis=0))
    q_rhs = jnp.concatenate(q_cols, axis=1)

    def score_sub(sub, carry):
        smax, smin = carry
        off = pl.multiple_of(sub * ts, ts)
        r = _dot(ki_ref[0, pl.ds(off, ts), :], qi_rhs)
        acc = wit[0:1, :] * jnp.maximum(r[:, 0:qb], 0.0)
        for h in range(1, IDX_HEADS):
            acc = acc + wit[h:h + 1, :] * jnp.maximum(r[:, h * qb:(h + 1) * qb], 0.0)
        key = off + lax.broadcasted_iota(jnp.int32, (ts, qb), 0)
        ok = key < lim
        s_scr[pl.ds(off, ts), :] = jnp.where(ok, acc, NEG_INF)
        smax = jnp.maximum(smax, _fold_max(jnp.where(ok, acc, NEG_INF)))
        smin = jnp.minimum(smin, -_fold_max(jnp.where(ok, -acc, NEG_INF)))
        return smax, smin

    n_sub = n_keys // ts
    n_full, n_half = n_sub // 2, n_sub % 2
    carry = (jnp.full((SUBLANES, qb), NEG_INF, F32), jnp.full((SUBLANES, qb), -NEG_INF, F32))
    carry = lax.fori_loop(0, n_full, lambda j, c: score_sub(2 * j + 1, score_sub(2 * j, c)), carry)
    smax, smin = lax.fori_loop(0, n_half, lambda j, c: score_sub(2 * n_full, c), carry)
    smax = jnp.max(smax, axis=0, keepdims=True)
    smin = jnp.min(smin, axis=0, keepdims=True)
    qsq = jnp.square(qt.astype(F32))
    kn = kn_scr[...]
    lane1 = lax.broadcasted_iota(jnp.int32, (1, LANES), 1)
    shift = jnp.zeros((1, qb), F32)
    for h in range(N_HEADS):
        qn2 = jnp.sum(_fold_rows(qsq[h * HEAD_DIM:(h + 1) * HEAD_DIM, :]), axis=0, keepdims=True)
        kmax2 = jnp.max(jnp.where(lane1 == h // GQA_GROUP, kn, 0.0))
        shift = jnp.maximum(shift, jnp.sqrt(qn2 * kmax2))
    _select_mask(s_scr, n_full, n_half, tk, smin, smax, lim.astype(F32), topk, selected=-shift)

    def pv_accumulate(vt, ps, alphas):
        for pair in range(N_HEADS // 2):
            g = (2 * pair) // GQA_GROUP
            o = _dot(vt[g * HEAD_DIM:(g + 1) * HEAD_DIM, :], jnp.concatenate(ps[2 * pair:2 * pair + 2], axis=1))
            for u in range(2):
                h = 2 * pair + u
                rows = slice(h * HEAD_DIM, (h + 1) * HEAD_DIM)
                prev = acc_scr[rows, :] if alphas is None else alphas[h] * acc_scr[rows, :]
                acc_scr[rows, :] = prev + o[:, u * qb:(u + 1) * qb]

    def logits_into(buf, sub):
        off = pl.multiple_of(jnp.minimum(sub * ts, k_ref.shape[1] - ts), ts)
        lg_scr[buf] = _dot(k_ref[0, pl.ds(off, ts), :], q_rhs)

    ones_rows = jnp.ones((2 * SUBLANES, ts), BF16)

    def softmax_pv(buf, sub, l):
        bias = s_scr[pl.ds(pl.multiple_of(sub * ts, ts), ts), :]
        ps = [jnp.exp2(lg_scr[buf, :, h * qb:(h + 1) * qb] + bias).astype(BF16) for h in range(N_HEADS)]
        vt = vt_ref[sub]
        l_rows = []
        for pair in range(N_HEADS // 2):
            g = (2 * pair) // GQA_GROUP
            lhs = jnp.concatenate([vt[g * HEAD_DIM:(g + 1) * HEAD_DIM, :], ones_rows], axis=0)
            o = _dot(lhs, jnp.concatenate(ps[2 * pair:2 * pair + 2], axis=1))
            for u in range(2):
                h = 2 * pair + u
                rows = slice(h * HEAD_DIM, (h + 1) * HEAD_DIM)
                acc_scr[rows, :] = acc_scr[rows, :] + o[0:HEAD_DIM, u * qb:(u + 1) * qb]
                l_rows.append(l[h:h + 1, :] + o[HEAD_DIM:HEAD_DIM + 1, u * qb:(u + 1) * qb])
        return jnp.concatenate(l_rows, axis=0)

    def attn_bounded(j, l):
        logits_into(1, 2 * j + 1)
        l = softmax_pv(0, 2 * j, l)
        logits_into(0, 2 * j + 2)
        return softmax_pv(1, 2 * j + 1, l)

    def attn_online(j, carry):
        m, l = carry
        off = pl.multiple_of(j * ts, ts)
        lg = _dot(k_ref[0, pl.ds(off, ts), :], q_rhs)
        bias = s_scr[pl.ds(off, ts), :]
        m_rows, l_rows, alphas, ps = [], [], [], []
        for h in range(N_HEADS):
            x = lg[:, h * qb:(h + 1) * qb] + bias
            m_new = jnp.maximum(m[h:h + 1, :], jnp.max(_fold_max(x), axis=0, keepdims=True))
            m_use = jnp.where(m_new > NEG_INF, m_new, 0.0)
            alpha = jnp.exp2(m[h:h + 1, :] - m_use)
            p = jnp.exp2(x - m_use)
            l_rows.append(alpha * l[h:h + 1, :] + jnp.sum(_fold_rows(p), axis=0, keepdims=True))
            m_rows.append(m_new)
            alphas.append(alpha)
            ps.append(p.astype(BF16))
        pv_accumulate(vt_ref[j], ps, alphas)
        return jnp.concatenate(m_rows, axis=0), jnp.concatenate(l_rows, axis=0)

    acc_scr[...] = jnp.zeros_like(acc_scr)
    logits_into(0, 0)
    l_fast = lax.fori_loop(0, n_full, attn_bounded, jnp.zeros((N_HEADS, qb), F32))
    l_fast = lax.fori_loop(0, n_half, lambda j, l: softmax_pv(0, 2 * n_full, l), l_fast)
    l_scr[...] = l_fast

    @pl.when(jnp.min(l_fast) < SOFTMAX_SUM_FLOOR)
    def _():
        acc_scr[...] = jnp.zeros_like(acc_scr)
        _, l_exact = lax.fori_loop(0, n_sub, attn_online,
                                   (jnp.full((N_HEADS, qb), NEG_INF, F32), jnp.zeros((N_HEADS, qb), F32)))
        l_scr[...] = l_exact

    l = l_scr[...]
    out_t = jnp.concatenate([acc_scr[h * HEAD_DIM:(h + 1) * HEAD_DIM, :] / l[h:h + 1, :] for h in range(N_HEADS)],
                            axis=0)
    o_ref[0] = out_t.T.astype(BF16)


def _fold_max(x):
    n = x.shape[0] // SUBLANES
    ways = FOLD_WAYS if n % FOLD_WAYS == 0 else 1
    part = jnp.max(x.reshape(n // ways, ways, SUBLANES, x.shape[1]), axis=0)
    return jnp.max(part, axis=0)


def _dsa_prompt(q, qi, wi, kb, vt, kib):
    b, t, _ = kb.shape
    qb = DSA_QUERIES
    assert t % KEY_TILE == 0 and t % qb == 0 and qb % CHUNK == 0
    topk = min(TOPK_MAX, t // 4)
    blk = lambda w: pl.BlockSpec((1, qb, w), lambda bi, i: (bi, i, 0))
    qblk = lambda rows: pl.BlockSpec((1, rows, qb), lambda bi, i: (bi * (t // qb) + i, 0, 0))
    whole = lambda w: pl.BlockSpec((1, t, w), lambda bi, i: (bi, 0, 0))
    return pl.pallas_call(
        functools.partial(_dsa_prompt_kernel, topk=topk),
        grid=(b, t // qb),
        in_specs=[qblk(ATT_WIDTH), qblk(IDX_WIDTH), qblk(IDX_HEADS), whole(KV_WIDTH),
                  pl.BlockSpec((t // KEY_SUB, KV_WIDTH, KEY_SUB), lambda bi, i: (bi, 0, 0)), whole(IDX_DIM)],
        out_specs=blk(ATT_WIDTH),
        out_shape=jax.ShapeDtypeStruct((b, t, ATT_WIDTH), BF16),
        scratch_shapes=[pltpu.VMEM((t, qb), F32), pltpu.VMEM((ATT_WIDTH, qb), F32),
                        pltpu.VMEM((N_HEADS, qb), F32), pltpu.VMEM((1, LANES), F32),
                        pltpu.VMEM((2, KEY_SUB, N_HEADS * qb), F32)],
        compiler_params=pltpu.CompilerParams(dimension_semantics=("arbitrary", "arbitrary"),
                                             vmem_limit_bytes=VMEM_LIMIT),
        name="dsa_prompt",
    )(q, qi, wi, kb, vt, kib)


def _dsa_sample_kernel(q_ref, qi_ref, wi_ref, kn_ref, vn_ref, kin_ref, ckt_ref, cvt_ref, ckit_ref, o_ref,
                       kcat, vcat_t, kicat, s_scr, *, topk):
    tq = q_ref.shape[1]
    past = ckt_ref.shape[1]
    total = kcat.shape[0]
    pad = total - past - tq
    kcat[0:past, :] = ckt_ref[...].T.astype(BF16)
    kcat[past:past + tq, :] = kn_ref[0]
    kcat[past + tq:total, :] = jnp.zeros((pad, KV_WIDTH), BF16)
    kit_wide = jnp.concatenate([ckit_ref[...], jnp.zeros((LANES - IDX_DIM, past), F32)], axis=0)
    kicat[0:past, :] = kit_wide.T[:, 0:IDX_DIM].astype(BF16)
    kicat[past:past + tq, :] = kin_ref[0]
    kicat[past + tq:total, :] = jnp.zeros((pad, IDX_DIM), BF16)
    vcat_t[:, 0:past] = cvt_ref[...].astype(BF16)
    v_new = jnp.concatenate([vn_ref[0].astype(F32), jnp.zeros((pad, KV_WIDTH), F32)], axis=0)
    vcat_t[:, past:total] = v_new.T.astype(BF16)

    lane = lax.broadcasted_iota(jnp.int32, (1, LANES), 1)
    qi = qi_ref[0]
    qi_stack = jnp.concatenate([qi[:, h * IDX_DIM:(h + 1) * IDX_DIM] for h in range(IDX_HEADS)], axis=0)
    wi_rows = jnp.concatenate([wi_ref[0], jnp.zeros((LANES - tq, LANES), F32)], axis=0).T
    w_flat = jnp.zeros((1, LANES), F32)
    for h in range(IDX_HEADS):
        row = wi_rows[h:h + 1, :]
        shifted = row if h == 0 else pltpu.roll(row, h * tq, 1)
        w_flat = jnp.where(lane // tq == h, shifted, w_flat)

    sc = w_flat * jnp.maximum(_dot_t(kicat[...], qi_stack), 0.0)
    for shift in (LANES // 2, LANES // 4, LANES // 8):
        sc = sc + pltpu.roll(sc, shift, 1)
    sc = jnp.where(lane < tq, sc, 0.0)
    for shift in (LANES // 8, LANES // 4, LANES // 2):
        sc = sc + pltpu.roll(sc, shift, 1)
    key = lax.broadcasted_iota(jnp.int32, (total, LANES), 0)
    ok = key < past + tq
    s_scr[...] = jnp.where(ok, sc, NEG_INF)
    smax = jnp.max(jnp.where(ok, sc, NEG_INF), axis=0, keepdims=True)
    smin = jnp.min(jnp.where(ok, sc, -NEG_INF), axis=0, keepdims=True)
    _select_mask(s_scr, 1, 0, total, smin, smax, jnp.full((1, LANES), float(past + tq), F32), topk)

    q = q_ref[0]
    zero = jnp.zeros((tq, HEAD_DIM), BF16)
    q_stack = jnp.concatenate(
        [jnp.concatenate([q[:, h * HEAD_DIM:(h + 1) * HEAD_DIM], zero] if h < GQA_GROUP
                         else [zero, q[:, h * HEAD_DIM:(h + 1) * HEAD_DIM]], axis=1) for h in range(N_HEADS)], axis=0)
    lg = _dot_t(kcat[...], q_stack) + s_scr[...]
    m = jnp.max(lg, axis=0, keepdims=True)
    p = jnp.exp2(lg - m)
    l = jnp.sum(p, axis=0, keepdims=True)
    o = _dot(vcat_t[...], p.astype(BF16))
    o = jnp.where(lane < GQA_GROUP * tq, o[0:HEAD_DIM, :], o[HEAD_DIM:2 * HEAD_DIM, :]) / l
    o_t = jnp.concatenate([o, jnp.zeros((LANES - HEAD_DIM, LANES), F32)], axis=0).T
    o_ref[0] = jnp.concatenate([o_t[h * tq:(h + 1) * tq, 0:HEAD_DIM] for h in range(N_HEADS)], axis=1).astype(BF16)


def _dsa_sample(q, qi, wi, kb, vb, kib, cache_k, cache_v, cache_kidx, layer):
    b, tq, _ = q.shape
    past = cache_k.shape[3]
    assert tq * N_HEADS == LANES and tq * IDX_HEADS == LANES and past % LANES == 0
    total = past + LANES
    topk = min(TOPK_MAX, (past + tq) // 4)
    blk = lambda w: pl.BlockSpec((1, tq, w), lambda bi: (bi, 0, 0))
    cache = lambda w: pl.BlockSpec((None, None, w, past), lambda bi: (layer, bi, 0, 0))
    return pl.pallas_call(
        functools.partial(_dsa_sample_kernel, topk=topk),
        grid=(b,),
        in_specs=[blk(ATT_WIDTH), blk(IDX_WIDTH), blk(LANES), blk(KV_WIDTH), blk(KV_WIDTH), blk(IDX_DIM),
                  cache(KV_WIDTH), cache(KV_WIDTH), cache(IDX_DIM)],
        out_specs=blk(ATT_WIDTH),
        out_shape=jax.ShapeDtypeStruct((b, tq, ATT_WIDTH), BF16),
        scratch_shapes=[pltpu.VMEM((total, KV_WIDTH), BF16), pltpu.VMEM((KV_WIDTH, total), BF16),
                        pltpu.VMEM((total, IDX_DIM), BF16), pltpu.VMEM((total, LANES), F32)],
        compiler_params=pltpu.CompilerParams(dimension_semantics=("arbitrary",), vmem_limit_bytes=VMEM_LIMIT),
        name="dsa_sample",
    )(q, qi, wi, kb, vb, kib, cache_k, cache_v, cache_kidx)


def _first_lane_where(mask, lane):
    return jnp.min(jnp.where(mask, lane, LANES), axis=1, keepdims=True)


def _post_kernel(att_ref, sa_ref, mb_ref, x_ref, wpa_ref, wout_ref, g2_ref, wr_ref, br_ref,
                 h_ref, xn2_ref, gate_ref, *sort_refs):
    pa = _dot(att_ref[...], wpa_ref[...])
    merged = sa_ref[...].astype(F32) * pa + mb_ref[...].astype(F32)
    h = x_ref[...] + _dot(merged.astype(BF16), wout_ref[...])
    h_ref[...] = h
    xn2 = _rms(h, g2_ref[...])
    xh = xn2.astype(BF16)
    xn2_ref[...] = xh

    xl = (xn2 - xh.astype(F32)).astype(BF16)
    wr = wr_ref[...]
    wh = wr.astype(BF16)
    wl = (wr - wh.astype(F32)).astype(BF16)
    lg = _dot(xh, wh) + (_dot(xl, wh) + _dot(xh, wl)) + br_ref[...]

    lane = lax.broadcasted_iota(jnp.int32, lg.shape, 1)
    is_g = lane < N_GROUPS
    mg = jnp.max(jnp.where(is_g, lg, NEG_INF), axis=1, keepdims=True)
    eg = jnp.where(is_g, jnp.exp(lg - mg), 0.0)
    pg = eg / jnp.sum(eg, axis=1, keepdims=True)
    g_w = jnp.max(pg, axis=1, keepdims=True)
    g_idx = _first_lane_where(jnp.logical_and(is_g, pg == g_w), lane)

    e_lo = ROUTER_E0 + EXP_PER_GROUP * g_idx
    is_e = jnp.logical_and(lane >= e_lo, lane < e_lo + EXP_PER_GROUP)
    me = jnp.max(jnp.where(is_e, lg, NEG_INF), axis=1, keepdims=True)
    ee = jnp.where(is_e, jnp.exp(lg - me), 0.0)
    pe = ee / jnp.sum(ee, axis=1, keepdims=True)
    e1 = jnp.max(pe, axis=1, keepdims=True)
    i1 = _first_lane_where(jnp.logical_and(is_e, pe == e1), lane)
    rest = jnp.logical_and(is_e, lane != i1)
    e2 = jnp.max(jnp.where(rest, pe, -1.0), axis=1, keepdims=True)
    i2 = _first_lane_where(jnp.logical_and(rest, pe == e2), lane)
    den = e1 + e2
    gate = jnp.where(lane == i1, e1 / den, jnp.where(lane == i2, e2 / den, 0.0)) * g_w
    group_gates = [jnp.where(lane < EXP_PER_GROUP, pltpu.roll(gate, LANES - (ROUTER_E0 + EXP_PER_GROUP * g), 1), 0.0)
                   for g in range(N_GROUPS)]
    if not sort_refs:
        for g in range(N_GROUPS):
            gate_ref[:, g * LANES:(g + 1) * LANES] = group_gates[g]
        return

    pt_ref, meta_ref = sort_refs
    gate_ref[...] = (group_gates[0] + group_gates[1]) + (group_gates[2] + group_gates[3])
    st = SORT_TOKENS
    ri = lax.broadcasted_iota(jnp.int32, (st, st), 0)
    cj = lax.broadcasted_iota(jnp.int32, (st, st), 1)
    earlier = jnp.where(cj < ri, 1.0, 0.0).astype(BF16)
    col = lax.broadcasted_iota(jnp.int32, (st, pt_ref.shape[1]), 1)
    onehot_all = jnp.where(lane == g_idx, 1.0, 0.0)
    for sb in range(lg.shape[0] // st):
        rs = slice(sb * st, (sb + 1) * st)
        onehot = onehot_all[rs]
        rank = jnp.sum(_dot(earlier, onehot.astype(BF16)) * onehot, axis=1, keepdims=True)
        cnt = jnp.sum(onehot, axis=0, keepdims=True)
        cnt_pad = jnp.floor((cnt + (SORT_PAD - 1.0)) * (1.0 / SORT_PAD)) * SORT_PAD
        seg_off = pltpu.roll(cnt_pad, 1, 1) + pltpu.roll(cnt_pad, 2, 1) + pltpu.roll(cnt_pad, 3, 1)
        dest = (rank + jnp.sum(onehot * seg_off, axis=1, keepdims=True)).astype(jnp.int32)
        pt_ref[rs, :] = jnp.where(col == dest, 1.0, 0.0).astype(BF16)
        meta_ref[sb] = jnp.concatenate([cnt_pad, seg_off, jnp.zeros((SUBLANES - 2, LANES), F32)], axis=0)


def _sort_rows(tm):
    return -(-(tm + N_GROUPS * SORT_PAD) // LANES) * LANES


def _post(att, sa, mb, x, w, *, tm, sort):
    n, d = x.shape
    row = lambda t: (t, 0)
    const = lambda t: (0, 0)
    gate_w = LANES if sort else N_GROUPS * LANES
    out_specs = [pl.BlockSpec((tm, d), row), pl.BlockSpec((tm, d), row), pl.BlockSpec((tm, gate_w), row)]
    out_shape = [jax.ShapeDtypeStruct((n, d), F32), jax.ShapeDtypeStruct((n, d), BF16),
                 jax.ShapeDtypeStruct((n, gate_w), F32)]
    if sort:
        assert tm % SORT_TOKENS == 0
        rows, per_step = _sort_rows(SORT_TOKENS), tm // SORT_TOKENS
        out_specs += [pl.BlockSpec((tm, rows), row), pl.BlockSpec((per_step, SUBLANES, LANES), lambda t: (t, 0, 0))]
        out_shape += [jax.ShapeDtypeStruct((n, rows), BF16),
                      jax.ShapeDtypeStruct((n // SORT_TOKENS, SUBLANES, LANES), F32)]
    resident = lambda shp: pl.BlockSpec(shp, const, pipeline_mode=pl.Buffered(1))
    return pl.pallas_call(
        _post_kernel,
        grid=(n // tm,),
        in_specs=[pl.BlockSpec((tm, ATT_WIDTH), row), pl.BlockSpec((tm, d), row), pl.BlockSpec((tm, d), row),
                  pl.BlockSpec((tm, d), row), resident(w["wpa"].shape), resident(w["wout"].shape),
                  pl.BlockSpec((1, d), const), resident(w["wr"].shape), pl.BlockSpec((1, ROUTER_LANES), const)],
        out_specs=out_specs,
        out_shape=out_shape,
        compiler_params=pltpu.CompilerParams(dimension_semantics=("arbitrary",), vmem_limit_bytes=VMEM_LIMIT),
        name="post_attn",
    )(att, sa, mb, x, w["wpa"], w["wout"], w["g2"], w["wr"], w["br"])


def _moe_kernel(xn2_ref, gate_ref, h_ref, wg_ref, wu_ref, wd_ref, fn_ref, o_ref, acc_ref, *, final):
    g = pl.program_id(1)

    @pl.when(g == 0)
    def _():
        acc_ref[...] = jnp.zeros_like(acc_ref)

    x = xn2_ref[...]
    gate = gate_ref[...]
    act = jax.nn.silu(_dot(x, wg_ref[0])) * _dot(x, wu_ref[0])
    parts = []
    for e in range(EXP_PER_GROUP):
        parts.append((act[:, e * EXP_HIDDEN:(e + 1) * EXP_HIDDEN] * gate[:, e:e + 1]).astype(BF16))
    acc_ref[...] += _dot(jnp.concatenate(parts, axis=1), wd_ref[0])

    @pl.when(g == N_GROUPS - 1)
    def _():
        y = h_ref[...] + acc_ref[...]
        o_ref[...] = _rms(y, fn_ref[...]) if final else y


def _moe(xn2, gate, h, w, fn, *, tm, final):
    n, d = h.shape
    row = lambda t, g: (t, 0)
    wspec = lambda shp: pl.BlockSpec((1,) + shp[1:], lambda t, g: (g, 0, 0))
    return pl.pallas_call(
        functools.partial(_moe_kernel, final=final),
        grid=(n // tm, N_GROUPS),
        in_specs=[pl.BlockSpec((tm, d), row), pl.BlockSpec((tm, LANES), lambda t, g: (t, g)), pl.BlockSpec((tm, d), row),
                  wspec(w["weg"].shape), wspec(w["weu"].shape), wspec(w["wed"].shape),
                  pl.BlockSpec((1, d), lambda t, g: (0, 0))],
        out_specs=pl.BlockSpec((tm, d), row),
        out_shape=jax.ShapeDtypeStruct((n, d), F32),
        scratch_shapes=[pltpu.VMEM((tm, d), F32)],
        compiler_params=pltpu.CompilerParams(dimension_semantics=("arbitrary", "arbitrary"),
                                             vmem_limit_bytes=VMEM_LIMIT),
        name="moe",
    )(xn2, gate, h, w["weg"], w["weu"], w["wed"], fn)


def _moe_sorted_kernel(meta_ref, xn2_ref, gate_ref, h_ref, pt_ref, wg_ref, wu_ref, wd_ref, fn_ref, o_ref,
                       xs_scr, gs_scr, ys_scr, *, final):
    t = pl.program_id(0)
    rows = pt_ref.shape[1]
    pt = pt_ref[...]
    perm = pt.astype(F32).T.astype(BF16)
    xs_scr[0:rows, :] = _dot(perm, xn2_ref[...]).astype(BF16)
    xs_scr[rows:rows + MOE_CHUNK, :] = jnp.zeros((MOE_CHUNK, xs_scr.shape[1]), BF16)
    gate = gate_ref[...]
    g_hi = gate.astype(BF16)
    g_rest = gate - g_hi.astype(F32)
    g_mid = g_rest.astype(BF16)
    g_lo = (g_rest - g_mid.astype(F32)).astype(BF16)
    gs_scr[0:rows, :] = _dot(perm, g_hi) + (_dot(perm, g_mid) + _dot(perm, g_lo))
    gs_scr[rows:rows + MOE_CHUNK, :] = jnp.zeros((MOE_CHUNK, LANES), F32)
    tokens = xn2_ref.shape[0]
    ys_scr[tokens:, :] = jnp.zeros((ys_scr.shape[0] - tokens, ys_scr.shape[1]), F32)

    for g in range(N_GROUPS):
        cnt = meta_ref[t * 2 * N_GROUPS + g]
        seg = meta_ref[t * 2 * N_GROUPS + N_GROUPS + g]

        def chunk(c, carry, g=g, cnt=cnt, seg=seg):
            r0 = pl.multiple_of(seg + c * MOE_CHUNK, SORT_PAD)
            x = xs_scr[pl.ds(r0, MOE_CHUNK), :]
            gt = gs_scr[pl.ds(r0, MOE_CHUNK), :]
            act = jax.nn.silu(_dot(x, wg_ref[g])) * _dot(x, wu_ref[g])
            parts = [(act[:, e * EXP_HIDDEN:(e + 1) * EXP_HIDDEN] * gt[:, e:e + 1]).astype(BF16)
                     for e in range(EXP_PER_GROUP)]
            ys_scr[pl.ds(r0, MOE_CHUNK), :] = _dot(jnp.concatenate(parts, axis=1), wd_ref[g])
            return carry

        lax.fori_loop(0, (cnt + MOE_CHUNK - 1) // MOE_CHUNK, chunk, 0)

    y = h_ref[...] + _dot(pt, ys_scr[0:rows, :].astype(BF16))
    o_ref[...] = _rms(y, fn_ref[...]) if final else y


def _moe_sorted(xn2, gate, h, pt, meta, w, fn, *, tm, final):
    n, d = h.shape
    rows = pt.shape[1]
    row = lambda t, m: (t, 0)
    resident = lambda shp: pl.BlockSpec(shp, lambda t, m: (0, 0, 0), pipeline_mode=pl.Buffered(1))
    grid_spec = pltpu.PrefetchScalarGridSpec(
        num_scalar_prefetch=1,
        grid=(n // tm,),
        in_specs=[pl.BlockSpec((tm, d), row), pl.BlockSpec((tm, LANES), row), pl.BlockSpec((tm, d), row),
                  pl.BlockSpec((tm, rows), row),
                  resident(w["weg"].shape), resident(w["weu"].shape), resident(w["wed"].shape),
                  pl.BlockSpec((1, d), lambda t, m: (0, 0))],
        out_specs=pl.BlockSpec((tm, d), row),
        scratch_shapes=[pltpu.VMEM((rows + MOE_CHUNK, d), BF16), pltpu.VMEM((rows + MOE_CHUNK, LANES), F32),
                        pltpu.VMEM((rows + MOE_CHUNK, d), F32)],
    )
    return pl.pallas_call(
        functools.partial(_moe_sorted_kernel, final=final),
        grid_spec=grid_spec,
        out_shape=jax.ShapeDtypeStruct((n, d), F32),
        compiler_params=pltpu.CompilerParams(dimension_semantics=("arbitrary",), vmem_limit_bytes=VMEM_LIMIT),
        name="moe_sorted",
    )(meta, xn2, gate, h, pt, w["weg"], w["weu"], w["wed"], fn)


def _pack_experts_kernel(w_ref, o_ref):
    for e in range(EXP_PER_GROUP):
        o_ref[:, e * EXP_HIDDEN:(e + 1) * EXP_HIDDEN] = w_ref[e].astype(BF16)


def _pack_experts(w):
    depth, n_groups, n_exp, d, f = w.shape
    rows = 2 * MXU_DIM
    return pl.pallas_call(
        _pack_experts_kernel,
        grid=(depth, n_groups, d // rows),
        in_specs=[pl.BlockSpec((None, None, n_exp, rows, f), lambda l, g, r: (l, g, 0, r, 0))],
        out_specs=pl.BlockSpec((None, None, rows, n_exp * f), lambda l, g, r: (l, g, r, 0)),
        out_shape=jax.ShapeDtypeStruct((depth, n_groups, d, n_exp * f), BF16),
        compiler_params=pltpu.CompilerParams(dimension_semantics=("arbitrary",) * 3, vmem_limit_bytes=VMEM_LIMIT),
        name="pack_experts",
    )(w)


def _rope_tables(pos, d):
    inv = ROPE_THETA ** (-jnp.arange(0, d, 2, dtype=F32) / d)
    ang = pos[:, None] * inv[None, :]
    cos, sin = jnp.cos(ang), jnp.sin(ang)
    reps = LANES // d
    return (jnp.tile(jnp.concatenate([cos, cos], axis=-1), (1, reps)),
            jnp.tile(jnp.concatenate([-sin, sin], axis=-1), (1, reps)))


def _layer_weights(l, norm1, w_in, sgu_ln_g, sgu_ln_b, sgu_w, sgu_b, w_pa, w_pb, w_out, norm2,
                   w_rg, b_rg, w_re, b_re, w_eg, w_eu, w_ed, sgu_chunk):
    d = w_in.shape[1]
    a_end = ATT_WIDTH + 2 * KV_WIDTH + IDX_WIDTH
    b_end = a_end + IDX_DIM + IDX_HEADS
    reps = SGU_CHUNK // sgu_chunk
    n_router = N_GROUPS + N_GROUPS * EXP_PER_GROUP
    return dict(
        g1=norm1[l][None, :],
        wa=w_in[l][:, :a_end].astype(BF16),
        wb=jnp.pad(w_in[l][:, a_end:b_end], ((0, 0), (0, LANES - (b_end - a_end)))).astype(BF16),
        wc=w_in[l][:, b_end:].astype(BF16),
        sguw=jnp.tile(sgu_w[l][:, :sgu_chunk, :sgu_chunk], (1, reps, reps)),
        sgub=jnp.repeat(jnp.tile(sgu_b[l][:, :sgu_chunk], (1, reps)).T, SGU_GDIM, axis=1),
        lng=sgu_ln_g[l][None, :], lnb=sgu_ln_b[l][None, :],
        wpb=w_pb[l].astype(BF16), wpa=w_pa[l].astype(BF16), wout=w_out[l].astype(BF16),
        g2=norm2[l][None, :],
        wr=jnp.pad(jnp.concatenate([w_rg[l], w_re[l]], axis=1), ((0, 0), (0, ROUTER_LANES - n_router))),
        br=jnp.pad(jnp.concatenate([b_rg[l], b_re[l]]), (0, ROUTER_LANES - n_router))[None, :],
        weg=w_eg[l], weu=w_eu[l],
        wed=w_ed[l].reshape(N_GROUPS, GROUP_HIDDEN, d).astype(BF16),
    )


def _token_tile(n):
    return 512 if n % 512 == 0 else n


def kernel(x_prompt, x_sample, cache_k, cache_v, cache_kidx, norm1, w_in, sgu_ln_g, sgu_ln_b, sgu_w, sgu_b,
           w_pa, w_pb, w_out, norm2, w_rg, b_rg, w_re, b_re, w_eg, w_eu, w_ed, final_norm):
    bp, tp, d = x_prompt.shape
    bs, ts, _ = x_sample.shape
    depth, _, past = cache_k.shape[:3]
    np_, ns = bp * tp, bs * ts
    tm_p, tm_s = _token_tile(tp), _token_tile(ns)
    assert tp % tm_p == 0 and tp % Q_BLOCK == 0 and SGU_CHUNK % ts == 0 and tm_s % SGU_CHUNK == 0
    assert tm_p % KEY_SUB == 0 and tm_s % KEY_SUB == 0

    pos_p = jnp.arange(tp, dtype=F32)
    pos_s = jnp.tile(past + jnp.arange(ts, dtype=F32), tm_s // ts)
    tabs_p = _rope_tables(pos_p, HEAD_DIM) + _rope_tables(pos_p, IDX_DIM)
    tabs_s = _rope_tables(pos_s, HEAD_DIM) + _rope_tables(pos_s, IDX_DIM)
    ckt = jnp.transpose(cache_k, (0, 1, 3, 4, 2)).reshape(depth, bs, KV_WIDTH, past)
    cvt = jnp.transpose(cache_v, (0, 1, 3, 4, 2)).reshape(depth, bs, KV_WIDTH, past)
    ckit = jnp.transpose(cache_kidx, (0, 1, 3, 2))
    fn = final_norm[None, :]
    weg_all, weu_all = _pack_experts(w_eg), _pack_experts(w_eu)

    hp = x_prompt.reshape(np_, d)
    hs = x_sample.reshape(ns, d)
    outs = [[] for _ in range(4)]
    kv_bufs = None
    for l in range(depth):
        wargs = (norm1, w_in, sgu_ln_g, sgu_ln_b, sgu_w, sgu_b, w_pa, w_pb, w_out, norm2,
                 w_rg, b_rg, w_re, b_re, weg_all, weu_all, w_ed)
        wp = _layer_weights(l, *wargs, sgu_chunk=SGU_CHUNK)
        ws = dict(wp)
        ws.update({k: v for k, v in _layer_weights(l, *wargs, sgu_chunk=ts).items() if k in ("sguw", "sgub")})
        final = l == depth - 1

        q, kt_all, kb, vt_all, _, vt, qi, kit_all, kib, wi, mb, sa = _inproj(
            hp, wp, tabs_p, tm=2 * tm_p, sgu_chunk=SGU_CHUNK, want_vln=False,
            kv_layer=l, kv_bufs=kv_bufs, kv_shape=(depth, bp, tp))
        kv_bufs = (kt_all, vt_all, kit_all)
        r3 = lambda a: a.reshape(bp, tp, a.shape[-1])
        att = _dsa_prompt(q, qi, wi, r3(kb), vt, r3(kib)).reshape(np_, ATT_WIDTH)
        h, xn2, gate, pt, meta = _post(att, sa, mb, hp, wp, tm=2 * tm_p, sort=True)
        meta = meta[:, 0:2, 0:N_GROUPS].astype(jnp.int32).reshape(-1)
        hp = _moe_sorted(xn2, gate, h, pt, meta, wp, fn, tm=SORT_TOKENS, final=final)

        q, kf, kb, vf, vb, _, qi, kif, kib, wi, mb, sa, vln = _inproj(hs, ws, tabs_s, tm=tm_s, sgu_chunk=ts,
                                                                      want_vln=True)
        r3 = lambda a: a.reshape(bs, ts, a.shape[-1])
        att = _dsa_sample(r3(q), r3(qi), r3(wi), r3(kb), r3(vb), r3(kib), ckt, cvt, ckit, l)
        h, xn2, gate = _post(att.reshape(ns, ATT_WIDTH), sa, mb, hs, ws, tm=tm_s, sort=False)
        hs = _moe(xn2, gate, h, ws, fn, tm=tm_s, final=final)
        outs[0].append(kf.reshape(bs, ts, N_KV_HEADS, HEAD_DIM))
        outs[1].append(vf.reshape(bs, ts, N_KV_HEADS, HEAD_DIM))
        outs[2].append(kif.reshape(bs, ts, IDX_DIM))
        outs[3].append(vln.reshape(bs, ts, SGU_WIDTH))

    kt_all, vt_all, kit_all = kv_bufs
    heads = lambda a: jnp.transpose(a.reshape(depth, bp, N_KV_HEADS, HEAD_DIM, tp), (0, 1, 4, 2, 3))
    return ((hp.reshape(bp, tp, d), hs.reshape(bs, ts, d), heads(kt_all), heads(vt_all),
             jnp.transpose(kit_all, (0, 1, 3, 2))) + tuple(jnp.stack(o) for o in outs))
```

```python
import functools

import jax
import jax.numpy as jnp
from jax import lax
from jax.experimental import pallas as pl
from jax.experimental.pallas import tpu as pltpu

F32 = jnp.float32
BF16 = jnp.bfloat16

CHUNK = 64
N_HEADS = 8
N_KV_HEADS = 2
HEAD_DIM = 64
GQA_GROUP = N_HEADS // N_KV_HEADS
ATT_WIDTH = N_HEADS * HEAD_DIM
KV_WIDTH = N_KV_HEADS * HEAD_DIM
IDX_HEADS = 8
IDX_DIM = 32
IDX_WIDTH = IDX_HEADS * IDX_DIM
TOPK_MAX = 256
Q_BLOCK = 128
DSA_QUERIES = 256
SGU_GROUPS = 4
SGU_CHUNK = 128
SGU_WIDTH = 512
SGU_GDIM = SGU_WIDTH // SGU_GROUPS
N_GROUPS = 4
EXP_PER_GROUP = 8
EXP_HIDDEN = 128
GROUP_HIDDEN = EXP_PER_GROUP * EXP_HIDDEN
ROPE_THETA = 10000.0
EPS = 1e-6
Q_SCALE = HEAD_DIM ** -0.5 * 1.4426950408889634

LANES = 128
SUBLANES = 8
MXU_DIM = 256
ROUTER_LANES = LANES
ROUTER_E0 = N_GROUPS
VMEM_LIMIT = 56 * 1024 * 1024
KEY_TILE = 2 * MXU_DIM
KEY_SUB = MXU_DIM
FOLD_WAYS = 4
COUNT_ROWS = 64
SEARCH_FIRST_STEPS = 17
SEARCH_SECOND_STEPS = 0
SEARCH_ROUND_STEPS = 1
SEARCH_MAX_ROUNDS = 34
SORT_TOKENS = 512
SORT_PAD = 16
MOE_CHUNK = 144
KNORM_SLACK = 1.0 + 2.0 ** -6
SOFTMAX_SUM_FLOOR = 2.0 ** -80
MAX_SUBNORMAL_BITS = 0x007FFFFF
NEG_INF = float("-inf")


def _dot(a, b):
    return jnp.dot(a, b, preferred_element_type=F32)


def _dot_t(a, b):
    return lax.dot_general(a, b, (((1,), (1,)), ((), ())), preferred_element_type=F32)


def _rms(x, g):
    return x * lax.rsqrt(jnp.mean(x * x, axis=-1, keepdims=True) + EPS) * g


def _swap_halves(x, d):
    lane = lax.broadcasted_iota(jnp.int32, x.shape, 1)
    first = (lane % d) < (d // 2)
    return jnp.where(first, pltpu.roll(x, LANES - d // 2, 1), pltpu.roll(x, d // 2, 1))


def _rope(x, cos, sin_signed, d):
    return x * cos + _swap_halves(x, d) * sin_signed


def _inproj_kernel(*refs, sgu_chunk, kv_transposed, n_alias, want_vln):
    (x_ref, g1_ref, wa_ref, wb_ref, wc_ref, cq_ref, sq_ref, ci_ref, si_ref,
     sguw_ref, sgub_ref, lng_ref, lnb_ref, wpb_ref) = refs[:14]
    (q_ref, kf_ref, kb_ref, vf_ref, vb_ref, vt_ref, qi_ref, kif_ref, kib_ref, wi_ref,
     mb_ref, sa_ref) = refs[14 + n_alias:26 + n_alias]
    tm = x_ref.shape[0]
    xn = _rms(x_ref[...], g1_ref[...]).astype(BF16)
    cq, sq, ci, si = cq_ref[...], sq_ref[...], ci_ref[...], si_ref[...]

    a = _dot(xn, wa_ref[...])
    def emit_queries(ref, val):
        if kv_transposed:
            for c in range(tm // DSA_QUERIES):
                ref[c] = val[c * DSA_QUERIES:(c + 1) * DSA_QUERIES, :].T[0:ref.shape[1], :].astype(ref.dtype)
        else:
            ref[...] = val.astype(ref.dtype)

    emit_queries(q_ref, jnp.concatenate(
        [_rope(a[:, c * LANES:(c + 1) * LANES], cq, sq, HEAD_DIM) * Q_SCALE for c in range(ATT_WIDTH // LANES)], axis=1))
    k = _rope(a[:, ATT_WIDTH:ATT_WIDTH + KV_WIDTH], cq, sq, HEAD_DIM)
    kb_ref[...] = k.astype(BF16)
    v = a[:, ATT_WIDTH + KV_WIDTH:ATT_WIDTH + 2 * KV_WIDTH]
    vb_ref[...] = v.astype(BF16)
    if kv_transposed:
        kf_ref[...] = k.T
        vf_ref[...] = v.T
    else:
        kf_ref[...] = k
        vf_ref[...] = v
    for c in range(tm // KEY_SUB):
        vt_ref[c] = v[c * KEY_SUB:(c + 1) * KEY_SUB, :].T.astype(BF16)
    qi0 = ATT_WIDTH + 2 * KV_WIDTH
    emit_queries(qi_ref, jnp.concatenate(
        [_rope(a[:, qi0 + c * LANES:qi0 + (c + 1) * LANES], ci, si, IDX_DIM) for c in range(IDX_WIDTH // LANES)], axis=1))

    b = _dot(xn, wb_ref[...])
    ki_wide = _rope(b, ci, si, IDX_DIM)
    ki = ki_wide[:, :IDX_DIM]
    kif_ref[...] = ki_wide.T[0:IDX_DIM, :] if kv_transposed else ki
    kib_ref[...] = ki.astype(BF16)
    emit_queries(wi_ref, pltpu.roll(b, LANES - IDX_DIM, 1) * (IDX_HEADS ** -0.5))

    u = jax.nn.gelu(_dot(xn, wc_ref[:, 0:SGU_WIDTH]))
    vs = jax.nn.gelu(_dot(xn, wc_ref[:, SGU_WIDTH:2 * SGU_WIDTH]))
    mu = jnp.mean(vs, axis=-1, keepdims=True)
    var = jnp.mean(jnp.square(vs - mu), axis=-1, keepdims=True)
    vln = (vs - mu) * lax.rsqrt(var + EPS) * lng_ref[...] + lnb_ref[...]
    if want_vln:
        refs[26 + n_alias][...] = vln
    ri = lax.broadcasted_iota(jnp.int32, (SGU_CHUNK, SGU_CHUNK), 0)
    cj = lax.broadcasted_iota(jnp.int32, (SGU_CHUNK, SGU_CHUNK), 1)
    mix_mask = (cj <= ri) & ((ri // sgu_chunk) == (cj // sgu_chunk))
    vln_b = vln.astype(BF16)
    w_eff = [jnp.where(mix_mask, sguw_ref[g], 0.0).astype(BF16) for g in range(SGU_GROUPS)]
    sg_rows = []
    for r in range(tm // SGU_CHUNK):
        rows = slice(r * SGU_CHUNK, (r + 1) * SGU_CHUNK)
        mixed = [_dot(w_eff[g], vln_b[rows, g * SGU_GDIM:(g + 1) * SGU_GDIM]) for g in range(SGU_GROUPS)]
        sg_rows.append(u[rows, :] * (jnp.concatenate(mixed, axis=1) + sgub_ref[...]))
    sg = jnp.concatenate(sg_rows, axis=0).astype(BF16)
    d = x_ref.shape[1]
    sa_ref[...] = jax.nn.sigmoid(_dot(xn, wc_ref[:, 2 * SGU_WIDTH:2 * SGU_WIDTH + d])).astype(BF16)
    gb = jax.nn.sigmoid(_dot(xn, wc_ref[:, 2 * SGU_WIDTH + d:]))
    mb_ref[...] = (gb * _dot(sg, wpb_ref[...])).astype(BF16)


def _inproj(x, w, tabs, *, tm, sgu_chunk, want_vln, kv_layer=None, kv_bufs=None, kv_shape=None):
    n, d = x.shape
    n_tab_tiles = tabs[0].shape[0] // tm
    row = lambda t: (t, 0)
    const = lambda t: (0, 0)
    tab = lambda t: (t % n_tab_tiles, 0)
    kv_transposed = kv_layer is not None
    if kv_transposed:
        depth, streams, steps = kv_shape
        per_stream = steps // tm
        kv_sds = lambda wd: jax.ShapeDtypeStruct((depth, streams, wd, steps), F32)
        kv_spec = lambda wd: pl.BlockSpec((None, None, wd, tm),
                                          lambda t: (kv_layer, t // per_stream, 0, t % per_stream))
    else:
        kv_sds = lambda wd: jax.ShapeDtypeStruct((n, wd), F32)
        kv_spec = lambda wd: pl.BlockSpec((tm, wd), row)
    tiled = lambda wd, dt: (jax.ShapeDtypeStruct((n, wd), dt), pl.BlockSpec((tm, wd), row))
    if kv_transposed:
        qb = DSA_QUERIES
        query = lambda wd, rows, dt: (jax.ShapeDtypeStruct((n // qb, rows, qb), dt),
                                      pl.BlockSpec((tm // qb, rows, qb), lambda t: (t, 0, 0)))
    else:
        query = lambda wd, rows, dt: tiled(wd, dt)
    outs = [
        query(ATT_WIDTH, ATT_WIDTH, BF16),
        (kv_sds(KV_WIDTH), kv_spec(KV_WIDTH)),
        tiled(KV_WIDTH, BF16),
        (kv_sds(KV_WIDTH), kv_spec(KV_WIDTH)),
        tiled(KV_WIDTH, BF16),
        (jax.ShapeDtypeStruct((n // KEY_SUB, KV_WIDTH, KEY_SUB), BF16),
         pl.BlockSpec((tm // KEY_SUB, KV_WIDTH, KEY_SUB), lambda t: (t, 0, 0))),
        query(IDX_WIDTH, IDX_WIDTH, BF16),
        (kv_sds(IDX_DIM), kv_spec(IDX_DIM)),
        tiled(IDX_DIM, BF16),
        query(LANES, IDX_HEADS, F32),
        tiled(d, BF16),
        tiled(d, BF16),
    ]
    if want_vln:
        outs.append(tiled(SGU_WIDTH, F32))
    in_specs = [
        pl.BlockSpec((tm, d), row),
        pl.BlockSpec((1, d), const),
        pl.BlockSpec(w["wa"].shape, const, pipeline_mode=pl.Buffered(1)),
        pl.BlockSpec(w["wb"].shape, const, pipeline_mode=pl.Buffered(1)),
        pl.BlockSpec(w["wc"].shape, const, pipeline_mode=pl.Buffered(1)),
        pl.BlockSpec((tm, LANES), tab), pl.BlockSpec((tm, LANES), tab),
        pl.BlockSpec((tm, LANES), tab), pl.BlockSpec((tm, LANES), tab),
        pl.BlockSpec(w["sguw"].shape, lambda t: (0, 0, 0)),
        pl.BlockSpec(w["sgub"].shape, const),
        pl.BlockSpec((1, SGU_WIDTH), const),
        pl.BlockSpec((1, SGU_WIDTH), const),
        pl.BlockSpec(w["wpb"].shape, const),
    ]
    args = [x, w["g1"], w["wa"], w["wb"], w["wc"], *tabs, w["sguw"], w["sgub"], w["lng"], w["lnb"], w["wpb"]]
    aliases = {}
    if kv_bufs is not None:
        for buf, out_idx in zip(kv_bufs, (1, 3, 7)):
            aliases[len(args)] = out_idx
            in_specs.append(pl.BlockSpec(memory_space=pl.ANY))
            args.append(buf)
    return pl.pallas_call(
        functools.partial(_inproj_kernel, sgu_chunk=sgu_chunk, kv_transposed=kv_transposed,
                          n_alias=len(aliases), want_vln=want_vln),
        grid=(n // tm,),
        in_specs=in_specs,
        out_specs=[o[1] for o in outs],
        out_shape=[o[0] for o in outs],
        input_output_aliases=aliases,
        compiler_params=pltpu.CompilerParams(dimension_semantics=("arbitrary",), vmem_limit_bytes=VMEM_LIMIT),
        name="inproj",
    )(*args)


def _float_key(x):
    b = lax.bitcast_convert_type(x, jnp.int32)
    k = b ^ ((b >> 31) & jnp.int32(0x7FFFFFFF))
    return jnp.where(k > MAX_SUBNORMAL_BITS, k - MAX_SUBNORMAL_BITS,
                     jnp.where(k < -MAX_SUBNORMAL_BITS - 1, k + MAX_SUBNORMAL_BITS, jnp.clip(k, -1, 0)))


def _key_float(k):
    k = jnp.where(k > 0, k + MAX_SUBNORMAL_BITS, jnp.where(k < -1, k - MAX_SUBNORMAL_BITS, k))
    return lax.bitcast_convert_type(k ^ ((k >> 31) & jnp.int32(0x7FFFFFFF)), F32)


def _fold_rows(x):
    n = x.shape[0] // SUBLANES
    ways = FOLD_WAYS if n % FOLD_WAYS == 0 else 1
    part = jnp.sum(x.reshape(n // ways, ways, SUBLANES, x.shape[1]), axis=0)
    return jnp.sum(part, axis=0)


def _topk_threshold(count_ge, snap, smin, smax, n_adm, kk):
    lo0 = _float_key(smin + 0.0)
    hx0 = _float_key(smax + 0.0) + 1

    def is_active(st):
        lo, hx, c_lo, _ = st
        return jnp.logical_and(c_lo != kk, lo + 1 < hx)

    def step(st, by_key):
        lo, hx, c_lo, c_hx = st
        active = is_active(st)
        if by_key:
            mid = (lo >> 1) + (hx >> 1) + (lo & hx & 1)
        else:
            mid = _float_key(0.5 * _key_float(lo) + 0.5 * _key_float(hx))
        mid = jnp.where(active, jnp.minimum(jnp.maximum(mid, lo + 1), hx - 1), lo)
        c = count_ge(_key_float(mid))
        up = jnp.logical_and(active, c >= kk)
        down = jnp.logical_and(active, c < kk)
        return (jnp.where(up, mid, lo), jnp.where(down, mid, hx),
                jnp.where(up, c, c_lo), jnp.where(down, c, c_hx))

    def snap_step(st):
        lo, hx, c_lo, c_hx = st
        active = is_active(st)
        first_in, last_in = snap(_key_float(lo), _key_float(hx))
        lo_n = jnp.where(active, _float_key(first_in + 0.0), lo)
        hx_n = jnp.where(active, _float_key(last_in + 0.0) + 1, hx)
        few = jnp.logical_and(active, c_lo - c_hx <= 2.0)
        same = first_in == last_in
        take_last = kk - c_hx <= 1.0
        thr_key = _float_key(jnp.where(take_last, last_in, first_in) + 0.0)
        c_ge = jnp.where(jnp.logical_and(take_last, jnp.logical_not(same)), c_hx + 1.0, c_lo)
        c_gt = jnp.where(jnp.logical_or(take_last, same), c_hx, c_hx + 1.0)
        return (jnp.where(few, thr_key, lo_n), jnp.where(few, thr_key + 1, hx_n),
                jnp.where(few, c_ge, c_lo), jnp.where(few, c_gt, c_hx))

    def n_active(st):
        return jnp.max(jnp.where(is_active(st), 1.0, 0.0))

    st = (lo0, hx0, n_adm, jnp.zeros_like(n_adm))
    for _ in range(SEARCH_FIRST_STEPS):
        st = step(st, by_key=False)
    st = snap_step(st)
    for _ in range(SEARCH_SECOND_STEPS):
        st = step(st, by_key=False)
    act = n_active(st)

    def cond(carry):
        it, act, _ = carry
        return jnp.logical_and(it < SEARCH_MAX_ROUNDS, act > 0.5)

    def body(carry):
        it, _, st = carry
        st = step(snap_step(st), by_key=True)
        for _ in range(SEARCH_ROUND_STEPS):
            st = step(st, by_key=False)
        return it + 1, n_active(st), st

    _, _, (lo, _, c_lo, c_hx) = lax.while_loop(cond, body, (jnp.int32(0), act, st))
    return _key_float(lo), c_lo, c_hx


def _demote_excess_ties(s_ref, n_chunks, thr, need):
    ri = lax.broadcasted_iota(jnp.int32, (LANES, LANES), 0)
    cj = lax.broadcasted_iota(jnp.int32, (LANES, LANES), 1)
    lower = jnp.where(cj <= ri, 1.0, 0.0).astype(BF16)

    def body(j, run):
        off = pl.multiple_of(j * LANES, LANES)
        blk = s_ref[pl.ds(off, LANES), :]
        tied = blk == thr
        tied_f = jnp.where(tied, 1.0, 0.0)
        prefix = _dot(lower, tied_f.astype(BF16)) + run
        s_ref[pl.ds(off, LANES), :] = jnp.where(jnp.logical_and(tied, prefix > need), NEG_INF, blk)
        return run + jnp.sum(tied_f, axis=0, keepdims=True)

    lax.fori_loop(0, n_chunks, body, jnp.zeros((1, s_ref.shape[1]), F32))


def _select_mask(s_ref, n_full, n_half, tk, smin, smax, n_adm, topk, selected=0.0):
    kk = jnp.minimum(n_adm, float(topk))
    half = tk // 2

    def over_rows(visit, carry):
        carry = lax.fori_loop(0, n_full, lambda j, c: visit(pl.multiple_of(j * tk, tk), tk, c), carry)
        return lax.fori_loop(0, n_half, lambda j, c: visit(pl.multiple_of(n_full * tk, half), half, c), carry)

    def count_ge(t):
        def visit(off, rows, acc):
            for r in range(0, rows, COUNT_ROWS):
                acc = acc + _fold_rows(jnp.where(s_ref[pl.ds(off + r, COUNT_ROWS), :] >= t, 1.0, 0.0))
            return acc
        acc = over_rows(visit, jnp.zeros((SUBLANES, s_ref.shape[1]), F32))
        return jnp.sum(acc, axis=0, keepdims=True)

    def snap(lo_f, hx_f):
        def visit(off, rows, carry):
            first_in, last_in = carry
            for r in range(0, rows, COUNT_ROWS):
                blk = s_ref[pl.ds(off + r, COUNT_ROWS), :]
                first_in = jnp.minimum(first_in, -_fold_max(jnp.where(blk >= lo_f, -blk, NEG_INF)))
                last_in = jnp.maximum(last_in, _fold_max(jnp.where(blk < hx_f, blk, NEG_INF)))
            return first_in, last_in
        first_in, last_in = over_rows(visit, (jnp.full((SUBLANES, s_ref.shape[1]), -NEG_INF, F32),
                                              jnp.full((SUBLANES, s_ref.shape[1]), NEG_INF, F32)))
        return jnp.min(first_in, axis=0, keepdims=True), jnp.max(last_in, axis=0, keepdims=True)

    thr, c_ge, c_gt = _topk_threshold(count_ge, snap, smin, smax, n_adm, kk)

    @pl.when(jnp.max(c_ge - kk) > 0.5)
    def _():
        _demote_excess_ties(s_ref, (n_full * tk + n_half * half) // LANES, thr, kk - c_gt)

    def mask_rows(off, rows, carry):
        s_ref[pl.ds(off, rows), :] = jnp.where(s_ref[pl.ds(off, rows), :] >= thr, selected, NEG_INF)
        return carry

    over_rows(mask_rows, 0)


def _dsa_prompt_kernel(q_ref, qi_ref, wi_ref, k_ref, vt_ref, ki_ref, o_ref, s_scr, acc_scr, l_scr, kn_scr, lg_scr,
                       *, topk):
    i = pl.program_id(1)
    tk, ts = KEY_TILE, KEY_SUB
    qb = q_ref.shape[2]
    n_keys = (i + 1) * qb
    qcol = lax.broadcasted_iota(jnp.int32, (1, qb), 1)
    lim = ((i * qb + qcol) // CHUNK + 1) * CHUNK

    @pl.when(i == 0)
    def _():
        ri = lax.broadcasted_iota(jnp.int32, (KV_WIDTH, LANES), 0)
        cj = lax.broadcasted_iota(jnp.int32, (KV_WIDTH, LANES), 1)
        head_sum = jnp.where(ri // HEAD_DIM == cj, 1.0, 0.0).astype(BF16)

        def body(j, acc):
            kt = k_ref[0, pl.ds(pl.multiple_of(j * tk, tk), tk), :].astype(F32)
            return jnp.maximum(acc, _fold_max(_dot((kt * kt).astype(BF16), head_sum)))

        acc = lax.fori_loop(0, k_ref.shape[1] // tk, body, jnp.zeros((SUBLANES, LANES), F32))
        kn_scr[...] = jnp.max(acc, axis=0, keepdims=True) * KNORM_SLACK

    qt = q_ref[0]
    qit = qi_ref[0]
    wit = wi_ref[0]
    qi_rhs = jnp.concatenate([qit[h * IDX_DIM:(h + 1) * IDX_DIM, :] for h in range(IDX_HEADS)], axis=1)
    zero = jnp.zeros((HEAD_DIM, qb), BF16)
    q_cols = []
    for h in range(N_HEADS):
        qh = qt[h * HEAD_DIM:(h + 1) * HEAD_DIM, :]
        q_cols.append(jnp.concatenate([qh, zero] if h < GQA_GROUP else [zero, qh], axis=0))
    q_rhs = jnp.concatenate(q_cols, axis=1)

    def score_sub(sub, carry):
        smax, smin = carry
        off = pl.multiple_of(sub * ts, ts)
        r = _dot(ki_ref[0, pl.ds(off, ts), :], qi_rhs)
        acc = wit[0:1, :] * jnp.maximum(r[:, 0:qb], 0.0)
        for h in range(1, IDX_HEADS):
            acc = acc + wit[h:h + 1, :] * jnp.maximum(r[:, h * qb:(h + 1) * qb], 0.0)
        key = off + lax.broadcasted_iota(jnp.int32, (ts, qb), 0)
        ok = key < lim
        s_scr[pl.ds(off, ts), :] = jnp.where(ok, acc, NEG_INF)
        smax = jnp.maximum(smax, _fold_max(jnp.where(ok, acc, NEG_INF)))
        smin = jnp.minimum(smin, -_fold_max(jnp.where(ok, -acc, NEG_INF)))
        return smax, smin

    n_sub = n_keys // ts
    n_full, n_half = n_sub // 2, n_sub % 2
    carry = (jnp.full((SUBLANES, qb), NEG_INF, F32), jnp.full((SUBLANES, qb), -NEG_INF, F32))
    carry = lax.fori_loop(0, n_full, lambda j, c: score_sub(2 * j + 1, score_sub(2 * j, c)), carry)
    smax, smin = lax.fori_loop(0, n_half, lambda j, c: score_sub(2 * n_full, c), carry)
    smax = jnp.max(smax, axis=0, keepdims=True)
    smin = jnp.min(smin, axis=0, keepdims=True)
    qsq = jnp.square(qt.astype(F32))
    kn = kn_scr[...]
    lane1 = lax.broadcasted_iota(jnp.int32, (1, LANES), 1)
    shift = jnp.zeros((1, qb), F32)
    for h in range(N_HEADS):
        qn2 = jnp.sum(_fold_rows(qsq[h * HEAD_DIM:(h + 1) * HEAD_DIM, :]), axis=0, keepdims=True)
        kmax2 = jnp.max(jnp.where(lane1 == h // GQA_GROUP, kn, 0.0))
        shift = jnp.maximum(shift, jnp.sqrt(qn2 * kmax2))
    _select_mask(s_scr, n_full, n_half, tk, smin, smax, lim.astype(F32), topk, selected=-shift)

    def pv_accumulate(vt, ps, alphas):
        for pair in range(N_HEADS // 2):
            g = (2 * pair) // GQA_GROUP
            o = _dot(vt[g * HEAD_DIM:(g + 1) * HEAD_DIM, :], jnp.concatenate(ps[2 * pair:2 * pair + 2], axis=1))
            for u in range(2):
                h = 2 * pair + u
                rows = slice(h * HEAD_DIM, (h + 1) * HEAD_DIM)
                prev = acc_scr[rows, :] if alphas is None else alphas[h] * acc_scr[rows, :]
                acc_scr[rows, :] = prev + o[:, u * qb:(u + 1) * qb]

    def logits_into(buf, sub):
        off = pl.multiple_of(jnp.minimum(sub * ts, k_ref.shape[1] - ts), ts)
        lg_scr[buf] = _dot(k_ref[0, pl.ds(off, ts), :], q_rhs)

    ones_rows = jnp.ones((2 * SUBLANES, ts), BF16)

    def softmax_pv(buf, sub, l):
        bias = s_scr[pl.ds(pl.multiple_of(sub * ts, ts), ts), :]
        ps = [jnp.exp2(lg_scr[buf, :, h * qb:(h + 1) * qb] + bias).astype(BF16) for h in range(N_HEADS)]
        vt = vt_ref[sub]
        l_rows = []
        for pair in range(N_HEADS // 2):
            g = (2 * pair) // GQA_GROUP
            lhs = jnp.concatenate([vt[g * HEAD_DIM:(g + 1) * HEAD_DIM, :], ones_rows], axis=0)
            o = _dot(lhs, jnp.concatenate(ps[2 * pair:2 * pair + 2], axis=1))
            for u in range(2):
                h = 2 * pair + u
                rows = slice(h * HEAD_DIM, (h + 1) * HEAD_DIM)
                acc_scr[rows, :] = acc_scr[rows, :] + o[0:HEAD_DIM, u * qb:(u + 1) * qb]
                l_rows.append(l[h:h + 1, :] + o[HEAD_DIM:HEAD_DIM + 1, u * qb:(u + 1) * qb])
        return jnp.concatenate(l_rows, axis=0)

    def attn_bounded(j, l):
        logits_into(1, 2 * j + 1)
        l = softmax_pv(0, 2 * j, l)
        logits_into(0, 2 * j + 2)
        return softmax_pv(1, 2 * j + 1, l)

    def attn_online(j, carry):
        m, l = carry
        off = pl.multiple_of(j * ts, ts)
        lg = _dot(k_ref[0, pl.ds(off, ts), :], q_rhs)
        bias = s_scr[pl.ds(off, ts), :]
        m_rows, l_rows, alphas, ps = [], [], [], []
        for h in range(N_HEADS):
            x = lg[:, h * qb:(h + 1) * qb] + bias
            m_new = jnp.maximum(m[h:h + 1, :], jnp.max(_fold_max(x), axis=0, keepdims=True))
            m_use = jnp.where(m_new > NEG_INF, m_new, 0.0)
            alpha = jnp.exp2(m[h:h + 1, :] - m_use)
            p = jnp.exp2(x - m_use)
            l_rows.append(alpha * l[h:h + 1, :] + jnp.sum(_fold_rows(p), axis=0, keepdims=True))
            m_rows.append(m_new)
            alphas.append(alpha)
            ps.append(p.astype(BF16))
        pv_accumulate(vt_ref[j], ps, alphas)
        return jnp.concatenate(m_rows, axis=0), jnp.concatenate(l_rows, axis=0)

    acc_scr[...] = jnp.zeros_like(acc_scr)
    logits_into(0, 0)
    l_fast = lax.fori_loop(0, n_full, attn_bounded, jnp.zeros((N_HEADS, qb), F32))
    l_fast = lax.fori_loop(0, n_half, lambda j, l: softmax_pv(0, 2 * n_full, l), l_fast)
    l_scr[...] = l_fast

    @pl.when(jnp.min(l_fast) < SOFTMAX_SUM_FLOOR)
    def _():
        acc_scr[...] = jnp.zeros_like(acc_scr)
        _, l_exact = lax.fori_loop(0, n_sub, attn_online,
                                   (jnp.full((N_HEADS, qb), NEG_INF, F32), jnp.zeros((N_HEADS, qb), F32)))
        l_scr[...] = l_exact

    l = l_scr[...]
    out_t = jnp.concatenate([acc_scr[h * HEAD_DIM:(h + 1) * HEAD_DIM, :] / l[h:h + 1, :] for h in range(N_HEADS)],
                            axis=0)
    o_ref[0] = out_t.T.astype(BF16)


def _fold_max(x):
    n = x.shape[0] // SUBLANES
    ways = FOLD_WAYS if n % FOLD_WAYS == 0 else 1
    part = jnp.max(x.reshape(n // ways, ways, SUBLANES, x.shape[1]), axis=0)
    return jnp.max(part, axis=0)


def _dsa_prompt(q, qi, wi, kb, vt, kib):
    b, t, _ = kb.shape
    qb = DSA_QUERIES
    assert t % KEY_TILE == 0 and t % qb == 0 and qb % CHUNK == 0
    topk = min(TOPK_MAX, t // 4)
    blk = lambda w: pl.BlockSpec((1, qb, w), lambda bi, i: (bi, i, 0))
    qblk = lambda rows: pl.BlockSpec((1, rows, qb), lambda bi, i: (bi * (t // qb) + i, 0, 0))
    whole = lambda w: pl.BlockSpec((1, t, w), lambda bi, i: (bi, 0, 0))
    return pl.pallas_call(
        functools.partial(_dsa_prompt_kernel, topk=topk),
        grid=(b, t // qb),
        in_specs=[qblk(ATT_WIDTH), qblk(IDX_WIDTH), qblk(IDX_HEADS), whole(KV_WIDTH),
                  pl.BlockSpec((t // KEY_SUB, KV_WIDTH, KEY_SUB), lambda bi, i: (bi, 0, 0)), whole(IDX_DIM)],
        out_specs=blk(ATT_WIDTH),
        out_shape=jax.ShapeDtypeStruct((b, t, ATT_WIDTH), BF16),
        scratch_shapes=[pltpu.VMEM((t, qb), F32), pltpu.VMEM((ATT_WIDTH, qb), F32),
                        pltpu.VMEM((N_HEADS, qb), F32), pltpu.VMEM((1, LANES), F32),
                        pltpu.VMEM((2, KEY_SUB, N_HEADS * qb), F32)],
        compiler_params=pltpu.CompilerParams(dimension_semantics=("arbitrary", "arbitrary"),
                                             vmem_limit_bytes=VMEM_LIMIT),
        name="dsa_prompt",
    )(q, qi, wi, kb, vt, kib)


def _dsa_sample_kernel(q_ref, qi_ref, wi_ref, kn_ref, vn_ref, kin_ref, ckt_ref, cvt_ref, ckit_ref, o_ref,
                       kcat, vcat_t, kicat, s_scr, *, topk):
    tq = q_ref.shape[1]
    past = ckt_ref.shape[1]
    total = kcat.shape[0]
    pad = total - past - tq
    kcat[0:past, :] = ckt_ref[...].T.astype(BF16)
    kcat[past:past + tq, :] = kn_ref[0]
    kcat[past + tq:total, :] = jnp.zeros((pad, KV_WIDTH), BF16)
    kit_wide = jnp.concatenate([ckit_ref[...], jnp.zeros((LANES - IDX_DIM, past), F32)], axis=0)
    kicat[0:past, :] = kit_wide.T[:, 0:IDX_DIM].astype(BF16)
    kicat[past:past + tq, :] = kin_ref[0]
    kicat[past + tq:total, :] = jnp.zeros((pad, IDX_DIM), BF16)
    vcat_t[:, 0:past] = cvt_ref[...].astype(BF16)
    v_new = jnp.concatenate([vn_ref[0].astype(F32), jnp.zeros((pad, KV_WIDTH), F32)], axis=0)
    vcat_t[:, past:total] = v_new.T.astype(BF16)

    lane = lax.broadcasted_iota(jnp.int32, (1, LANES), 1)
    qi = qi_ref[0]
    qi_stack = jnp.concatenate([qi[:, h * IDX_DIM:(h + 1) * IDX_DIM] for h in range(IDX_HEADS)], axis=0)
    wi_rows = jnp.concatenate([wi_ref[0], jnp.zeros((LANES - tq, LANES), F32)], axis=0).T
    w_flat = jnp.zeros((1, LANES), F32)
    for h in range(IDX_HEADS):
        row = wi_rows[h:h + 1, :]
        shifted = row if h == 0 else pltpu.roll(row, h * tq, 1)
        w_flat = jnp.where(lane // tq == h, shifted, w_flat)

    sc = w_flat * jnp.maximum(_dot_t(kicat[...], qi_stack), 0.0)
    for shift in (LANES // 2, LANES // 4, LANES // 8):
        sc = sc + pltpu.roll(sc, shift, 1)
    sc = jnp.where(lane < tq, sc, 0.0)
    for shift in (LANES // 8, LANES // 4, LANES // 2):
        sc = sc + pltpu.roll(sc, shift, 1)
    key = lax.broadcasted_iota(jnp.int32, (total, LANES), 0)
    ok = key < past + tq
    s_scr[...] = jnp.where(ok, sc, NEG_INF)
    smax = jnp.max(jnp.where(ok, sc, NEG_INF), axis=0, keepdims=True)
    smin = jnp.min(jnp.where(ok, sc, -NEG_INF), axis=0, keepdims=True)
    _select_mask(s_scr, 1, 0, total, smin, smax, jnp.full((1, LANES), float(past + tq), F32), topk)

    q = q_ref[0]
    zero = jnp.zeros((tq, HEAD_DIM), BF16)
    q_stack = jnp.concatenate(
        [jnp.concatenate([q[:, h * HEAD_DIM:(h + 1) * HEAD_DIM], zero] if h < GQA_GROUP
                         else [zero, q[:, h * HEAD_DIM:(h + 1) * HEAD_DIM]], axis=1) for h in range(N_HEADS)], axis=0)
    lg = _dot_t(kcat[...], q_stack) + s_scr[...]
    m = jnp.max(lg, axis=0, keepdims=True)
    p = jnp.exp2(lg - m)
    l = jnp.sum(p, axis=0, keepdims=True)
    o = _dot(vcat_t[...], p.astype(BF16))
    o = jnp.where(lane < GQA_GROUP * tq, o[0:HEAD_DIM, :], o[HEAD_DIM:2 * HEAD_DIM, :]) / l
    o_t = jnp.concatenate([o, jnp.zeros((LANES - HEAD_DIM, LANES), F32)], axis=0).T
    o_ref[0] = jnp.concatenate([o_t[h * tq:(h + 1) * tq, 0:HEAD_DIM] for h in range(N_HEADS)], axis=1).astype(BF16)


def _dsa_sample(q, qi, wi, kb, vb, kib, cache_k, cache_v, cache_kidx, layer):
    b, tq, _ = q.shape
    past = cache_k.shape[3]
    assert tq * N_HEADS == LANES and tq * IDX_HEADS == LANES and past % LANES == 0
    total = past + LANES
    topk = min(TOPK_MAX, (past + tq) // 4)
    blk = lambda w: pl.BlockSpec((1, tq, w), lambda bi: (bi, 0, 0))
    cache = lambda w: pl.BlockSpec((None, None, w, past), lambda bi: (layer, bi, 0, 0))
    return pl.pallas_call(
        functools.partial(_dsa_sample_kernel, topk=topk),
        grid=(b,),
        in_specs=[blk(ATT_WIDTH), blk(IDX_WIDTH), blk(LANES), blk(KV_WIDTH), blk(KV_WIDTH), blk(IDX_DIM),
                  cache(KV_WIDTH), cache(KV_WIDTH), cache(IDX_DIM)],
        out_specs=blk(ATT_WIDTH),
        out_shape=jax.ShapeDtypeStruct((b, tq, ATT_WIDTH), BF16),
        scratch_shapes=[pltpu.VMEM((total, KV_WIDTH), BF16), pltpu.VMEM((KV_WIDTH, total), BF16),
                        pltpu.VMEM((total, IDX_DIM), BF16), pltpu.VMEM((total, LANES), F32)],
        compiler_params=pltpu.CompilerParams(dimension_semantics=("arbitrary",), vmem_limit_bytes=VMEM_LIMIT),
        name="dsa_sample",
    )(q, qi, wi, kb, vb, kib, cache_k, cache_v, cache_kidx)


def _first_lane_where(mask, lane):
    return jnp.min(jnp.where(mask, lane, LANES), axis=1, keepdims=True)


def _post_kernel(att_ref, sa_ref, mb_ref, x_ref, wpa_ref, wout_ref, g2_ref, wr_ref, br_ref,
                 h_ref, xn2_ref, gate_ref, *sort_refs):
    pa = _dot(att_ref[...], wpa_ref[...])
    merged = sa_ref[...].astype(F32) * pa + mb_ref[...].astype(F32)
    h = x_ref[...] + _dot(merged.astype(BF16), wout_ref[...])
    h_ref[...] = h
    xn2 = _rms(h, g2_ref[...])
    xh = xn2.astype(BF16)
    xn2_ref[...] = xh

    xl = (xn2 - xh.astype(F32)).astype(BF16)
    wr = wr_ref[...]
    wh = wr.astype(BF16)
    wl = (wr - wh.astype(F32)).astype(BF16)
    lg = _dot(xh, wh) + (_dot(xl, wh) + _dot(xh, wl)) + br_ref[...]

    lane = lax.broadcasted_iota(jnp.int32, lg.shape, 1)
    is_g = lane < N_GROUPS
    mg = jnp.max(jnp.where(is_g, lg, NEG_INF), axis=1, keepdims=True)
    eg = jnp.where(is_g, jnp.exp(lg - mg), 0.0)
    pg = eg / jnp.sum(eg, axis=1, keepdims=True)
    g_w = jnp.max(pg, axis=1, keepdims=True)
    g_idx = _first_lane_where(jnp.logical_and(is_g, pg == g_w), lane)

    e_lo = ROUTER_E0 + EXP_PER_GROUP * g_idx
    is_e = jnp.logical_and(lane >= e_lo, lane < e_lo + EXP_PER_GROUP)
    me = jnp.max(jnp.where(is_e, lg, NEG_INF), axis=1, keepdims=True)
    ee = jnp.where(is_e, jnp.exp(lg - me), 0.0)
    pe = ee / jnp.sum(ee, axis=1, keepdims=True)
    e1 = jnp.max(pe, axis=1, keepdims=True)
    i1 = _first_lane_where(jnp.logical_and(is_e, pe == e1), lane)
    rest = jnp.logical_and(is_e, lane != i1)
    e2 = jnp.max(jnp.where(rest, pe, -1.0), axis=1, keepdims=True)
    i2 = _first_lane_where(jnp.logical_and(rest, pe == e2), lane)
    den = e1 + e2
    gate = jnp.where(lane == i1, e1 / den, jnp.where(lane == i2, e2 / den, 0.0)) * g_w
    group_gates = [jnp.where(lane < EXP_PER_GROUP, pltpu.roll(gate, LANES - (ROUTER_E0 + EXP_PER_GROUP * g), 1), 0.0)
                   for g in range(N_GROUPS)]
    if not sort_refs:
        for g in range(N_GROUPS):
            gate_ref[:, g * LANES:(g + 1) * LANES] = group_gates[g]
        return

    pt_ref, meta_ref = sort_refs
    gate_ref[...] = (group_gates[0] + group_gates[1]) + (group_gates[2] + group_gates[3])
    st = SORT_TOKENS
    ri = lax.broadcasted_iota(jnp.int32, (st, st), 0)
    cj = lax.broadcasted_iota(jnp.int32, (st, st), 1)
    earlier = jnp.where(cj < ri, 1.0, 0.0).astype(BF16)
    col = lax.broadcasted_iota(jnp.int32, (st, pt_ref.shape[1]), 1)
    onehot_all = jnp.where(lane == g_idx, 1.0, 0.0)
    for sb in range(lg.shape[0] // st):
        rs = slice(sb * st, (sb + 1) * st)
        onehot = onehot_all[rs]
        rank = jnp.sum(_dot(earlier, onehot.astype(BF16)) * onehot, axis=1, keepdims=True)
        cnt = jnp.sum(onehot, axis=0, keepdims=True)
        cnt_pad = jnp.floor((cnt + (SORT_PAD - 1.0)) * (1.0 / SORT_PAD)) * SORT_PAD
        seg_off = pltpu.roll(cnt_pad, 1, 1) + pltpu.roll(cnt_pad, 2, 1) + pltpu.roll(cnt_pad, 3, 1)
        dest = (rank + jnp.sum(onehot * seg_off, axis=1, keepdims=True)).astype(jnp.int32)
        pt_ref[rs, :] = jnp.where(col == dest, 1.0, 0.0).astype(BF16)
        meta_ref[sb] = jnp.concatenate([cnt_pad, seg_off, jnp.zeros((SUBLANES - 2, LANES), F32)], axis=0)


def _sort_rows(tm):
    return -(-(tm + N_GROUPS * SORT_PAD) // LANES) * LANES


def _post(att, sa, mb, x, w, *, tm, sort):
    n, d = x.shape
    row = lambda t: (t, 0)
    const = lambda t: (0, 0)
    gate_w = LANES if sort else N_GROUPS * LANES
    out_specs = [pl.BlockSpec((tm, d), row), pl.BlockSpec((tm, d), row), pl.BlockSpec((tm, gate_w), row)]
    out_shape = [jax.ShapeDtypeStruct((n, d), F32), jax.ShapeDtypeStruct((n, d), BF16),
                 jax.ShapeDtypeStruct((n, gate_w), F32)]
    if sort:
        assert tm % SORT_TOKENS == 0
        rows, per_step = _sort_rows(SORT_TOKENS), tm // SORT_TOKENS
        out_specs += [pl.BlockSpec((tm, rows), row), pl.BlockSpec((per_step, SUBLANES, LANES), lambda t: (t, 0, 0))]
        out_shape += [jax.ShapeDtypeStruct((n, rows), BF16),
                      jax.ShapeDtypeStruct((n // SORT_TOKENS, SUBLANES, LANES), F32)]
    resident = lambda shp: pl.BlockSpec(shp, const, pipeline_mode=pl.Buffered(1))
    return pl.pallas_call(
        _post_kernel,
        grid=(n // tm,),
        in_specs=[pl.BlockSpec((tm, ATT_WIDTH), row), pl.BlockSpec((tm, d), row), pl.BlockSpec((tm, d), row),
                  pl.BlockSpec((tm, d), row), resident(w["wpa"].shape), resident(w["wout"].shape),
                  pl.BlockSpec((1, d), const), resident(w["wr"].shape), pl.BlockSpec((1, ROUTER_LANES), const)],
        out_specs=out_specs,
        out_shape=out_shape,
        compiler_params=pltpu.CompilerParams(dimension_semantics=("arbitrary",), vmem_limit_bytes=VMEM_LIMIT),
        name="post_attn",
    )(att, sa, mb, x, w["wpa"], w["wout"], w["g2"], w["wr"], w["br"])


def _moe_kernel(xn2_ref, gate_ref, h_ref, wg_ref, wu_ref, wd_ref, fn_ref, o_ref, acc_ref, *, final):
    g = pl.program_id(1)

    @pl.when(g == 0)
    def _():
        acc_ref[...] = jnp.zeros_like(acc_ref)

    x = xn2_ref[...]
    gate = gate_ref[...]
    act = jax.nn.silu(_dot(x, wg_ref[0])) * _dot(x, wu_ref[0])
    parts = []
    for e in range(EXP_PER_GROUP):
        parts.append((act[:, e * EXP_HIDDEN:(e + 1) * EXP_HIDDEN] * gate[:, e:e + 1]).astype(BF16))
    acc_ref[...] += _dot(jnp.concatenate(parts, axis=1), wd_ref[0])

    @pl.when(g == N_GROUPS - 1)
    def _():
        y = h_ref[...] + acc_ref[...]
        o_ref[...] = _rms(y, fn_ref[...]) if final else y


def _moe(xn2, gate, h, w, fn, *, tm, final):
    n, d = h.shape
    row = lambda t, g: (t, 0)
    wspec = lambda shp: pl.BlockSpec((1,) + shp[1:], lambda t, g: (g, 0, 0))
    return pl.pallas_call(
        functools.partial(_moe_kernel, final=final),
        grid=(n // tm, N_GROUPS),
        in_specs=[pl.BlockSpec((tm, d), row), pl.BlockSpec((tm, LANES), lambda t, g: (t, g)), pl.BlockSpec((tm, d), row),
                  wspec(w["weg"].shape), wspec(w["weu"].shape), wspec(w["wed"].shape),
                  pl.BlockSpec((1, d), lambda t, g: (0, 0))],
        out_specs=pl.BlockSpec((tm, d), row),
        out_shape=jax.ShapeDtypeStruct((n, d), F32),
        scratch_shapes=[pltpu.VMEM((tm, d), F32)],
        compiler_params=pltpu.CompilerParams(dimension_semantics=("arbitrary", "arbitrary"),
                                             vmem_limit_bytes=VMEM_LIMIT),
        name="moe",
    )(xn2, gate, h, w["weg"], w["weu"], w["wed"], fn)


def _moe_sorted_kernel(meta_ref, xn2_ref, gate_ref, h_ref, pt_ref, wg_ref, wu_ref, wd_ref, fn_ref, o_ref,
                       xs_scr, gs_scr, ys_scr, *, final):
    t = pl.program_id(0)
    rows = pt_ref.shape[1]
    pt = pt_ref[...]
    perm = pt.astype(F32).T.astype(BF16)
    xs_scr[0:rows, :] = _dot(perm, xn2_ref[...]).astype(BF16)
    xs_scr[rows:rows + MOE_CHUNK, :] = jnp.zeros((MOE_CHUNK, xs_scr.shape[1]), BF16)
    gate = gate_ref[...]
    g_hi = gate.astype(BF16)
    g_rest = gate - g_hi.astype(F32)
    g_mid = g_rest.astype(BF16)
    g_lo = (g_rest - g_mid.astype(F32)).astype(BF16)
    terms = (g_hi.astype(F32) + pltpu.roll(g_mid.astype(F32), EXP_PER_GROUP, 1)
             + pltpu.roll(g_lo.astype(F32), 2 * EXP_PER_GROUP, 1)).astype(BF16)
    moved = _dot(perm, terms)
    gs_scr[0:rows, :] = (moved + pltpu.roll(moved, LANES - EXP_PER_GROUP, 1)
                         + pltpu.roll(moved, LANES - 2 * EXP_PER_GROUP, 1))
    gs_scr[rows:rows + MOE_CHUNK, :] = jnp.zeros((MOE_CHUNK, LANES), F32)
    tokens = xn2_ref.shape[0]
    ys_scr[tokens:, :] = jnp.zeros((ys_scr.shape[0] - tokens, ys_scr.shape[1]), F32)

    for g in range(N_GROUPS):
        cnt = meta_ref[t * 2 * N_GROUPS + g]
        seg = meta_ref[t * 2 * N_GROUPS + N_GROUPS + g]

        def chunk(c, carry, g=g, cnt=cnt, seg=seg):
            r0 = pl.multiple_of(seg + c * MOE_CHUNK, SORT_PAD)
            x = xs_scr[pl.ds(r0, MOE_CHUNK), :]
            gt = gs_scr[pl.ds(r0, MOE_CHUNK), :]
            act = jax.nn.silu(_dot(x, wg_ref[g])) * _dot(x, wu_ref[g])
            parts = [(act[:, e * EXP_HIDDEN:(e + 1) * EXP_HIDDEN] * gt[:, e:e + 1]).astype(BF16)
                     for e in range(EXP_PER_GROUP)]
            ys_scr[pl.ds(r0, MOE_CHUNK), :] = _dot(jnp.concatenate(parts, axis=1), wd_ref[g])
            return carry

        lax.fori_loop(0, (cnt + MOE_CHUNK - 1) // MOE_CHUNK, chunk, 0)

    y = h_ref[...] + _dot(pt, ys_scr[0:rows, :].astype(BF16))
    o_ref[...] = _rms(y, fn_ref[...]) if final else y


def _moe_sorted(xn2, gate, h, pt, meta, w, fn, *, tm, final):
    n, d = h.shape
    rows = pt.shape[1]
    row = lambda t, m: (t, 0)
    resident = lambda shp: pl.BlockSpec(shp, lambda t, m: (0, 0, 0), pipeline_mode=pl.Buffered(1))
    grid_spec = pltpu.PrefetchScalarGridSpec(
        num_scalar_prefetch=1,
        grid=(n // tm,),
        in_specs=[pl.BlockSpec((tm, d), row), pl.BlockSpec((tm, LANES), row), pl.BlockSpec((tm, d), row),
                  pl.BlockSpec((tm, rows), row),
                  resident(w["weg"].shape), resident(w["weu"].shape), resident(w["wed"].shape),
                  pl.BlockSpec((1, d), lambda t, m: (0, 0))],
        out_specs=pl.BlockSpec((tm, d), row),
        scratch_shapes=[pltpu.VMEM((rows + MOE_CHUNK, d), BF16), pltpu.VMEM((rows + MOE_CHUNK, LANES), F32),
                        pltpu.VMEM((rows + MOE_CHUNK, d), F32)],
    )
    return pl.pallas_call(
        functools.partial(_moe_sorted_kernel, final=final),
        grid_spec=grid_spec,
        out_shape=jax.ShapeDtypeStruct((n, d), F32),
        compiler_params=pltpu.CompilerParams(dimension_semantics=("arbitrary",), vmem_limit_bytes=VMEM_LIMIT),
        name="moe_sorted",
    )(meta, xn2, gate, h, pt, w["weg"], w["weu"], w["wed"], fn)


def _pack_experts_kernel(w_ref, o_ref):
    for e in range(EXP_PER_GROUP):
        o_ref[:, e * EXP_HIDDEN:(e + 1) * EXP_HIDDEN] = w_ref[e].astype(BF16)


def _pack_experts(w):
    depth, n_groups, n_exp, d, f = w.shape
    rows = 2 * MXU_DIM
    return pl.pallas_call(
        _pack_experts_kernel,
        grid=(depth, n_groups, d // rows),
        in_specs=[pl.BlockSpec((None, None, n_exp, rows, f), lambda l, g, r: (l, g, 0, r, 0))],
        out_specs=pl.BlockSpec((None, None, rows, n_exp * f), lambda l, g, r: (l, g, r, 0)),
        out_shape=jax.ShapeDtypeStruct((depth, n_groups, d, n_exp * f), BF16),
        compiler_params=pltpu.CompilerParams(dimension_semantics=("arbitrary",) * 3, vmem_limit_bytes=VMEM_LIMIT),
        name="pack_experts",
    )(w)


def _rope_tables(pos, d):
    inv = ROPE_THETA ** (-jnp.arange(0, d, 2, dtype=F32) / d)
    ang = pos[:, None] * inv[None, :]
    cos, sin = jnp.cos(ang), jnp.sin(ang)
    reps = LANES // d
    return (jnp.tile(jnp.concatenate([cos, cos], axis=-1), (1, reps)),
            jnp.tile(jnp.concatenate([-sin, sin], axis=-1), (1, reps)))


def _layer_weights(l, norm1, w_in, sgu_ln_g, sgu_ln_b, sgu_w, sgu_b, w_pa, w_pb, w_out, norm2,
                   w_rg, b_rg, w_re, b_re, w_eg, w_eu, w_ed, sgu_chunk):
    d = w_in.shape[1]
    a_end = ATT_WIDTH + 2 * KV_WIDTH + IDX_WIDTH
    b_end = a_end + IDX_DIM + IDX_HEADS
    reps = SGU_CHUNK // sgu_chunk
    n_router = N_GROUPS + N_GROUPS * EXP_PER_GROUP
    return dict(
        g1=norm1[l][None, :],
        wa=w_in[l][:, :a_end].astype(BF16),
        wb=jnp.pad(w_in[l][:, a_end:b_end], ((0, 0), (0, LANES - (b_end - a_end)))).astype(BF16),
        wc=w_in[l][:, b_end:].astype(BF16),
        sguw=jnp.tile(sgu_w[l][:, :sgu_chunk, :sgu_chunk], (1, reps, reps)),
        sgub=jnp.repeat(jnp.tile(sgu_b[l][:, :sgu_chunk], (1, reps)).T, SGU_GDIM, axis=1),
        lng=sgu_ln_g[l][None, :], lnb=sgu_ln_b[l][None, :],
        wpb=w_pb[l].astype(BF16), wpa=w_pa[l].astype(BF16), wout=w_out[l].astype(BF16),
        g2=norm2[l][None, :],
        wr=jnp.pad(jnp.concatenate([w_rg[l], w_re[l]], axis=1), ((0, 0), (0, ROUTER_LANES - n_router))),
        br=jnp.pad(jnp.concatenate([b_rg[l], b_re[l]]), (0, ROUTER_LANES - n_router))[None, :],
        weg=w_eg[l], weu=w_eu[l],
        wed=w_ed[l].reshape(N_GROUPS, GROUP_HIDDEN, d).astype(BF16),
    )


def _token_tile(n):
    return 512 if n % 512 == 0 else n


def kernel(x_prompt, x_sample, cache_k, cache_v, cache_kidx, norm1, w_in, sgu_ln_g, sgu_ln_b, sgu_w, sgu_b,
           w_pa, w_pb, w_out, norm2, w_rg, b_rg, w_re, b_re, w_eg, w_eu, w_ed, final_norm):
    bp, tp, d = x_prompt.shape
    bs, ts, _ = x_sample.shape
    depth, _, past = cache_k.shape[:3]
    np_, ns = bp * tp, bs * ts
    tm_p, tm_s = _token_tile(tp), _token_tile(ns)
    assert tp % tm_p == 0 and tp % Q_BLOCK == 0 and SGU_CHUNK % ts == 0 and tm_s % SGU_CHUNK == 0
    assert tm_p % KEY_SUB == 0 and tm_s % KEY_SUB == 0

    pos_p = jnp.arange(tp, dtype=F32)
    pos_s = jnp.tile(past + jnp.arange(ts, dtype=F32), tm_s // ts)
    tabs_p = _rope_tables(pos_p, HEAD_DIM) + _rope_tables(pos_p, IDX_DIM)
    tabs_s = _rope_tables(pos_s, HEAD_DIM) + _rope_tables(pos_s, IDX_DIM)
    ckt = jnp.transpose(cache_k, (0, 1, 3, 4, 2)).reshape(depth, bs, KV_WIDTH, past)
    cvt = jnp.transpose(cache_v, (0, 1, 3, 4, 2)).reshape(depth, bs, KV_WIDTH, past)
    ckit = jnp.transpose(cache_kidx, (0, 1, 3, 2))
    fn = final_norm[None, :]
    weg_all, weu_all = _pack_experts(w_eg), _pack_experts(w_eu)

    hp = x_prompt.reshape(np_, d)
    hs = x_sample.reshape(ns, d)
    outs = [[] for _ in range(4)]
    kv_bufs = None
    for l in range(depth):
        wargs = (norm1, w_in, sgu_ln_g, sgu_ln_b, sgu_w, sgu_b, w_pa, w_pb, w_out, norm2,
                 w_rg, b_rg, w_re, b_re, weg_all, weu_all, w_ed)
        wp = _layer_weights(l, *wargs, sgu_chunk=SGU_CHUNK)
        ws = dict(wp)
        ws.update({k: v for k, v in _layer_weights(l, *wargs, sgu_chunk=ts).items() if k in ("sguw", "sgub")})
        final = l == depth - 1

        q, kt_all, kb, vt_all, _, vt, qi, kit_all, kib, wi, mb, sa = _inproj(
            hp, wp, tabs_p, tm=2 * tm_p, sgu_chunk=SGU_CHUNK, want_vln=False,
            kv_layer=l, kv_bufs=kv_bufs, kv_shape=(depth, bp, tp))
        kv_bufs = (kt_all, vt_all, kit_all)
        r3 = lambda a: a.reshape(bp, tp, a.shape[-1])
        att = _dsa_prompt(q, qi, wi, r3(kb), vt, r3(kib)).reshape(np_, ATT_WIDTH)
        h, xn2, gate, pt, meta = _post(att, sa, mb, hp, wp, tm=2 * tm_p, sort=True)
        meta = meta[:, 0:2, 0:N_GROUPS].astype(jnp.int32).reshape(-1)
        hp = _moe_sorted(xn2, gate, h, pt, meta, wp, fn, tm=SORT_TOKENS, final=final)

        q, kf, kb, vf, vb, _, qi, kif, kib, wi, mb, sa, vln = _inproj(hs, ws, tabs_s, tm=tm_s, sgu_chunk=ts,
                                                                      want_vln=True)
        r3 = lambda a: a.reshape(bs, ts, a.shape[-1])
        att = _dsa_sample(r3(q), r3(qi), r3(wi), r3(kb), r3(vb), r3(kib), ckt, cvt, ckit, l)
        h, xn2, gate = _post(att.reshape(ns, ATT_WIDTH), sa, mb, hs, ws, tm=tm_s, sort=False)
        hs = _moe(xn2, gate, h, ws, fn, tm=tm_s, final=final)
        outs[0].append(kf.reshape(bs, ts, N_KV_HEADS, HEAD_DIM))
        outs[1].append(vf.reshape(bs, ts, N_KV_HEADS, HEAD_DIM))
        outs[2].append(kif.reshape(bs, ts, IDX_DIM))
        outs[3].append(vln.reshape(bs, ts, SGU_WIDTH))

    kt_all, vt_all, kit_all = kv_bufs
    heads = lambda a: jnp.transpose(a.reshape(depth, bp, N_KV_HEADS, HEAD_DIM, tp), (0, 1, 4, 2, 3))
    return ((hp.reshape(bp, tp, d), hs.reshape(bs, ts, d), heads(kt_all), heads(vt_all),
             jnp.transpose(kit_all, (0, 1, 3, 2))) + tuple(jnp.stack(o) for o in outs))
```
